```python
import math
import jax, jax.numpy as jnp
from jax import lax
import numpy as np

D_MODEL = 1024
BATCH = 8
SEQ = 2048
DEPTH = 2
DEC_BATCH = 128
DEC_SEQ = 4
PAST_LEN = 2048
PAGE_SIZE = 128

N_MIXERS = 2
N_ATTN_LAYERS = (DEPTH + 1) // 2
N_RET_LAYERS = DEPTH // 2

ATT_HEADS = 8
ATT_KV_HEADS = 2
ATT_HEAD_DIM = D_MODEL // ATT_HEADS
ATT_GROUP = ATT_HEADS // ATT_KV_HEADS
IDX_HEADS = 4
IDX_DIM = 64
TOPK_MAX = 256
Q_BLOCK = 128
ATT_SPLITS = (ATT_HEADS * ATT_HEAD_DIM, ATT_KV_HEADS * ATT_HEAD_DIM, ATT_KV_HEADS * ATT_HEAD_DIM,
              IDX_HEADS * IDX_DIM, IDX_DIM, IDX_HEADS)
ATT_IN_DIM = sum(ATT_SPLITS)

NUM_BUCKETS = 32
MAX_DISTANCE = 128

RET_HEADS = D_MODEL // 256
RET_KEY_DIM = D_MODEL // RET_HEADS
RET_VAL_DIM = 2 * RET_KEY_DIM
RET_CHUNK = 128
RET_IN_DIM = 2 * RET_HEADS * RET_KEY_DIM + 2 * RET_HEADS * RET_VAL_DIM

D_FF = 4 * D_MODEL

EPS = 1e-6

kernel_name = "dsa_retention_hybrid_step"

F32 = jnp.float32


def _rms(xf):
    return xf * lax.rsqrt(jnp.mean(xf * xf, axis=-1, keepdims=True) + EPS)


def rms_norm(x, g):
    return (_rms(x.astype(F32)) * g.astype(F32)).astype(x.dtype)


def t5_bucket(dist):
    n = jnp.maximum(dist, 0)
    max_exact = NUM_BUCKETS // 2
    nf = jnp.maximum(n, 1).astype(F32)
    large = max_exact + (jnp.log(nf / max_exact) / math.log(MAX_DISTANCE / max_exact)
                         * (NUM_BUCKETS - max_exact)).astype(jnp.int32)
    large = jnp.minimum(large, NUM_BUCKETS - 1)
    return jnp.where(n < max_exact, n, large)


def dsa_project(h, w_in, q_gain, k_gain):
    B, T, _ = h.shape
    z = h @ w_in
    q, k, v, qi, ki, wi = jnp.split(z, np.cumsum(ATT_SPLITS)[:-1].tolist(), axis=-1)
    q = rms_norm(q.reshape(B, T, ATT_HEADS, ATT_HEAD_DIM), q_gain)
    k = rms_norm(k.reshape(B, T, ATT_KV_HEADS, ATT_HEAD_DIM), k_gain)
    v = v.reshape(B, T, ATT_KV_HEADS, ATT_HEAD_DIM)
    qi = qi.reshape(B, T, IDX_HEADS, IDX_DIM)
    return q, k, v, qi, ki, wi


def dsa_select_attend(q, qi, wi, k, v, ki, q_pos, k_pos, topk, rel_bias):
    T = q.shape[0]
    causal = k_pos[None, :] <= q_pos[:, None]
    idx = jnp.einsum('thd,sd->ths', qi.astype(F32), ki.astype(F32))
    score = jnp.einsum('th,ths->ts', wi.astype(F32), jax.nn.relu(idx))
    score = jnp.where(causal, score, -jnp.inf)
    _, sel = lax.top_k(score, topk)
    ks = jnp.take(k, sel, axis=0).astype(F32)
    vs = jnp.take(v, sel, axis=0).astype(F32)
    sel_pos = jnp.take(k_pos, sel, axis=0)
    valid = sel_pos <= q_pos[:, None]
    qg = q.reshape(T, ATT_KV_HEADS, ATT_GROUP, ATT_HEAD_DIM).astype(F32)
    logits = jnp.einsum('tgrd,tkgd->tgrk', qg, ks) * (ATT_HEAD_DIM ** -0.5)
    bias = jnp.take(rel_bias, t5_bucket(q_pos[:, None] - sel_pos), axis=0)
    bias = bias.reshape(T, topk, ATT_KV_HEADS, ATT_GROUP).transpose(0, 2, 3, 1).astype(F32)
    logits = jnp.where(valid[:, None, None, :], logits + bias, -jnp.inf)
    p = jax.nn.softmax(logits, axis=-1)
    o = jnp.einsum('tgrk,tkgd->tgrd', p, vs)
    return o.reshape(T, ATT_HEADS, ATT_HEAD_DIM).astype(q.dtype)


def dsa_prompt(h, w_in, q_gain, k_gain, w_out, rel_bias):
    B, T, _ = h.shape
    q, k, v, qi, ki, wi = dsa_project(h, w_in, q_gain, k_gain)
    topk = min(TOPK_MAX, T // 4)
    nb = T // Q_BLOCK
    pos = jnp.arange(T, dtype=jnp.int32)
    blk = lambda a: a.reshape(B, nb, Q_BLOCK, *a.shape[2:])

    def per_seq(args):
        qb, qib, wib, kk, vv, kki = args

        def per_block(bargs):
            qq, qqi, ww, qp = bargs
            return dsa_select_attend(qq, qqi, ww, kk, vv, kki, qp, pos, topk, rel_bias)

        return lax.map(per_block, (qb, qib, wib, pos.reshape(nb, Q_BLOCK)))

    o = lax.map(per_seq, (blk(q), blk(qi), blk(wi), k, v, ki))
    y = o.reshape(B, T, ATT_HEADS * ATT_HEAD_DIM) @ w_out
    return y, k, v, ki


def dsa_sample(h, cache_k, cache_v, cache_kidx, page_table, w_in, q_gain, k_gain, w_out, rel_bias):
    DB, T, _ = h.shape
    q, k, v, qi, ki, wi = dsa_project(h, w_in, q_gain, k_gain)
    past_len = page_table.shape[1] * cache_k.shape[1]
    past_k = cache_k[page_table].reshape(DB, past_len, ATT_KV_HEADS, ATT_HEAD_DIM).astype(k.dtype)
    past_v = cache_v[page_table].reshape(DB, past_len, ATT_KV_HEADS, ATT_HEAD_DIM).astype(v.dtype)
    past_ki = cache_kidx[page_table].reshape(DB, past_len, IDX_DIM).astype(ki.dtype)
    k_all = jnp.concatenate([past_k, k], axis=1)
    v_all = jnp.concatenate([past_v, v], axis=1)
    ki_all = jnp.concatenate([past_ki, ki], axis=1)
    L = past_len + T
    topk = min(TOPK_MAX, L // 4)
    q_pos = past_len + jnp.arange(T, dtype=jnp.int32)
    k_pos = jnp.arange(L, dtype=jnp.int32)
    o = jax.vmap(lambda a, b, c, d, e, f: dsa_select_attend(a, b, c, d, e, f, q_pos, k_pos, topk, rel_bias))(
        q, qi, wi, k_all, v_all, ki_all)
    y = o.reshape(DB, T, ATT_HEADS * ATT_HEAD_DIM) @ w_out
    return y, k, v, ki


def rotate(x, pos):
    half = x.shape[-1] // 2
    theta = 1.0 / (10000.0 ** jnp.linspace(0.0, 1.0, half, dtype=F32))
    ang = pos.astype(F32)[:, None] * theta[None, :]
    cos = jnp.cos(ang)[None, :, None, :]
    sin = jnp.sin(ang)[None, :, None, :]
    xf = x.astype(F32)
    x1, x2 = xf[..., :half], xf[..., half:]
    return jnp.concatenate([x1 * cos - x2 * sin, x1 * sin + x2 * cos], axis=-1).astype(x.dtype)


def retention_chunked(q, k, v, state0):
    B, T = q.shape[:2]
    C = RET_CHUNK if T % RET_CHUNK == 0 else T
    nc = T // C
    log_g = jnp.log1p(-jnp.exp2(-5.0 - jnp.arange(RET_HEADS, dtype=F32)))
    i = jnp.arange(C, dtype=F32)
    diff = i[:, None] - i[None, :]
    decay_mask = jnp.where(diff >= 0, jnp.exp(log_g[:, None, None] * jnp.maximum(diff, 0.0)), 0.0)
    q_decay = jnp.exp(log_g[None, :] * (i[:, None] + 1.0))
    k_decay = jnp.exp(log_g[None, :] * (C - 1.0 - i[:, None]))
    chunk_decay = jnp.exp(log_g * C)
    to_chunks = lambda a: jnp.moveaxis(a.astype(F32).reshape(B, nc, C, *a.shape[2:]), 1, 0)

    def step(S, inp):
        qc, kc, vc = inp
        att = jnp.einsum('bihd,bjhd->bhij', qc, kc) * decay_mask
        o = (jnp.einsum('bhij,bjhe->bihe', att, vc)
             + jnp.einsum('bihd,bhde->bihe', qc * q_decay[None, :, :, None], S))
        S = (chunk_decay[None, :, None, None] * S
             + jnp.einsum('bjhd,bjhe->bhde', kc * k_decay[None, :, :, None], vc))
        return S, o

    S, o = lax.scan(step, state0.astype(F32), (to_chunks(q), to_chunks(k), to_chunks(v)))
    o = jnp.moveaxis(o, 0, 1).reshape(B, T, RET_HEADS, RET_VAL_DIM)
    return o, S


def retention_mixer(h, state0, pos, w_in, w_out):
    B, T, _ = h.shape
    z = h @ w_in
    qk = RET_HEADS * RET_KEY_DIM
    vd = RET_HEADS * RET_VAL_DIM
    q, k, v, g = jnp.split(z, [qk, 2 * qk, 2 * qk + vd], axis=-1)
    q = rotate(q.reshape(B, T, RET_HEADS, RET_KEY_DIM), pos)
    k = rotate(k.reshape(B, T, RET_HEADS, RET_KEY_DIM), pos) * (RET_KEY_DIM ** -0.5)
    v = v.reshape(B, T, RET_HEADS, RET_VAL_DIM)
    o, S = retention_chunked(q, k, v, state0)
    o = _rms(o) * jax.nn.silu(g.astype(F32)).reshape(B, T, RET_HEADS, RET_VAL_DIM)
    y = o.reshape(B, T, vd).astype(h.dtype) @ w_out
    return y, S


def sqrelu_mlp(h, w1, w2):
    return jnp.square(jax.nn.relu(h @ w1)) @ w2


def setup_inputs(seed: int = 0) -> dict:
    key = jax.random.key(seed)
    ks = jax.random.split(key, 20)
    n_pages = PAST_LEN // PAGE_SIZE
    n_used = DEC_BATCH * n_pages
    n_phys = n_used + n_used // 4
    nrm = lambda kk, shape, s: jax.random.normal(kk, shape, F32) * s
    page_table = jax.random.permutation(ks[0], n_phys)[:n_used].reshape(DEC_BATCH, n_pages).astype(jnp.int32)
    return {
        "x_prompt": nrm(ks[1], (BATCH, SEQ, D_MODEL), 1.0),
        "x_sample": nrm(ks[2], (DEC_BATCH, DEC_SEQ, D_MODEL), 1.0),
        "cache_k": nrm(ks[3], (N_ATTN_LAYERS, n_phys, PAGE_SIZE, ATT_KV_HEADS, ATT_HEAD_DIM), 1.0),
        "cache_v": nrm(ks[4], (N_ATTN_LAYERS, n_phys, PAGE_SIZE, ATT_KV_HEADS, ATT_HEAD_DIM), 1.0),
        "cache_kidx": nrm(ks[5], (N_ATTN_LAYERS, n_phys, PAGE_SIZE, IDX_DIM), 1.0),
        "state_ret": nrm(ks[6], (N_RET_LAYERS, DEC_BATCH, RET_HEADS, RET_KEY_DIM, RET_VAL_DIM), 0.1),
        "page_table": page_table,
        "rel_bias": nrm(ks[7], (NUM_BUCKETS, ATT_HEADS), 0.3),
        "ln_mix": 1.0 + nrm(ks[8], (DEPTH, D_MODEL), 0.01),
        "ln_mlp": 1.0 + nrm(ks[9], (DEPTH, D_MODEL), 0.01),
        "att_w_in": nrm(ks[10], (N_ATTN_LAYERS, D_MODEL, ATT_IN_DIM), D_MODEL ** -0.5),
        "att_q_gain": 1.0 + nrm(ks[11], (N_ATTN_LAYERS, ATT_HEAD_DIM), 0.01),
        "att_k_gain": 1.0 + nrm(ks[12], (N_ATTN_LAYERS, ATT_HEAD_DIM), 0.01),
        "att_w_out": nrm(ks[13], (N_ATTN_LAYERS, ATT_HEADS * ATT_HEAD_DIM, D_MODEL), (ATT_HEADS * ATT_HEAD_DIM) ** -0.5),
        "ret_w_in": nrm(ks[14], (N_RET_LAYERS, D_MODEL, RET_IN_DIM), D_MODEL ** -0.5),
        "ret_w_out": nrm(ks[15], (N_RET_LAYERS, RET_HEADS * RET_VAL_DIM, D_MODEL), (RET_HEADS * RET_VAL_DIM) ** -0.5),
        "mlp_w_in": nrm(ks[16], (DEPTH, D_MODEL, D_FF), D_MODEL ** -0.5),
        "mlp_w_out": nrm(ks[17], (DEPTH, D_FF, D_MODEL), D_FF ** -0.5),
    }


def reference(x_prompt, x_sample, cache_k, cache_v, cache_kidx, state_ret, page_table,
              rel_bias, ln_mix, ln_mlp, att_w_in, att_q_gain, att_k_gain, att_w_out,
              ret_w_in, ret_w_out, mlp_w_in, mlp_w_out):
    yp, ys = x_prompt, x_sample
    Bp, Tp, _ = x_prompt.shape
    Ts = x_sample.shape[1]
    past_len = page_table.shape[1] * cache_k.shape[2]
    pos_p = jnp.arange(Tp, dtype=jnp.int32)
    pos_s = past_len + jnp.arange(Ts, dtype=jnp.int32)
    kp_l, vp_l, kip_l, rp_l = [], [], [], []
    ks_l, vs_l, kis_l, rs_l = [], [], [], []
    for i in range(DEPTH):
        j = i // N_MIXERS
        hp = rms_norm(yp, ln_mix[i])
        hs = rms_norm(ys, ln_mix[i])
        if i % N_MIXERS == 0:
            mp, kp, vp, kip = dsa_prompt(hp, att_w_in[j], att_q_gain[j], att_k_gain[j], att_w_out[j], rel_bias)
            ms, kk, vv, kki = dsa_sample(hs, cache_k[j], cache_v[j], cache_kidx[j], page_table,
                                         att_w_in[j], att_q_gain[j], att_k_gain[j], att_w_out[j], rel_bias)
            kp_l.append(kp); vp_l.append(vp); kip_l.append(kip)
            ks_l.append(kk); vs_l.append(vv); kis_l.append(kki)
        else:
            zero_state = jnp.zeros((Bp, RET_HEADS, RET_KEY_DIM, RET_VAL_DIM), F32)
            mp, Sp = retention_mixer(hp, zero_state, pos_p, ret_w_in[j], ret_w_out[j])
            ms, Ss = retention_mixer(hs, state_ret[j], pos_s, ret_w_in[j], ret_w_out[j])
            rp_l.append(Sp.astype(x_prompt.dtype)); rs_l.append(Ss.astype(state_ret.dtype))
        yp = yp + mp
        ys = ys + ms
        yp = yp + sqrelu_mlp(rms_norm(yp, ln_mlp[i]), mlp_w_in[i], mlp_w_out[i])
        ys = ys + sqrelu_mlp(rms_norm(ys, ln_mlp[i]), mlp_w_in[i], mlp_w_out[i])
    k_prompt = jnp.stack(kp_l)
    v_prompt = jnp.stack(vp_l)
    kidx_prompt = jnp.stack(kip_l)
    ret_prompt = jnp.stack(rp_l)
    k_sample = jnp.stack(ks_l)
    v_sample = jnp.stack(vs_l)
    kidx_sample = jnp.stack(kis_l)
    ret_sample = jnp.stack(rs_l)
    return (yp, ys, k_prompt, v_prompt, kidx_prompt, ret_prompt, k_sample, v_sample, kidx_sample, ret_sample)
```

```python
import functools
import math

import jax
import jax.numpy as jnp
import numpy as np
from jax import lax
from jax.experimental import pallas as pl
from jax.experimental.pallas import tpu as pltpu

F32 = jnp.float32
BF16 = jnp.bfloat16
I32 = jnp.int32

D_MODEL = 1024
PAGE = 128
ATT_HEADS = 8
ATT_KV_HEADS = 2
ATT_GROUP = ATT_HEADS // ATT_KV_HEADS
HEAD_DIM = 128
IDX_HEADS = 4
IDX_DIM = 64
TOPK_MAX = 256
NUM_BUCKETS = 32
MAX_DISTANCE = 128
RET_HEADS = 4
RET_KEY = 256
RET_VAL = 512
RET_CHUNK = 128
D_FF = 4 * D_MODEL
EPS = 1e-6

Q_OFF = 0
K_OFF = ATT_HEADS * HEAD_DIM
V_OFF = K_OFF + ATT_KV_HEADS * HEAD_DIM
QI_OFF = V_OFF + ATT_KV_HEADS * HEAD_DIM
KI_OFF = QI_OFF + IDX_HEADS * IDX_DIM
WI_OFF = KI_OFF + IDX_DIM
ATT_IN = WI_OFF + IDX_HEADS
ATT_IN_PAD = 1920

LANES = 128
ROW_TILE = 512
VMEM_LIMIT = 56 * 1024 * 1024

ATT_SCALE = HEAD_DIM ** -0.5
RET_KSCALE = RET_KEY ** -0.5
INT_MIN = -(2 ** 31)
NEG = -1e30


def _bucket_bounds():
    n = np.arange(0, 8192)
    max_exact = NUM_BUCKETS // 2
    val = np.log(np.maximum(n, 1) / max_exact) / math.log(MAX_DISTANCE / max_exact) * (NUM_BUCKETS - max_exact)
    frac = np.abs(val - np.round(val))
    risky = (n > max_exact) & (n < MAX_DISTANCE) & (frac < 1e-4)
    assert not risky.any()
    large = np.minimum(max_exact + np.floor(val + 1e-9).astype(np.int64), NUM_BUCKETS - 1)
    bucket = np.where(n < max_exact, n, large)
    assert (np.diff(bucket) >= 0).all() and (bucket[MAX_DISTANCE:] == NUM_BUCKETS - 1).all()
    return [int(np.argmax(bucket >= b)) for b in range(NUM_BUCKETS)]


BUCKET_BOUNDS = _bucket_bounds()


def _const_spec(shape):
    zeros = (0,) * len(shape)
    return pl.BlockSpec(shape, lambda *_: zeros, pipeline_mode=pl.Buffered(1))


def _params(n_axes):
    return pltpu.CompilerParams(dimension_semantics=("arbitrary",) * n_axes,
                                vmem_limit_bytes=VMEM_LIMIT)


def _rms_rows(a):
    return a * lax.rsqrt(jnp.mean(a * a, axis=-1, keepdims=True) + EPS)


def _sortable_key(score):
    score = jnp.where(score == 0.0, 0.0, score)
    bits = pltpu.bitcast(score, I32)
    return bits ^ ((bits >> 31) & jnp.int32(0x7FFFFFFF))


def _bias_from_distance(relb_ref, head, dist):
    val = jnp.full(dist.shape, relb_ref[0, head], F32)
    for b in range(1, NUM_BUCKETS):
        val = jnp.where(dist >= BUCKET_BOUNDS[b], relb_ref[b, head], val)
    return val


def _attn_proj_kernel(x_ref, g_ref, w_ref, qg_ref, kg_ref, *outs, transposed):
    x = x_ref[...]
    h = _rms_rows(x) * g_ref[...]
    z = jnp.dot(h.astype(BF16), w_ref[...], preferred_element_type=F32)
    qg = qg_ref[...]
    kg = kg_ref[...]
    k = jnp.concatenate(
        [_rms_rows(z[:, K_OFF + g * HEAD_DIM:K_OFF + (g + 1) * HEAD_DIM]) * kg for g in range(ATT_KV_HEADS)],
        axis=1)
    v = z[:, V_OFF:QI_OFF]
    qi = z[:, QI_OFF:KI_OFF]
    zl = z[:, KI_OFF:ATT_IN_PAD]
    ki = zl[:, :IDX_DIM]
    if transposed:
        qT_ref, qiT_ref, wiT_ref, vT_ref, kb_ref, kib_ref, k_ref, v_ref, ki_ref = outs
        for hh in range(ATT_HEADS):
            qh = _rms_rows(z[:, hh * HEAD_DIM:(hh + 1) * HEAD_DIM]) * qg
            qT_ref[hh * HEAD_DIM:(hh + 1) * HEAD_DIM, :] = qh.T.astype(BF16)
        for c in range(IDX_HEADS * IDX_DIM // LANES):
            qiT_ref[c * LANES:(c + 1) * LANES, :] = qi[:, c * LANES:(c + 1) * LANES].T.astype(BF16)
        wiT_ref[...] = zl.T[IDX_DIM:IDX_DIM + 8, :]
        for g in range(ATT_KV_HEADS):
            vT = v[:, g * HEAD_DIM:(g + 1) * HEAD_DIM].T
            for c in range(x.shape[0] // PAGE):
                vT_ref[c, g * HEAD_DIM:(g + 1) * HEAD_DIM, :] = vT[:, c * PAGE:(c + 1) * PAGE].astype(BF16)
        kb_ref[...] = k.astype(BF16)
        kib_ref[...] = ki.astype(BF16)
    else:
        q_ref, qi_ref, zl_ref, k_ref, v_ref, ki_ref = outs
        for hh in range(ATT_HEADS):
            q_ref[:, hh * HEAD_DIM:(hh + 1) * HEAD_DIM] = _rms_rows(z[:, hh * HEAD_DIM:(hh + 1) * HEAD_DIM]) * qg
        qi_ref[...] = qi
        zl_ref[...] = zl
    k_ref[...] = k
    v_ref[...] = v
    ki_ref[...] = ki


def _attn_proj(x, g, w, qg, kg, *, transposed):
    rows = x.shape[0]
    tm = ROW_TILE
    nt = rows // tm
    kvw = ATT_KV_HEADS * HEAD_DIM
    row_spec = lambda width: pl.BlockSpec((tm, width), lambda i: (i, 0))
    col_spec = lambda height: pl.BlockSpec((height, tm), lambda i: (0, i))
    leaves_shape = [jax.ShapeDtypeStruct((rows, kvw), F32), jax.ShapeDtypeStruct((rows, kvw), F32),
                    jax.ShapeDtypeStruct((rows, IDX_DIM), F32)]
    leaves_spec = [row_spec(kvw), row_spec(kvw), row_spec(IDX_DIM)]
    if transposed:
        out_shape = [jax.ShapeDtypeStruct((ATT_HEADS * HEAD_DIM, rows), BF16),
                     jax.ShapeDtypeStruct((IDX_HEADS * IDX_DIM, rows), BF16),
                     jax.ShapeDtypeStruct((8, rows), F32),
                     jax.ShapeDtypeStruct((rows // PAGE, kvw, PAGE), BF16),
                     jax.ShapeDtypeStruct((rows, kvw), BF16),
                     jax.ShapeDtypeStruct((rows, IDX_DIM), BF16)] + leaves_shape
        out_specs = [col_spec(ATT_HEADS * HEAD_DIM), col_spec(IDX_HEADS * IDX_DIM), col_spec(8),
                     pl.BlockSpec((tm // PAGE, kvw, PAGE), lambda i: (i, 0, 0)),
                     row_spec(kvw), row_spec(IDX_DIM)] + leaves_spec
    else:
        out_shape = [jax.ShapeDtypeStruct((rows, ATT_HEADS * HEAD_DIM), F32),
                     jax.ShapeDtypeStruct((rows, IDX_HEADS * IDX_DIM), F32),
                     jax.ShapeDtypeStruct((rows, LANES), F32)] + leaves_shape
        out_specs = [row_spec(ATT_HEADS * HEAD_DIM), row_spec(IDX_HEADS * IDX_DIM), row_spec(LANES)] + leaves_spec
    return pl.pallas_call(
        functools.partial(_attn_proj_kernel, transposed=transposed),
        grid=(nt,),
        in_specs=[row_spec(D_MODEL), _const_spec((1, D_MODEL)), _const_spec((D_MODEL, ATT_IN_PAD)),
                  _const_spec((1, HEAD_DIM)), _const_spec((1, HEAD_DIM))],
        out_specs=out_specs,
        out_shape=out_shape,
        compiler_params=_params(1),
        name="attn_proj_t" if transposed else "attn_proj_r",
    )(x, g, w, qg, kg)


def _dsa_prompt_kernel(relb_ref, qT_ref, qiT_ref, wiT_ref, k_ref, vT_ref, ki_ref, o_ref,
                       keys_sc, mb_sc, bias_sc, acc_sc, *, topk):
    b = pl.program_id(0)
    i = pl.program_id(1)
    nkb = i + 1
    srow = lax.broadcasted_iota(I32, (PAGE, PAGE), 0)
    tcol = lax.broadcasted_iota(I32, (PAGE, PAGE), 1)

    @pl.when((b == 0) & (i == 0))
    def _init_bias():
        for hh in range(ATT_HEADS):
            for rel in range(2):
                dist = jnp.maximum(rel * PAGE + tcol - srow, 0)
                bias_sc[hh, rel] = _bias_from_distance(relb_ref, hh, dist)
            bias_sc[hh, 2] = jnp.full((PAGE, PAGE), relb_ref[NUM_BUCKETS - 1, hh], F32)

    wiT = wiT_ref[...]

    def score_body(j, carry):
        off = pl.multiple_of(j * PAGE, PAGE)
        kij = ki_ref[pl.ds(off, PAGE), :]
        sc = jnp.zeros((PAGE, PAGE), F32)
        for hh in range(IDX_HEADS):
            idx = jnp.dot(kij, qiT_ref[hh * IDX_DIM:(hh + 1) * IDX_DIM, :], preferred_element_type=F32)
            sc = sc + wiT[hh:hh + 1, :] * jnp.maximum(idx, 0.0)
        causal = jnp.logical_or(j < i, srow <= tcol)
        keys_sc[j] = jnp.where(causal, _sortable_key(sc), INT_MIN)
        return carry

    lax.fori_loop(0, nkb, score_body, 0)

    lane = lax.broadcasted_iota(I32, (1, PAGE), 1)
    kprime = jnp.minimum(topk, i * PAGE + lane + 1).astype(F32)

    def count(pred_of_keys):
        def body(j, cnt):
            return cnt + jnp.where(pred_of_keys(keys_sc[j]), 1.0, 0.0)
        cnt = lax.fori_loop(0, nkb, body, jnp.zeros((PAGE, PAGE), F32))
        return jnp.sum(cnt, axis=0, keepdims=True)

    cand0 = jnp.zeros((1, PAGE), I32)
    thr0 = jnp.where(count(lambda key: key >= cand0) >= kprime, cand0, INT_MIN)

    def bit_body(bi, thr):
        cand = thr | lax.shift_left(jnp.int32(1), 30 - bi)
        return jnp.where(count(lambda key: key >= cand) >= kprime, cand, thr)

    thr = lax.fori_loop(0, 31, bit_body, thr0)
    need = kprime - count(lambda key: key > thr)

    tri = jnp.where(srow > tcol, 1.0, 0.0).astype(BF16)

    def mask_body(j, carry):
        key = keys_sc[j]
        eqf = jnp.where(key == thr, 1.0, 0.0)
        rank = jnp.dot(tri, eqf.astype(BF16), preferred_element_type=F32) + carry
        tie = jnp.where(jnp.logical_and(key == thr, rank < need), 0.0, NEG)
        mb_sc[j] = jnp.where(key > thr, 0.0, tie)
        return carry + jnp.sum(eqf, axis=0, keepdims=True)

    lax.fori_loop(0, nkb, mask_body, jnp.zeros((1, PAGE), F32))

    for g in range(ATT_KV_HEADS):
        qcat = jnp.concatenate(
            [qT_ref[(ATT_GROUP * g + r) * HEAD_DIM:(ATT_GROUP * g + r + 1) * HEAD_DIM, :] for r in range(ATT_GROUP)],
            axis=1)
        acc_sc[...] = jnp.zeros_like(acc_sc)

        def att_body(j, ml, g=g, qcat=qcat):
            m, l = ml
            off = pl.multiple_of(j * PAGE, PAGE)
            kj = k_ref[pl.ds(off, PAGE), g * HEAD_DIM:(g + 1) * HEAD_DIM]
            logits = jnp.dot(kj, qcat, preferred_element_type=F32) * ATT_SCALE
            mb = mb_sc[j]
            relc = jnp.minimum(i - j, 2)
            logits = jnp.concatenate(
                [logits[:, r * PAGE:(r + 1) * PAGE] + (mb + bias_sc[ATT_GROUP * g + r, relc])
                 for r in range(ATT_GROUP)], axis=1)
            m_new = jnp.maximum(m, jnp.max(logits, axis=0, keepdims=True))
            alpha = jnp.exp(m - m_new)
            p = jnp.exp(logits - m_new)
            l_new = l * alpha + jnp.sum(p, axis=0, keepdims=True)
            vTj = vT_ref[j, g * HEAD_DIM:(g + 1) * HEAD_DIM, :]
            acc_sc[...] = acc_sc[...] * alpha + jnp.dot(vTj, p.astype(BF16), preferred_element_type=F32)
            return m_new, l_new

        width = ATT_GROUP * PAGE
        _, l = lax.fori_loop(0, nkb, att_body, (jnp.full((1, width), NEG, F32), jnp.zeros((1, width), F32)))
        outT = acc_sc[...] * (1.0 / l)
        for r in range(ATT_GROUP):
            hh = ATT_GROUP * g + r
            o_ref[:, hh * HEAD_DIM:(hh + 1) * HEAD_DIM] = outT[:, r * PAGE:(r + 1) * PAGE].T.astype(BF16)


def _dsa_prompt(rel_bias, qT, qiT, wiT, kb, vT3, kib, *, batch, seq):
    nb = seq // PAGE
    rows = batch * seq
    kvw = ATT_KV_HEADS * HEAD_DIM
    qcol = lambda height: pl.BlockSpec((height, PAGE), lambda b, i: (0, b * nb + i))
    return pl.pallas_call(
        functools.partial(_dsa_prompt_kernel, topk=min(TOPK_MAX, seq // 4)),
        grid=(batch, nb),
        in_specs=[pl.BlockSpec(memory_space=pltpu.SMEM),
                  qcol(ATT_HEADS * HEAD_DIM), qcol(IDX_HEADS * IDX_DIM), qcol(8),
                  pl.BlockSpec((seq, kvw), lambda b, i: (b, 0)),
                  pl.BlockSpec((nb, kvw, PAGE), lambda b, i: (b, 0, 0)),
                  pl.BlockSpec((seq, IDX_DIM), lambda b, i: (b, 0))],
        out_specs=pl.BlockSpec((PAGE, ATT_HEADS * HEAD_DIM), lambda b, i: (b * nb + i, 0)),
        out_shape=jax.ShapeDtypeStruct((rows, ATT_HEADS * HEAD_DIM), BF16),
        scratch_shapes=[pltpu.VMEM((nb, PAGE, PAGE), I32), pltpu.VMEM((nb, PAGE, PAGE), F32),
                        pltpu.VMEM((ATT_HEADS, 3, PAGE, PAGE), F32),
                        pltpu.VMEM((HEAD_DIM, ATT_GROUP * PAGE), F32)],
        compiler_params=_params(2),
        name="dsa_prompt",
    )(rel_bias, qT, qiT, wiT, kb, vT3, kib)


TQ = 8


def _dsa_sample_kernel(pt_ref, relb_ref, q_ref, qi_ref, zl_ref, kn_ref, vn_ref, *rest, n_pages, n_new):
    ki_pages = rest[:n_pages]
    k_pages = rest[n_pages:2 * n_pages]
    v_pages = rest[2 * n_pages:3 * n_pages]
    o_ref, bias_sc = rest[3 * n_pages:]
    del pt_ref
    past = n_pages * PAGE
    n_blk = n_pages + 1
    rows_g = ATT_GROUP * TQ

    trow = lax.broadcasted_iota(I32, (TQ, PAGE), 0)
    scol = lax.broadcasted_iota(I32, (TQ, PAGE), 1)

    @pl.when(pl.program_id(0) == 0)
    def _init_bias():
        for g in range(ATT_KV_HEADS):
            for r in range(ATT_GROUP):
                hh = ATT_GROUP * g + r
                rs = slice(r * TQ, (r + 1) * TQ)
                bias_sc[g, 0, rs, :] = jnp.full((TQ, PAGE), relb_ref[NUM_BUCKETS - 1, hh], F32)
                bias_sc[g, 1, rs, :] = _bias_from_distance(relb_ref, hh, jnp.maximum(PAGE + trow - scol, 0))
                bias_sc[g, 2, rs, :] = _bias_from_distance(relb_ref, hh, jnp.maximum(trow - scol, 0))

    zl = zl_ref[...]
    pad_rows = lambda a: jnp.concatenate([a, jnp.zeros((PAGE - TQ, a.shape[1]), a.dtype)], axis=0)
    ki_blocks = [r[...].astype(BF16) for r in ki_pages] + [pad_rows(zl[:, :IDX_DIM]).astype(BF16)]
    qi = qi_ref[...]
    nt = (((1,), (1,)), ((), ()))

    keys = []
    for p in range(n_blk):
        idx = lax.dot_general(qi, ki_blocks[p], nt, preferred_element_type=F32)
        sc = jnp.zeros((TQ, PAGE), F32)
        for hh in range(IDX_HEADS):
            sc = sc + zl[:, IDX_DIM + hh:IDX_DIM + hh + 1] * jnp.maximum(idx[hh * TQ:(hh + 1) * TQ], 0.0)
        key = _sortable_key(sc)
        if p == n_pages:
            key = jnp.where(jnp.logical_and(scol <= trow, scol < n_new), key, INT_MIN)
        keys.append(key)
    keys = jnp.concatenate(keys, axis=1)

    kprime = float(min(TOPK_MAX, (past + n_new) // 4))

    def count(mask):
        return jnp.sum(jnp.where(mask, 1.0, 0.0), axis=1, keepdims=True)

    cand0 = jnp.zeros((TQ, 1), I32)
    thr0 = jnp.where(count(keys >= cand0) >= kprime, cand0, INT_MIN)

    def bit_body(bi, thr):
        cand = thr | lax.shift_left(jnp.int32(1), 30 - bi)
        return jnp.where(count(keys >= cand) >= kprime, cand, thr)

    thr = lax.fori_loop(0, 31, bit_body, thr0)
    need = kprime - count(keys > thr)

    s_r = lax.broadcasted_iota(I32, (PAGE, PAGE), 0)
    s_c = lax.broadcasted_iota(I32, (PAGE, PAGE), 1)
    triu = jnp.where(s_r < s_c, 1.0, 0.0).astype(BF16)
    carry = jnp.zeros((TQ, 1), F32)
    mbs = []
    for p in range(n_blk):
        key = keys[:, p * PAGE:(p + 1) * PAGE]
        eqf = jnp.where(key == thr, 1.0, 0.0)
        rank = jnp.dot(eqf.astype(BF16), triu, preferred_element_type=F32) + carry
        tie = jnp.where(jnp.logical_and(key == thr, rank < need), 0.0, NEG)
        mbs.append(jnp.where(key > thr, 0.0, tie))
        carry = carry + jnp.sum(eqf, axis=1, keepdims=True)

    kn = pad_rows(kn_ref[...]).astype(BF16)
    vn = pad_rows(vn_ref[...]).astype(BF16)
    for g in range(ATT_KV_HEADS):
        gs = slice(g * HEAD_DIM, (g + 1) * HEAD_DIM)
        qg = q_ref[g * rows_g:(g + 1) * rows_g, :]
        logits = []
        for p in range(n_blk):
            kp = kn[:, gs] if p == n_pages else k_pages[p][:, gs].astype(BF16)
            lg = lax.dot_general(qg, kp, nt, preferred_element_type=F32) * ATT_SCALE
            kind = 2 if p == n_pages else (1 if p == n_pages - 1 else 0)
            mb = jnp.concatenate([mbs[p]] * ATT_GROUP, axis=0)
            logits.append(lg + (mb + bias_sc[g, kind]))
        m = functools.reduce(jnp.maximum, [jnp.max(lg, axis=1, keepdims=True) for lg in logits])
        l = jnp.zeros((rows_g, 1), F32)
        acc = jnp.zeros((rows_g, HEAD_DIM), F32)
        for p in range(n_blk):
            pr = jnp.exp(logits[p] - m)
            l = l + jnp.sum(pr, axis=1, keepdims=True)
            vp = vn[:, gs] if p == n_pages else v_pages[p][:, gs].astype(BF16)
            acc = acc + jnp.dot(pr.astype(BF16), vp, preferred_element_type=F32)
        o_ref[g * rows_g:(g + 1) * rows_g, :] = (acc * (1.0 / l)).astype(BF16)


def _dsa_sample(page_table, rel_bias, q8, qi8, zl8, kn8, vn8, cache_ki, cache_k, cache_v, *, n_new):
    nsamp, n_pages = page_table.shape
    kvw = ATT_KV_HEADS * HEAD_DIM
    per_sample = lambda *tail: pl.BlockSpec((None,) + tail, lambda b, pt: (b,) + (0,) * len(tail))

    def page_spec(width, p):
        return pl.BlockSpec((None, PAGE, width), lambda b, pt: (pt[b * n_pages + p], 0, 0))

    in_specs = ([pl.BlockSpec(memory_space=pltpu.SMEM),
                 per_sample(ATT_HEADS * TQ, HEAD_DIM), per_sample(IDX_HEADS * TQ, IDX_DIM),
                 per_sample(TQ, LANES), per_sample(TQ, kvw), per_sample(TQ, kvw)]
                + [page_spec(IDX_DIM, p) for p in range(n_pages)]
                + [page_spec(kvw, p) for p in range(n_pages)]
                + [page_spec(kvw, p) for p in range(n_pages)])
    grid_spec = pltpu.PrefetchScalarGridSpec(
        num_scalar_prefetch=1, grid=(nsamp,), in_specs=in_specs,
        out_specs=per_sample(ATT_HEADS * TQ, HEAD_DIM),
        scratch_shapes=[pltpu.VMEM((ATT_KV_HEADS, 3, ATT_GROUP * TQ, PAGE), F32)])
    return pl.pallas_call(
        functools.partial(_dsa_sample_kernel, n_pages=n_pages, n_new=n_new),
        grid_spec=grid_spec,
        out_shape=jax.ShapeDtypeStruct((nsamp, ATT_HEADS * TQ, HEAD_DIM), BF16),
        compiler_params=_params(1),
        name="dsa_sample",
    )(page_table.reshape(-1), rel_bias, q8, qi8, zl8, kn8, vn8,
      *([cache_ki] * n_pages), *([cache_k] * n_pages), *([cache_v] * n_pages))


def _post_mix_kernel(x_ref, o_ref, wo_ref, g_ref, w1_ref, w2_ref, y_ref):
    y = x_ref[...] + jnp.dot(o_ref[...], wo_ref[...], preferred_element_type=F32)
    h = (_rms_rows(y) * g_ref[...]).astype(BF16)
    acc = y
    for c in range(D_FF // D_MODEL):
        a = jnp.dot(h, w1_ref[:, c * D_MODEL:(c + 1) * D_MODEL], preferred_element_type=F32)
        a = jnp.square(jnp.maximum(a, 0.0)).astype(BF16)
        acc = acc + jnp.dot(a, w2_ref[c * D_MODEL:(c + 1) * D_MODEL, :], preferred_element_type=F32)
    y_ref[...] = acc


def _post_mix(x, o, wo, g, w1, w2):
    rows = x.shape[0]
    tm = ROW_TILE
    ko = o.shape[1]
    return pl.pallas_call(
        _post_mix_kernel,
        grid=(rows // tm,),
        in_specs=[pl.BlockSpec((tm, D_MODEL), lambda i: (i, 0)), pl.BlockSpec((tm, ko), lambda i: (i, 0)),
                  _const_spec((ko, D_MODEL)), _const_spec((1, D_MODEL)),
                  _const_spec((D_MODEL, D_FF)), _const_spec((D_FF, D_MODEL))],
        out_specs=pl.BlockSpec((tm, D_MODEL), lambda i: (i, 0)),
        out_shape=jax.ShapeDtypeStruct((rows, D_MODEL), F32),
        compiler_params=_params(1),
        name="post_mix",
    )(x, o, wo, g, w1, w2)


def _ret_proj_kernel(x_ref, g_ref, w_ref, cos_ref, sin_ref, q_ref, k_ref, v_ref, gate_ref):
    hb = (_rms_rows(x_ref[...]) * g_ref[...]).astype(BF16)
    cos = cos_ref[...]
    sin = sin_ref[...]
    half = RET_KEY // 2
    qk = RET_HEADS * RET_KEY
    vd = RET_HEADS * RET_VAL

    def rotated(col0, scale):
        z = jnp.dot(hb, w_ref[:, col0:col0 + RET_KEY], preferred_element_type=F32)
        x1, x2 = z[:, :half], z[:, half:]
        return (x1 * cos - x2 * sin) * scale, (x1 * sin + x2 * cos) * scale

    for hh in range(RET_HEADS):
        a, b = rotated(hh * RET_KEY, 1.0)
        q_ref[:, hh * RET_KEY:hh * RET_KEY + half] = a
        q_ref[:, hh * RET_KEY + half:(hh + 1) * RET_KEY] = b
        a, b = rotated(qk + hh * RET_KEY, RET_KSCALE)
        k_ref[:, hh * RET_KEY:hh * RET_KEY + half] = a
        k_ref[:, hh * RET_KEY + half:(hh + 1) * RET_KEY] = b
    for hh in range(RET_HEADS):
        cs = slice(hh * RET_VAL, (hh + 1) * RET_VAL)
        v_ref[:, cs] = jnp.dot(hb, w_ref[:, 2 * qk + hh * RET_VAL:2 * qk + (hh + 1) * RET_VAL],
                               preferred_element_type=F32).astype(BF16)
        gate_ref[:, cs] = jnp.dot(hb, w_ref[:, 2 * qk + vd + hh * RET_VAL:2 * qk + vd + (hh + 1) * RET_VAL],
                                  preferred_element_type=F32)


def _ret_proj(x, g, w, cos, sin):
    rows = x.shape[0]
    tm = ROW_TILE
    qk = RET_HEADS * RET_KEY
    vd = RET_HEADS * RET_VAL
    n_pos_tiles = cos.shape[0] // tm
    row_spec = lambda width: pl.BlockSpec((tm, width), lambda i: (i, 0))
    pos_spec = pl.BlockSpec((tm, RET_KEY // 2), lambda i: (i % n_pos_tiles, 0))
    return pl.pallas_call(
        _ret_proj_kernel,
        grid=(rows // tm,),
        in_specs=[row_spec(D_MODEL), _const_spec((1, D_MODEL)), _const_spec((D_MODEL, 2 * qk + 2 * vd)),
                  pos_spec, pos_spec],
        out_specs=[row_spec(qk), row_spec(qk), row_spec(vd), row_spec(vd)],
        out_shape=[jax.ShapeDtypeStruct((rows, qk), F32), jax.ShapeDtypeStruct((rows, qk), F32),
                   jax.ShapeDtypeStruct((rows, vd), BF16), jax.ShapeDtypeStruct((rows, vd), F32)],
        compiler_params=_params(1),
        name="ret_proj",
    )(x, g, w, cos, sin)


def _retention_kernel(cd_ref, q_ref, k_ref, v_ref, g_ref, dm_ref, qd_ref, kd_ref, *rest, has_state0):
    if has_state0:
        s0_ref, o_ref, s_ref = rest
    else:
        o_ref, s_ref = rest

    @pl.when(pl.program_id(1) == 0)
    def _init_state():
        s_ref[...] = s0_ref[...] if has_state0 else jnp.zeros_like(s_ref)

    nt = (((1,), (1,)), ((), ()))
    tn = (((0,), (0,)), ((), ()))
    for hh in range(RET_HEADS):
        qh = q_ref[:, hh * RET_KEY:(hh + 1) * RET_KEY]
        kh = k_ref[:, hh * RET_KEY:(hh + 1) * RET_KEY]
        vh = v_ref[:, hh * RET_VAL:(hh + 1) * RET_VAL]
        att = lax.dot_general(qh.astype(BF16), kh.astype(BF16), nt, preferred_element_type=F32) * dm_ref[hh]
        state = s_ref[hh]
        o = (jnp.dot(att.astype(BF16), vh, preferred_element_type=F32)
             + jnp.dot((qh * qd_ref[hh]).astype(BF16), state.astype(BF16), preferred_element_type=F32))
        kd = (kh * kd_ref[hh]).astype(BF16)
        s_ref[hh] = cd_ref[hh] * state + lax.dot_general(kd, vh, tn, preferred_element_type=F32)
        gate = g_ref[:, hh * RET_VAL:(hh + 1) * RET_VAL]
        on = _rms_rows(o) * (gate * jax.nn.sigmoid(gate))
        o_ref[:, hh * RET_VAL:(hh + 1) * RET_VAL] = on.astype(BF16)


def _retention(q, k, v, gate, consts, state0):
    dm, qd, kd, cd = consts
    nb, t, _ = q.shape
    c = dm.shape[1]
    qk = RET_HEADS * RET_KEY
    vd = RET_HEADS * RET_VAL
    tok = lambda width: pl.BlockSpec((None, c, width), lambda b, ci: (b, ci, 0))
    st_spec = pl.BlockSpec((None, RET_HEADS, RET_KEY, RET_VAL), lambda b, ci: (b, 0, 0, 0))
    in_specs = [pl.BlockSpec(memory_space=pltpu.SMEM), tok(qk), tok(qk), tok(vd), tok(vd),
                _const_spec(dm.shape), _const_spec(qd.shape), _const_spec(kd.shape)]
    args = [cd, q, k, v, gate, dm, qd, kd]
    if state0 is not None:
        in_specs.append(st_spec)
        args.append(state0)
    return pl.pallas_call(
        functools.partial(_retention_kernel, has_state0=state0 is not None),
        grid=(nb, t // c),
        in_specs=in_specs,
        out_specs=[tok(vd), st_spec],
        out_shape=[jax.ShapeDtypeStruct((nb, t, vd), BF16),
                   jax.ShapeDtypeStruct((nb, RET_HEADS, RET_KEY, RET_VAL), F32)],
        compiler_params=_params(2),
        name="retention_s" if state0 is not None else "retention_p",
    )(*args)


def _retention_consts(c, c_pad):
    log_g = jnp.log1p(-jnp.exp2(-5.0 - jnp.arange(RET_HEADS, dtype=F32)))
    i = jnp.arange(c, dtype=F32)
    diff = i[:, None] - i[None, :]
    dm = jnp.where(diff >= 0, jnp.exp(log_g[:, None, None] * jnp.maximum(diff, 0.0)), 0.0)
    qd = jnp.exp(log_g[:, None] * (i[None, :] + 1.0))
    kd = jnp.exp(log_g[:, None] * (c - 1.0 - i[None, :]))
    cd = jnp.exp(log_g * c)
    pad = c_pad - c
    dm = jnp.pad(dm, ((0, 0), (0, pad), (0, pad)))
    qd = jnp.broadcast_to(jnp.pad(qd, ((0, 0), (0, pad)))[:, :, None], (RET_HEADS, c_pad, RET_KEY))
    kd = jnp.broadcast_to(jnp.pad(kd, ((0, 0), (0, pad)))[:, :, None], (RET_HEADS, c_pad, RET_KEY))
    return dm, qd, kd, cd


def _rotation_tables(pos):
    half = RET_KEY // 2
    theta = 1.0 / (10000.0 ** jnp.linspace(0.0, 1.0, half, dtype=F32))
    ang = pos.astype(F32)[:, None] * theta[None, :]
    return jnp.cos(ang), jnp.sin(ang)


def kernel(x_prompt, x_sample, cache_k, cache_v, cache_kidx, state_ret, page_table, rel_bias, ln_mix, ln_mlp,
           att_w_in, att_q_gain, att_k_gain, att_w_out, ret_w_in, ret_w_out, mlp_w_in, mlp_w_out):
    bp, tp, _ = x_prompt.shape
    bs, ts, _ = x_sample.shape
    n_phys = cache_k.shape[1]
    assert ln_mix.shape[0] == 2 and att_w_in.shape[0] == 1 and ret_w_in.shape[0] == 1
    assert tp % RET_CHUNK == 0 and ts <= TQ and cache_k.shape[2] == PAGE
    kvw = ATT_KV_HEADS * HEAD_DIM
    past = page_table.shape[1] * PAGE

    w_att = jnp.pad(att_w_in[0], ((0, 0), (0, ATT_IN_PAD - ATT_IN))).astype(BF16)
    w_att_out = att_w_out[0].astype(BF16)
    w_ret = ret_w_in[0].astype(BF16)
    w_ret_out = ret_w_out[0].astype(BF16)
    w1 = mlp_w_in.astype(BF16)
    w2 = mlp_w_out.astype(BF16)
    row = lambda a: a.reshape(1, -1)

    xp = x_prompt.reshape(bp * tp, D_MODEL)
    xs = x_sample.reshape(bs * ts, D_MODEL)

    qT, qiT, wiT, vT3, kb, kib, kp, vp, kip = _attn_proj(
        xp, row(ln_mix[0]), w_att, row(att_q_gain[0]), row(att_k_gain[0]), transposed=True)
    op = _dsa_prompt(rel_bias, qT, qiT, wiT, kb, vT3, kib, batch=bp, seq=tp)
    yp = _post_mix(xp, op, w_att_out, row(ln_mlp[0]), w1[0], w2[0])

    qs, qis, zls, ks, vs, kis = _attn_proj(
        xs, row(ln_mix[0]), w_att, row(att_q_gain[0]), row(att_k_gain[0]), transposed=False)
    pad_t = lambda a: jnp.pad(a, ((0, 0), (0, TQ - ts)) + ((0, 0),) * (a.ndim - 2))
    q8 = pad_t(qs.reshape(bs, ts, ATT_HEADS, HEAD_DIM)).transpose(0, 2, 1, 3)
    q8 = q8.reshape(bs, ATT_HEADS * TQ, HEAD_DIM).astype(BF16)
    qi8 = pad_t(qis.reshape(bs, ts, IDX_HEADS, IDX_DIM)).transpose(0, 2, 1, 3)
    qi8 = qi8.reshape(bs, IDX_HEADS * TQ, IDX_DIM).astype(BF16)
    zl8 = pad_t(zls.reshape(bs, ts, LANES))
    kn8 = pad_t(ks.reshape(bs, ts, kvw))
    vn8 = pad_t(vs.reshape(bs, ts, kvw))
    os8 = _dsa_sample(page_table, rel_bias, q8, qi8, zl8, kn8, vn8,
                      cache_kidx[0], cache_k[0].reshape(n_phys, PAGE, kvw), cache_v[0].reshape(n_phys, PAGE, kvw),
                      n_new=ts)
    os_ = os8.reshape(bs, ATT_HEADS, TQ, HEAD_DIM)[:, :, :ts].transpose(0, 2, 1, 3)
    os_ = os_.reshape(bs * ts, ATT_HEADS * HEAD_DIM)
    ys = _post_mix(xs, os_, w_att_out, row(ln_mlp[0]), w1[0], w2[0])

    cos_p, sin_p = _rotation_tables(jnp.arange(tp, dtype=jnp.int32))
    q, k, v, gate = _ret_proj(yp, row(ln_mix[1]), w_ret, cos_p, sin_p)
    r3 = lambda a: a.reshape(bp, tp, a.shape[-1])
    orp, state_p = _retention(r3(q), r3(k), r3(v), r3(gate), _retention_consts(RET_CHUNK, RET_CHUNK), None)
    yp = _post_mix(yp, orp.reshape(bp * tp, -1), w_ret_out, row(ln_mlp[1]), w1[1], w2[1])

    cos_s, sin_s = _rotation_tables(past + jnp.arange(ts, dtype=jnp.int32))
    tile_s = lambda a: jnp.tile(a, (bs, 1))
    q, k, v, gate = _ret_proj(ys, row(ln_mix[1]), w_ret, tile_s(cos_s), tile_s(sin_s))
    r3s = lambda a: pad_t(a.reshape(bs, ts, a.shape[-1]))
    ors, state_s = _retention(r3s(q), r3s(k), r3s(v), r3s(gate), _retention_consts(ts, TQ), state_ret[0])
    ys = _post_mix(ys, ors[:, :ts].reshape(bs * ts, -1), w_ret_out, row(ln_mlp[1]), w1[1], w2[1])

    return (yp.reshape(bp, tp, D_MODEL), ys.reshape(bs, ts, D_MODEL),
            kp.reshape(1, bp, tp, ATT_KV_HEADS, HEAD_DIM), vp.reshape(1, bp, tp, ATT_KV_HEADS, HEAD_DIM),
            kip.reshape(1, bp, tp, IDX_DIM), state_p[None],
            ks.reshape(1, bs, ts, ATT_KV_HEADS, HEAD_DIM), vs.reshape(1, bs, ts, ATT_KV_HEADS, HEAD_DIM),
            kis.reshape(1, bs, ts, IDX_DIM), state_s[None])
```

```python
import functools
import math

import jax
import jax.numpy as jnp
import numpy as np
from jax import lax
from jax.experimental import pallas as pl
from jax.experimental.pallas import tpu as pltpu

F32 = jnp.float32
BF16 = jnp.bfloat16
I32 = jnp.int32

D_MODEL = 1024
PAGE = 128
ATT_HEADS = 8
ATT_KV_HEADS = 2
ATT_GROUP = ATT_HEADS // ATT_KV_HEADS
HEAD_DIM = 128
IDX_HEADS = 4
IDX_DIM = 64
TOPK_MAX = 256
NUM_BUCKETS = 32
MAX_DISTANCE = 128
RET_HEADS = 4
RET_KEY = 256
RET_VAL = 512
RET_CHUNK = 128
D_FF = 4 * D_MODEL
EPS = 1e-6

Q_OFF = 0
K_OFF = ATT_HEADS * HEAD_DIM
V_OFF = K_OFF + ATT_KV_HEADS * HEAD_DIM
QI_OFF = V_OFF + ATT_KV_HEADS * HEAD_DIM
KI_OFF = QI_OFF + IDX_HEADS * IDX_DIM
WI_OFF = KI_OFF + IDX_DIM
ATT_IN = WI_OFF + IDX_HEADS
ATT_IN_PAD = 1920

LANES = 128
ROW_TILE = 512
VMEM_LIMIT = 56 * 1024 * 1024

ATT_SCALE = HEAD_DIM ** -0.5
RET_KSCALE = RET_KEY ** -0.5
INT_MIN = -(2 ** 31)
NEG = -1e30


def _bucket_bounds():
    n = np.arange(0, 8192)
    max_exact = NUM_BUCKETS // 2
    val = np.log(np.maximum(n, 1) / max_exact) / math.log(MAX_DISTANCE / max_exact) * (NUM_BUCKETS - max_exact)
    frac = np.abs(val - np.round(val))
    risky = (n > max_exact) & (n < MAX_DISTANCE) & (frac < 1e-4)
    assert not risky.any()
    large = np.minimum(max_exact + np.floor(val + 1e-9).astype(np.int64), NUM_BUCKETS - 1)
    bucket = np.where(n < max_exact, n, large)
    assert (np.diff(bucket) >= 0).all() and (bucket[MAX_DISTANCE:] == NUM_BUCKETS - 1).all()
    return [int(np.argmax(bucket >= b)) for b in range(NUM_BUCKETS)]


BUCKET_BOUNDS = _bucket_bounds()


def _const_spec(shape):
    zeros = (0,) * len(shape)
    return pl.BlockSpec(shape, lambda *_: zeros, pipeline_mode=pl.Buffered(1))


def _params(n_axes):
    return pltpu.CompilerParams(dimension_semantics=("arbitrary",) * n_axes,
                                vmem_limit_bytes=VMEM_LIMIT)


def _rms_rows(a):
    return a * lax.rsqrt(jnp.mean(a * a, axis=-1, keepdims=True) + EPS)


def _sortable_key(score):
    score = jnp.where(score == 0.0, 0.0, score)
    bits = pltpu.bitcast(score, I32)
    return bits ^ ((bits >> 31) & jnp.int32(0x7FFFFFFF))


def _bias_from_distance(relb_ref, head, dist):
    val = jnp.full(dist.shape, relb_ref[0, head], F32)
    for b in range(1, NUM_BUCKETS):
        val = jnp.where(dist >= BUCKET_BOUNDS[b], relb_ref[b, head], val)
    return val


def _attn_proj_kernel(x_ref, g_ref, w_ref, qg_ref, kg_ref, *outs, transposed):
    x = x_ref[...]
    h = _rms_rows(x) * g_ref[...]
    z = jnp.dot(h.astype(BF16), w_ref[...], preferred_element_type=F32)
    qg = qg_ref[...]
    kg = kg_ref[...]
    k = jnp.concatenate(
        [_rms_rows(z[:, K_OFF + g * HEAD_DIM:K_OFF + (g + 1) * HEAD_DIM]) * kg for g in range(ATT_KV_HEADS)],
        axis=1)
    v = z[:, V_OFF:QI_OFF]
    qi = z[:, QI_OFF:KI_OFF]
    zl = z[:, KI_OFF:ATT_IN_PAD]
    ki = zl[:, :IDX_DIM]
    if transposed:
        qT_ref, qiT_ref, wiT_ref, vT_ref, kb_ref, kib_ref, k_ref, v_ref, ki_ref = outs
        for hh in range(ATT_HEADS):
            qh = _rms_rows(z[:, hh * HEAD_DIM:(hh + 1) * HEAD_DIM]) * qg
            qT_ref[hh * HEAD_DIM:(hh + 1) * HEAD_DIM, :] = qh.T.astype(BF16)
        for c in range(IDX_HEADS * IDX_DIM // LANES):
            qiT_ref[c * LANES:(c + 1) * LANES, :] = qi[:, c * LANES:(c + 1) * LANES].T.astype(BF16)
        wiT_ref[...] = zl.T[IDX_DIM:IDX_DIM + 8, :]
        for g in range(ATT_KV_HEADS):
            vT = v[:, g * HEAD_DIM:(g + 1) * HEAD_DIM].T
            for c in range(x.shape[0] // PAGE):
                vT_ref[c, g * HEAD_DIM:(g + 1) * HEAD_DIM, :] = vT[:, c * PAGE:(c + 1) * PAGE].astype(BF16)
        kb_ref[...] = k.astype(BF16)
        kib_ref[...] = ki.astype(BF16)
    else:
        q_ref, qi_ref, zl_ref, k_ref, v_ref, ki_ref = outs
        for hh in range(ATT_HEADS):
            q_ref[:, hh * HEAD_DIM:(hh + 1) * HEAD_DIM] = _rms_rows(z[:, hh * HEAD_DIM:(hh + 1) * HEAD_DIM]) * qg
        qi_ref[...] = qi
        zl_ref[...] = zl
    k_ref[...] = k
    v_ref[...] = v
    ki_ref[...] = ki


def _attn_proj(x, g, w, qg, kg, *, transposed):
    rows = x.shape[0]
    tm = ROW_TILE
    nt = rows // tm
    kvw = ATT_KV_HEADS * HEAD_DIM
    row_spec = lambda width: pl.BlockSpec((tm, width), lambda i: (i, 0))
    col_spec = lambda height: pl.BlockSpec((height, tm), lambda i: (0, i))
    leaves_shape = [jax.ShapeDtypeStruct((rows, kvw), F32), jax.ShapeDtypeStruct((rows, kvw), F32),
                    jax.ShapeDtypeStruct((rows, IDX_DIM), F32)]
    leaves_spec = [row_spec(kvw), row_spec(kvw), row_spec(IDX_DIM)]
    if transposed:
        out_shape = [jax.ShapeDtypeStruct((ATT_HEADS * HEAD_DIM, rows), BF16),
                     jax.ShapeDtypeStruct((IDX_HEADS * IDX_DIM, rows), BF16),
                     jax.ShapeDtypeStruct((8, rows), F32),
                     jax.ShapeDtypeStruct((rows // PAGE, kvw, PAGE), BF16),
                     jax.ShapeDtypeStruct((rows, kvw), BF16),
                     jax.ShapeDtypeStruct((rows, IDX_DIM), BF16)] + leaves_shape
        out_specs = [col_spec(ATT_HEADS * HEAD_DIM), col_spec(IDX_HEADS * IDX_DIM), col_spec(8),
                     pl.BlockSpec((tm // PAGE, kvw, PAGE), lambda i: (i, 0, 0)),
                     row_spec(kvw), row_spec(IDX_DIM)] + leaves_spec
    else:
        out_shape = [jax.ShapeDtypeStruct((rows, ATT_HEADS * HEAD_DIM), F32),
                     jax.ShapeDtypeStruct((rows, IDX_HEADS * IDX_DIM), F32),
                     jax.ShapeDtypeStruct((rows, LANES), F32)] + leaves_shape
        out_specs = [row_spec(ATT_HEADS * HEAD_DIM), row_spec(IDX_HEADS * IDX_DIM), row_spec(LANES)] + leaves_spec
    return pl.pallas_call(
        functools.partial(_attn_proj_kernel, transposed=transposed),
        grid=(nt,),
        in_specs=[row_spec(D_MODEL), _const_spec((1, D_MODEL)), _const_spec((D_MODEL, ATT_IN_PAD)),
                  _const_spec((1, HEAD_DIM)), _const_spec((1, HEAD_DIM))],
        out_specs=out_specs,
        out_shape=out_shape,
        compiler_params=_params(1),
        name="attn_proj_t" if transposed else "attn_proj_r",
    )(x, g, w, qg, kg)


def _dsa_prompt_kernel(relb_ref, qT_ref, qiT_ref, wiT_ref, k_ref, vT_ref, ki_ref, o_ref,
                       keys_sc, mb_sc, bias_sc, logit_sc, acc_sc, *, topk):
    b = pl.program_id(0)
    i = pl.program_id(1)
    nkb = i + 1
    srow = lax.broadcasted_iota(I32, (PAGE, PAGE), 0)
    tcol = lax.broadcasted_iota(I32, (PAGE, PAGE), 1)

    @pl.when((b == 0) & (i == 0))
    def _init_bias():
        for hh in range(ATT_HEADS):
            for rel in range(2):
                dist = jnp.maximum(rel * PAGE + tcol - srow, 0)
                bias_sc[hh, rel] = _bias_from_distance(relb_ref, hh, dist)
            bias_sc[hh, 2] = jnp.full((PAGE, PAGE), relb_ref[NUM_BUCKETS - 1, hh], F32)

    wiT = wiT_ref[...]

    npair = (i + 2) // 2

    def score_block(j):
        off = pl.multiple_of(j * PAGE, PAGE)
        kij = ki_ref[pl.ds(off, PAGE), :]
        sc = jnp.zeros((PAGE, PAGE), F32)
        for hh in range(IDX_HEADS):
            idx = jnp.dot(kij, qiT_ref[hh * IDX_DIM:(hh + 1) * IDX_DIM, :], preferred_element_type=F32)
            sc = sc + wiT[hh:hh + 1, :] * jnp.maximum(idx, 0.0)
        causal = jnp.logical_or(j < i, jnp.logical_and(j == i, srow <= tcol))
        keys_sc[j] = jnp.where(causal, _sortable_key(sc), INT_MIN)

    def score_body(jj, carry):
        score_block(2 * jj)
        score_block(2 * jj + 1)
        return carry

    lax.fori_loop(0, npair, score_body, 0)

    lane = lax.broadcasted_iota(I32, (1, PAGE), 1)
    kprime = jnp.minimum(topk, i * PAGE + lane + 1).astype(F32)

    def count(pred_of_keys):
        def body(j, cnt):
            return cnt + jnp.where(pred_of_keys(keys_sc[j]), 1.0, 0.0)
        cnt = lax.fori_loop(0, nkb, body, jnp.zeros((PAGE, PAGE), F32))
        return jnp.sum(cnt, axis=0, keepdims=True)

    cand0 = jnp.zeros((1, PAGE), I32)
    thr0 = jnp.where(count(lambda key: key >= cand0) >= kprime, cand0, INT_MIN)

    def bit_body(bi, thr):
        cand = thr | lax.shift_left(jnp.int32(1), 30 - bi)
        return jnp.where(count(lambda key: key >= cand) >= kprime, cand, thr)

    thr = lax.fori_loop(0, 31, bit_body, thr0)
    need = kprime - count(lambda key: key > thr)

    tri = jnp.where(srow > tcol, 1.0, 0.0).astype(BF16)

    def mask_block(j, carry):
        key = keys_sc[j]
        eqf = jnp.where(key == thr, 1.0, 0.0)
        rank = jnp.dot(tri, eqf.astype(BF16), preferred_element_type=F32) + carry
        tie = jnp.where(jnp.logical_and(key == thr, rank < need), 0.0, NEG)
        mb_sc[j] = jnp.where(key > thr, 0.0, tie)
        return carry + jnp.sum(eqf, axis=0, keepdims=True)

    lax.fori_loop(0, npair, lambda jj, c: mask_block(2 * jj + 1, mask_block(2 * jj, c)),
                  jnp.zeros((1, PAGE), F32))

    groups = range(ATT_KV_HEADS)
    width = ATT_GROUP * PAGE
    qcats = [jnp.concatenate(
        [qT_ref[(ATT_GROUP * g + r) * HEAD_DIM:(ATT_GROUP * g + r + 1) * HEAD_DIM, :] for r in range(ATT_GROUP)],
        axis=1) for g in groups]

    def logit_block(j, ms):
        off = pl.multiple_of(j * PAGE, PAGE)
        mb = mb_sc[j]
        relc = jnp.clip(i - j, 0, 2)
        out = []
        for g in groups:
            kj = k_ref[pl.ds(off, PAGE), g * HEAD_DIM:(g + 1) * HEAD_DIM]
            logits = jnp.dot(kj, qcats[g], preferred_element_type=F32) * ATT_SCALE
            logits = jnp.concatenate(
                [logits[:, r * PAGE:(r + 1) * PAGE] + (mb + bias_sc[ATT_GROUP * g + r, relc])
                 for r in range(ATT_GROUP)], axis=1)
            logit_sc[g, j] = logits
            out.append(jnp.maximum(ms[g], jnp.max(logits, axis=0, keepdims=True)))
        return tuple(out)

    ms = lax.fori_loop(0, npair, lambda jj, c: logit_block(2 * jj + 1, logit_block(2 * jj, c)),
                       tuple(jnp.full((1, width), NEG, F32) for _ in groups))

    def value_block(j, ls):
        contrib, out = [], []
        for g in groups:
            p = jnp.exp(logit_sc[g, j] - ms[g])
            out.append(ls[g] + jnp.sum(p, axis=0, keepdims=True))
            vTj = vT_ref[j, g * HEAD_DIM:(g + 1) * HEAD_DIM, :]
            contrib.append(jnp.dot(vTj, p.astype(BF16), preferred_element_type=F32))
        return contrib, tuple(out)

    def value_body(jj, ls):
        c0, ls = value_block(2 * jj, ls)
        c1, ls = value_block(2 * jj + 1, ls)
        for g in groups:
            acc_sc[g] = acc_sc[g] + (c0[g] + c1[g])
        return ls

    acc_sc[...] = jnp.zeros_like(acc_sc)
    ls = lax.fori_loop(0, npair, value_body, tuple(jnp.zeros((1, width), F32) for _ in groups))
    for g in groups:
        outT = acc_sc[g] * (1.0 / ls[g])
        for r in range(ATT_GROUP):
            hh = ATT_GROUP * g + r
            o_ref[:, hh * HEAD_DIM:(hh + 1) * HEAD_DIM] = outT[:, r * PAGE:(r + 1) * PAGE].T.astype(BF16)


def _dsa_prompt(rel_bias, qT, qiT, wiT, kb, vT3, kib, *, batch, seq):
    nb = seq // PAGE
    rows = batch * seq
    kvw = ATT_KV_HEADS * HEAD_DIM
    qcol = lambda height: pl.BlockSpec((height, PAGE), lambda b, i: (0, b * nb + i))
    return pl.pallas_call(
        functools.partial(_dsa_prompt_kernel, topk=min(TOPK_MAX, seq // 4)),
        grid=(batch, nb),
        in_specs=[pl.BlockSpec(memory_space=pltpu.SMEM),
                  qcol(ATT_HEADS * HEAD_DIM), qcol(IDX_HEADS * IDX_DIM), qcol(8),
                  pl.BlockSpec((seq, kvw), lambda b, i: (b, 0)),
                  pl.BlockSpec((nb, kvw, PAGE), lambda b, i: (b, 0, 0)),
                  pl.BlockSpec((seq, IDX_DIM), lambda b, i: (b, 0))],
        out_specs=pl.BlockSpec((PAGE, ATT_HEADS * HEAD_DIM), lambda b, i: (b * nb + i, 0)),
        out_shape=jax.ShapeDtypeStruct((rows, ATT_HEADS * HEAD_DIM), BF16),
        scratch_shapes=[pltpu.VMEM((nb, PAGE, PAGE), I32), pltpu.VMEM((nb, PAGE, PAGE), F32),
                        pltpu.VMEM((ATT_HEADS, 3, PAGE, PAGE), F32),
                        pltpu.VMEM((ATT_KV_HEADS, nb, PAGE, ATT_GROUP * PAGE), F32),
                        pltpu.VMEM((ATT_KV_HEADS, HEAD_DIM, ATT_GROUP * PAGE), F32)],
        compiler_params=_params(2),
        name="dsa_prompt",
    )(rel_bias, qT, qiT, wiT, kb, vT3, kib)


TQ = 8


def _dsa_sample_kernel(pt_ref, relb_ref, q_ref, qi_ref, zl_ref, kn_ref, vn_ref, *rest, n_pages, n_new):
    ki_pages = rest[:n_pages]
    k_pages = rest[n_pages:2 * n_pages]
    v_pages = rest[2 * n_pages:3 * n_pages]
    o_ref, bias_sc = rest[3 * n_pages:]
    del pt_ref
    past = n_pages * PAGE
    n_blk = n_pages + 1
    rows_g = ATT_GROUP * TQ

    trow = lax.broadcasted_iota(I32, (TQ, PAGE), 0)
    scol = lax.broadcasted_iota(I32, (TQ, PAGE), 1)

    @pl.when(pl.program_id(0) == 0)
    def _init_bias():
        for g in range(ATT_KV_HEADS):
            for r in range(ATT_GROUP):
                hh = ATT_GROUP * g + r
                rs = slice(r * TQ, (r + 1) * TQ)
                bias_sc[g, 0, rs, :] = jnp.full((TQ, PAGE), relb_ref[NUM_BUCKETS - 1, hh], F32)
                bias_sc[g, 1, rs, :] = _bias_from_distance(relb_ref, hh, jnp.maximum(PAGE + trow - scol, 0))
                bias_sc[g, 2, rs, :] = _bias_from_distance(relb_ref, hh, jnp.maximum(trow - scol, 0))

    zl = zl_ref[...]
    pad_rows = lambda a: jnp.concatenate([a, jnp.zeros((PAGE - TQ, a.shape[1]), a.dtype)], axis=0)
    ki_new = pad_rows(zl[:, :IDX_DIM]).astype(BF16)
    qi = qi_ref[...]
    nt = (((1,), (1,)), ((), ()))

    keys = []
    for p in range(n_blk):
        if p == n_pages:
            idx = lax.dot_general(qi, ki_new, nt, preferred_element_type=F32)
        else:
            idx = jnp.dot(qi, ki_pages[p][...].astype(BF16), preferred_element_type=F32)
        sc = jnp.zeros((TQ, PAGE), F32)
        for hh in range(IDX_HEADS):
            sc = sc + zl[:, IDX_DIM + hh:IDX_DIM + hh + 1] * jnp.maximum(idx[hh * TQ:(hh + 1) * TQ], 0.0)
        key = _sortable_key(sc)
        if p == n_pages:
            key = jnp.where(jnp.logical_and(scol <= trow, scol < n_new), key, INT_MIN)
        keys.append(key)
    keys = jnp.concatenate(keys, axis=1)

    kprime = float(min(TOPK_MAX, (past + n_new) // 4))

    def count(mask):
        return jnp.sum(jnp.where(mask, 1.0, 0.0), axis=1, keepdims=True)

    cand0 = jnp.zeros((TQ, 1), I32)
    thr0 = jnp.where(count(keys >= cand0) >= kprime, cand0, INT_MIN)

    def bit_body(bi, thr):
        cand = thr | lax.shift_left(jnp.int32(1), 30 - bi)
        return jnp.where(count(keys >= cand) >= kprime, cand, thr)

    thr = lax.fori_loop(0, 31, bit_body, thr0)
    need = kprime - count(keys > thr)

    s_r = lax.broadcasted_iota(I32, (PAGE, PAGE), 0)
    s_c = lax.broadcasted_iota(I32, (PAGE, PAGE), 1)
    triu = jnp.where(s_r < s_c, 1.0, 0.0).astype(BF16)
    carry = jnp.zeros((TQ, 1), F32)
    mbs = []
    for p in range(n_blk):
        key = keys[:, p * PAGE:(p + 1) * PAGE]
        eqf = jnp.where(key == thr, 1.0, 0.0)
        rank = jnp.dot(eqf.astype(BF16), triu, preferred_element_type=F32) + carry
        tie = jnp.where(jnp.logical_and(key == thr, rank < need), 0.0, NEG)
        mbs.append(jnp.where(key > thr, 0.0, tie))
        carry = carry + jnp.sum(eqf, axis=1, keepdims=True)

    kn = pad_rows(kn_ref[...]).astype(BF16)
    vn = pad_rows(vn_ref[...]).astype(BF16)
    for g in range(ATT_KV_HEADS):
        gs = slice(g * HEAD_DIM, (g + 1) * HEAD_DIM)
        qg = q_ref[g * rows_g:(g + 1) * rows_g, :]
        logits = []
        for p in range(n_blk):
            kp = kn[:, gs] if p == n_pages else k_pages[p][pl.ds(g, PAGE, stride=ATT_KV_HEADS), :].astype(BF16)
            lg = lax.dot_general(qg, kp, nt, preferred_element_type=F32) * ATT_SCALE
            kind = 2 if p == n_pages else (1 if p == n_pages - 1 else 0)
            mb = jnp.concatenate([mbs[p]] * ATT_GROUP, axis=0)
            logits.append(lg + (mb + bias_sc[g, kind]))
        m = functools.reduce(jnp.maximum, [jnp.max(lg, axis=1, keepdims=True) for lg in logits])
        l = jnp.zeros((rows_g, 1), F32)
        acc = jnp.zeros((rows_g, HEAD_DIM), F32)
        for p in range(n_blk):
            pr = jnp.exp(logits[p] - m)
            l = l + jnp.sum(pr, axis=1, keepdims=True)
            vp = vn[:, gs] if p == n_pages else v_pages[p][pl.ds(g, PAGE, stride=ATT_KV_HEADS), :].astype(BF16)
            acc = acc + jnp.dot(pr.astype(BF16), vp, preferred_element_type=F32)
        o_ref[g * rows_g:(g + 1) * rows_g, :] = (acc * (1.0 / l)).astype(BF16)


def _dsa_sample(page_table, rel_bias, q8, qi8, zl8, kn8, vn8, cache_ki, cache_k, cache_v, *, n_new):
    nsamp, n_pages = page_table.shape
    kvw = ATT_KV_HEADS * HEAD_DIM
    per_sample = lambda *tail: pl.BlockSpec((None,) + tail, lambda b, pt: (b,) + (0,) * len(tail))

    def ki_page_spec(p):
        return pl.BlockSpec((None, IDX_DIM, PAGE), lambda b, pt: (pt[b * n_pages + p], 0, 0))

    def kv_page_spec(p):
        return pl.BlockSpec((PAGE * ATT_KV_HEADS, HEAD_DIM), lambda b, pt: (pt[b * n_pages + p], 0))

    in_specs = ([pl.BlockSpec(memory_space=pltpu.SMEM),
                 per_sample(ATT_HEADS * TQ, HEAD_DIM), per_sample(IDX_HEADS * TQ, IDX_DIM),
                 per_sample(TQ, LANES), per_sample(TQ, kvw), per_sample(TQ, kvw)]
                + [ki_page_spec(p) for p in range(n_pages)]
                + [kv_page_spec(p) for p in range(n_pages)]
                + [kv_page_spec(p) for p in range(n_pages)])
    grid_spec = pltpu.PrefetchScalarGridSpec(
        num_scalar_prefetch=1, grid=(nsamp,), in_specs=in_specs,
        out_specs=per_sample(ATT_HEADS * TQ, HEAD_DIM),
        scratch_shapes=[pltpu.VMEM((ATT_KV_HEADS, 3, ATT_GROUP * TQ, PAGE), F32)])
    return pl.pallas_call(
        functools.partial(_dsa_sample_kernel, n_pages=n_pages, n_new=n_new),
        grid_spec=grid_spec,
        out_shape=jax.ShapeDtypeStruct((nsamp, ATT_HEADS * TQ, HEAD_DIM), BF16),
        compiler_params=_params(1),
        name="dsa_sample",
    )(page_table.reshape(-1), rel_bias, q8, qi8, zl8, kn8, vn8,
      *([cache_ki] * n_pages), *([cache_k] * n_pages), *([cache_v] * n_pages))


def _post_mix_kernel(x_ref, o_ref, wo_ref, g_ref, w1_ref, w2_ref, y_ref):
    y = x_ref[...] + jnp.dot(o_ref[...], wo_ref[...], preferred_element_type=F32)
    h = (_rms_rows(y) * g_ref[...]).astype(BF16)
    acc = y
    for c in range(D_FF // D_MODEL):
        a = jnp.dot(h, w1_ref[:, c * D_MODEL:(c + 1) * D_MODEL], preferred_element_type=F32)
        a = jnp.square(jnp.maximum(a, 0.0)).astype(BF16)
        acc = acc + jnp.dot(a, w2_ref[c * D_MODEL:(c + 1) * D_MODEL, :], preferred_element_type=F32)
    y_ref[...] = acc


def _post_mix(x, o, wo, g, w1, w2):
    rows = x.shape[0]
    tm = ROW_TILE
    ko = o.shape[1]
    return pl.pallas_call(
        _post_mix_kernel,
        grid=(rows // tm,),
        in_specs=[pl.BlockSpec((tm, D_MODEL), lambda i: (i, 0)), pl.BlockSpec((tm, ko), lambda i: (i, 0)),
                  _const_spec((ko, D_MODEL)), _const_spec((1, D_MODEL)),
                  _const_spec((D_MODEL, D_FF)), _const_spec((D_FF, D_MODEL))],
        out_specs=pl.BlockSpec((tm, D_MODEL), lambda i: (i, 0)),
        out_shape=jax.ShapeDtypeStruct((rows, D_MODEL), F32),
        compiler_params=_params(1),
        name="post_mix",
    )(x, o, wo, g, w1, w2)


def _ret_proj_kernel(x_ref, g_ref, w_ref, cos_ref, sin_ref, q_ref, k_ref, v_ref, gate_ref):
    hb = (_rms_rows(x_ref[...]) * g_ref[...]).astype(BF16)
    cos = cos_ref[...]
    sin = sin_ref[...]
    half = RET_KEY // 2
    qk = RET_HEADS * RET_KEY
    vd = RET_HEADS * RET_VAL

    def rotated(col0, scale):
        z = jnp.dot(hb, w_ref[:, col0:col0 + RET_KEY], preferred_element_type=F32)
        x1, x2 = z[:, :half], z[:, half:]
        return (x1 * cos - x2 * sin) * scale, (x1 * sin + x2 * cos) * scale

    for hh in range(RET_HEADS):
        a, b = rotated(hh * RET_KEY, 1.0)
        q_ref[:, hh * RET_KEY:hh * RET_KEY + half] = a
        q_ref[:, hh * RET_KEY + half:(hh + 1) * RET_KEY] = b
        a, b = rotated(qk + hh * RET_KEY, RET_KSCALE)
        k_ref[:, hh * RET_KEY:hh * RET_KEY + half] = a
        k_ref[:, hh * RET_KEY + half:(hh + 1) * RET_KEY] = b
    for hh in range(RET_HEADS):
        cs = slice(hh * RET_VAL, (hh + 1) * RET_VAL)
        v_ref[:, cs] = jnp.dot(hb, w_ref[:, 2 * qk + hh * RET_VAL:2 * qk + (hh + 1) * RET_VAL],
                               preferred_element_type=F32).astype(BF16)
        gate_ref[:, cs] = jnp.dot(hb, w_ref[:, 2 * qk + vd + hh * RET_VAL:2 * qk + vd + (hh + 1) * RET_VAL],
                                  preferred_element_type=F32)


def _ret_proj(x, g, w, cos, sin):
    rows = x.shape[0]
    tm = ROW_TILE
    qk = RET_HEADS * RET_KEY
    vd = RET_HEADS * RET_VAL
    n_pos_tiles = cos.shape[0] // tm
    row_spec = lambda width: pl.BlockSpec((tm, width), lambda i: (i, 0))
    pos_spec = pl.BlockSpec((tm, RET_KEY // 2), lambda i: (i % n_pos_tiles, 0))
    return pl.pallas_call(
        _ret_proj_kernel,
        grid=(rows // tm,),
        in_specs=[row_spec(D_MODEL), _const_spec((1, D_MODEL)), _const_spec((D_MODEL, 2 * qk + 2 * vd)),
                  pos_spec, pos_spec],
        out_specs=[row_spec(qk), row_spec(qk), row_spec(vd), row_spec(vd)],
        out_shape=[jax.ShapeDtypeStruct((rows, qk), F32), jax.ShapeDtypeStruct((rows, qk), F32),
                   jax.ShapeDtypeStruct((rows, vd), BF16), jax.ShapeDtypeStruct((rows, vd), F32)],
        compiler_params=_params(1),
        name="ret_proj",
    )(x, g, w, cos, sin)


def _retention_kernel(cd_ref, q_ref, k_ref, v_ref, g_ref, dm_ref, qd_ref, kd_ref, *rest, has_state0):
    if has_state0:
        s0_ref, o_ref, s_ref = rest
    else:
        o_ref, s_ref = rest

    @pl.when(pl.program_id(1) == 0)
    def _init_state():
        s_ref[...] = s0_ref[...] if has_state0 else jnp.zeros_like(s_ref)

    nt = (((1,), (1,)), ((), ()))
    tn = (((0,), (0,)), ((), ()))
    for hh in range(RET_HEADS):
        qh = q_ref[:, hh * RET_KEY:(hh + 1) * RET_KEY]
        kh = k_ref[:, hh * RET_KEY:(hh + 1) * RET_KEY]
        vh = v_ref[:, hh * RET_VAL:(hh + 1) * RET_VAL]
        att = lax.dot_general(qh.astype(BF16), kh.astype(BF16), nt, preferred_element_type=F32) * dm_ref[hh]
        state = s_ref[hh]
        o = (jnp.dot(att.astype(BF16), vh, preferred_element_type=F32)
             + jnp.dot((qh * qd_ref[hh]).astype(BF16), state.astype(BF16), preferred_element_type=F32))
        kd = (kh * kd_ref[hh]).astype(BF16)
        s_ref[hh] = cd_ref[hh] * state + lax.dot_general(kd, vh, tn, preferred_element_type=F32)
        gate = g_ref[:, hh * RET_VAL:(hh + 1) * RET_VAL]
        on = _rms_rows(o) * (gate * jax.nn.sigmoid(gate))
        o_ref[:, hh * RET_VAL:(hh + 1) * RET_VAL] = on.astype(BF16)


def _retention(q, k, v, gate, consts, state0):
    dm, qd, kd, cd = consts
    nb, t, _ = q.shape
    c = dm.shape[1]
    qk = RET_HEADS * RET_KEY
    vd = RET_HEADS * RET_VAL
    tok = lambda width: pl.BlockSpec((None, c, width), lambda b, ci: (b, ci, 0))
    st_spec = pl.BlockSpec((None, RET_HEADS, RET_KEY, RET_VAL), lambda b, ci: (b, 0, 0, 0))
    in_specs = [pl.BlockSpec(memory_space=pltpu.SMEM), tok(qk), tok(qk), tok(vd), tok(vd),
                _const_spec(dm.shape), _const_spec(qd.shape), _const_spec(kd.shape)]
    args = [cd, q, k, v, gate, dm, qd, kd]
    if state0 is not None:
        in_specs.append(st_spec)
        args.append(state0)
    return pl.pallas_call(
        functools.partial(_retention_kernel, has_state0=state0 is not None),
        grid=(nb, t // c),
        in_specs=in_specs,
        out_specs=[tok(vd), st_spec],
        out_shape=[jax.ShapeDtypeStruct((nb, t, vd), BF16),
                   jax.ShapeDtypeStruct((nb, RET_HEADS, RET_KEY, RET_VAL), F32)],
        compiler_params=_params(2),
        name="retention_s" if state0 is not None else "retention_p",
    )(*args)


def _retention_consts(c, c_pad):
    log_g = jnp.log1p(-jnp.exp2(-5.0 - jnp.arange(RET_HEADS, dtype=F32)))
    i = jnp.arange(c, dtype=F32)
    diff = i[:, None] - i[None, :]
    dm = jnp.where(diff >= 0, jnp.exp(log_g[:, None, None] * jnp.maximum(diff, 0.0)), 0.0)
    qd = jnp.exp(log_g[:, None] * (i[None, :] + 1.0))
    kd = jnp.exp(log_g[:, None] * (c - 1.0 - i[None, :]))
    cd = jnp.exp(log_g * c)
    pad = c_pad - c
    dm = jnp.pad(dm, ((0, 0), (0, pad), (0, pad)))
    qd = jnp.broadcast_to(jnp.pad(qd, ((0, 0), (0, pad)))[:, :, None], (RET_HEADS, c_pad, RET_KEY))
    kd = jnp.broadcast_to(jnp.pad(kd, ((0, 0), (0, pad)))[:, :, None], (RET_HEADS, c_pad, RET_KEY))
    return dm, qd, kd, cd


def _rotation_tables(pos):
    half = RET_KEY // 2
    theta = 1.0 / (10000.0 ** jnp.linspace(0.0, 1.0, half, dtype=F32))
    ang = pos.astype(F32)[:, None] * theta[None, :]
    return jnp.cos(ang), jnp.sin(ang)


def kernel(x_prompt, x_sample, cache_k, cache_v, cache_kidx, state_ret, page_table, rel_bias, ln_mix, ln_mlp,
           att_w_in, att_q_gain, att_k_gain, att_w_out, ret_w_in, ret_w_out, mlp_w_in, mlp_w_out):
    bp, tp, _ = x_prompt.shape
    bs, ts, _ = x_sample.shape
    n_phys = cache_k.shape[1]
    assert ln_mix.shape[0] == 2 and att_w_in.shape[0] == 1 and ret_w_in.shape[0] == 1
    assert tp % RET_CHUNK == 0 and ts <= TQ and cache_k.shape[2] == PAGE
    kvw = ATT_KV_HEADS * HEAD_DIM
    past = page_table.shape[1] * PAGE

    w_att = jnp.pad(att_w_in[0], ((0, 0), (0, ATT_IN_PAD - ATT_IN))).astype(BF16)
    w_att_out = att_w_out[0].astype(BF16)
    w_ret = ret_w_in[0].astype(BF16)
    w_ret_out = ret_w_out[0].astype(BF16)
    w1 = mlp_w_in.astype(BF16)
    w2 = mlp_w_out.astype(BF16)
    row = lambda a: a.reshape(1, -1)

    xp = x_prompt.reshape(bp * tp, D_MODEL)
    xs = x_sample.reshape(bs * ts, D_MODEL)

    qT, qiT, wiT, vT3, kb, kib, kp, vp, kip = _attn_proj(
        xp, row(ln_mix[0]), w_att, row(att_q_gain[0]), row(att_k_gain[0]), transposed=True)
    op = _dsa_prompt(rel_bias, qT, qiT, wiT, kb, vT3, kib, batch=bp, seq=tp)
    yp = _post_mix(xp, op, w_att_out, row(ln_mlp[0]), w1[0], w2[0])

    qs, qis, zls, ks, vs, kis = _attn_proj(
        xs, row(ln_mix[0]), w_att, row(att_q_gain[0]), row(att_k_gain[0]), transposed=False)
    pad_t = lambda a: jnp.pad(a, ((0, 0), (0, TQ - ts)) + ((0, 0),) * (a.ndim - 2))
    q8 = pad_t(qs.reshape(bs, ts, ATT_HEADS, HEAD_DIM)).transpose(0, 2, 1, 3)
    q8 = q8.reshape(bs, ATT_HEADS * TQ, HEAD_DIM).astype(BF16)
    qi8 = pad_t(qis.reshape(bs, ts, IDX_HEADS, IDX_DIM)).transpose(0, 2, 1, 3)
    qi8 = qi8.reshape(bs, IDX_HEADS * TQ, IDX_DIM).astype(BF16)
    zl8 = pad_t(zls.reshape(bs, ts, LANES))
    kn8 = pad_t(ks.reshape(bs, ts, kvw))
    vn8 = pad_t(vs.reshape(bs, ts, kvw))
    os8 = _dsa_sample(page_table, rel_bias, q8, qi8, zl8, kn8, vn8,
                      jnp.swapaxes(cache_kidx[0], 1, 2),
                      cache_k[0].reshape(n_phys * PAGE * ATT_KV_HEADS, HEAD_DIM),
                      cache_v[0].reshape(n_phys * PAGE * ATT_KV_HEADS, HEAD_DIM), n_new=ts)
    os_ = os8.reshape(bs, ATT_HEADS, TQ, HEAD_DIM)[:, :, :ts].transpose(0, 2, 1, 3)
    os_ = os_.reshape(bs * ts, ATT_HEADS * HEAD_DIM)
    ys = _post_mix(xs, os_, w_att_out, row(ln_mlp[0]), w1[0], w2[0])

    cos_p, sin_p = _rotation_tables(jnp.arange(tp, dtype=jnp.int32))
    q, k, v, gate = _ret_proj(yp, row(ln_mix[1]), w_ret, cos_p, sin_p)
    r3 = lambda a: a.reshape(bp, tp, a.shape[-1])
    orp, state_p = _retention(r3(q), r3(k), r3(v), r3(gate), _retention_consts(RET_CHUNK, RET_CHUNK), None)
    yp = _post_mix(yp, orp.reshape(bp * tp, -1), w_ret_out, row(ln_mlp[1]), w1[1], w2[1])

    cos_s, sin_s = _rotation_tables(past + jnp.arange(ts, dtype=jnp.int32))
    tile_s = lambda a: jnp.tile(a, (bs, 1))
    q, k, v, gate = _ret_proj(ys, row(ln_mix[1]), w_ret, tile_s(cos_s), tile_s(sin_s))
    r3s = lambda a: pad_t(a.reshape(bs, ts, a.shape[-1]))
    ors, state_s = _retention(r3s(q), r3s(k), r3s(v), r3s(gate), _retention_consts(ts, TQ), state_ret[0])
    ys = _post_mix(ys, ors[:, :ts].reshape(bs * ts, -1), w_ret_out, row(ln_mlp[1]), w1[1], w2[1])

    return (yp.reshape(bp, tp, D_MODEL), ys.reshape(bs, ts, D_MODEL),
            kp.reshape(1, bp, tp, ATT_KV_HEADS, HEAD_DIM), vp.reshape(1, bp, tp, ATT_KV_HEADS, HEAD_DIM),
            kip.reshape(1, bp, tp, IDX_DIM), state_p[None],
            ks.reshape(1, bs, ts, ATT_KV_HEADS, HEAD_DIM), vs.reshape(1, bs, ts, ATT_KV_HEADS, HEAD_DIM),
            kis.reshape(1, bs, ts, IDX_DIM), state_s[None])
```

```python
import functools
import math

import jax
import jax.numpy as jnp
import numpy as np
from jax import lax
from jax.experimental import pallas as pl
from jax.experimental.pallas import tpu as pltpu

F32 = jnp.float32
BF16 = jnp.bfloat16
I32 = jnp.int32

D_MODEL = 1024
PAGE = 128
ATT_HEADS = 8
ATT_KV_HEADS = 2
ATT_GROUP = ATT_HEADS // ATT_KV_HEADS
HEAD_DIM = 128
IDX_HEADS = 4
IDX_DIM = 64
TOPK_MAX = 256
NUM_BUCKETS = 32
MAX_DISTANCE = 128
RET_HEADS = 4
RET_KEY = 256
RET_VAL = 512
RET_CHUNK = 128
D_FF = 4 * D_MODEL
EPS = 1e-6

Q_OFF = 0
K_OFF = ATT_HEADS * HEAD_DIM
V_OFF = K_OFF + ATT_KV_HEADS * HEAD_DIM
QI_OFF = V_OFF + ATT_KV_HEADS * HEAD_DIM
KI_OFF = QI_OFF + IDX_HEADS * IDX_DIM
WI_OFF = KI_OFF + IDX_DIM
ATT_IN = WI_OFF + IDX_HEADS
ATT_IN_PAD = 1920

LANES = 128
ROW_TILE = 512
VMEM_LIMIT = 56 * 1024 * 1024

ATT_SCALE = HEAD_DIM ** -0.5
RET_KSCALE = RET_KEY ** -0.5
INT_MIN = -(2 ** 31)
NEG = -1e30


def _bucket_bounds():
    n = np.arange(0, 8192)
    max_exact = NUM_BUCKETS // 2
    val = np.log(np.maximum(n, 1) / max_exact) / math.log(MAX_DISTANCE / max_exact) * (NUM_BUCKETS - max_exact)
    frac = np.abs(val - np.round(val))
    risky = (n > max_exact) & (n < MAX_DISTANCE) & (frac < 1e-4)
    assert not risky.any()
    large = np.minimum(max_exact + np.floor(val + 1e-9).astype(np.int64), NUM_BUCKETS - 1)
    bucket = np.where(n < max_exact, n, large)
    assert (np.diff(bucket) >= 0).all() and (bucket[MAX_DISTANCE:] == NUM_BUCKETS - 1).all()
    return [int(np.argmax(bucket >= b)) for b in range(NUM_BUCKETS)]


BUCKET_BOUNDS = _bucket_bounds()


def _const_spec(shape):
    zeros = (0,) * len(shape)
    return pl.BlockSpec(shape, lambda *_: zeros, pipeline_mode=pl.Buffered(1))


def _params(n_axes):
    return pltpu.CompilerParams(dimension_semantics=("arbitrary",) * n_axes,
                                vmem_limit_bytes=VMEM_LIMIT)


def _rms_rows(a):
    return a * lax.rsqrt(jnp.mean(a * a, axis=-1, keepdims=True) + EPS)


def _sortable_key(score):
    score = jnp.where(score == 0.0, 0.0, score)
    bits = pltpu.bitcast(score, I32)
    return bits ^ ((bits >> 31) & jnp.int32(0x7FFFFFFF))


def _bias_from_distance(relb_ref, head, dist):
    val = jnp.full(dist.shape, relb_ref[0, head], F32)
    for b in range(1, NUM_BUCKETS):
        val = jnp.where(dist >= BUCKET_BOUNDS[b], relb_ref[b, head], val)
    return val


def _attn_proj_kernel(x_ref, g_ref, w_ref, qg_ref, kg_ref, *outs, transposed):
    x = x_ref[...]
    h = _rms_rows(x) * g_ref[...]
    z = jnp.dot(h.astype(BF16), w_ref[...], preferred_element_type=F32)
    qg = qg_ref[...]
    kg = kg_ref[...]
    k = jnp.concatenate(
        [_rms_rows(z[:, K_OFF + g * HEAD_DIM:K_OFF + (g + 1) * HEAD_DIM]) * kg for g in range(ATT_KV_HEADS)],
        axis=1)
    v = z[:, V_OFF:QI_OFF]
    qi = z[:, QI_OFF:KI_OFF]
    zl = z[:, KI_OFF:ATT_IN_PAD]
    ki = zl[:, :IDX_DIM]
    if transposed:
        qT_ref, qiT_ref, wiT_ref, vT_ref, kb_ref, kib_ref, k_ref, v_ref, ki_ref = outs
        for hh in range(ATT_HEADS):
            qh = _rms_rows(z[:, hh * HEAD_DIM:(hh + 1) * HEAD_DIM]) * qg * ATT_SCALE
            qT_ref[hh * HEAD_DIM:(hh + 1) * HEAD_DIM, :] = qh.T.astype(BF16)
        for c in range(IDX_HEADS * IDX_DIM // LANES):
            qiT_ref[c * LANES:(c + 1) * LANES, :] = qi[:, c * LANES:(c + 1) * LANES].T.astype(BF16)
        wiT_ref[...] = zl.T[IDX_DIM:IDX_DIM + 8, :]
        for g in range(ATT_KV_HEADS):
            vT = v[:, g * HEAD_DIM:(g + 1) * HEAD_DIM].T
            for c in range(x.shape[0] // PAGE):
                vT_ref[c, g * HEAD_DIM:(g + 1) * HEAD_DIM, :] = vT[:, c * PAGE:(c + 1) * PAGE].astype(BF16)
        kb_ref[...] = k.astype(BF16)
        kib_ref[...] = ki.astype(BF16)
    else:
        q_ref, qi_ref, zl_ref, k_ref, v_ref, ki_ref = outs
        for hh in range(ATT_HEADS):
            q_ref[:, hh * HEAD_DIM:(hh + 1) * HEAD_DIM] = _rms_rows(z[:, hh * HEAD_DIM:(hh + 1) * HEAD_DIM]) * qg * ATT_SCALE
        qi_ref[...] = qi
        zl_ref[...] = zl
    k_ref[...] = k
    v_ref[...] = v
    ki_ref[...] = ki


def _attn_proj(x, g, w, qg, kg, *, transposed):
    rows = x.shape[0]
    tm = ROW_TILE
    nt = rows // tm
    kvw = ATT_KV_HEADS * HEAD_DIM
    row_spec = lambda width: pl.BlockSpec((tm, width), lambda i: (i, 0))
    col_spec = lambda height: pl.BlockSpec((height, tm), lambda i: (0, i))
    leaves_shape = [jax.ShapeDtypeStruct((rows, kvw), F32), jax.ShapeDtypeStruct((rows, kvw), F32),
                    jax.ShapeDtypeStruct((rows, IDX_DIM), F32)]
    leaves_spec = [row_spec(kvw), row_spec(kvw), row_spec(IDX_DIM)]
    if transposed:
        out_shape = [jax.ShapeDtypeStruct((ATT_HEADS * HEAD_DIM, rows), BF16),
                     jax.ShapeDtypeStruct((IDX_HEADS * IDX_DIM, rows), BF16),
                     jax.ShapeDtypeStruct((8, rows), F32),
                     jax.ShapeDtypeStruct((rows // PAGE, kvw, PAGE), BF16),
                     jax.ShapeDtypeStruct((rows, kvw), BF16),
                     jax.ShapeDtypeStruct((rows, IDX_DIM), BF16)] + leaves_shape
        out_specs = [col_spec(ATT_HEADS * HEAD_DIM), col_spec(IDX_HEADS * IDX_DIM), col_spec(8),
                     pl.BlockSpec((tm // PAGE, kvw, PAGE), lambda i: (i, 0, 0)),
                     row_spec(kvw), row_spec(IDX_DIM)] + leaves_spec
    else:
        out_shape = [jax.ShapeDtypeStruct((rows, ATT_HEADS * HEAD_DIM), F32),
                     jax.ShapeDtypeStruct((rows, IDX_HEADS * IDX_DIM), F32),
                     jax.ShapeDtypeStruct((rows, LANES), F32)] + leaves_shape
        out_specs = [row_spec(ATT_HEADS * HEAD_DIM), row_spec(IDX_HEADS * IDX_DIM), row_spec(LANES)] + leaves_spec
    return pl.pallas_call(
        functools.partial(_attn_proj_kernel, transposed=transposed),
        grid=(nt,),
        in_specs=[row_spec(D_MODEL), _const_spec((1, D_MODEL)), _const_spec((D_MODEL, ATT_IN_PAD)),
                  _const_spec((1, HEAD_DIM)), _const_spec((1, HEAD_DIM))],
        out_specs=out_specs,
        out_shape=out_shape,
        compiler_params=_params(1),
        name="attn_proj_t" if transposed else "attn_proj_r",
    )(x, g, w, qg, kg)


def _dsa_prompt_kernel(relb_ref, qT_ref, qiT_ref, wiT_ref, k_ref, vT_ref, ki_ref, o_ref,
                       keys_sc, mb_sc, bias_sc, logit_sc, acc_sc, *, topk):
    b = pl.program_id(0)
    i = pl.program_id(1)
    nkb = i + 1
    srow = lax.broadcasted_iota(I32, (PAGE, PAGE), 0)
    tcol = lax.broadcasted_iota(I32, (PAGE, PAGE), 1)

    @pl.when((b == 0) & (i == 0))
    def _init_bias():
        for hh in range(ATT_HEADS):
            for rel in range(2):
                dist = jnp.maximum(rel * PAGE + tcol - srow, 0)
                bias_sc[hh, rel] = _bias_from_distance(relb_ref, hh, dist)
            bias_sc[hh, 2] = jnp.full((PAGE, PAGE), relb_ref[NUM_BUCKETS - 1, hh], F32)

    wiT = wiT_ref[...]

    npair = (i + 2) // 2

    def score_block(j):
        off = pl.multiple_of(j * PAGE, PAGE)
        kij = ki_ref[pl.ds(off, PAGE), :]
        sc = jnp.zeros((PAGE, PAGE), F32)
        for hh in range(IDX_HEADS):
            idx = jnp.dot(kij, qiT_ref[hh * IDX_DIM:(hh + 1) * IDX_DIM, :], preferred_element_type=F32)
            sc = sc + wiT[hh:hh + 1, :] * jnp.maximum(idx, 0.0)
        causal = jnp.logical_or(j < i, jnp.logical_and(j == i, srow <= tcol))
        keys_sc[j] = jnp.where(causal, _sortable_key(sc), INT_MIN)

    nquad = (i + 4) // 4

    def score_body(jj, carry):
        for u in range(4):
            score_block(4 * jj + u)
        return carry

    lax.fori_loop(0, nquad, score_body, 0)

    lane = lax.broadcasted_iota(I32, (1, PAGE), 1)
    kprime = jnp.minimum(topk, i * PAGE + lane + 1).astype(F32)

    def count(pred_of_keys):
        def body(j, cnt):
            return cnt + jnp.where(pred_of_keys(keys_sc[j]), 1.0, 0.0)
        cnt = lax.fori_loop(0, nkb, body, jnp.zeros((PAGE, PAGE), F32))
        return jnp.sum(cnt, axis=0, keepdims=True)

    cand0 = jnp.zeros((1, PAGE), I32)
    thr0 = jnp.where(count(lambda key: key >= cand0) >= kprime, cand0, INT_MIN)

    def bit_body(bi, thr):
        cand = thr | lax.shift_left(jnp.int32(1), 30 - bi)
        return jnp.where(count(lambda key: key >= cand) >= kprime, cand, thr)

    thr = lax.fori_loop(0, 31, bit_body, thr0)
    need = kprime - count(lambda key: key > thr)

    tri = jnp.where(srow > tcol, 1.0, 0.0).astype(BF16)

    def mask_block(j, carry):
        key = keys_sc[j]
        eqf = jnp.where(key == thr, 1.0, 0.0)
        rank = jnp.dot(tri, eqf.astype(BF16), preferred_element_type=F32) + carry
        tie = jnp.where(jnp.logical_and(key == thr, rank < need), 0.0, NEG)
        mb_sc[j] = jnp.where(key > thr, 0.0, tie)
        return carry + jnp.sum(eqf, axis=0, keepdims=True)

    def mask_body(jj, carry):
        for u in range(4):
            carry = mask_block(4 * jj + u, carry)
        return carry

    lax.fori_loop(0, nquad, mask_body, jnp.zeros((1, PAGE), F32))

    groups = range(ATT_KV_HEADS)
    width = ATT_GROUP * PAGE
    qcats = [jnp.concatenate(
        [qT_ref[(ATT_GROUP * g + r) * HEAD_DIM:(ATT_GROUP * g + r + 1) * HEAD_DIM, :] for r in range(ATT_GROUP)],
        axis=1) for g in groups]

    def logit_block(j, ms):
        off = pl.multiple_of(j * PAGE, PAGE)
        mb = mb_sc[j]
        relc = jnp.clip(i - j, 0, 2)
        out = []
        for g in groups:
            kj = k_ref[pl.ds(off, PAGE), g * HEAD_DIM:(g + 1) * HEAD_DIM]
            logits = jnp.dot(kj, qcats[g], preferred_element_type=F32)
            logits = jnp.concatenate(
                [logits[:, r * PAGE:(r + 1) * PAGE] + (mb + bias_sc[ATT_GROUP * g + r, relc])
                 for r in range(ATT_GROUP)], axis=1)
            logit_sc[g, j] = logits
            out.append(jnp.maximum(ms[g], jnp.max(logits, axis=0, keepdims=True)))
        return tuple(out)

    ms = lax.fori_loop(0, npair, lambda jj, c: logit_block(2 * jj + 1, logit_block(2 * jj, c)),
                       tuple(jnp.full((1, width), NEG, F32) for _ in groups))

    def value_block(j, ls):
        contrib, out = [], []
        for g in groups:
            p = jnp.exp(logit_sc[g, j] - ms[g])
            out.append(ls[g] + jnp.sum(p, axis=0, keepdims=True))
            vTj = vT_ref[j, g * HEAD_DIM:(g + 1) * HEAD_DIM, :]
            contrib.append(jnp.dot(vTj, p.astype(BF16), preferred_element_type=F32))
        return contrib, tuple(out)

    def value_body(jj, ls):
        c0, ls = value_block(2 * jj, ls)
        c1, ls = value_block(2 * jj + 1, ls)
        for g in groups:
            acc_sc[g] = acc_sc[g] + (c0[g] + c1[g])
        return ls

    acc_sc[...] = jnp.zeros_like(acc_sc)
    ls = lax.fori_loop(0, npair, value_body, tuple(jnp.zeros((1, width), F32) for _ in groups))
    for g in groups:
        outT = acc_sc[g] * (1.0 / ls[g])
        for r in range(ATT_GROUP):
            hh = ATT_GROUP * g + r
            o_ref[:, hh * HEAD_DIM:(hh + 1) * HEAD_DIM] = outT[:, r * PAGE:(r + 1) * PAGE].T.astype(BF16)


def _dsa_prompt(rel_bias, qT, qiT, wiT, kb, vT3, kib, *, batch, seq):
    nb = seq // PAGE
    rows = batch * seq
    kvw = ATT_KV_HEADS * HEAD_DIM
    qcol = lambda height: pl.BlockSpec((height, PAGE), lambda b, i: (0, b * nb + i))
    return pl.pallas_call(
        functools.partial(_dsa_prompt_kernel, topk=min(TOPK_MAX, seq // 4)),
        grid=(batch, nb),
        in_specs=[pl.BlockSpec(memory_space=pltpu.SMEM),
                  qcol(ATT_HEADS * HEAD_DIM), qcol(IDX_HEADS * IDX_DIM), qcol(8),
                  pl.BlockSpec((seq, kvw), lambda b, i: (b, 0)),
                  pl.BlockSpec((nb, kvw, PAGE), lambda b, i: (b, 0, 0)),
                  pl.BlockSpec((seq, IDX_DIM), lambda b, i: (b, 0))],
        out_specs=pl.BlockSpec((PAGE, ATT_HEADS * HEAD_DIM), lambda b, i: (b * nb + i, 0)),
        out_shape=jax.ShapeDtypeStruct((rows, ATT_HEADS * HEAD_DIM), BF16),
        scratch_shapes=[pltpu.VMEM((nb, PAGE, PAGE), I32), pltpu.VMEM((nb, PAGE, PAGE), F32),
                        pltpu.VMEM((ATT_HEADS, 3, PAGE, PAGE), F32),
                        pltpu.VMEM((ATT_KV_HEADS, nb, PAGE, ATT_GROUP * PAGE), F32),
                        pltpu.VMEM((ATT_KV_HEADS, HEAD_DIM, ATT_GROUP * PAGE), F32)],
        compiler_params=_params(2),
        name="dsa_prompt",
    )(rel_bias, qT, qiT, wiT, kb, vT3, kib)


TQ = 8
SEL_SAMPLES = 8


def _dsa_select_kernel(pt_ref, qi_ref, zl_ref, *rest, n_pages, n_new):
    ns = SEL_SAMPLES
    ki_pages = rest[:ns * n_pages]
    mb_ref, keys_sc = rest[ns * n_pages:]
    del pt_ref
    n_blk = n_pages + 1
    rows = ns * TQ
    trow = lax.broadcasted_iota(I32, (TQ, PAGE), 0)
    scol = lax.broadcasted_iota(I32, (TQ, PAGE), 1)
    pad_rows = lambda a: jnp.concatenate([a, jnp.zeros((PAGE - TQ, a.shape[1]), a.dtype)], axis=0)
    nt = (((1,), (1,)), ((), ()))

    for s in range(ns):
        zl = zl_ref[s]
        qi = qi_ref[s]
        for p in range(n_blk):
            if p == n_pages:
                ki_new = pad_rows(zl[:, :IDX_DIM]).astype(BF16)
                idx = lax.dot_general(qi, ki_new, nt, preferred_element_type=F32)
            else:
                idx = jnp.dot(qi, ki_pages[s * n_pages + p][...].astype(BF16), preferred_element_type=F32)
            sc = jnp.zeros((TQ, PAGE), F32)
            for hh in range(IDX_HEADS):
                sc = sc + zl[:, IDX_DIM + hh:IDX_DIM + hh + 1] * jnp.maximum(idx[hh * TQ:(hh + 1) * TQ], 0.0)
            key = _sortable_key(sc)
            if p == n_pages:
                key = jnp.where(jnp.logical_and(scol <= trow, scol < n_new), key, INT_MIN)
            keys_sc[s * TQ:(s + 1) * TQ, p * PAGE:(p + 1) * PAGE] = key

    kprime = float(min(TOPK_MAX, (n_pages * PAGE + n_new) // 4))

    def count(pred_of_keys):
        return jnp.sum(jnp.where(pred_of_keys(keys_sc[...]), 1.0, 0.0), axis=1, keepdims=True)

    cand0 = jnp.zeros((rows, 1), I32)
    thr0 = jnp.where(count(lambda key: key >= cand0) >= kprime, cand0, INT_MIN)

    def bit_body(bi, thr):
        cand = thr | lax.shift_left(jnp.int32(1), 30 - bi)
        return jnp.where(count(lambda key: key >= cand) >= kprime, cand, thr)

    thr = lax.fori_loop(0, 31, bit_body, thr0)
    need = kprime - count(lambda key: key > thr)

    s_r = lax.broadcasted_iota(I32, (PAGE, PAGE), 0)
    s_c = lax.broadcasted_iota(I32, (PAGE, PAGE), 1)
    triu = jnp.where(s_r < s_c, 1.0, 0.0).astype(BF16)
    carry = jnp.zeros((rows, 1), F32)
    for p in range(n_blk):
        key = keys_sc[:, p * PAGE:(p + 1) * PAGE]
        eqf = jnp.where(key == thr, 1.0, 0.0)
        rank = jnp.dot(eqf.astype(BF16), triu, preferred_element_type=F32) + carry
        tie = jnp.where(jnp.logical_and(key == thr, rank < need), 0.0, NEG)
        mb_ref[:, p * PAGE:(p + 1) * PAGE] = jnp.where(key > thr, 0.0, tie)
        carry = carry + jnp.sum(eqf, axis=1, keepdims=True)


def _dsa_select(page_table, qi8, zl8, cache_ki, *, n_new):
    nsamp, n_pages = page_table.shape
    ns = SEL_SAMPLES
    width = (n_pages + 1) * PAGE

    def ki_page_spec(s, p):
        return pl.BlockSpec((None, IDX_DIM, PAGE), lambda b, pt: (pt[(b * ns + s) * n_pages + p], 0, 0))

    in_specs = ([pl.BlockSpec((ns, IDX_HEADS * TQ, IDX_DIM), lambda b, pt: (b, 0, 0)),
                 pl.BlockSpec((ns, TQ, LANES), lambda b, pt: (b, 0, 0))]
                + [ki_page_spec(s, p) for s in range(ns) for p in range(n_pages)])
    grid_spec = pltpu.PrefetchScalarGridSpec(
        num_scalar_prefetch=1, grid=(nsamp // ns,), in_specs=in_specs,
        out_specs=pl.BlockSpec((ns * TQ, width), lambda b, pt: (b, 0)),
        scratch_shapes=[pltpu.VMEM((ns * TQ, width), I32)])
    return pl.pallas_call(
        functools.partial(_dsa_select_kernel, n_pages=n_pages, n_new=n_new),
        grid_spec=grid_spec,
        out_shape=jax.ShapeDtypeStruct((nsamp * TQ, width), F32),
        compiler_params=_params(1),
        name="dsa_select",
    )(page_table.reshape(-1), qi8, zl8, *([cache_ki] * (ns * n_pages)))


def _dsa_sample_kernel(pt_ref, relb_ref, q_ref, mb_ref, kn_ref, vn_ref, *rest, n_pages):
    k_pages = rest[:n_pages]
    v_pages = rest[n_pages:2 * n_pages]
    o_ref, bias_sc, kall_sc, vall_sc = rest[2 * n_pages:]
    del pt_ref
    past = n_pages * PAGE
    rows_g = ATT_GROUP * TQ

    trow = lax.broadcasted_iota(I32, (TQ, PAGE), 0)
    scol = lax.broadcasted_iota(I32, (TQ, PAGE), 1)

    @pl.when(pl.program_id(0) == 0)
    def _init_bias():
        for g in range(ATT_KV_HEADS):
            for r in range(ATT_GROUP):
                hh = ATT_GROUP * g + r
                rs = slice(r * TQ, (r + 1) * TQ)
                far = jnp.full((TQ, PAGE), relb_ref[NUM_BUCKETS - 1, hh], F32)
                for p in range(n_pages - 1):
                    bias_sc[g, rs, p * PAGE:(p + 1) * PAGE] = far
                bias_sc[g, rs, past - PAGE:past] = _bias_from_distance(
                    relb_ref, hh, jnp.maximum(PAGE + trow - scol, 0))
                bias_sc[g, rs, past:past + PAGE] = _bias_from_distance(relb_ref, hh, jnp.maximum(trow - scol, 0))

    nt = (((1,), (1,)), ((), ()))
    mb = jnp.concatenate([mb_ref[...]] * ATT_GROUP, axis=0)
    pad_rows = lambda a: jnp.concatenate([a, jnp.zeros((PAGE - TQ, a.shape[1]), a.dtype)], axis=0)
    for g in range(ATT_KV_HEADS):
        gs = slice(g * HEAD_DIM, (g + 1) * HEAD_DIM)
        for p in range(n_pages):
            ps = slice(p * PAGE, (p + 1) * PAGE)
            kall_sc[g, ps, :] = k_pages[p][pl.ds(g, PAGE, stride=ATT_KV_HEADS), :].astype(BF16)
            vall_sc[g, ps, :] = v_pages[p][pl.ds(g, PAGE, stride=ATT_KV_HEADS), :].astype(BF16)
        kall_sc[g, past:past + PAGE, :] = pad_rows(kn_ref[:, gs]).astype(BF16)
        vall_sc[g, past:past + PAGE, :] = pad_rows(vn_ref[:, gs]).astype(BF16)
    for g in range(ATT_KV_HEADS):
        qg = q_ref[g * rows_g:(g + 1) * rows_g, :]
        logits = lax.dot_general(qg, kall_sc[g], nt, preferred_element_type=F32)
        logits = logits + (mb + bias_sc[g])
        m = jnp.max(logits, axis=1, keepdims=True)
        pr = jnp.exp(logits - m)
        l = jnp.sum(pr, axis=1, keepdims=True)
        acc = jnp.dot(pr.astype(BF16), vall_sc[g], preferred_element_type=F32)
        o_ref[g * rows_g:(g + 1) * rows_g, :] = (acc * (1.0 / l)).astype(BF16)


def _dsa_sample(page_table, rel_bias, q8, mb, kn8, vn8, cache_k, cache_v):
    nsamp, n_pages = page_table.shape
    kvw = ATT_KV_HEADS * HEAD_DIM
    per_sample = lambda *tail: pl.BlockSpec((None,) + tail, lambda b, pt: (b,) + (0,) * len(tail))

    def kv_page_spec(p):
        return pl.BlockSpec((PAGE * ATT_KV_HEADS, HEAD_DIM), lambda b, pt: (pt[b * n_pages + p], 0))

    in_specs = ([pl.BlockSpec(memory_space=pltpu.SMEM),
                 per_sample(ATT_HEADS * TQ, HEAD_DIM),
                 pl.BlockSpec((TQ, mb.shape[1]), lambda b, pt: (b, 0)),
                 per_sample(TQ, kvw), per_sample(TQ, kvw)]
                + [kv_page_spec(p) for p in range(n_pages)]
                + [kv_page_spec(p) for p in range(n_pages)])
    grid_spec = pltpu.PrefetchScalarGridSpec(
        num_scalar_prefetch=1, grid=(nsamp,), in_specs=in_specs,
        out_specs=per_sample(ATT_HEADS * TQ, HEAD_DIM),
        scratch_shapes=[pltpu.VMEM((ATT_KV_HEADS, ATT_GROUP * TQ, mb.shape[1]), F32),
                        pltpu.VMEM((ATT_KV_HEADS, mb.shape[1], HEAD_DIM), BF16),
                        pltpu.VMEM((ATT_KV_HEADS, mb.shape[1], HEAD_DIM), BF16)])
    return pl.pallas_call(
        functools.partial(_dsa_sample_kernel, n_pages=n_pages),
        grid_spec=grid_spec,
        out_shape=jax.ShapeDtypeStruct((nsamp, ATT_HEADS * TQ, HEAD_DIM), BF16),
        compiler_params=_params(1),
        name="dsa_sample",
    )(page_table.reshape(-1), rel_bias, q8, mb, kn8, vn8, *([cache_k] * n_pages), *([cache_v] * n_pages))


def _post_mix_kernel(x_ref, o_ref, wo_ref, g_ref, w1_ref, w2_ref, y_ref):
    y = x_ref[...] + jnp.dot(o_ref[...], wo_ref[...], preferred_element_type=F32)
    h = (_rms_rows(y) * g_ref[...]).astype(BF16)
    acc = y
    for c in range(D_FF // D_MODEL):
        a = jnp.dot(h, w1_ref[:, c * D_MODEL:(c + 1) * D_MODEL], preferred_element_type=F32)
        a = jnp.square(jnp.maximum(a, 0.0)).astype(BF16)
        acc = acc + jnp.dot(a, w2_ref[c * D_MODEL:(c + 1) * D_MODEL, :], preferred_element_type=F32)
    y_ref[...] = acc


def _post_mix(x, o, wo, g, w1, w2):
    rows = x.shape[0]
    tm = ROW_TILE
    ko = o.shape[1]
    return pl.pallas_call(
        _post_mix_kernel,
        grid=(rows // tm,),
        in_specs=[pl.BlockSpec((tm, D_MODEL), lambda i: (i, 0)), pl.BlockSpec((tm, ko), lambda i: (i, 0)),
                  _const_spec((ko, D_MODEL)), _const_spec((1, D_MODEL)),
                  _const_spec((D_MODEL, D_FF)), _const_spec((D_FF, D_MODEL))],
        out_specs=pl.BlockSpec((tm, D_MODEL), lambda i: (i, 0)),
        out_shape=jax.ShapeDtypeStruct((rows, D_MODEL), F32),
        compiler_params=_params(1),
        name="post_mix",
    )(x, o, wo, g, w1, w2)


def _ret_proj_kernel(x_ref, g_ref, w_ref, cos_ref, sin_ref, q_ref, k_ref, v_ref, gate_ref):
    hb = (_rms_rows(x_ref[...]) * g_ref[...]).astype(BF16)
    cos = cos_ref[...]
    sin = sin_ref[...]
    half = RET_KEY // 2
    qk = RET_HEADS * RET_KEY
    vd = RET_HEADS * RET_VAL

    def rotated(col0, scale):
        z = jnp.dot(hb, w_ref[:, col0:col0 + RET_KEY], preferred_element_type=F32)
        x1, x2 = z[:, :half], z[:, half:]
        return (x1 * cos - x2 * sin) * scale, (x1 * sin + x2 * cos) * scale

    for hh in range(RET_HEADS):
        a, b = rotated(hh * RET_KEY, 1.0)
        q_ref[:, hh * RET_KEY:hh * RET_KEY + half] = a
        q_ref[:, hh * RET_KEY + half:(hh + 1) * RET_KEY] = b
        a, b = rotated(qk + hh * RET_KEY, RET_KSCALE)
        k_ref[:, hh * RET_KEY:hh * RET_KEY + half] = a
        k_ref[:, hh * RET_KEY + half:(hh + 1) * RET_KEY] = b
    for hh in range(RET_HEADS):
        cs = slice(hh * RET_VAL, (hh + 1) * RET_VAL)
        v_ref[:, cs] = jnp.dot(hb, w_ref[:, 2 * qk + hh * RET_VAL:2 * qk + (hh + 1) * RET_VAL],
                               preferred_element_type=F32).astype(BF16)
        gate_ref[:, cs] = jnp.dot(hb, w_ref[:, 2 * qk + vd + hh * RET_VAL:2 * qk + vd + (hh + 1) * RET_VAL],
                                  preferred_element_type=F32)


def _ret_proj(x, g, w, cos, sin):
    rows = x.shape[0]
    tm = ROW_TILE
    qk = RET_HEADS * RET_KEY
    vd = RET_HEADS * RET_VAL
    n_pos_tiles = cos.shape[0] // tm
    row_spec = lambda width: pl.BlockSpec((tm, width), lambda i: (i, 0))
    pos_spec = pl.BlockSpec((tm, RET_KEY // 2), lambda i: (i % n_pos_tiles, 0))
    return pl.pallas_call(
        _ret_proj_kernel,
        grid=(rows // tm,),
        in_specs=[row_spec(D_MODEL), _const_spec((1, D_MODEL)), _const_spec((D_MODEL, 2 * qk + 2 * vd)),
                  pos_spec, pos_spec],
        out_specs=[row_spec(qk), row_spec(qk), row_spec(vd), row_spec(vd)],
        out_shape=[jax.ShapeDtypeStruct((rows, qk), F32), jax.ShapeDtypeStruct((rows, qk), F32),
                   jax.ShapeDtypeStruct((rows, vd), BF16), jax.ShapeDtypeStruct((rows, vd), F32)],
        compiler_params=_params(1),
        name="ret_proj",
    )(x, g, w, cos, sin)


def _retention_kernel(cd_ref, q_ref, k_ref, v_ref, g_ref, dm_ref, qd_ref, kd_ref, *rest, has_state0):
    if has_state0:
        s0_ref, o_ref, s_ref = rest
    else:
        o_ref, s_ref = rest

    @pl.when(pl.program_id(1) == 0)
    def _init_state():
        s_ref[...] = s0_ref[...] if has_state0 else jnp.zeros_like(s_ref)

    nt = (((1,), (1,)), ((), ()))
    tn = (((0,), (0,)), ((), ()))
    for hh in range(RET_HEADS):
        qh = q_ref[:, hh * RET_KEY:(hh + 1) * RET_KEY]
        kh = k_ref[:, hh * RET_KEY:(hh + 1) * RET_KEY]
        vh = v_ref[:, hh * RET_VAL:(hh + 1) * RET_VAL]
        att = lax.dot_general(qh.astype(BF16), kh.astype(BF16), nt, preferred_element_type=F32) * dm_ref[hh]
        state = s_ref[hh]
        o = (jnp.dot(att.astype(BF16), vh, preferred_element_type=F32)
             + jnp.dot((qh * qd_ref[hh]).astype(BF16), state.astype(BF16), preferred_element_type=F32))
        kd = (kh * kd_ref[hh]).astype(BF16)
        s_ref[hh] = cd_ref[hh] * state + lax.dot_general(kd, vh, tn, preferred_element_type=F32)
        gate = g_ref[:, hh * RET_VAL:(hh + 1) * RET_VAL]
        on = _rms_rows(o) * (gate * jax.nn.sigmoid(gate))
        o_ref[:, hh * RET_VAL:(hh + 1) * RET_VAL] = on.astype(BF16)


def _retention(q, k, v, gate, consts, state0):
    dm, qd, kd, cd = consts
    nb, t, _ = q.shape
    c = dm.shape[1]
    qk = RET_HEADS * RET_KEY
    vd = RET_HEADS * RET_VAL
    tok = lambda width: pl.BlockSpec((None, c, width), lambda b, ci: (b, ci, 0))
    st_spec = pl.BlockSpec((None, RET_HEADS, RET_KEY, RET_VAL), lambda b, ci: (b, 0, 0, 0))
    in_specs = [pl.BlockSpec(memory_space=pltpu.SMEM), tok(qk), tok(qk), tok(vd), tok(vd),
                _const_spec(dm.shape), _const_spec(qd.shape), _const_spec(kd.shape)]
    args = [cd, q, k, v, gate, dm, qd, kd]
    if state0 is not None:
        in_specs.append(st_spec)
        args.append(state0)
    return pl.pallas_call(
        functools.partial(_retention_kernel, has_state0=state0 is not None),
        grid=(nb, t // c),
        in_specs=in_specs,
        out_specs=[tok(vd), st_spec],
        out_shape=[jax.ShapeDtypeStruct((nb, t, vd), BF16),
                   jax.ShapeDtypeStruct((nb, RET_HEADS, RET_KEY, RET_VAL), F32)],
        compiler_params=_params(2),
        name="retention_s" if state0 is not None else "retention_p",
    )(*args)


def _retention_consts(c, c_pad):
    log_g = jnp.log1p(-jnp.exp2(-5.0 - jnp.arange(RET_HEADS, dtype=F32)))
    i = jnp.arange(c, dtype=F32)
    diff = i[:, None] - i[None, :]
    dm = jnp.where(diff >= 0, jnp.exp(log_g[:, None, None] * jnp.maximum(diff, 0.0)), 0.0)
    qd = jnp.exp(log_g[:, None] * (i[None, :] + 1.0))
    kd = jnp.exp(log_g[:, None] * (c - 1.0 - i[None, :]))
    cd = jnp.exp(log_g * c)
    pad = c_pad - c
    dm = jnp.pad(dm, ((0, 0), (0, pad), (0, pad)))
    qd = jnp.broadcast_to(jnp.pad(qd, ((0, 0), (0, pad)))[:, :, None], (RET_HEADS, c_pad, RET_KEY))
    kd = jnp.broadcast_to(jnp.pad(kd, ((0, 0), (0, pad)))[:, :, None], (RET_HEADS, c_pad, RET_KEY))
    return dm, qd, kd, cd


def _rotation_tables(pos):
    half = RET_KEY // 2
    theta = 1.0 / (10000.0 ** jnp.linspace(0.0, 1.0, half, dtype=F32))
    ang = pos.astype(F32)[:, None] * theta[None, :]
    return jnp.cos(ang), jnp.sin(ang)


def kernel(x_prompt, x_sample, cache_k, cache_v, cache_kidx, state_ret, page_table, rel_bias, ln_mix, ln_mlp,
           att_w_in, att_q_gain, att_k_gain, att_w_out, ret_w_in, ret_w_out, mlp_w_in, mlp_w_out):
    bp, tp, _ = x_prompt.shape
    bs, ts, _ = x_sample.shape
    n_phys = cache_k.shape[1]
    assert ln_mix.shape[0] == 2 and att_w_in.shape[0] == 1 and ret_w_in.shape[0] == 1
    assert tp % RET_CHUNK == 0 and ts <= TQ and cache_k.shape[2] == PAGE
    assert (tp // PAGE) % 4 == 0 and bs % SEL_SAMPLES == 0
    kvw = ATT_KV_HEADS * HEAD_DIM
    past = page_table.shape[1] * PAGE

    w_att = jnp.pad(att_w_in[0], ((0, 0), (0, ATT_IN_PAD - ATT_IN))).astype(BF16)
    w_att_out = att_w_out[0].astype(BF16)
    w_ret = ret_w_in[0].astype(BF16)
    w_ret_out = ret_w_out[0].astype(BF16)
    w1 = mlp_w_in.astype(BF16)
    w2 = mlp_w_out.astype(BF16)
    row = lambda a: a.reshape(1, -1)

    xp = x_prompt.reshape(bp * tp, D_MODEL)
    xs = x_sample.reshape(bs * ts, D_MODEL)

    qT, qiT, wiT, vT3, kb, kib, kp, vp, kip = _attn_proj(
        xp, row(ln_mix[0]), w_att, row(att_q_gain[0]), row(att_k_gain[0]), transposed=True)
    op = _dsa_prompt(rel_bias, qT, qiT, wiT, kb, vT3, kib, batch=bp, seq=tp)
    yp = _post_mix(xp, op, w_att_out, row(ln_mlp[0]), w1[0], w2[0])

    qs, qis, zls, ks, vs, kis = _attn_proj(
        xs, row(ln_mix[0]), w_att, row(att_q_gain[0]), row(att_k_gain[0]), transposed=False)
    pad_t = lambda a: jnp.pad(a, ((0, 0), (0, TQ - ts)) + ((0, 0),) * (a.ndim - 2))
    q8 = pad_t(qs.reshape(bs, ts, ATT_HEADS, HEAD_DIM)).transpose(0, 2, 1, 3)
    q8 = q8.reshape(bs, ATT_HEADS * TQ, HEAD_DIM).astype(BF16)
    qi8 = pad_t(qis.reshape(bs, ts, IDX_HEADS, IDX_DIM)).transpose(0, 2, 1, 3)
    qi8 = qi8.reshape(bs, IDX_HEADS * TQ, IDX_DIM).astype(BF16)
    zl8 = pad_t(zls.reshape(bs, ts, LANES))
    kn8 = pad_t(ks.reshape(bs, ts, kvw))
    vn8 = pad_t(vs.reshape(bs, ts, kvw))
    mb = _dsa_select(page_table, qi8, zl8, jnp.swapaxes(cache_kidx[0], 1, 2), n_new=ts)
    os8 = _dsa_sample(page_table, rel_bias, q8, mb, kn8, vn8,
                      cache_k[0].reshape(n_phys * PAGE * ATT_KV_HEADS, HEAD_DIM),
                      cache_v[0].reshape(n_phys * PAGE * ATT_KV_HEADS, HEAD_DIM))
    os_ = os8.reshape(bs, ATT_HEADS, TQ, HEAD_DIM)[:, :, :ts].transpose(0, 2, 1, 3)
    os_ = os_.reshape(bs * ts, ATT_HEADS * HEAD_DIM)
    ys = _post_mix(xs, os_, w_att_out, row(ln_mlp[0]), w1[0], w2[0])

    cos_p, sin_p = _rotation_tables(jnp.arange(tp, dtype=jnp.int32))
    q, k, v, gate = _ret_proj(yp, row(ln_mix[1]), w_ret, cos_p, sin_p)
    r3 = lambda a: a.reshape(bp, tp, a.shape[-1])
    orp, state_p = _retention(r3(q), r3(k), r3(v), r3(gate), _retention_consts(RET_CHUNK, RET_CHUNK), None)
    yp = _post_mix(yp, orp.reshape(bp * tp, -1), w_ret_out, row(ln_mlp[1]), w1[1], w2[1])

    cos_s, sin_s = _rotation_tables(past + jnp.arange(ts, dtype=jnp.int32))
    tile_s = lambda a: jnp.tile(a, (bs, 1))
    q, k, v, gate = _ret_proj(ys, row(ln_mix[1]), w_ret, tile_s(cos_s), tile_s(sin_s))
    r3s = lambda a: pad_t(a.reshape(bs, ts, a.shape[-1]))
    ors, state_s = _retention(r3s(q), r3s(k), r3s(v), r3s(gate), _retention_consts(ts, TQ), state_ret[0])
    ys = _post_mix(ys, ors[:, :ts].reshape(bs * ts, -1), w_ret_out, row(ln_mlp[1]), w1[1], w2[1])

    return (yp.reshape(bp, tp, D_MODEL), ys.reshape(bs, ts, D_MODEL),
            kp.reshape(1, bp, tp, ATT_KV_HEADS, HEAD_DIM), vp.reshape(1, bp, tp, ATT_KV_HEADS, HEAD_DIM),
            kip.reshape(1, bp, tp, IDX_DIM), state_p[None],
            ks.reshape(1, bs, ts, ATT_KV_HEADS, HEAD_DIM), vs.reshape(1, bs, ts, ATT_KV_HEADS, HEAD_DIM),
            kis.reshape(1, bs, ts, IDX_DIM), state_s[None])
```

```python
import functools
import math

import jax
import jax.numpy as jnp
import numpy as np
from jax import lax
from jax.experimental import pallas as pl
from jax.experimental.pallas import tpu as pltpu

F32 = jnp.float32
BF16 = jnp.bfloat16
I32 = jnp.int32

D_MODEL = 1024
PAGE = 128
ATT_HEADS = 8
ATT_KV_HEADS = 2
ATT_GROUP = ATT_HEADS // ATT_KV_HEADS
HEAD_DIM = 128
IDX_HEADS = 4
IDX_DIM = 64
TOPK_MAX = 256
NUM_BUCKETS = 32
MAX_DISTANCE = 128
RET_HEADS = 4
RET_KEY = 256
RET_VAL = 512
RET_CHUNK = 256
D_FF = 4 * D_MODEL
EPS = 1e-6

Q_OFF = 0
K_OFF = ATT_HEADS * HEAD_DIM
V_OFF = K_OFF + ATT_KV_HEADS * HEAD_DIM
QI_OFF = V_OFF + ATT_KV_HEADS * HEAD_DIM
KI_OFF = QI_OFF + IDX_HEADS * IDX_DIM
WI_OFF = KI_OFF + IDX_DIM
ATT_IN = WI_OFF + IDX_HEADS
ATT_IN_PAD = 1920

LANES = 128
ROW_TILE = 512
VMEM_LIMIT = 56 * 1024 * 1024

ATT_SCALE = HEAD_DIM ** -0.5
RET_KSCALE = RET_KEY ** -0.5
INT_MIN = -(2 ** 31)
NEG = -1e30


def _bucket_bounds():
    n = np.arange(0, 8192)
    max_exact = NUM_BUCKETS // 2
    val = np.log(np.maximum(n, 1) / max_exact) / math.log(MAX_DISTANCE / max_exact) * (NUM_BUCKETS - max_exact)
    frac = np.abs(val - np.round(val))
    risky = (n > max_exact) & (n < MAX_DISTANCE) & (frac < 1e-4)
    assert not risky.any()
    large = np.minimum(max_exact + np.floor(val + 1e-9).astype(np.int64), NUM_BUCKETS - 1)
    bucket = np.where(n < max_exact, n, large)
    assert (np.diff(bucket) >= 0).all() and (bucket[MAX_DISTANCE:] == NUM_BUCKETS - 1).all()
    return [int(np.argmax(bucket >= b)) for b in range(NUM_BUCKETS)]


BUCKET_BOUNDS = _bucket_bounds()


def _const_spec(shape):
    zeros = (0,) * len(shape)
    return pl.BlockSpec(shape, lambda *_: zeros, pipeline_mode=pl.Buffered(1))


def _params(n_axes):
    return pltpu.CompilerParams(dimension_semantics=("arbitrary",) * n_axes,
                                vmem_limit_bytes=VMEM_LIMIT)


def _rms_rows(a):
    return a * lax.rsqrt(jnp.mean(a * a, axis=-1, keepdims=True) + EPS)


def _sortable_key(score):
    score = jnp.where(score == 0.0, 0.0, score)
    bits = pltpu.bitcast(score, I32)
    return bits ^ ((bits >> 31) & jnp.int32(0x7FFFFFFF))


def _bias_from_distance(relb_ref, head, dist):
    val = jnp.full(dist.shape, relb_ref[0, head], F32)
    for b in range(1, NUM_BUCKETS):
        val = jnp.where(dist >= BUCKET_BOUNDS[b], relb_ref[b, head], val)
    return val


def _attn_proj_kernel(x_ref, g_ref, w_ref, qg_ref, kg_ref, *outs, transposed):
    x = x_ref[...]
    h = _rms_rows(x) * g_ref[...]
    z = jnp.dot(h.astype(BF16), w_ref[...], preferred_element_type=F32)
    qg = qg_ref[...]
    kg = kg_ref[...]
    k = jnp.concatenate(
        [_rms_rows(z[:, K_OFF + g * HEAD_DIM:K_OFF + (g + 1) * HEAD_DIM]) * kg for g in range(ATT_KV_HEADS)],
        axis=1)
    v = z[:, V_OFF:QI_OFF]
    qi = z[:, QI_OFF:KI_OFF]
    zl = z[:, KI_OFF:ATT_IN_PAD]
    ki = zl[:, :IDX_DIM]
    if transposed:
        qT_ref, qiT_ref, wiT_ref, vT_ref, kb_ref, kib_ref, k_ref, v_ref, ki_ref = outs
        for hh in range(ATT_HEADS):
            qh = _rms_rows(z[:, hh * HEAD_DIM:(hh + 1) * HEAD_DIM]) * qg * ATT_SCALE
            qT_ref[hh * HEAD_DIM:(hh + 1) * HEAD_DIM, :] = qh.T.astype(BF16)
        for c in range(IDX_HEADS * IDX_DIM // LANES):
            qiT_ref[c * LANES:(c + 1) * LANES, :] = qi[:, c * LANES:(c + 1) * LANES].T.astype(BF16)
        wiT_ref[...] = zl.T[IDX_DIM:IDX_DIM + 8, :]
        for g in range(ATT_KV_HEADS):
            vT = v[:, g * HEAD_DIM:(g + 1) * HEAD_DIM].T
            for c in range(x.shape[0] // PAGE):
                vT_ref[c, g * HEAD_DIM:(g + 1) * HEAD_DIM, :] = vT[:, c * PAGE:(c + 1) * PAGE].astype(BF16)
        kb_ref[...] = k.astype(BF16)
        kib_ref[...] = ki.astype(BF16)
    else:
        q_ref, qi_ref, zl_ref, k_ref, v_ref, ki_ref = outs
        for hh in range(ATT_HEADS):
            q_ref[:, hh * HEAD_DIM:(hh + 1) * HEAD_DIM] = _rms_rows(z[:, hh * HEAD_DIM:(hh + 1) * HEAD_DIM]) * qg * ATT_SCALE
        qi_ref[...] = qi
        zl_ref[...] = zl
    k_ref[...] = k
    v_ref[...] = v
    ki_ref[...] = ki


def _attn_proj(x, g, w, qg, kg, *, transposed):
    rows = x.shape[0]
    tm = ROW_TILE
    nt = rows // tm
    kvw = ATT_KV_HEADS * HEAD_DIM
    row_spec = lambda width: pl.BlockSpec((tm, width), lambda i: (i, 0))
    col_spec = lambda height: pl.BlockSpec((height, tm), lambda i: (0, i))
    leaves_shape = [jax.ShapeDtypeStruct((rows, kvw), F32), jax.ShapeDtypeStruct((rows, kvw), F32),
                    jax.ShapeDtypeStruct((rows, IDX_DIM), F32)]
    leaves_spec = [row_spec(kvw), row_spec(kvw), row_spec(IDX_DIM)]
    if transposed:
        out_shape = [jax.ShapeDtypeStruct((ATT_HEADS * HEAD_DIM, rows), BF16),
                     jax.ShapeDtypeStruct((IDX_HEADS * IDX_DIM, rows), BF16),
                     jax.ShapeDtypeStruct((8, rows), F32),
                     jax.ShapeDtypeStruct((rows // PAGE, kvw, PAGE), BF16),
                     jax.ShapeDtypeStruct((rows, kvw), BF16),
                     jax.ShapeDtypeStruct((rows, IDX_DIM), BF16)] + leaves_shape
        out_specs = [col_spec(ATT_HEADS * HEAD_DIM), col_spec(IDX_HEADS * IDX_DIM), col_spec(8),
                     pl.BlockSpec((tm // PAGE, kvw, PAGE), lambda i: (i, 0, 0)),
                     row_spec(kvw), row_spec(IDX_DIM)] + leaves_spec
    else:
        out_shape = [jax.ShapeDtypeStruct((rows, ATT_HEADS * HEAD_DIM), F32),
                     jax.ShapeDtypeStruct((rows, IDX_HEADS * IDX_DIM), F32),
                     jax.ShapeDtypeStruct((rows, LANES), F32)] + leaves_shape
        out_specs = [row_spec(ATT_HEADS * HEAD_DIM), row_spec(IDX_HEADS * IDX_DIM), row_spec(LANES)] + leaves_spec
    return pl.pallas_call(
        functools.partial(_attn_proj_kernel, transposed=transposed),
        grid=(nt,),
        in_specs=[row_spec(D_MODEL), _const_spec((1, D_MODEL)), _const_spec((D_MODEL, ATT_IN_PAD)),
                  _const_spec((1, HEAD_DIM)), _const_spec((1, HEAD_DIM))],
        out_specs=out_specs,
        out_shape=out_shape,
        compiler_params=_params(1),
        name="attn_proj_t" if transposed else "attn_proj_r",
    )(x, g, w, qg, kg)


def _dsa_prompt_kernel(relb_ref, qT_ref, qiT_ref, wiT_ref, k_ref, vT_ref, ki_ref, o_ref,
                       keys_sc, mb_sc, bias_sc, logit_sc, acc_sc, *, topk):
    b = pl.program_id(0)
    i = pl.program_id(1)
    srow = lax.broadcasted_iota(I32, (PAGE, PAGE), 0)
    tcol = lax.broadcasted_iota(I32, (PAGE, PAGE), 1)

    @pl.when((b == 0) & (i == 0))
    def _init_bias():
        for hh in range(ATT_HEADS):
            for rel in range(2):
                dist = jnp.maximum(rel * PAGE + tcol - srow, 0)
                bias_sc[hh, rel] = _bias_from_distance(relb_ref, hh, dist)
            bias_sc[hh, 2] = jnp.full((PAGE, PAGE), relb_ref[NUM_BUCKETS - 1, hh], F32)

    wiT = wiT_ref[...]

    npair = (i + 2) // 2

    def score_block(j):
        off = pl.multiple_of(j * PAGE, PAGE)
        kij = ki_ref[pl.ds(off, PAGE), :]
        sc = jnp.zeros((PAGE, PAGE), F32)
        for hh in range(IDX_HEADS):
            idx = jnp.dot(kij, qiT_ref[hh * IDX_DIM:(hh + 1) * IDX_DIM, :], preferred_element_type=F32)
            sc = sc + wiT[hh:hh + 1, :] * jnp.maximum(idx, 0.0)
        causal = jnp.logical_or(j < i, jnp.logical_and(j == i, srow <= tcol))
        keys_sc[j] = jnp.where(causal, _sortable_key(sc), INT_MIN)

    nquad = (i + 4) // 4

    def score_body(jj, carry):
        for u in range(4):
            score_block(4 * jj + u)
        return carry

    lax.fori_loop(0, nquad, score_body, 0)

    lane = lax.broadcasted_iota(I32, (1, PAGE), 1)
    kprime = jnp.minimum(topk, i * PAGE + lane + 1).astype(F32)

    def count(pred_of_keys):
        def body(jj, cnt):
            cnt = cnt + jnp.where(pred_of_keys(keys_sc[2 * jj]), 1.0, 0.0)
            return cnt + jnp.where(pred_of_keys(keys_sc[2 * jj + 1]), 1.0, 0.0)
        cnt = lax.fori_loop(0, npair, body, jnp.zeros((PAGE, PAGE), F32))
        return jnp.sum(cnt, axis=0, keepdims=True)

    cand0 = jnp.zeros((1, PAGE), I32)
    thr0 = jnp.where(count(lambda key: key >= cand0) >= kprime, cand0, INT_MIN)

    def bit_body(bi, thr):
        cand = thr | lax.shift_left(jnp.int32(1), 30 - bi)
        return jnp.where(count(lambda key: key >= cand) >= kprime, cand, thr)

    thr = lax.fori_loop(0, 31, bit_body, thr0)
    need = kprime - count(lambda key: key > thr)

    tri = jnp.where(srow > tcol, 1.0, 0.0).astype(BF16)

    def mask_block(j, carry):
        key = keys_sc[j]
        eqf = jnp.where(key == thr, 1.0, 0.0)
        rank = jnp.dot(tri, eqf.astype(BF16), preferred_element_type=F32) + carry
        tie = jnp.where(jnp.logical_and(key == thr, rank < need), 0.0, NEG)
        mb_sc[j] = jnp.where(key > thr, 0.0, tie)
        return carry + jnp.sum(eqf, axis=0, keepdims=True)

    def mask_body(jj, carry):
        for u in range(4):
            carry = mask_block(4 * jj + u, carry)
        return carry

    lax.fori_loop(0, nquad, mask_body, jnp.zeros((1, PAGE), F32))

    groups = range(ATT_KV_HEADS)
    width = ATT_GROUP * PAGE
    qcats = [jnp.concatenate(
        [qT_ref[(ATT_GROUP * g + r) * HEAD_DIM:(ATT_GROUP * g + r + 1) * HEAD_DIM, :] for r in range(ATT_GROUP)],
        axis=1) for g in groups]

    def logit_block(j, ms):
        off = pl.multiple_of(j * PAGE, PAGE)
        mb = mb_sc[j]
        relc = jnp.clip(i - j, 0, 2)
        out = []
        for g in groups:
            kj = k_ref[pl.ds(off, PAGE), g * HEAD_DIM:(g + 1) * HEAD_DIM]
            logits = jnp.dot(kj, qcats[g], preferred_element_type=F32)
            logits = jnp.concatenate(
                [logits[:, r * PAGE:(r + 1) * PAGE] + (mb + bias_sc[ATT_GROUP * g + r, relc])
                 for r in range(ATT_GROUP)], axis=1)
            logit_sc[g, j] = logits
            out.append(jnp.maximum(ms[g], jnp.max(logits, axis=0, keepdims=True)))
        return tuple(out)

    ms = lax.fori_loop(0, npair, lambda jj, c: logit_block(2 * jj + 1, logit_block(2 * jj, c)),
                       tuple(jnp.full((1, width), NEG, F32) for _ in groups))

    def value_block(j, ls):
        contrib, out = [], []
        for g in groups:
            p = jnp.exp(logit_sc[g, j] - ms[g])
            out.append(ls[g] + jnp.sum(p, axis=0, keepdims=True))
            vTj = vT_ref[j, g * HEAD_DIM:(g + 1) * HEAD_DIM, :]
            contrib.append(jnp.dot(vTj, p.astype(BF16), preferred_element_type=F32))
        return contrib, tuple(out)

    def value_body(jj, ls):
        c0, ls = value_block(2 * jj, ls)
        c1, ls = value_block(2 * jj + 1, ls)
        for g in groups:
            acc_sc[g] = acc_sc[g] + (c0[g] + c1[g])
        return ls

    acc_sc[...] = jnp.zeros_like(acc_sc)
    ls = lax.fori_loop(0, npair, value_body, tuple(jnp.zeros((1, width), F32) for _ in groups))
    for g in groups:
        outT = acc_sc[g] * (1.0 / ls[g])
        for r in range(ATT_GROUP):
            hh = ATT_GROUP * g + r
            o_ref[:, hh * HEAD_DIM:(hh + 1) * HEAD_DIM] = outT[:, r * PAGE:(r + 1) * PAGE].T.astype(BF16)


def _dsa_prompt(rel_bias, qT, qiT, wiT, kb, vT3, kib, *, batch, seq):
    nb = seq // PAGE
    rows = batch * seq
    kvw = ATT_KV_HEADS * HEAD_DIM
    qcol = lambda height: pl.BlockSpec((height, PAGE), lambda b, i: (0, b * nb + i))
    return pl.pallas_call(
        functools.partial(_dsa_prompt_kernel, topk=min(TOPK_MAX, seq // 4)),
        grid=(batch, nb),
        in_specs=[pl.BlockSpec(memory_space=pltpu.SMEM),
                  qcol(ATT_HEADS * HEAD_DIM), qcol(IDX_HEADS * IDX_DIM), qcol(8),
                  pl.BlockSpec((seq, kvw), lambda b, i: (b, 0)),
                  pl.BlockSpec((nb, kvw, PAGE), lambda b, i: (b, 0, 0)),
                  pl.BlockSpec((seq, IDX_DIM), lambda b, i: (b, 0))],
        out_specs=pl.BlockSpec((PAGE, ATT_HEADS * HEAD_DIM), lambda b, i: (b * nb + i, 0)),
        out_shape=jax.ShapeDtypeStruct((rows, ATT_HEADS * HEAD_DIM), BF16),
        scratch_shapes=[pltpu.VMEM((nb, PAGE, PAGE), I32), pltpu.VMEM((nb, PAGE, PAGE), F32),
                        pltpu.VMEM((ATT_HEADS, 3, PAGE, PAGE), F32),
                        pltpu.VMEM((ATT_KV_HEADS, nb, PAGE, ATT_GROUP * PAGE), F32),
                        pltpu.VMEM((ATT_KV_HEADS, HEAD_DIM, ATT_GROUP * PAGE), F32)],
        compiler_params=_params(2),
        name="dsa_prompt",
    )(rel_bias, qT, qiT, wiT, kb, vT3, kib)


TQ = 8
SEL_SAMPLES = 8


def _dsa_select_kernel(pt_ref, qi_ref, zl_ref, *rest, n_pages, n_new):
    ns = SEL_SAMPLES
    ki_pages = rest[:ns * n_pages]
    mb_ref, keys_sc = rest[ns * n_pages:]
    del pt_ref
    n_blk = n_pages + 1
    rows = ns * TQ
    trow = lax.broadcasted_iota(I32, (TQ, PAGE), 0)
    scol = lax.broadcasted_iota(I32, (TQ, PAGE), 1)
    pad_rows = lambda a: jnp.concatenate([a, jnp.zeros((PAGE - TQ, a.shape[1]), a.dtype)], axis=0)
    nt = (((1,), (1,)), ((), ()))

    for s in range(ns):
        zl = zl_ref[s]
        qi = qi_ref[s]
        for p in range(n_blk):
            if p == n_pages:
                ki_new = pad_rows(zl[:, :IDX_DIM]).astype(BF16)
                idx = lax.dot_general(qi, ki_new, nt, preferred_element_type=F32)
            else:
                idx = jnp.dot(qi, ki_pages[s * n_pages + p][...].astype(BF16), preferred_element_type=F32)
            sc = jnp.zeros((TQ, PAGE), F32)
            for hh in range(IDX_HEADS):
                sc = sc + zl[:, IDX_DIM + hh:IDX_DIM + hh + 1] * jnp.maximum(idx[hh * TQ:(hh + 1) * TQ], 0.0)
            key = _sortable_key(sc)
            if p == n_pages:
                key = jnp.where(jnp.logical_and(scol <= trow, scol < n_new), key, INT_MIN)
            keys_sc[s * TQ:(s + 1) * TQ, p * PAGE:(p + 1) * PAGE] = key

    kprime = float(min(TOPK_MAX, (n_pages * PAGE + n_new) // 4))

    def count(pred_of_keys):
        return jnp.sum(jnp.where(pred_of_keys(keys_sc[...]), 1.0, 0.0), axis=1, keepdims=True)

    cand0 = jnp.zeros((rows, 1), I32)
    thr0 = jnp.where(count(lambda key: key >= cand0) >= kprime, cand0, INT_MIN)

    def bit_body(bi, thr):
        cand = thr | lax.shift_left(jnp.int32(1), 30 - bi)
        return jnp.where(count(lambda key: key >= cand) >= kprime, cand, thr)

    thr = lax.fori_loop(0, 31, bit_body, thr0)
    need = kprime - count(lambda key: key > thr)

    s_r = lax.broadcasted_iota(I32, (PAGE, PAGE), 0)
    s_c = lax.broadcasted_iota(I32, (PAGE, PAGE), 1)
    triu = jnp.where(s_r < s_c, 1.0, 0.0).astype(BF16)
    carry = jnp.zeros((rows, 1), F32)
    for p in range(n_blk):
        key = keys_sc[:, p * PAGE:(p + 1) * PAGE]
        eqf = jnp.where(key == thr, 1.0, 0.0)
        rank = jnp.dot(eqf.astype(BF16), triu, preferred_element_type=F32) + carry
        tie = jnp.where(jnp.logical_and(key == thr, rank < need), 0.0, NEG)
        mb_ref[:, p * PAGE:(p + 1) * PAGE] = jnp.where(key > thr, 0.0, tie)
        carry = carry + jnp.sum(eqf, axis=1, keepdims=True)


def _dsa_select(page_table, qi8, zl8, cache_ki, *, n_new):
    nsamp, n_pages = page_table.shape
    ns = SEL_SAMPLES
    width = (n_pages + 1) * PAGE

    def ki_page_spec(s, p):
        return pl.BlockSpec((None, IDX_DIM, PAGE), lambda b, pt: (pt[(b * ns + s) * n_pages + p], 0, 0))

    in_specs = ([pl.BlockSpec((ns, IDX_HEADS * TQ, IDX_DIM), lambda b, pt: (b, 0, 0)),
                 pl.BlockSpec((ns, TQ, LANES), lambda b, pt: (b, 0, 0))]
                + [ki_page_spec(s, p) for s in range(ns) for p in range(n_pages)])
    grid_spec = pltpu.PrefetchScalarGridSpec(
        num_scalar_prefetch=1, grid=(nsamp // ns,), in_specs=in_specs,
        out_specs=pl.BlockSpec((ns * TQ, width), lambda b, pt: (b, 0)),
        scratch_shapes=[pltpu.VMEM((ns * TQ, width), I32)])
    return pl.pallas_call(
        functools.partial(_dsa_select_kernel, n_pages=n_pages, n_new=n_new),
        grid_spec=grid_spec,
        out_shape=jax.ShapeDtypeStruct((nsamp * TQ, width), F32),
        compiler_params=_params(1),
        name="dsa_select",
    )(page_table.reshape(-1), qi8, zl8, *([cache_ki] * (ns * n_pages)))


def _dsa_sample_kernel(pt_ref, relb_ref, q_ref, mb_ref, kn_ref, vn_ref, *rest, n_pages):
    k_pages = rest[:n_pages]
    v_pages = rest[n_pages:2 * n_pages]
    o_ref, bias_sc, kall_sc, vall_sc = rest[2 * n_pages:]
    del pt_ref
    past = n_pages * PAGE
    rows_g = ATT_GROUP * TQ

    trow = lax.broadcasted_iota(I32, (TQ, PAGE), 0)
    scol = lax.broadcasted_iota(I32, (TQ, PAGE), 1)

    @pl.when(pl.program_id(0) == 0)
    def _init_bias():
        for g in range(ATT_KV_HEADS):
            for r in range(ATT_GROUP):
                hh = ATT_GROUP * g + r
                rs = slice(r * TQ, (r + 1) * TQ)
                far = jnp.full((TQ, PAGE), relb_ref[NUM_BUCKETS - 1, hh], F32)
                for p in range(n_pages - 1):
                    bias_sc[g, rs, p * PAGE:(p + 1) * PAGE] = far
                bias_sc[g, rs, past - PAGE:past] = _bias_from_distance(
                    relb_ref, hh, jnp.maximum(PAGE + trow - scol, 0))
                bias_sc[g, rs, past:past + PAGE] = _bias_from_distance(relb_ref, hh, jnp.maximum(trow - scol, 0))

    nt = (((1,), (1,)), ((), ()))
    mb = jnp.concatenate([mb_ref[...]] * ATT_GROUP, axis=0)
    pad_rows = lambda a: jnp.concatenate([a, jnp.zeros((PAGE - TQ, a.shape[1]), a.dtype)], axis=0)
    for g in range(ATT_KV_HEADS):
        gs = slice(g * HEAD_DIM, (g + 1) * HEAD_DIM)
        for p in range(n_pages):
            ps = slice(p * PAGE, (p + 1) * PAGE)
            kall_sc[g, ps, :] = k_pages[p][pl.ds(g, PAGE, stride=ATT_KV_HEADS), :].astype(BF16)
            vall_sc[g, ps, :] = v_pages[p][pl.ds(g, PAGE, stride=ATT_KV_HEADS), :].astype(BF16)
        kall_sc[g, past:past + PAGE, :] = pad_rows(kn_ref[:, gs]).astype(BF16)
        vall_sc[g, past:past + PAGE, :] = pad_rows(vn_ref[:, gs]).astype(BF16)
    for g in range(ATT_KV_HEADS):
        qg = q_ref[g * rows_g:(g + 1) * rows_g, :]
        logits = lax.dot_general(qg, kall_sc[g], nt, preferred_element_type=F32)
        logits = logits + (mb + bias_sc[g])
        m = jnp.max(logits, axis=1, keepdims=True)
        pr = jnp.exp(logits - m)
        l = jnp.sum(pr, axis=1, keepdims=True)
        acc = jnp.dot(pr.astype(BF16), vall_sc[g], preferred_element_type=F32)
        o_ref[g * rows_g:(g + 1) * rows_g, :] = (acc * (1.0 / l)).astype(BF16)


def _dsa_sample(page_table, rel_bias, q8, mb, kn8, vn8, cache_k, cache_v):
    nsamp, n_pages = page_table.shape
    kvw = ATT_KV_HEADS * HEAD_DIM
    per_sample = lambda *tail: pl.BlockSpec((None,) + tail, lambda b, pt: (b,) + (0,) * len(tail))

    def kv_page_spec(p):
        return pl.BlockSpec((PAGE * ATT_KV_HEADS, HEAD_DIM), lambda b, pt: (pt[b * n_pages + p], 0))

    in_specs = ([pl.BlockSpec(memory_space=pltpu.SMEM),
                 per_sample(ATT_HEADS * TQ, HEAD_DIM),
                 pl.BlockSpec((TQ, mb.shape[1]), lambda b, pt: (b, 0)),
                 per_sample(TQ, kvw), per_sample(TQ, kvw)]
                + [kv_page_spec(p) for p in range(n_pages)]
                + [kv_page_spec(p) for p in range(n_pages)])
    grid_spec = pltpu.PrefetchScalarGridSpec(
        num_scalar_prefetch=1, grid=(nsamp,), in_specs=in_specs,
        out_specs=per_sample(ATT_HEADS * TQ, HEAD_DIM),
        scratch_shapes=[pltpu.VMEM((ATT_KV_HEADS, ATT_GROUP * TQ, mb.shape[1]), F32),
                        pltpu.VMEM((ATT_KV_HEADS, mb.shape[1], HEAD_DIM), BF16),
                        pltpu.VMEM((ATT_KV_HEADS, mb.shape[1], HEAD_DIM), BF16)])
    return pl.pallas_call(
        functools.partial(_dsa_sample_kernel, n_pages=n_pages),
        grid_spec=grid_spec,
        out_shape=jax.ShapeDtypeStruct((nsamp, ATT_HEADS * TQ, HEAD_DIM), BF16),
        compiler_params=_params(1),
        name="dsa_sample",
    )(page_table.reshape(-1), rel_bias, q8, mb, kn8, vn8, *([cache_k] * n_pages), *([cache_v] * n_pages))


def _post_mix_kernel(x_ref, o_ref, wo_ref, g_ref, w1_ref, w2_ref, y_ref):
    y = x_ref[...] + jnp.dot(o_ref[...], wo_ref[...], preferred_element_type=F32)
    h = (_rms_rows(y) * g_ref[...]).astype(BF16)
    acc = y
    for c in range(D_FF // D_MODEL):
        a = jnp.dot(h, w1_ref[:, c * D_MODEL:(c + 1) * D_MODEL], preferred_element_type=F32)
        a = jnp.square(jnp.maximum(a, 0.0)).astype(BF16)
        acc = acc + jnp.dot(a, w2_ref[c * D_MODEL:(c + 1) * D_MODEL, :], preferred_element_type=F32)
    y_ref[...] = acc


def _post_mix(x, o, wo, g, w1, w2):
    rows = x.shape[0]
    tm = ROW_TILE
    ko = o.shape[1]
    return pl.pallas_call(
        _post_mix_kernel,
        grid=(rows // tm,),
        in_specs=[pl.BlockSpec((tm, D_MODEL), lambda i: (i, 0)), pl.BlockSpec((tm, ko), lambda i: (i, 0)),
                  _const_spec((ko, D_MODEL)), _const_spec((1, D_MODEL)),
                  _const_spec((D_MODEL, D_FF)), _const_spec((D_FF, D_MODEL))],
        out_specs=pl.BlockSpec((tm, D_MODEL), lambda i: (i, 0)),
        out_shape=jax.ShapeDtypeStruct((rows, D_MODEL), F32),
        compiler_params=_params(1),
        name="post_mix",
    )(x, o, wo, g, w1, w2)


def _ret_proj_kernel(x_ref, g_ref, w_ref, cos_ref, sin_ref, q_ref, k_ref, v_ref, gate_ref):
    hb = (_rms_rows(x_ref[...]) * g_ref[...]).astype(BF16)
    cos = cos_ref[...]
    sin = sin_ref[...]
    half = RET_KEY // 2
    qk = RET_HEADS * RET_KEY
    vd = RET_HEADS * RET_VAL

    def rotated(col0, scale):
        z = jnp.dot(hb, w_ref[:, col0:col0 + RET_KEY], preferred_element_type=F32)
        x1, x2 = z[:, :half], z[:, half:]
        return (x1 * cos - x2 * sin) * scale, (x1 * sin + x2 * cos) * scale

    for hh in range(RET_HEADS):
        a, b = rotated(hh * RET_KEY, 1.0)
        q_ref[:, hh * RET_KEY:hh * RET_KEY + half] = a
        q_ref[:, hh * RET_KEY + half:(hh + 1) * RET_KEY] = b
        a, b = rotated(qk + hh * RET_KEY, RET_KSCALE)
        k_ref[:, hh * RET_KEY:hh * RET_KEY + half] = a
        k_ref[:, hh * RET_KEY + half:(hh + 1) * RET_KEY] = b
    for hh in range(RET_HEADS):
        cs = slice(hh * RET_VAL, (hh + 1) * RET_VAL)
        v_ref[:, cs] = jnp.dot(hb, w_ref[:, 2 * qk + hh * RET_VAL:2 * qk + (hh + 1) * RET_VAL],
                               preferred_element_type=F32).astype(BF16)
        gate_ref[:, cs] = jnp.dot(hb, w_ref[:, 2 * qk + vd + hh * RET_VAL:2 * qk + vd + (hh + 1) * RET_VAL],
                                  preferred_element_type=F32)


def _ret_proj(x, g, w, cos, sin):
    rows = x.shape[0]
    tm = ROW_TILE
    qk = RET_HEADS * RET_KEY
    vd = RET_HEADS * RET_VAL
    n_pos_tiles = cos.shape[0] // tm
    row_spec = lambda width: pl.BlockSpec((tm, width), lambda i: (i, 0))
    pos_spec = pl.BlockSpec((tm, RET_KEY // 2), lambda i: (i % n_pos_tiles, 0))
    return pl.pallas_call(
        _ret_proj_kernel,
        grid=(rows // tm,),
        in_specs=[row_spec(D_MODEL), _const_spec((1, D_MODEL)), _const_spec((D_MODEL, 2 * qk + 2 * vd)),
                  pos_spec, pos_spec],
        out_specs=[row_spec(qk), row_spec(qk), row_spec(vd), row_spec(vd)],
        out_shape=[jax.ShapeDtypeStruct((rows, qk), F32), jax.ShapeDtypeStruct((rows, qk), F32),
                   jax.ShapeDtypeStruct((rows, vd), BF16), jax.ShapeDtypeStruct((rows, vd), F32)],
        compiler_params=_params(1),
        name="ret_proj",
    )(x, g, w, cos, sin)


def _retention_kernel(cd_ref, q_ref, k_ref, v_ref, g_ref, dm_ref, qd_ref, kd_ref, *rest, has_state0):
    if has_state0:
        s0_ref, o_ref, s_ref = rest
    else:
        o_ref, s_ref = rest

    @pl.when(pl.program_id(1) == 0)
    def _init_state():
        s_ref[...] = s0_ref[...] if has_state0 else jnp.zeros_like(s_ref)

    nt = (((1,), (1,)), ((), ()))
    tn = (((0,), (0,)), ((), ()))
    units = [(bi, hh) for bi in range(q_ref.shape[0]) for hh in range(RET_HEADS)]
    ks = lambda hh: slice(hh * RET_KEY, (hh + 1) * RET_KEY)
    vs = lambda hh: slice(hh * RET_VAL, (hh + 1) * RET_VAL)
    att = [lax.dot_general(q_ref[bi, :, ks(hh)].astype(BF16), k_ref[bi, :, ks(hh)].astype(BF16), nt,
                           preferred_element_type=F32) * dm_ref[hh] for bi, hh in units]
    cross = [jnp.dot((q_ref[bi, :, ks(hh)] * qd_ref[hh]).astype(BF16), s_ref[bi, hh].astype(BF16),
                     preferred_element_type=F32) for bi, hh in units]
    for (bi, hh), a, x in zip(units, att, cross):
        o = jnp.dot(a.astype(BF16), v_ref[bi, :, vs(hh)], preferred_element_type=F32) + x
        gate = g_ref[bi, :, vs(hh)]
        o_ref[bi, :, vs(hh)] = (_rms_rows(o) * (gate * jax.nn.sigmoid(gate))).astype(BF16)
    for bi, hh in units:
        kd = (k_ref[bi, :, ks(hh)] * kd_ref[hh]).astype(BF16)
        s_ref[bi, hh] = cd_ref[hh] * s_ref[bi, hh] + lax.dot_general(kd, v_ref[bi, :, vs(hh)], tn,
                                                                     preferred_element_type=F32)


def _retention(q, k, v, gate, consts, state0, *, seqs_per_step):
    dm, qd, kd, cd = consts
    nb, t, _ = q.shape
    c = dm.shape[1]
    bb = seqs_per_step
    qk = RET_HEADS * RET_KEY
    vd = RET_HEADS * RET_VAL
    tok = lambda width: pl.BlockSpec((bb, c, width), lambda b, ci: (b, ci, 0))
    st_spec = pl.BlockSpec((bb, RET_HEADS, RET_KEY, RET_VAL), lambda b, ci: (b, 0, 0, 0))
    in_specs = [pl.BlockSpec(memory_space=pltpu.SMEM), tok(qk), tok(qk), tok(vd), tok(vd),
                _const_spec(dm.shape), _const_spec(qd.shape), _const_spec(kd.shape)]
    args = [cd, q, k, v, gate, dm, qd, kd]
    if state0 is not None:
        in_specs.append(st_spec)
        args.append(state0)
    return pl.pallas_call(
        functools.partial(_retention_kernel, has_state0=state0 is not None),
        grid=(nb // bb, t // c),
        in_specs=in_specs,
        out_specs=[tok(vd), st_spec],
        out_shape=[jax.ShapeDtypeStruct((nb, t, vd), BF16),
                   jax.ShapeDtypeStruct((nb, RET_HEADS, RET_KEY, RET_VAL), F32)],
        compiler_params=_params(2),
        name="retention_s" if state0 is not None else "retention_p",
    )(*args)


def _retention_consts(c, c_pad):
    log_g = jnp.log1p(-jnp.exp2(-5.0 - jnp.arange(RET_HEADS, dtype=F32)))
    i = jnp.arange(c, dtype=F32)
    diff = i[:, None] - i[None, :]
    dm = jnp.where(diff >= 0, jnp.exp(log_g[:, None, None] * jnp.maximum(diff, 0.0)), 0.0)
    qd = jnp.exp(log_g[:, None] * (i[None, :] + 1.0))
    kd = jnp.exp(log_g[:, None] * (c - 1.0 - i[None, :]))
    cd = jnp.exp(log_g * c)
    pad = c_pad - c
    dm = jnp.pad(dm, ((0, 0), (0, pad), (0, pad)))
    qd = jnp.broadcast_to(jnp.pad(qd, ((0, 0), (0, pad)))[:, :, None], (RET_HEADS, c_pad, RET_KEY))
    kd = jnp.broadcast_to(jnp.pad(kd, ((0, 0), (0, pad)))[:, :, None], (RET_HEADS, c_pad, RET_KEY))
    return dm, qd, kd, cd


def _rotation_tables(pos):
    half = RET_KEY // 2
    theta = 1.0 / (10000.0 ** jnp.linspace(0.0, 1.0, half, dtype=F32))
    ang = pos.astype(F32)[:, None] * theta[None, :]
    return jnp.cos(ang), jnp.sin(ang)


def kernel(x_prompt, x_sample, cache_k, cache_v, cache_kidx, state_ret, page_table, rel_bias, ln_mix, ln_mlp,
           att_w_in, att_q_gain, att_k_gain, att_w_out, ret_w_in, ret_w_out, mlp_w_in, mlp_w_out):
    bp, tp, _ = x_prompt.shape
    bs, ts, _ = x_sample.shape
    n_phys = cache_k.shape[1]
    assert ln_mix.shape[0] == 2 and att_w_in.shape[0] == 1 and ret_w_in.shape[0] == 1
    assert tp % RET_CHUNK == 0 and ts <= TQ and cache_k.shape[2] == PAGE
    assert (tp // PAGE) % 4 == 0 and bs % SEL_SAMPLES == 0
    kvw = ATT_KV_HEADS * HEAD_DIM
    past = page_table.shape[1] * PAGE

    w_att = jnp.pad(att_w_in[0], ((0, 0), (0, ATT_IN_PAD - ATT_IN))).astype(BF16)
    w_att_out = att_w_out[0].astype(BF16)
    w_ret = ret_w_in[0].astype(BF16)
    w_ret_out = ret_w_out[0].astype(BF16)
    w1 = mlp_w_in.astype(BF16)
    w2 = mlp_w_out.astype(BF16)
    row = lambda a: a.reshape(1, -1)

    xp = x_prompt.reshape(bp * tp, D_MODEL)
    xs = x_sample.reshape(bs * ts, D_MODEL)

    qT, qiT, wiT, vT3, kb, kib, kp, vp, kip = _attn_proj(
        xp, row(ln_mix[0]), w_att, row(att_q_gain[0]), row(att_k_gain[0]), transposed=True)
    op = _dsa_prompt(rel_bias, qT, qiT, wiT, kb, vT3, kib, batch=bp, seq=tp)
    yp = _post_mix(xp, op, w_att_out, row(ln_mlp[0]), w1[0], w2[0])

    qs, qis, zls, ks, vs, kis = _attn_proj(
        xs, row(ln_mix[0]), w_att, row(att_q_gain[0]), row(att_k_gain[0]), transposed=False)
    pad_t = lambda a: jnp.pad(a, ((0, 0), (0, TQ - ts)) + ((0, 0),) * (a.ndim - 2))
    q8 = pad_t(qs.reshape(bs, ts, ATT_HEADS, HEAD_DIM)).transpose(0, 2, 1, 3)
    q8 = q8.reshape(bs, ATT_HEADS * TQ, HEAD_DIM).astype(BF16)
    qi8 = pad_t(qis.reshape(bs, ts, IDX_HEADS, IDX_DIM)).transpose(0, 2, 1, 3)
    qi8 = qi8.reshape(bs, IDX_HEADS * TQ, IDX_DIM).astype(BF16)
    zl8 = pad_t(zls.reshape(bs, ts, LANES))
    kn8 = pad_t(ks.reshape(bs, ts, kvw))
    vn8 = pad_t(vs.reshape(bs, ts, kvw))
    mb = _dsa_select(page_table, qi8, zl8, jnp.swapaxes(cache_kidx[0], 1, 2), n_new=ts)
    os8 = _dsa_sample(page_table, rel_bias, q8, mb, kn8, vn8,
                      cache_k[0].reshape(n_phys * PAGE * ATT_KV_HEADS, HEAD_DIM),
                      cache_v[0].reshape(n_phys * PAGE * ATT_KV_HEADS, HEAD_DIM))
    os_ = os8.reshape(bs, ATT_HEADS, TQ, HEAD_DIM)[:, :, :ts].transpose(0, 2, 1, 3)
    os_ = os_.reshape(bs * ts, ATT_HEADS * HEAD_DIM)
    ys = _post_mix(xs, os_, w_att_out, row(ln_mlp[0]), w1[0], w2[0])

    cos_p, sin_p = _rotation_tables(jnp.arange(tp, dtype=jnp.int32))
    q, k, v, gate = _ret_proj(yp, row(ln_mix[1]), w_ret, cos_p, sin_p)
    r3 = lambda a: a.reshape(bp, tp, a.shape[-1])
    orp, state_p = _retention(r3(q), r3(k), r3(v), r3(gate), _retention_consts(RET_CHUNK, RET_CHUNK), None,
                            seqs_per_step=1)
    yp = _post_mix(yp, orp.reshape(bp * tp, -1), w_ret_out, row(ln_mlp[1]), w1[1], w2[1])

    cos_s, sin_s = _rotation_tables(past + jnp.arange(ts, dtype=jnp.int32))
    tile_s = lambda a: jnp.tile(a, (bs, 1))
    q, k, v, gate = _ret_proj(ys, row(ln_mix[1]), w_ret, tile_s(cos_s), tile_s(sin_s))
    r3s = lambda a: pad_t(a.reshape(bs, ts, a.shape[-1]))
    ors, state_s = _retention(r3s(q), r3s(k), r3s(v), r3s(gate), _retention_consts(ts, TQ), state_ret[0],
                            seqs_per_step=2)
    ys = _post_mix(ys, ors[:, :ts].reshape(bs * ts, -1), w_ret_out, row(ln_mlp[1]), w1[1], w2[1])

    return (yp.reshape(bp, tp, D_MODEL), ys.reshape(bs, ts, D_MODEL),
            kp.reshape(1, bp, tp, ATT_KV_HEADS, HEAD_DIM), vp.reshape(1, bp, tp, ATT_KV_HEADS, HEAD_DIM),
            kip.reshape(1, bp, tp, IDX_DIM), state_p[None],
            ks.reshape(1, bs, ts, ATT_KV_HEADS, HEAD_DIM), vs.reshape(1, bs, ts, ATT_KV_HEADS, HEAD_DIM),
            kis.reshape(1, bs, ts, IDX_DIM), state_s[None])
```

```python
import functools
import math

import jax
import jax.numpy as jnp
import numpy as np
from jax import lax
from jax.experimental import pallas as pl
from jax.experimental.pallas import tpu as pltpu

F32 = jnp.float32
BF16 = jnp.bfloat16
I32 = jnp.int32

D_MODEL = 1024
PAGE = 128
ATT_HEADS = 8
ATT_KV_HEADS = 2
ATT_GROUP = ATT_HEADS // ATT_KV_HEADS
HEAD_DIM = 128
IDX_HEADS = 4
IDX_DIM = 64
TOPK_MAX = 256
NUM_BUCKETS = 32
MAX_DISTANCE = 128
RET_HEADS = 4
RET_KEY = 256
RET_VAL = 512
RET_CHUNK = 256
D_FF = 4 * D_MODEL
EPS = 1e-6

Q_OFF = 0
K_OFF = ATT_HEADS * HEAD_DIM
V_OFF = K_OFF + ATT_KV_HEADS * HEAD_DIM
QI_OFF = V_OFF + ATT_KV_HEADS * HEAD_DIM
KI_OFF = QI_OFF + IDX_HEADS * IDX_DIM
WI_OFF = KI_OFF + IDX_DIM
ATT_IN = WI_OFF + IDX_HEADS
ATT_IN_PAD = 1920

LANES = 128
SUBLANES = 8
ROW_TILE = 512
VMEM_LIMIT = 56 * 1024 * 1024

ATT_SCALE = HEAD_DIM ** -0.5
RET_KSCALE = RET_KEY ** -0.5
INT_MIN = -(2 ** 31)
NEG = -1e30


def _bucket_bounds():
    n = np.arange(0, 8192)
    max_exact = NUM_BUCKETS // 2
    val = np.log(np.maximum(n, 1) / max_exact) / math.log(MAX_DISTANCE / max_exact) * (NUM_BUCKETS - max_exact)
    frac = np.abs(val - np.round(val))
    risky = (n > max_exact) & (n < MAX_DISTANCE) & (frac < 1e-4)
    assert not risky.any()
    large = np.minimum(max_exact + np.floor(val + 1e-9).astype(np.int64), NUM_BUCKETS - 1)
    bucket = np.where(n < max_exact, n, large)
    assert (np.diff(bucket) >= 0).all() and (bucket[MAX_DISTANCE:] == NUM_BUCKETS - 1).all()
    return [int(np.argmax(bucket >= b)) for b in range(NUM_BUCKETS)]


BUCKET_BOUNDS = _bucket_bounds()


def _const_spec(shape):
    zeros = (0,) * len(shape)
    return pl.BlockSpec(shape, lambda *_: zeros, pipeline_mode=pl.Buffered(1))


def _params(n_axes):
    return pltpu.CompilerParams(dimension_semantics=("arbitrary",) * n_axes,
                                vmem_limit_bytes=VMEM_LIMIT)


def _rms_rows(a):
    return a * lax.rsqrt(jnp.mean(a * a, axis=-1, keepdims=True) + EPS)


def _canonical_zero(score):
    return jnp.where(score == 0.0, 0.0, score)


def _sortable_key(score):
    bits = pltpu.bitcast(score, I32)
    return bits ^ ((bits >> 31) & jnp.int32(0x7FFFFFFF))


def _key_to_float(key):
    return pltpu.bitcast(key ^ ((key >> 31) & jnp.int32(0x7FFFFFFF)), F32)


KEY_NEG_INF = -2139095041
KEY_POS_INF = 2139095040


def _bracket_kth_largest(count_ge, kprime, guess):
    g0 = jnp.clip(guess, KEY_NEG_INF, KEY_POS_INF - 1)
    g1 = g0 + 1
    lo = jnp.full(guess.shape, KEY_NEG_INF, I32)
    hi = jnp.full(guess.shape, KEY_POS_INF, I32)

    def narrow(lo, hi, x, cnt):
        ok = cnt >= kprime
        return jnp.where(ok, jnp.maximum(lo, x), lo), jnp.where(ok, hi, jnp.minimum(hi, x))

    c0, c1 = count_ge([_key_to_float(g0), _key_to_float(g1)])
    lo, hi = narrow(lo, hi, g0, c0)
    lo, hi = narrow(lo, hi, g1, c1)

    def open_bracket(c):
        return jnp.max(jnp.where(c[1] != c[0] + 1, 1.0, 0.0)) > 0.5

    def bisect(c):
        lo, hi = c
        mid = (lo >> 1) + (hi >> 1) + (lo & hi & 1)
        return narrow(lo, hi, mid, count_ge([_key_to_float(mid)])[0])

    lo, _ = lax.while_loop(open_bracket, bisect, (lo, hi))
    return lo


def _bit_transpose32(words):
    a = list(words)
    j, m = 16, 0x0000FFFF
    while j:
        k = 0
        while k < 32:
            t = (a[k] ^ lax.shift_right_logical(a[k + j], jnp.int32(j))) & jnp.int32(m)
            a[k] = a[k] ^ t
            a[k + j] = a[k + j] ^ lax.shift_left(t, jnp.int32(j))
            k = (k + j + 1) & ~j
        j >>= 1
        m ^= (m << j) & 0xFFFFFFFF
    return a


def _bias_from_distance(relb_ref, head, dist):
    val = jnp.full(dist.shape, relb_ref[0, head], F32)
    for b in range(1, NUM_BUCKETS):
        val = jnp.where(dist >= BUCKET_BOUNDS[b], relb_ref[b, head], val)
    return val


def _attn_proj_kernel(x_ref, g_ref, w_ref, qg_ref, kg_ref, *outs, transposed):
    x = x_ref[...]
    h = _rms_rows(x) * g_ref[...]
    z = jnp.dot(h.astype(BF16), w_ref[...], preferred_element_type=F32)
    qg = qg_ref[...]
    kg = kg_ref[...]
    k = jnp.concatenate(
        [_rms_rows(z[:, K_OFF + g * HEAD_DIM:K_OFF + (g + 1) * HEAD_DIM]) * kg for g in range(ATT_KV_HEADS)],
        axis=1)
    v = z[:, V_OFF:QI_OFF]
    qi = z[:, QI_OFF:KI_OFF]
    zl = z[:, KI_OFF:ATT_IN_PAD]
    ki = zl[:, :IDX_DIM]
    if transposed:
        qT_ref, qiT_ref, wiT_ref, vT_ref, kb_ref, kib_ref, k_ref, v_ref, ki_ref = outs
        for hh in range(ATT_HEADS):
            qh = _rms_rows(z[:, hh * HEAD_DIM:(hh + 1) * HEAD_DIM]) * qg * ATT_SCALE
            qT_ref[hh * HEAD_DIM:(hh + 1) * HEAD_DIM, :] = qh.T.astype(BF16)
        for c in range(IDX_HEADS * IDX_DIM // LANES):
            qiT_ref[c * LANES:(c + 1) * LANES, :] = qi[:, c * LANES:(c + 1) * LANES].T.astype(BF16)
        wiT_ref[...] = zl.T[IDX_DIM:IDX_DIM + 8, :]
        for g in range(ATT_KV_HEADS):
            vT = v[:, g * HEAD_DIM:(g + 1) * HEAD_DIM].T
            for c in range(x.shape[0] // PAGE):
                vT_ref[c, g * HEAD_DIM:(g + 1) * HEAD_DIM, :] = vT[:, c * PAGE:(c + 1) * PAGE].astype(BF16)
        kb_ref[...] = k.astype(BF16)
        kib_ref[...] = ki.astype(BF16)
    else:
        q_ref, qi_ref, zl_ref, k_ref, v_ref, ki_ref = outs
        for hh in range(ATT_HEADS):
            q_ref[:, hh * HEAD_DIM:(hh + 1) * HEAD_DIM] = _rms_rows(z[:, hh * HEAD_DIM:(hh + 1) * HEAD_DIM]) * qg * ATT_SCALE
        qi_ref[...] = qi
        zl_ref[...] = zl
    k_ref[...] = k
    v_ref[...] = v
    ki_ref[...] = ki


def _attn_proj(x, g, w, qg, kg, *, transposed):
    rows = x.shape[0]
    tm = ROW_TILE
    nt = rows // tm
    kvw = ATT_KV_HEADS * HEAD_DIM
    row_spec = lambda width: pl.BlockSpec((tm, width), lambda i: (i, 0))
    col_spec = lambda height: pl.BlockSpec((height, tm), lambda i: (0, i))
    leaves_shape = [jax.ShapeDtypeStruct((rows, kvw), F32), jax.ShapeDtypeStruct((rows, kvw), F32),
                    jax.ShapeDtypeStruct((rows, IDX_DIM), F32)]
    leaves_spec = [row_spec(kvw), row_spec(kvw), row_spec(IDX_DIM)]
    if transposed:
        out_shape = [jax.ShapeDtypeStruct((ATT_HEADS * HEAD_DIM, rows), BF16),
                     jax.ShapeDtypeStruct((IDX_HEADS * IDX_DIM, rows), BF16),
                     jax.ShapeDtypeStruct((8, rows), F32),
                     jax.ShapeDtypeStruct((rows // PAGE, kvw, PAGE), BF16),
                     jax.ShapeDtypeStruct((rows, kvw), BF16),
                     jax.ShapeDtypeStruct((rows, IDX_DIM), BF16)] + leaves_shape
        out_specs = [col_spec(ATT_HEADS * HEAD_DIM), col_spec(IDX_HEADS * IDX_DIM), col_spec(8),
                     pl.BlockSpec((tm // PAGE, kvw, PAGE), lambda i: (i, 0, 0)),
                     row_spec(kvw), row_spec(IDX_DIM)] + leaves_spec
    else:
        out_shape = [jax.ShapeDtypeStruct((rows, ATT_HEADS * HEAD_DIM), F32),
                     jax.ShapeDtypeStruct((rows, IDX_HEADS * IDX_DIM), F32),
                     jax.ShapeDtypeStruct((rows, LANES), F32)] + leaves_shape
        out_specs = [row_spec(ATT_HEADS * HEAD_DIM), row_spec(IDX_HEADS * IDX_DIM), row_spec(LANES)] + leaves_spec
    return pl.pallas_call(
        functools.partial(_attn_proj_kernel, transposed=transposed),
        grid=(nt,),
        in_specs=[row_spec(D_MODEL), _const_spec((1, D_MODEL)), _const_spec((D_MODEL, ATT_IN_PAD)),
                  _const_spec((1, HEAD_DIM)), _const_spec((1, HEAD_DIM))],
        out_specs=out_specs,
        out_shape=out_shape,
        compiler_params=_params(1),
        name="attn_proj_t" if transposed else "attn_proj_r",
    )(x, g, w, qg, kg)


def _dsa_prompt_kernel(relb_ref, qT_ref, qiT_ref, wiT_ref, k_ref, vT_ref, ki_ref, o_ref,
                       keys_sc, scores_sc, planes_sc, live_sc, mb_sc, bias_sc, logit_sc, acc_sc, *, topk):
    b = pl.program_id(0)
    i = pl.program_id(1)
    srow = lax.broadcasted_iota(I32, (PAGE, PAGE), 0)
    tcol = lax.broadcasted_iota(I32, (PAGE, PAGE), 1)

    @pl.when((b == 0) & (i == 0))
    def _init_bias():
        planes_sc[...] = jnp.zeros_like(planes_sc)
        live_sc[...] = jnp.zeros_like(live_sc)
        for hh in range(ATT_HEADS):
            for rel in range(2):
                dist = jnp.maximum(rel * PAGE + tcol - srow, 0)
                bias_sc[hh, rel] = _bias_from_distance(relb_ref, hh, dist)
            bias_sc[hh, 2] = jnp.full((PAGE, PAGE), relb_ref[NUM_BUCKETS - 1, hh], F32)

    wiT = wiT_ref[...]

    npair = (i + 2) // 2

    def score_block(j):
        off = pl.multiple_of(j * PAGE, PAGE)
        kij = ki_ref[pl.ds(off, PAGE), :]
        sc = jnp.zeros((PAGE, PAGE), F32)
        for hh in range(IDX_HEADS):
            idx = jnp.dot(kij, qiT_ref[hh * IDX_DIM:(hh + 1) * IDX_DIM, :], preferred_element_type=F32)
            sc = sc + wiT[hh:hh + 1, :] * jnp.maximum(idx, 0.0)
        sc = _canonical_zero(sc)
        causal = jnp.logical_or(j < i, jnp.logical_and(j == i, srow <= tcol))
        scores_sc[j] = jnp.where(causal, sc, -jnp.inf)
        keys_sc[j] = jnp.where(causal, _sortable_key(sc), INT_MIN)

    nquad = (i + 4) // 4

    def score_body(jj, carry):
        for u in range(4):
            score_block(4 * jj + u)
        return carry

    lax.fori_loop(0, nquad, score_body, 0)

    lane = lax.broadcasted_iota(I32, (1, PAGE), 1)
    kprime = jnp.minimum(topk, i * PAGE + lane + 1).astype(F32)

    def plane_body(jj, carry):
        rows = [keys_sc[2 * jj + blk, k * SUBLANES:(k + 1) * SUBLANES, :] ^ INT_MIN
                for blk in range(2) for k in range(PAGE // SUBLANES)]
        planes = _bit_transpose32(rows)
        live = planes[0]
        for p in range(32):
            planes_sc[jj, p] = planes[p]
            live = live | planes[p]
        live_sc[jj] = live
        return carry

    lax.fori_loop(0, npair, plane_body, 0)

    n_pairs_max = keys_sc.shape[0] // 2

    def bit_body(bi, carry):
        thr_u, alive, above = carry
        cnt = jnp.zeros((SUBLANES, PAGE), I32)
        with_bit, at_least = [], []
        for p in range(n_pairs_max):
            t = alive[p] & planes_sc[p, bi]
            ge = above[p] | t
            cnt = cnt + lax.population_count(ge)
            with_bit.append(t)
            at_least.append(ge)
        take = jnp.sum(cnt.astype(F32), axis=0, keepdims=True) >= kprime
        alive = tuple(jnp.where(take, t, a ^ t) for t, a in zip(with_bit, alive))
        above = tuple(jnp.where(take, g, ge) for g, ge in zip(above, at_least))
        thr_u = thr_u | jnp.where(take, lax.shift_left(jnp.int32(1), 31 - bi), 0)
        return thr_u, alive, above

    zero_words = jnp.zeros((SUBLANES, PAGE), I32)
    thr_u, _, _ = lax.fori_loop(
        0, 32, bit_body,
        (jnp.zeros((1, PAGE), I32),
         tuple(jnp.where(p < npair, live_sc[p], zero_words) for p in range(n_pairs_max)),
         (zero_words,) * n_pairs_max))
    def count_scores(preds):
        def body(jj, cnts):
            for blk in range(2):
                sc = scores_sc[2 * jj + blk]
                cnts = tuple(c + jnp.where(p(sc), 1.0, 0.0) for c, p in zip(cnts, preds))
            return cnts
        cnts = lax.fori_loop(0, npair, body, tuple(jnp.zeros((PAGE, PAGE), F32) for _ in preds))
        return [jnp.sum(c, axis=0, keepdims=True) for c in cnts]

    def count_ge(thresholds):
        return count_scores([lambda sc, x=x: sc >= x for x in thresholds])

    thr = _key_to_float(_bracket_kth_largest(count_ge, kprime, thr_u ^ INT_MIN))
    need = kprime - count_scores([lambda sc: sc > thr])[0]

    tri = jnp.where(srow > tcol, 1.0, 0.0).astype(BF16)

    def mask_block(j, carry):
        sc = scores_sc[j]
        eqf = jnp.where(sc == thr, 1.0, 0.0)
        rank = jnp.dot(tri, eqf.astype(BF16), preferred_element_type=F32) + carry
        tie = jnp.where(jnp.logical_and(sc == thr, rank < need), 0.0, NEG)
        mb_sc[j] = jnp.where(sc > thr, 0.0, tie)
        return carry + jnp.sum(eqf, axis=0, keepdims=True)

    def mask_body(jj, carry):
        for u in range(4):
            carry = mask_block(4 * jj + u, carry)
        return carry

    lax.fori_loop(0, nquad, mask_body, jnp.zeros((1, PAGE), F32))

    groups = range(ATT_KV_HEADS)
    width = ATT_GROUP * PAGE
    qcats = [jnp.concatenate(
        [qT_ref[(ATT_GROUP * g + r) * HEAD_DIM:(ATT_GROUP * g + r + 1) * HEAD_DIM, :] for r in range(ATT_GROUP)],
        axis=1) for g in groups]

    def logit_block(j, ms):
        off = pl.multiple_of(j * PAGE, PAGE)
        mb = mb_sc[j]
        relc = jnp.clip(i - j, 0, 2)
        out = []
        for g in groups:
            kj = k_ref[pl.ds(off, PAGE), g * HEAD_DIM:(g + 1) * HEAD_DIM]
            logits = jnp.dot(kj, qcats[g], preferred_element_type=F32)
            logits = jnp.concatenate(
                [logits[:, r * PAGE:(r + 1) * PAGE] + (mb + bias_sc[ATT_GROUP * g + r, relc])
                 for r in range(ATT_GROUP)], axis=1)
            logit_sc[g, j] = logits
            out.append(jnp.maximum(ms[g], jnp.max(logits, axis=0, keepdims=True)))
        return tuple(out)

    ms = lax.fori_loop(0, npair, lambda jj, c: logit_block(2 * jj + 1, logit_block(2 * jj, c)),
                       tuple(jnp.full((1, width), NEG, F32) for _ in groups))

    def value_block(j, ls):
        contrib, out = [], []
        for g in groups:
            p = jnp.exp(logit_sc[g, j] - ms[g])
            out.append(ls[g] + jnp.sum(p, axis=0, keepdims=True))
            vTj = vT_ref[j, g * HEAD_DIM:(g + 1) * HEAD_DIM, :]
            contrib.append(jnp.dot(vTj, p.astype(BF16), preferred_element_type=F32))
        return contrib, tuple(out)

    def value_body(jj, ls):
        c0, ls = value_block(2 * jj, ls)
        c1, ls = value_block(2 * jj + 1, ls)
        for g in groups:
            acc_sc[g] = acc_sc[g] + (c0[g] + c1[g])
        return ls

    acc_sc[...] = jnp.zeros_like(acc_sc)
    ls = lax.fori_loop(0, npair, value_body, tuple(jnp.zeros((1, width), F32) for _ in groups))
    for g in groups:
        outT = acc_sc[g] * (1.0 / ls[g])
        for r in range(ATT_GROUP):
            hh = ATT_GROUP * g + r
            o_ref[:, hh * HEAD_DIM:(hh + 1) * HEAD_DIM] = outT[:, r * PAGE:(r + 1) * PAGE].T.astype(BF16)


def _dsa_prompt(rel_bias, qT, qiT, wiT, kb, vT3, kib, *, batch, seq):
    nb = seq // PAGE
    rows = batch * seq
    kvw = ATT_KV_HEADS * HEAD_DIM
    qcol = lambda height: pl.BlockSpec((height, PAGE), lambda b, i: (0, b * nb + i))
    return pl.pallas_call(
        functools.partial(_dsa_prompt_kernel, topk=min(TOPK_MAX, seq // 4)),
        grid=(batch, nb),
        in_specs=[pl.BlockSpec(memory_space=pltpu.SMEM),
                  qcol(ATT_HEADS * HEAD_DIM), qcol(IDX_HEADS * IDX_DIM), qcol(8),
                  pl.BlockSpec((seq, kvw), lambda b, i: (b, 0)),
                  pl.BlockSpec((nb, kvw, PAGE), lambda b, i: (b, 0, 0)),
                  pl.BlockSpec((seq, IDX_DIM), lambda b, i: (b, 0))],
        out_specs=pl.BlockSpec((PAGE, ATT_HEADS * HEAD_DIM), lambda b, i: (b * nb + i, 0)),
        out_shape=jax.ShapeDtypeStruct((rows, ATT_HEADS * HEAD_DIM), BF16),
        scratch_shapes=[pltpu.VMEM((nb, PAGE, PAGE), I32), pltpu.VMEM((nb, PAGE, PAGE), F32),
                        pltpu.VMEM((nb // 2, 32, SUBLANES, PAGE), I32), pltpu.VMEM((nb // 2, SUBLANES, PAGE), I32),
                        pltpu.VMEM((nb, PAGE, PAGE), F32),
                        pltpu.VMEM((ATT_HEADS, 3, PAGE, PAGE), F32),
                        pltpu.VMEM((ATT_KV_HEADS, nb, PAGE, ATT_GROUP * PAGE), F32),
                        pltpu.VMEM((ATT_KV_HEADS, HEAD_DIM, ATT_GROUP * PAGE), F32)],
        compiler_params=_params(2),
        name="dsa_prompt",
    )(rel_bias, qT, qiT, wiT, kb, vT3, kib)


TQ = 8
SEL_SAMPLES = 8


def _dsa_select_kernel(pt_ref, qi_ref, zl_ref, *rest, n_pages, n_new):
    ns = SEL_SAMPLES
    ki_pages = rest[:ns * n_pages]
    mb_ref, sc_sc = rest[ns * n_pages:]
    del pt_ref
    n_blk = n_pages + 1
    rows = ns * TQ
    trow = lax.broadcasted_iota(I32, (TQ, PAGE), 0)
    scol = lax.broadcasted_iota(I32, (TQ, PAGE), 1)
    pad_rows = lambda a: jnp.concatenate([a, jnp.zeros((PAGE - TQ, a.shape[1]), a.dtype)], axis=0)
    nt = (((1,), (1,)), ((), ()))

    for s in range(ns):
        zl = zl_ref[s]
        qi = qi_ref[s]
        for p in range(n_blk):
            if p == n_pages:
                ki_new = pad_rows(zl[:, :IDX_DIM]).astype(BF16)
                idx = lax.dot_general(qi, ki_new, nt, preferred_element_type=F32)
            else:
                idx = jnp.dot(qi, ki_pages[s * n_pages + p][...].astype(BF16), preferred_element_type=F32)
            sc = jnp.zeros((TQ, PAGE), F32)
            for hh in range(IDX_HEADS):
                sc = sc + zl[:, IDX_DIM + hh:IDX_DIM + hh + 1] * jnp.maximum(idx[hh * TQ:(hh + 1) * TQ], 0.0)
            sc = _canonical_zero(sc)
            if p == n_pages:
                sc = jnp.where(jnp.logical_and(scol <= trow, scol < n_new), sc, -jnp.inf)
            sc_sc[s * TQ:(s + 1) * TQ, p * PAGE:(p + 1) * PAGE] = sc

    kprime = float(min(TOPK_MAX, (n_pages * PAGE + n_new) // 4))

    def count(pred_of_scores):
        return jnp.sum(jnp.where(pred_of_scores(sc_sc[...]), 1.0, 0.0), axis=1, keepdims=True)

    cand0 = jnp.zeros((rows, 1), I32)
    thr0 = jnp.where(count(lambda sc: sc >= _key_to_float(cand0)) >= kprime, cand0, INT_MIN)

    def bit_body(bi, thr_key):
        cand = thr_key | lax.shift_left(jnp.int32(1), 30 - bi)
        return jnp.where(count(lambda sc: sc >= _key_to_float(cand)) >= kprime, cand, thr_key)

    thr = _key_to_float(lax.fori_loop(0, 31, bit_body, thr0))
    need = kprime - count(lambda sc: sc > thr)

    s_r = lax.broadcasted_iota(I32, (PAGE, PAGE), 0)
    s_c = lax.broadcasted_iota(I32, (PAGE, PAGE), 1)
    triu = jnp.where(s_r < s_c, 1.0, 0.0).astype(BF16)
    carry = jnp.zeros((rows, 1), F32)
    for p in range(n_blk):
        sc = sc_sc[:, p * PAGE:(p + 1) * PAGE]
        eqf = jnp.where(sc == thr, 1.0, 0.0)
        rank = jnp.dot(eqf.astype(BF16), triu, preferred_element_type=F32) + carry
        tie = jnp.where(jnp.logical_and(sc == thr, rank < need), 0.0, NEG)
        mb_ref[:, p * PAGE:(p + 1) * PAGE] = jnp.where(sc > thr, 0.0, tie)
        carry = carry + jnp.sum(eqf, axis=1, keepdims=True)


def _dsa_select(page_table, qi8, zl8, cache_ki, *, n_new):
    nsamp, n_pages = page_table.shape
    ns = SEL_SAMPLES
    width = (n_pages + 1) * PAGE

    def ki_page_spec(s, p):
        return pl.BlockSpec((None, IDX_DIM, PAGE), lambda b, pt: (pt[(b * ns + s) * n_pages + p], 0, 0))

    in_specs = ([pl.BlockSpec((ns, IDX_HEADS * TQ, IDX_DIM), lambda b, pt: (b, 0, 0)),
                 pl.BlockSpec((ns, TQ, LANES), lambda b, pt: (b, 0, 0))]
                + [ki_page_spec(s, p) for s in range(ns) for p in range(n_pages)])
    grid_spec = pltpu.PrefetchScalarGridSpec(
        num_scalar_prefetch=1, grid=(nsamp // ns,), in_specs=in_specs,
        out_specs=pl.BlockSpec((ns * TQ, width), lambda b, pt: (b, 0)),
        scratch_shapes=[pltpu.VMEM((ns * TQ, width), F32)])
    return pl.pallas_call(
        functools.partial(_dsa_select_kernel, n_pages=n_pages, n_new=n_new),
        grid_spec=grid_spec,
        out_shape=jax.ShapeDtypeStruct((nsamp * TQ, width), F32),
        compiler_params=_params(1),
        name="dsa_select",
    )(page_table.reshape(-1), qi8, zl8, *([cache_ki] * (ns * n_pages)))


def _dsa_sample_kernel(pt_ref, relb_ref, q_ref, mb_ref, kn_ref, vn_ref, *rest, n_pages):
    k_pages = rest[:n_pages]
    v_pages = rest[n_pages:2 * n_pages]
    o_ref, bias_sc, kall_sc, vall_sc = rest[2 * n_pages:]
    del pt_ref
    past = n_pages * PAGE
    rows_g = ATT_GROUP * TQ

    trow = lax.broadcasted_iota(I32, (TQ, PAGE), 0)
    scol = lax.broadcasted_iota(I32, (TQ, PAGE), 1)

    @pl.when(pl.program_id(0) == 0)
    def _init_bias():
        for g in range(ATT_KV_HEADS):
            for r in range(ATT_GROUP):
                hh = ATT_GROUP * g + r
                rs = slice(r * TQ, (r + 1) * TQ)
                far = jnp.full((TQ, PAGE), relb_ref[NUM_BUCKETS - 1, hh], F32)
                for p in range(n_pages - 1):
                    bias_sc[g, rs, p * PAGE:(p + 1) * PAGE] = far
                bias_sc[g, rs, past - PAGE:past] = _bias_from_distance(
                    relb_ref, hh, jnp.maximum(PAGE + trow - scol, 0))
                bias_sc[g, rs, past:past + PAGE] = _bias_from_distance(relb_ref, hh, jnp.maximum(trow - scol, 0))

    nt = (((1,), (1,)), ((), ()))
    mb = jnp.concatenate([mb_ref[...]] * ATT_GROUP, axis=0)
    pad_rows = lambda a: jnp.concatenate([a, jnp.zeros((PAGE - TQ, a.shape[1]), a.dtype)], axis=0)
    for g in range(ATT_KV_HEADS):
        gs = slice(g * HEAD_DIM, (g + 1) * HEAD_DIM)
        for p in range(n_pages):
            ps = slice(p * PAGE, (p + 1) * PAGE)
            kall_sc[g, ps, :] = k_pages[p][pl.ds(g, PAGE, stride=ATT_KV_HEADS), :].astype(BF16)
            vall_sc[g, ps, :] = v_pages[p][pl.ds(g, PAGE, stride=ATT_KV_HEADS), :].astype(BF16)
        kall_sc[g, past:past + PAGE, :] = pad_rows(kn_ref[:, gs]).astype(BF16)
        vall_sc[g, past:past + PAGE, :] = pad_rows(vn_ref[:, gs]).astype(BF16)
    for g in range(ATT_KV_HEADS):
        qg = q_ref[g * rows_g:(g + 1) * rows_g, :]
        logits = lax.dot_general(qg, kall_sc[g], nt, preferred_element_type=F32)
        logits = logits + (mb + bias_sc[g])
        m = jnp.max(logits, axis=1, keepdims=True)
        pr = jnp.exp(logits - m)
        l = jnp.sum(pr, axis=1, keepdims=True)
        acc = jnp.dot(pr.astype(BF16), vall_sc[g], preferred_element_type=F32)
        o_ref[g * rows_g:(g + 1) * rows_g, :] = (acc * (1.0 / l)).astype(BF16)


def _dsa_sample(page_table, rel_bias, q8, mb, kn8, vn8, cache_k, cache_v):
    nsamp, n_pages = page_table.shape
    kvw = ATT_KV_HEADS * HEAD_DIM
    per_sample = lambda *tail: pl.BlockSpec((None,) + tail, lambda b, pt: (b,) + (0,) * len(tail))

    def kv_page_spec(p):
        return pl.BlockSpec((PAGE * ATT_KV_HEADS, HEAD_DIM), lambda b, pt: (pt[b * n_pages + p], 0))

    in_specs = ([pl.BlockSpec(memory_space=pltpu.SMEM),
                 per_sample(ATT_HEADS * TQ, HEAD_DIM),
                 pl.BlockSpec((TQ, mb.shape[1]), lambda b, pt: (b, 0)),
                 per_sample(TQ, kvw), per_sample(TQ, kvw)]
                + [kv_page_spec(p) for p in range(n_pages)]
                + [kv_page_spec(p) for p in range(n_pages)])
    grid_spec = pltpu.PrefetchScalarGridSpec(
        num_scalar_prefetch=1, grid=(nsamp,), in_specs=in_specs,
        out_specs=per_sample(ATT_HEADS * TQ, HEAD_DIM),
        scratch_shapes=[pltpu.VMEM((ATT_KV_HEADS, ATT_GROUP * TQ, mb.shape[1]), F32),
                        pltpu.VMEM((ATT_KV_HEADS, mb.shape[1], HEAD_DIM), BF16),
                        pltpu.VMEM((ATT_KV_HEADS, mb.shape[1], HEAD_DIM), BF16)])
    return pl.pallas_call(
        functools.partial(_dsa_sample_kernel, n_pages=n_pages),
        grid_spec=grid_spec,
        out_shape=jax.ShapeDtypeStruct((nsamp, ATT_HEADS * TQ, HEAD_DIM), BF16),
        compiler_params=_params(1),
        name="dsa_sample",
    )(page_table.reshape(-1), rel_bias, q8, mb, kn8, vn8, *([cache_k] * n_pages), *([cache_v] * n_pages))


def _post_mix_kernel(x_ref, o_ref, wo_ref, g_ref, w1_ref, w2_ref, y_ref):
    y = x_ref[...] + jnp.dot(o_ref[...], wo_ref[...], preferred_element_type=F32)
    h = (_rms_rows(y) * g_ref[...]).astype(BF16)
    acc = y
    for c in range(D_FF // D_MODEL):
        a = jnp.dot(h, w1_ref[:, c * D_MODEL:(c + 1) * D_MODEL], preferred_element_type=F32)
        a = jnp.square(jnp.maximum(a, 0.0)).astype(BF16)
        acc = acc + jnp.dot(a, w2_ref[c * D_MODEL:(c + 1) * D_MODEL, :], preferred_element_type=F32)
    y_ref[...] = acc


def _post_mix(x, o, wo, g, w1, w2):
    rows = x.shape[0]
    tm = ROW_TILE
    ko = o.shape[1]
    return pl.pallas_call(
        _post_mix_kernel,
        grid=(rows // tm,),
        in_specs=[pl.BlockSpec((tm, D_MODEL), lambda i: (i, 0)), pl.BlockSpec((tm, ko), lambda i: (i, 0)),
                  _const_spec((ko, D_MODEL)), _const_spec((1, D_MODEL)),
                  _const_spec((D_MODEL, D_FF)), _const_spec((D_FF, D_MODEL))],
        out_specs=pl.BlockSpec((tm, D_MODEL), lambda i: (i, 0)),
        out_shape=jax.ShapeDtypeStruct((rows, D_MODEL), F32),
        compiler_params=_params(1),
        name="post_mix",
    )(x, o, wo, g, w1, w2)


def _ret_proj_kernel(x_ref, g_ref, w_ref, cos_ref, sin_ref, q_ref, k_ref, v_ref, gate_ref):
    hb = (_rms_rows(x_ref[...]) * g_ref[...]).astype(BF16)
    cos = cos_ref[...]
    sin = sin_ref[...]
    half = RET_KEY // 2
    qk = RET_HEADS * RET_KEY
    vd = RET_HEADS * RET_VAL

    def rotated(col0, scale):
        z = jnp.dot(hb, w_ref[:, col0:col0 + RET_KEY], preferred_element_type=F32)
        x1, x2 = z[:, :half], z[:, half:]
        return (x1 * cos - x2 * sin) * scale, (x1 * sin + x2 * cos) * scale

    for hh in range(RET_HEADS):
        a, b = rotated(hh * RET_KEY, 1.0)
        q_ref[:, hh * RET_KEY:hh * RET_KEY + half] = a
        q_ref[:, hh * RET_KEY + half:(hh + 1) * RET_KEY] = b
        a, b = rotated(qk + hh * RET_KEY, RET_KSCALE)
        k_ref[:, hh * RET_KEY:hh * RET_KEY + half] = a
        k_ref[:, hh * RET_KEY + half:(hh + 1) * RET_KEY] = b
    for hh in range(RET_HEADS):
        cs = slice(hh * RET_VAL, (hh + 1) * RET_VAL)
        v_ref[:, cs] = jnp.dot(hb, w_ref[:, 2 * qk + hh * RET_VAL:2 * qk + (hh + 1) * RET_VAL],
                               preferred_element_type=F32).astype(BF16)
        gate_ref[:, cs] = jnp.dot(hb, w_ref[:, 2 * qk + vd + hh * RET_VAL:2 * qk + vd + (hh + 1) * RET_VAL],
                                  preferred_element_type=F32)


def _ret_proj(x, g, w, cos, sin):
    rows = x.shape[0]
    tm = ROW_TILE
    qk = RET_HEADS * RET_KEY
    vd = RET_HEADS * RET_VAL
    n_pos_tiles = cos.shape[0] // tm
    row_spec = lambda width: pl.BlockSpec((tm, width), lambda i: (i, 0))
    pos_spec = pl.BlockSpec((tm, RET_KEY // 2), lambda i: (i % n_pos_tiles, 0))
    return pl.pallas_call(
        _ret_proj_kernel,
        grid=(rows // tm,),
        in_specs=[row_spec(D_MODEL), _const_spec((1, D_MODEL)), _const_spec((D_MODEL, 2 * qk + 2 * vd)),
                  pos_spec, pos_spec],
        out_specs=[row_spec(qk), row_spec(qk), row_spec(vd), row_spec(vd)],
        out_shape=[jax.ShapeDtypeStruct((rows, qk), F32), jax.ShapeDtypeStruct((rows, qk), F32),
                   jax.ShapeDtypeStruct((rows, vd), BF16), jax.ShapeDtypeStruct((rows, vd), F32)],
        compiler_params=_params(1),
        name="ret_proj",
    )(x, g, w, cos, sin)


def _retention_kernel(cd_ref, q_ref, k_ref, v_ref, g_ref, dm_ref, qd_ref, kd_ref, *rest, has_state0):
    if has_state0:
        s0_ref, o_ref, s_ref = rest
    else:
        o_ref, s_ref = rest

    @pl.when(pl.program_id(1) == 0)
    def _init_state():
        s_ref[...] = s0_ref[...] if has_state0 else jnp.zeros_like(s_ref)

    nt = (((1,), (1,)), ((), ()))
    tn = (((0,), (0,)), ((), ()))
    units = [(bi, hh) for bi in range(q_ref.shape[0]) for hh in range(RET_HEADS)]
    ks = lambda hh: slice(hh * RET_KEY, (hh + 1) * RET_KEY)
    vs = lambda hh: slice(hh * RET_VAL, (hh + 1) * RET_VAL)
    att = [lax.dot_general(q_ref[bi, :, ks(hh)].astype(BF16), k_ref[bi, :, ks(hh)].astype(BF16), nt,
                           preferred_element_type=F32) * dm_ref[hh] for bi, hh in units]
    cross = [jnp.dot((q_ref[bi, :, ks(hh)] * qd_ref[hh]).astype(BF16), s_ref[bi, hh].astype(BF16),
                     preferred_element_type=F32) for bi, hh in units]
    for (bi, hh), a, x in zip(units, att, cross):
        o = jnp.dot(a.astype(BF16), v_ref[bi, :, vs(hh)], preferred_element_type=F32) + x
        gate = g_ref[bi, :, vs(hh)]
        o_ref[bi, :, vs(hh)] = (_rms_rows(o) * (gate * jax.nn.sigmoid(gate))).astype(BF16)
    for bi, hh in units:
        kd = (k_ref[bi, :, ks(hh)] * kd_ref[hh]).astype(BF16)
        s_ref[bi, hh] = cd_ref[hh] * s_ref[bi, hh] + lax.dot_general(kd, v_ref[bi, :, vs(hh)], tn,
                                                                     preferred_element_type=F32)


def _retention(q, k, v, gate, consts, state0, *, seqs_per_step):
    dm, qd, kd, cd = consts
    nb, t, _ = q.shape
    c = dm.shape[1]
    bb = seqs_per_step
    qk = RET_HEADS * RET_KEY
    vd = RET_HEADS * RET_VAL
    tok = lambda width: pl.BlockSpec((bb, c, width), lambda b, ci: (b, ci, 0))
    st_spec = pl.BlockSpec((bb, RET_HEADS, RET_KEY, RET_VAL), lambda b, ci: (b, 0, 0, 0))
    in_specs = [pl.BlockSpec(memory_space=pltpu.SMEM), tok(qk), tok(qk), tok(vd), tok(vd),
                _const_spec(dm.shape), _const_spec(qd.shape), _const_spec(kd.shape)]
    args = [cd, q, k, v, gate, dm, qd, kd]
    if state0 is not None:
        in_specs.append(st_spec)
        args.append(state0)
    return pl.pallas_call(
        functools.partial(_retention_kernel, has_state0=state0 is not None),
        grid=(nb // bb, t // c),
        in_specs=in_specs,
        out_specs=[tok(vd), st_spec],
        out_shape=[jax.ShapeDtypeStruct((nb, t, vd), BF16),
                   jax.ShapeDtypeStruct((nb, RET_HEADS, RET_KEY, RET_VAL), F32)],
        compiler_params=_params(2),
        name="retention_s" if state0 is not None else "retention_p",
    )(*args)


def _retention_consts(c, c_pad):
    log_g = jnp.log1p(-jnp.exp2(-5.0 - jnp.arange(RET_HEADS, dtype=F32)))
    i = jnp.arange(c, dtype=F32)
    diff = i[:, None] - i[None, :]
    dm = jnp.where(diff >= 0, jnp.exp(log_g[:, None, None] * jnp.maximum(diff, 0.0)), 0.0)
    qd = jnp.exp(log_g[:, None] * (i[None, :] + 1.0))
    kd = jnp.exp(log_g[:, None] * (c - 1.0 - i[None, :]))
    cd = jnp.exp(log_g * c)
    pad = c_pad - c
    dm = jnp.pad(dm, ((0, 0), (0, pad), (0, pad)))
    qd = jnp.broadcast_to(jnp.pad(qd, ((0, 0), (0, pad)))[:, :, None], (RET_HEADS, c_pad, RET_KEY))
    kd = jnp.broadcast_to(jnp.pad(kd, ((0, 0), (0, pad)))[:, :, None], (RET_HEADS, c_pad, RET_KEY))
    return dm, qd, kd, cd


def _rotation_tables(pos):
    half = RET_KEY // 2
    theta = 1.0 / (10000.0 ** jnp.linspace(0.0, 1.0, half, dtype=F32))
    ang = pos.astype(F32)[:, None] * theta[None, :]
    return jnp.cos(ang), jnp.sin(ang)


def kernel(x_prompt, x_sample, cache_k, cache_v, cache_kidx, state_ret, page_table, rel_bias, ln_mix, ln_mlp,
           att_w_in, att_q_gain, att_k_gain, att_w_out, ret_w_in, ret_w_out, mlp_w_in, mlp_w_out):
    bp, tp, _ = x_prompt.shape
    bs, ts, _ = x_sample.shape
    n_phys = cache_k.shape[1]
    assert ln_mix.shape[0] == 2 and att_w_in.shape[0] == 1 and ret_w_in.shape[0] == 1
    assert tp % RET_CHUNK == 0 and ts <= TQ and cache_k.shape[2] == PAGE
    assert (tp // PAGE) % 4 == 0 and bs % SEL_SAMPLES == 0
    kvw = ATT_KV_HEADS * HEAD_DIM
    past = page_table.shape[1] * PAGE

    w_att = jnp.pad(att_w_in[0], ((0, 0), (0, ATT_IN_PAD - ATT_IN))).astype(BF16)
    w_att_out = att_w_out[0].astype(BF16)
    w_ret = ret_w_in[0].astype(BF16)
    w_ret_out = ret_w_out[0].astype(BF16)
    w1 = mlp_w_in.astype(BF16)
    w2 = mlp_w_out.astype(BF16)
    row = lambda a: a.reshape(1, -1)

    xp = x_prompt.reshape(bp * tp, D_MODEL)
    xs = x_sample.reshape(bs * ts, D_MODEL)

    qT, qiT, wiT, vT3, kb, kib, kp, vp, kip = _attn_proj(
        xp, row(ln_mix[0]), w_att, row(att_q_gain[0]), row(att_k_gain[0]), transposed=True)
    op = _dsa_prompt(rel_bias, qT, qiT, wiT, kb, vT3, kib, batch=bp, seq=tp)
    yp = _post_mix(xp, op, w_att_out, row(ln_mlp[0]), w1[0], w2[0])

    qs, qis, zls, ks, vs, kis = _attn_proj(
        xs, row(ln_mix[0]), w_att, row(att_q_gain[0]), row(att_k_gain[0]), transposed=False)
    pad_t = lambda a: jnp.pad(a, ((0, 0), (0, TQ - ts)) + ((0, 0),) * (a.ndim - 2))
    q8 = pad_t(qs.reshape(bs, ts, ATT_HEADS, HEAD_DIM)).transpose(0, 2, 1, 3)
    q8 = q8.reshape(bs, ATT_HEADS * TQ, HEAD_DIM).astype(BF16)
    qi8 = pad_t(qis.reshape(bs, ts, IDX_HEADS, IDX_DIM)).transpose(0, 2, 1, 3)
    qi8 = qi8.reshape(bs, IDX_HEADS * TQ, IDX_DIM).astype(BF16)
    zl8 = pad_t(zls.reshape(bs, ts, LANES))
    kn8 = pad_t(ks.reshape(bs, ts, kvw))
    vn8 = pad_t(vs.reshape(bs, ts, kvw))
    mb = _dsa_select(page_table, qi8, zl8, jnp.swapaxes(cache_kidx[0], 1, 2), n_new=ts)
    os8 = _dsa_sample(page_table, rel_bias, q8, mb, kn8, vn8,
                      cache_k[0].reshape(n_phys * PAGE * ATT_KV_HEADS, HEAD_DIM),
                      cache_v[0].reshape(n_phys * PAGE * ATT_KV_HEADS, HEAD_DIM))
    os_ = os8.reshape(bs, ATT_HEADS, TQ, HEAD_DIM)[:, :, :ts].transpose(0, 2, 1, 3)
    os_ = os_.reshape(bs * ts, ATT_HEADS * HEAD_DIM)
    ys = _post_mix(xs, os_, w_att_out, row(ln_mlp[0]), w1[0], w2[0])

    cos_p, sin_p = _rotation_tables(jnp.arange(tp, dtype=jnp.int32))
    q, k, v, gate = _ret_proj(yp, row(ln_mix[1]), w_ret, cos_p, sin_p)
    r3 = lambda a: a.reshape(bp, tp, a.shape[-1])
    orp, state_p = _retention(r3(q), r3(k), r3(v), r3(gate), _retention_consts(RET_CHUNK, RET_CHUNK), None,
                            seqs_per_step=1)
    yp = _post_mix(yp, orp.reshape(bp * tp, -1), w_ret_out, row(ln_mlp[1]), w1[1], w2[1])

    cos_s, sin_s = _rotation_tables(past + jnp.arange(ts, dtype=jnp.int32))
    tile_s = lambda a: jnp.tile(a, (bs, 1))
    q, k, v, gate = _ret_proj(ys, row(ln_mix[1]), w_ret, tile_s(cos_s), tile_s(sin_s))
    r3s = lambda a: pad_t(a.reshape(bs, ts, a.shape[-1]))
    ors, state_s = _retention(r3s(q), r3s(k), r3s(v), r3s(gate), _retention_consts(ts, TQ), state_ret[0],
                            seqs_per_step=2)
    ys = _post_mix(ys, ors[:, :ts].reshape(bs * ts, -1), w_ret_out, row(ln_mlp[1]), w1[1], w2[1])

    return (yp.reshape(bp, tp, D_MODEL), ys.reshape(bs, ts, D_MODEL),
            kp.reshape(1, bp, tp, ATT_KV_HEADS, HEAD_DIM), vp.reshape(1, bp, tp, ATT_KV_HEADS, HEAD_DIM),
            kip.reshape(1, bp, tp, IDX_DIM), state_p[None],
            ks.reshape(1, bs, ts, ATT_KV_HEADS, HEAD_DIM), vs.reshape(1, bs, ts, ATT_KV_HEADS, HEAD_DIM),
            kis.reshape(1, bs, ts, IDX_DIM), state_s[None])
```

```python
import functools
import math

import jax
import jax.numpy as jnp
import numpy as np
from jax import lax
from jax.experimental import pallas as pl
from jax.experimental.pallas import tpu as pltpu

F32 = jnp.float32
BF16 = jnp.bfloat16
I32 = jnp.int32

D_MODEL = 1024
PAGE = 128
ATT_HEADS = 8
ATT_KV_HEADS = 2
ATT_GROUP = ATT_HEADS // ATT_KV_HEADS
HEAD_DIM = 128
IDX_HEADS = 4
IDX_DIM = 64
TOPK_MAX = 256
NUM_BUCKETS = 32
MAX_DISTANCE = 128
RET_HEADS = 4
RET_KEY = 256
RET_VAL = 512
RET_CHUNK = 256
D_FF = 4 * D_MODEL
EPS = 1e-6

Q_OFF = 0
K_OFF = ATT_HEADS * HEAD_DIM
V_OFF = K_OFF + ATT_KV_HEADS * HEAD_DIM
QI_OFF = V_OFF + ATT_KV_HEADS * HEAD_DIM
KI_OFF = QI_OFF + IDX_HEADS * IDX_DIM
WI_OFF = KI_OFF + IDX_DIM
ATT_IN = WI_OFF + IDX_HEADS
ATT_IN_PAD = 1920

LANES = 128
SUBLANES = 8
ROW_TILE = 512
VMEM_LIMIT = 56 * 1024 * 1024

ATT_SCALE = HEAD_DIM ** -0.5
RET_KSCALE = RET_KEY ** -0.5
INT_MIN = -(2 ** 31)
NEG = -1e30


def _bucket_bounds():
    n = np.arange(0, 8192)
    max_exact = NUM_BUCKETS // 2
    val = np.log(np.maximum(n, 1) / max_exact) / math.log(MAX_DISTANCE / max_exact) * (NUM_BUCKETS - max_exact)
    frac = np.abs(val - np.round(val))
    risky = (n > max_exact) & (n < MAX_DISTANCE) & (frac < 1e-4)
    assert not risky.any()
    large = np.minimum(max_exact + np.floor(val + 1e-9).astype(np.int64), NUM_BUCKETS - 1)
    bucket = np.where(n < max_exact, n, large)
    assert (np.diff(bucket) >= 0).all() and (bucket[MAX_DISTANCE:] == NUM_BUCKETS - 1).all()
    return [int(np.argmax(bucket >= b)) for b in range(NUM_BUCKETS)]


BUCKET_BOUNDS = _bucket_bounds()


def _const_spec(shape):
    zeros = (0,) * len(shape)
    return pl.BlockSpec(shape, lambda *_: zeros, pipeline_mode=pl.Buffered(1))


def _params(n_axes):
    return pltpu.CompilerParams(dimension_semantics=("arbitrary",) * n_axes,
                                vmem_limit_bytes=VMEM_LIMIT)


def _rms_rows(a):
    return a * lax.rsqrt(jnp.mean(a * a, axis=-1, keepdims=True) + EPS)


def _canonical_zero(score):
    return jnp.where(score == 0.0, 0.0, score)


def _sortable_key(score):
    bits = pltpu.bitcast(score, I32)
    return bits ^ ((bits >> 31) & jnp.int32(0x7FFFFFFF))


def _key_to_float(key):
    return pltpu.bitcast(key ^ ((key >> 31) & jnp.int32(0x7FFFFFFF)), F32)


KEY_NEG_INF = -2139095041
KEY_POS_INF = 2139095040
KEY_SUBNORMAL_LO = -8388608
KEY_SUBNORMAL_HI = 8388607


def _bracket_kth_largest(count_ge, kprime, guess):
    def canon(x):
        return jnp.where(jnp.logical_and(x >= KEY_SUBNORMAL_LO, x <= KEY_SUBNORMAL_HI), 0, x)

    def succ(x):
        xc = canon(x)
        return jnp.where(xc == 0, KEY_SUBNORMAL_HI + 1, canon(xc + 1))

    g0 = jnp.clip(guess, KEY_NEG_INF, KEY_POS_INF - 1)
    g1 = succ(g0)
    lo = jnp.full(guess.shape, KEY_NEG_INF, I32)
    hi = jnp.full(guess.shape, KEY_POS_INF, I32)

    def narrow(lo, hi, x, cnt):
        ok = cnt >= kprime
        return jnp.where(ok, jnp.maximum(lo, x), lo), jnp.where(ok, hi, jnp.minimum(hi, x))

    c0, c1 = count_ge([_key_to_float(g0), _key_to_float(g1)])
    lo, hi = narrow(lo, hi, g0, c0)
    lo, hi = narrow(lo, hi, g1, c1)

    def open_bracket(c):
        return jnp.max(jnp.where(canon(c[1]) > succ(c[0]), 1.0, 0.0)) > 0.5

    def bisect(c):
        lo, hi = c
        mid = jnp.where(canon(hi) > succ(lo), (lo >> 1) + (hi >> 1) + (lo & hi & 1), lo)
        return narrow(lo, hi, mid, count_ge([_key_to_float(mid)])[0])

    lo, _ = lax.while_loop(open_bracket, bisect, (lo, hi))
    return canon(lo)


def _bit_transpose32(words):
    a = list(words)
    j, m = 16, 0x0000FFFF
    while j:
        k = 0
        while k < 32:
            t = (a[k] ^ lax.shift_right_logical(a[k + j], jnp.int32(j))) & jnp.int32(m)
            a[k] = a[k] ^ t
            a[k + j] = a[k + j] ^ lax.shift_left(t, jnp.int32(j))
            k = (k + j + 1) & ~j
        j >>= 1
        m ^= (m << j) & 0xFFFFFFFF
    return a


def _bias_from_distance(relb_ref, head, dist):
    val = jnp.full(dist.shape, relb_ref[0, head], F32)
    for b in range(1, NUM_BUCKETS):
        val = jnp.where(dist >= BUCKET_BOUNDS[b], relb_ref[b, head], val)
    return val


def _attn_proj_kernel(x_ref, g_ref, w_ref, qg_ref, kg_ref, *outs, transposed):
    x = x_ref[...]
    h = _rms_rows(x) * g_ref[...]
    z = jnp.dot(h.astype(BF16), w_ref[...], preferred_element_type=F32)
    qg = qg_ref[...]
    kg = kg_ref[...]
    k = jnp.concatenate(
        [_rms_rows(z[:, K_OFF + g * HEAD_DIM:K_OFF + (g + 1) * HEAD_DIM]) * kg for g in range(ATT_KV_HEADS)],
        axis=1)
    v = z[:, V_OFF:QI_OFF]
    qi = z[:, QI_OFF:KI_OFF]
    zl = z[:, KI_OFF:ATT_IN_PAD]
    ki = zl[:, :IDX_DIM]
    if transposed:
        qT_ref, qiT_ref, wiT_ref, vT_ref, kb_ref, kib_ref, k_ref, v_ref, ki_ref = outs
        for hh in range(ATT_HEADS):
            qh = _rms_rows(z[:, hh * HEAD_DIM:(hh + 1) * HEAD_DIM]) * qg * ATT_SCALE
            qT_ref[hh * HEAD_DIM:(hh + 1) * HEAD_DIM, :] = qh.T.astype(BF16)
        for c in range(IDX_HEADS * IDX_DIM // LANES):
            qiT_ref[c * LANES:(c + 1) * LANES, :] = qi[:, c * LANES:(c + 1) * LANES].T.astype(BF16)
        wiT_ref[...] = zl.T[IDX_DIM:IDX_DIM + 8, :]
        for g in range(ATT_KV_HEADS):
            vT = v[:, g * HEAD_DIM:(g + 1) * HEAD_DIM].T
            for c in range(x.shape[0] // PAGE):
                vT_ref[c, g * HEAD_DIM:(g + 1) * HEAD_DIM, :] = vT[:, c * PAGE:(c + 1) * PAGE].astype(BF16)
        kb_ref[...] = k.astype(BF16)
        kib_ref[...] = ki.astype(BF16)
    else:
        q_ref, qi_ref, zl_ref, k_ref, v_ref, ki_ref = outs
        for hh in range(ATT_HEADS):
            q_ref[:, hh * HEAD_DIM:(hh + 1) * HEAD_DIM] = _rms_rows(z[:, hh * HEAD_DIM:(hh + 1) * HEAD_DIM]) * qg * ATT_SCALE
        qi_ref[...] = qi
        zl_ref[...] = zl
    k_ref[...] = k
    v_ref[...] = v
    ki_ref[...] = ki


def _attn_proj(x, g, w, qg, kg, *, transposed):
    rows = x.shape[0]
    tm = ROW_TILE
    nt = rows // tm
    kvw = ATT_KV_HEADS * HEAD_DIM
    row_spec = lambda width: pl.BlockSpec((tm, width), lambda i: (i, 0))
    col_spec = lambda height: pl.BlockSpec((height, tm), lambda i: (0, i))
    leaves_shape = [jax.ShapeDtypeStruct((rows, kvw), F32), jax.ShapeDtypeStruct((rows, kvw), F32),
                    jax.ShapeDtypeStruct((rows, IDX_DIM), F32)]
    leaves_spec = [row_spec(kvw), row_spec(kvw), row_spec(IDX_DIM)]
    if transposed:
        out_shape = [jax.ShapeDtypeStruct((ATT_HEADS * HEAD_DIM, rows), BF16),
                     jax.ShapeDtypeStruct((IDX_HEADS * IDX_DIM, rows), BF16),
                     jax.ShapeDtypeStruct((8, rows), F32),
                     jax.ShapeDtypeStruct((rows // PAGE, kvw, PAGE), BF16),
                     jax.ShapeDtypeStruct((rows, kvw), BF16),
                     jax.ShapeDtypeStruct((rows, IDX_DIM), BF16)] + leaves_shape
        out_specs = [col_spec(ATT_HEADS * HEAD_DIM), col_spec(IDX_HEADS * IDX_DIM), col_spec(8),
                     pl.BlockSpec((tm // PAGE, kvw, PAGE), lambda i: (i, 0, 0)),
                     row_spec(kvw), row_spec(IDX_DIM)] + leaves_spec
    else:
        out_shape = [jax.ShapeDtypeStruct((rows, ATT_HEADS * HEAD_DIM), F32),
                     jax.ShapeDtypeStruct((rows, IDX_HEADS * IDX_DIM), F32),
                     jax.ShapeDtypeStruct((rows, LANES), F32)] + leaves_shape
        out_specs = [row_spec(ATT_HEADS * HEAD_DIM), row_spec(IDX_HEADS * IDX_DIM), row_spec(LANES)] + leaves_spec
    return pl.pallas_call(
        functools.partial(_attn_proj_kernel, transposed=transposed),
        grid=(nt,),
        in_specs=[row_spec(D_MODEL), _const_spec((1, D_MODEL)), _const_spec((D_MODEL, ATT_IN_PAD)),
                  _const_spec((1, HEAD_DIM)), _const_spec((1, HEAD_DIM))],
        out_specs=out_specs,
        out_shape=out_shape,
        compiler_params=_params(1),
        name="attn_proj_t" if transposed else "attn_proj_r",
    )(x, g, w, qg, kg)


def _dsa_prompt_kernel(relb_ref, qT_ref, qiT_ref, wiT_ref, k_ref, vT_ref, ki_ref, o_ref,
                       keys_sc, scores_sc, planes_sc, live_sc, mb_sc, bias_sc, logit_sc, acc_sc, *, topk):
    b = pl.program_id(0)
    i = pl.program_id(1)
    srow = lax.broadcasted_iota(I32, (PAGE, PAGE), 0)
    tcol = lax.broadcasted_iota(I32, (PAGE, PAGE), 1)

    @pl.when((b == 0) & (i == 0))
    def _init_bias():
        planes_sc[...] = jnp.zeros_like(planes_sc)
        live_sc[...] = jnp.zeros_like(live_sc)
        for hh in range(ATT_HEADS):
            for rel in range(2):
                dist = jnp.maximum(rel * PAGE + tcol - srow, 0)
                bias_sc[hh, rel] = _bias_from_distance(relb_ref, hh, dist)
            bias_sc[hh, 2] = jnp.full((PAGE, PAGE), relb_ref[NUM_BUCKETS - 1, hh], F32)

    wiT = wiT_ref[...]

    npair = (i + 2) // 2

    def score_block(j):
        off = pl.multiple_of(j * PAGE, PAGE)
        kij = ki_ref[pl.ds(off, PAGE), :]
        sc = jnp.zeros((PAGE, PAGE), F32)
        for hh in range(IDX_HEADS):
            idx = jnp.dot(kij, qiT_ref[hh * IDX_DIM:(hh + 1) * IDX_DIM, :], preferred_element_type=F32)
            sc = sc + wiT[hh:hh + 1, :] * jnp.maximum(idx, 0.0)
        sc = _canonical_zero(sc)
        causal = jnp.logical_or(j < i, jnp.logical_and(j == i, srow <= tcol))
        scores_sc[j] = jnp.where(causal, sc, -jnp.inf)
        keys_sc[j] = jnp.where(causal, _sortable_key(sc), INT_MIN)

    nquad = (i + 4) // 4

    def score_body(jj, carry):
        for u in range(4):
            score_block(4 * jj + u)
        return carry

    lax.fori_loop(0, nquad, score_body, 0)

    lane = lax.broadcasted_iota(I32, (1, PAGE), 1)
    kprime = jnp.minimum(topk, i * PAGE + lane + 1).astype(F32)

    def plane_body(jj, carry):
        rows = [keys_sc[2 * jj + blk, k * SUBLANES:(k + 1) * SUBLANES, :] ^ INT_MIN
                for blk in range(2) for k in range(PAGE // SUBLANES)]
        planes = _bit_transpose32(rows)
        live = planes[0]
        for p in range(32):
            planes_sc[jj, p] = planes[p]
            live = live | planes[p]
        live_sc[jj] = live
        return carry

    lax.fori_loop(0, npair, plane_body, 0)

    n_pairs_max = keys_sc.shape[0] // 2

    def bit_body(bi, carry):
        thr_u, alive, above = carry
        cnt = jnp.zeros((SUBLANES, PAGE), I32)
        with_bit, at_least = [], []
        for p in range(n_pairs_max):
            t = alive[p] & planes_sc[p, bi]
            ge = above[p] | t
            cnt = cnt + lax.population_count(ge)
            with_bit.append(t)
            at_least.append(ge)
        take = jnp.sum(cnt.astype(F32), axis=0, keepdims=True) >= kprime
        alive = tuple(jnp.where(take, t, a ^ t) for t, a in zip(with_bit, alive))
        above = tuple(jnp.where(take, g, ge) for g, ge in zip(above, at_least))
        thr_u = thr_u | jnp.where(take, lax.shift_left(jnp.int32(1), 31 - bi), 0)
        return thr_u, alive, above

    zero_words = jnp.zeros((SUBLANES, PAGE), I32)
    thr_u, _, _ = lax.fori_loop(
        0, 32, bit_body,
        (jnp.zeros((1, PAGE), I32),
         tuple(jnp.where(p < npair, live_sc[p], zero_words) for p in range(n_pairs_max)),
         (zero_words,) * n_pairs_max))
    def count_scores(preds):
        def body(jj, cnts):
            for blk in range(2):
                sc = scores_sc[2 * jj + blk]
                cnts = tuple(c + jnp.where(p(sc), 1.0, 0.0) for c, p in zip(cnts, preds))
            return cnts
        cnts = lax.fori_loop(0, npair, body, tuple(jnp.zeros((PAGE, PAGE), F32) for _ in preds))
        return [jnp.sum(c, axis=0, keepdims=True) for c in cnts]

    def count_ge(thresholds):
        return count_scores([lambda sc, x=x: sc >= x for x in thresholds])

    thr = _key_to_float(_bracket_kth_largest(count_ge, kprime, thr_u ^ INT_MIN))
    need = kprime - count_scores([lambda sc: sc > thr])[0]

    tri = jnp.where(srow > tcol, 1.0, 0.0).astype(BF16)

    def mask_block(j, carry):
        sc = scores_sc[j]
        eqf = jnp.where(sc == thr, 1.0, 0.0)
        rank = jnp.dot(tri, eqf.astype(BF16), preferred_element_type=F32) + carry
        tie = jnp.where(jnp.logical_and(sc == thr, rank < need), 0.0, NEG)
        mb_sc[j] = jnp.where(sc > thr, 0.0, tie)
        return carry + jnp.sum(eqf, axis=0, keepdims=True)

    def mask_body(jj, carry):
        for u in range(4):
            carry = mask_block(4 * jj + u, carry)
        return carry

    lax.fori_loop(0, nquad, mask_body, jnp.zeros((1, PAGE), F32))

    groups = range(ATT_KV_HEADS)
    width = ATT_GROUP * PAGE
    qcats = [jnp.concatenate(
        [qT_ref[(ATT_GROUP * g + r) * HEAD_DIM:(ATT_GROUP * g + r + 1) * HEAD_DIM, :] for r in range(ATT_GROUP)],
        axis=1) for g in groups]

    def logit_block(j, ms):
        off = pl.multiple_of(j * PAGE, PAGE)
        mb = mb_sc[j]
        relc = jnp.clip(i - j, 0, 2)
        out = []
        for g in groups:
            kj = k_ref[pl.ds(off, PAGE), g * HEAD_DIM:(g + 1) * HEAD_DIM]
            logits = jnp.dot(kj, qcats[g], preferred_element_type=F32)
            logits = jnp.concatenate(
                [logits[:, r * PAGE:(r + 1) * PAGE] + (mb + bias_sc[ATT_GROUP * g + r, relc])
                 for r in range(ATT_GROUP)], axis=1)
            logit_sc[g, j] = logits
            out.append(jnp.maximum(ms[g], jnp.max(logits, axis=0, keepdims=True)))
        return tuple(out)

    ms = lax.fori_loop(0, npair, lambda jj, c: logit_block(2 * jj + 1, logit_block(2 * jj, c)),
                       tuple(jnp.full((1, width), NEG, F32) for _ in groups))

    def value_block(j, ls):
        contrib, out = [], []
        for g in groups:
            p = jnp.exp(logit_sc[g, j] - ms[g])
            out.append(ls[g] + jnp.sum(p, axis=0, keepdims=True))
            vTj = vT_ref[j, g * HEAD_DIM:(g + 1) * HEAD_DIM, :]
            contrib.append(jnp.dot(vTj, p.astype(BF16), preferred_element_type=F32))
        return contrib, tuple(out)

    def value_body(jj, ls):
        c0, ls = value_block(2 * jj, ls)
        c1, ls = value_block(2 * jj + 1, ls)
        for g in groups:
            acc_sc[g] = acc_sc[g] + (c0[g] + c1[g])
        return ls

    acc_sc[...] = jnp.zeros_like(acc_sc)
    ls = lax.fori_loop(0, npair, value_body, tuple(jnp.zeros((1, width), F32) for _ in groups))
    for g in groups:
        outT = acc_sc[g] * (1.0 / ls[g])
        for r in range(ATT_GROUP):
            hh = ATT_GROUP * g + r
            o_ref[:, hh * HEAD_DIM:(hh + 1) * HEAD_DIM] = outT[:, r * PAGE:(r + 1) * PAGE].T.astype(BF16)


def _dsa_prompt(rel_bias, qT, qiT, wiT, kb, vT3, kib, *, batch, seq):
    nb = seq // PAGE
    rows = batch * seq
    kvw = ATT_KV_HEADS * HEAD_DIM
    qcol = lambda height: pl.BlockSpec((height, PAGE), lambda b, i: (0, b * nb + i))
    return pl.pallas_call(
        functools.partial(_dsa_prompt_kernel, topk=min(TOPK_MAX, seq // 4)),
        grid=(batch, nb),
        in_specs=[pl.BlockSpec(memory_space=pltpu.SMEM),
                  qcol(ATT_HEADS * HEAD_DIM), qcol(IDX_HEADS * IDX_DIM), qcol(8),
                  pl.BlockSpec((seq, kvw), lambda b, i: (b, 0)),
                  pl.BlockSpec((nb, kvw, PAGE), lambda b, i: (b, 0, 0)),
                  pl.BlockSpec((seq, IDX_DIM), lambda b, i: (b, 0))],
        out_specs=pl.BlockSpec((PAGE, ATT_HEADS * HEAD_DIM), lambda b, i: (b * nb + i, 0)),
        out_shape=jax.ShapeDtypeStruct((rows, ATT_HEADS * HEAD_DIM), BF16),
        scratch_shapes=[pltpu.VMEM((nb, PAGE, PAGE), I32), pltpu.VMEM((nb, PAGE, PAGE), F32),
                        pltpu.VMEM((nb // 2, 32, SUBLANES, PAGE), I32), pltpu.VMEM((nb // 2, SUBLANES, PAGE), I32),
                        pltpu.VMEM((nb, PAGE, PAGE), F32),
                        pltpu.VMEM((ATT_HEADS, 3, PAGE, PAGE), F32),
                        pltpu.VMEM((ATT_KV_HEADS, nb, PAGE, ATT_GROUP * PAGE), F32),
                        pltpu.VMEM((ATT_KV_HEADS, HEAD_DIM, ATT_GROUP * PAGE), F32)],
        compiler_params=_params(2),
        name="dsa_prompt",
    )(rel_bias, qT, qiT, wiT, kb, vT3, kib)


TQ = 8
SEL_SAMPLES = 8


def _dsa_select_kernel(pt_ref, qi_ref, zl_ref, *rest, n_pages, n_new):
    ns = SEL_SAMPLES
    ki_pages = rest[:ns * n_pages]
    mb_ref, sc_sc = rest[ns * n_pages:]
    del pt_ref
    n_blk = n_pages + 1
    rows = ns * TQ
    trow = lax.broadcasted_iota(I32, (TQ, PAGE), 0)
    scol = lax.broadcasted_iota(I32, (TQ, PAGE), 1)
    pad_rows = lambda a: jnp.concatenate([a, jnp.zeros((PAGE - TQ, a.shape[1]), a.dtype)], axis=0)
    nt = (((1,), (1,)), ((), ()))

    for s in range(ns):
        zl = zl_ref[s]
        qi = qi_ref[s]
        for p in range(n_blk):
            if p == n_pages:
                ki_new = pad_rows(zl[:, :IDX_DIM]).astype(BF16)
                idx = lax.dot_general(qi, ki_new, nt, preferred_element_type=F32)
            else:
                idx = jnp.dot(qi, ki_pages[s * n_pages + p][...].astype(BF16), preferred_element_type=F32)
            sc = jnp.zeros((TQ, PAGE), F32)
            for hh in range(IDX_HEADS):
                sc = sc + zl[:, IDX_DIM + hh:IDX_DIM + hh + 1] * jnp.maximum(idx[hh * TQ:(hh + 1) * TQ], 0.0)
            sc = _canonical_zero(sc)
            if p == n_pages:
                sc = jnp.where(jnp.logical_and(scol <= trow, scol < n_new), sc, -jnp.inf)
            sc_sc[s * TQ:(s + 1) * TQ, p * PAGE:(p + 1) * PAGE] = sc

    kprime = float(min(TOPK_MAX, (n_pages * PAGE + n_new) // 4))

    def count(pred_of_scores):
        return jnp.sum(jnp.where(pred_of_scores(sc_sc[...]), 1.0, 0.0), axis=1, keepdims=True)

    cand0 = jnp.zeros((rows, 1), I32)
    thr0 = jnp.where(count(lambda sc: sc >= _key_to_float(cand0)) >= kprime, cand0, INT_MIN)

    def bit_body(bi, thr_key):
        cand = thr_key | lax.shift_left(jnp.int32(1), 30 - bi)
        return jnp.where(count(lambda sc: sc >= _key_to_float(cand)) >= kprime, cand, thr_key)

    thr = _key_to_float(lax.fori_loop(0, 31, bit_body, thr0))
    need = kprime - count(lambda sc: sc > thr)

    s_r = lax.broadcasted_iota(I32, (PAGE, PAGE), 0)
    s_c = lax.broadcasted_iota(I32, (PAGE, PAGE), 1)
    triu = jnp.where(s_r < s_c, 1.0, 0.0).astype(BF16)
    carry = jnp.zeros((rows, 1), F32)
    for p in range(n_blk):
        sc = sc_sc[:, p * PAGE:(p + 1) * PAGE]
        eqf = jnp.where(sc == thr, 1.0, 0.0)
        rank = jnp.dot(eqf.astype(BF16), triu, preferred_element_type=F32) + carry
        tie = jnp.where(jnp.logical_and(sc == thr, rank < need), 0.0, NEG)
        mb_ref[:, p * PAGE:(p + 1) * PAGE] = jnp.where(sc > thr, 0.0, tie)
        carry = carry + jnp.sum(eqf, axis=1, keepdims=True)


def _dsa_select(page_table, qi8, zl8, cache_ki, *, n_new):
    nsamp, n_pages = page_table.shape
    ns = SEL_SAMPLES
    width = (n_pages + 1) * PAGE

    def ki_page_spec(s, p):
        return pl.BlockSpec((None, IDX_DIM, PAGE), lambda b, pt: (pt[(b * ns + s) * n_pages + p], 0, 0))

    in_specs = ([pl.BlockSpec((ns, IDX_HEADS * TQ, IDX_DIM), lambda b, pt: (b, 0, 0)),
                 pl.BlockSpec((ns, TQ, LANES), lambda b, pt: (b, 0, 0))]
                + [ki_page_spec(s, p) for s in range(ns) for p in range(n_pages)])
    grid_spec = pltpu.PrefetchScalarGridSpec(
        num_scalar_prefetch=1, grid=(nsamp // ns,), in_specs=in_specs,
        out_specs=pl.BlockSpec((ns * TQ, width), lambda b, pt: (b, 0)),
        scratch_shapes=[pltpu.VMEM((ns * TQ, width), F32)])
    return pl.pallas_call(
        functools.partial(_dsa_select_kernel, n_pages=n_pages, n_new=n_new),
        grid_spec=grid_spec,
        out_shape=jax.ShapeDtypeStruct((nsamp * TQ, width), F32),
        compiler_params=_params(1),
        name="dsa_select",
    )(page_table.reshape(-1), qi8, zl8, *([cache_ki] * (ns * n_pages)))


def _dsa_sample_kernel(pt_ref, relb_ref, q_ref, mb_ref, kn_ref, vn_ref, *rest, n_pages):
    k_pages = rest[:n_pages]
    v_pages = rest[n_pages:2 * n_pages]
    o_ref, bias_sc, kall_sc, vall_sc = rest[2 * n_pages:]
    del pt_ref
    past = n_pages * PAGE
    rows_g = ATT_GROUP * TQ

    trow = lax.broadcasted_iota(I32, (TQ, PAGE), 0)
    scol = lax.broadcasted_iota(I32, (TQ, PAGE), 1)

    @pl.when(pl.program_id(0) == 0)
    def _init_bias():
        for g in range(ATT_KV_HEADS):
            for r in range(ATT_GROUP):
                hh = ATT_GROUP * g + r
                rs = slice(r * TQ, (r + 1) * TQ)
                far = jnp.full((TQ, PAGE), relb_ref[NUM_BUCKETS - 1, hh], F32)
                for p in range(n_pages - 1):
                    bias_sc[g, rs, p * PAGE:(p + 1) * PAGE] = far
                bias_sc[g, rs, past - PAGE:past] = _bias_from_distance(
                    relb_ref, hh, jnp.maximum(PAGE + trow - scol, 0))
                bias_sc[g, rs, past:past + PAGE] = _bias_from_distance(relb_ref, hh, jnp.maximum(trow - scol, 0))

    nt = (((1,), (1,)), ((), ()))
    mb = jnp.concatenate([mb_ref[...]] * ATT_GROUP, axis=0)
    pad_rows = lambda a: jnp.concatenate([a, jnp.zeros((PAGE - TQ, a.shape[1]), a.dtype)], axis=0)
    for g in range(ATT_KV_HEADS):
        gs = slice(g * HEAD_DIM, (g + 1) * HEAD_DIM)
        for p in range(n_pages):
            ps = slice(p * PAGE, (p + 1) * PAGE)
            kall_sc[g, ps, :] = k_pages[p][pl.ds(g, PAGE, stride=ATT_KV_HEADS), :].astype(BF16)
            vall_sc[g, ps, :] = v_pages[p][pl.ds(g, PAGE, stride=ATT_KV_HEADS), :].astype(BF16)
        kall_sc[g, past:past + PAGE, :] = pad_rows(kn_ref[:, gs]).astype(BF16)
        vall_sc[g, past:past + PAGE, :] = pad_rows(vn_ref[:, gs]).astype(BF16)
    for g in range(ATT_KV_HEADS):
        qg = q_ref[g * rows_g:(g + 1) * rows_g, :]
        logits = lax.dot_general(qg, kall_sc[g], nt, preferred_element_type=F32)
        logits = logits + (mb + bias_sc[g])
        m = jnp.max(logits, axis=1, keepdims=True)
        pr = jnp.exp(logits - m)
        l = jnp.sum(pr, axis=1, keepdims=True)
        acc = jnp.dot(pr.astype(BF16), vall_sc[g], preferred_element_type=F32)
        o_ref[g * rows_g:(g + 1) * rows_g, :] = (acc * (1.0 / l)).astype(BF16)


def _dsa_sample(page_table, rel_bias, q8, mb, kn8, vn8, cache_k, cache_v):
    nsamp, n_pages = page_table.shape
    kvw = ATT_KV_HEADS * HEAD_DIM
    per_sample = lambda *tail: pl.BlockSpec((None,) + tail, lambda b, pt: (b,) + (0,) * len(tail))

    def kv_page_spec(p):
        return pl.BlockSpec((PAGE * ATT_KV_HEADS, HEAD_DIM), lambda b, pt: (pt[b * n_pages + p], 0))

    in_specs = ([pl.BlockSpec(memory_space=pltpu.SMEM),
                 per_sample(ATT_HEADS * TQ, HEAD_DIM),
                 pl.BlockSpec((TQ, mb.shape[1]), lambda b, pt: (b, 0)),
                 per_sample(TQ, kvw), per_sample(TQ, kvw)]
                + [kv_page_spec(p) for p in range(n_pages)]
                + [kv_page_spec(p) for p in range(n_pages)])
    grid_spec = pltpu.PrefetchScalarGridSpec(
        num_scalar_prefetch=1, grid=(nsamp,), in_specs=in_specs,
        out_specs=per_sample(ATT_HEADS * TQ, HEAD_DIM),
        scratch_shapes=[pltpu.VMEM((ATT_KV_HEADS, ATT_GROUP * TQ, mb.shape[1]), F32),
                        pltpu.VMEM((ATT_KV_HEADS, mb.shape[1], HEAD_DIM), BF16),
                        pltpu.VMEM((ATT_KV_HEADS, mb.shape[1], HEAD_DIM), BF16)])
    return pl.pallas_call(
        functools.partial(_dsa_sample_kernel, n_pages=n_pages),
        grid_spec=grid_spec,
        out_shape=jax.ShapeDtypeStruct((nsamp, ATT_HEADS * TQ, HEAD_DIM), BF16),
        compiler_params=_params(1),
        name="dsa_sample",
    )(page_table.reshape(-1), rel_bias, q8, mb, kn8, vn8, *([cache_k] * n_pages), *([cache_v] * n_pages))


def _post_mix_kernel(x_ref, o_ref, wo_ref, g_ref, w1_ref, w2_ref, y_ref):
    y = x_ref[...] + jnp.dot(o_ref[...], wo_ref[...], preferred_element_type=F32)
    h = (_rms_rows(y) * g_ref[...]).astype(BF16)
    acc = y
    for c in range(D_FF // D_MODEL):
        a = jnp.dot(h, w1_ref[:, c * D_MODEL:(c + 1) * D_MODEL], preferred_element_type=F32)
        a = jnp.square(jnp.maximum(a, 0.0)).astype(BF16)
        acc = acc + jnp.dot(a, w2_ref[c * D_MODEL:(c + 1) * D_MODEL, :], preferred_element_type=F32)
    y_ref[...] = acc


def _post_mix(x, o, wo, g, w1, w2):
    rows = x.shape[0]
    tm = ROW_TILE
    ko = o.shape[1]
    return pl.pallas_call(
        _post_mix_kernel,
        grid=(rows // tm,),
        in_specs=[pl.BlockSpec((tm, D_MODEL), lambda i: (i, 0)), pl.BlockSpec((tm, ko), lambda i: (i, 0)),
                  _const_spec((ko, D_MODEL)), _const_spec((1, D_MODEL)),
                  _const_spec((D_MODEL, D_FF)), _const_spec((D_FF, D_MODEL))],
        out_specs=pl.BlockSpec((tm, D_MODEL), lambda i: (i, 0)),
        out_shape=jax.ShapeDtypeStruct((rows, D_MODEL), F32),
        compiler_params=_params(1),
        name="post_mix",
    )(x, o, wo, g, w1, w2)


def _ret_proj_kernel(x_ref, g_ref, w_ref, cos_ref, sin_ref, q_ref, k_ref, v_ref, gate_ref):
    hb = (_rms_rows(x_ref[...]) * g_ref[...]).astype(BF16)
    cos = cos_ref[...]
    sin = sin_ref[...]
    half = RET_KEY // 2
    qk = RET_HEADS * RET_KEY
    vd = RET_HEADS * RET_VAL

    def rotated(col0, scale):
        z = jnp.dot(hb, w_ref[:, col0:col0 + RET_KEY], preferred_element_type=F32)
        x1, x2 = z[:, :half], z[:, half:]
        return (x1 * cos - x2 * sin) * scale, (x1 * sin + x2 * cos) * scale

    for hh in range(RET_HEADS):
        a, b = rotated(hh * RET_KEY, 1.0)
        q_ref[:, hh * RET_KEY:hh * RET_KEY + half] = a
        q_ref[:, hh * RET_KEY + half:(hh + 1) * RET_KEY] = b
        a, b = rotated(qk + hh * RET_KEY, RET_KSCALE)
        k_ref[:, hh * RET_KEY:hh * RET_KEY + half] = a
        k_ref[:, hh * RET_KEY + half:(hh + 1) * RET_KEY] = b
    for hh in range(RET_HEADS):
        cs = slice(hh * RET_VAL, (hh + 1) * RET_VAL)
        v_ref[:, cs] = jnp.dot(hb, w_ref[:, 2 * qk + hh * RET_VAL:2 * qk + (hh + 1) * RET_VAL],
                               preferred_element_type=F32).astype(BF16)
        gate_ref[:, cs] = jnp.dot(hb, w_ref[:, 2 * qk + vd + hh * RET_VAL:2 * qk + vd + (hh + 1) * RET_VAL],
                                  preferred_element_type=F32)


def _ret_proj(x, g, w, cos, sin):
    rows = x.shape[0]
    tm = ROW_TILE
    qk = RET_HEADS * RET_KEY
    vd = RET_HEADS * RET_VAL
    n_pos_tiles = cos.shape[0] // tm
    row_spec = lambda width: pl.BlockSpec((tm, width), lambda i: (i, 0))
    pos_spec = pl.BlockSpec((tm, RET_KEY // 2), lambda i: (i % n_pos_tiles, 0))
    return pl.pallas_call(
        _ret_proj_kernel,
        grid=(rows // tm,),
        in_specs=[row_spec(D_MODEL), _const_spec((1, D_MODEL)), _const_spec((D_MODEL, 2 * qk + 2 * vd)),
                  pos_spec, pos_spec],
        out_specs=[row_spec(qk), row_spec(qk), row_spec(vd), row_spec(vd)],
        out_shape=[jax.ShapeDtypeStruct((rows, qk), F32), jax.ShapeDtypeStruct((rows, qk), F32),
                   jax.ShapeDtypeStruct((rows, vd), BF16), jax.ShapeDtypeStruct((rows, vd), F32)],
        compiler_params=_params(1),
        name="ret_proj",
    )(x, g, w, cos, sin)


def _retention_kernel(cd_ref, q_ref, k_ref, v_ref, g_ref, dm_ref, qd_ref, kd_ref, *rest, has_state0):
    if has_state0:
        s0_ref, o_ref, s_ref = rest
    else:
        o_ref, s_ref = rest

    @pl.when(pl.program_id(1) == 0)
    def _init_state():
        s_ref[...] = s0_ref[...] if has_state0 else jnp.zeros_like(s_ref)

    nt = (((1,), (1,)), ((), ()))
    tn = (((0,), (0,)), ((), ()))
    units = [(bi, hh) for bi in range(q_ref.shape[0]) for hh in range(RET_HEADS)]
    ks = lambda hh: slice(hh * RET_KEY, (hh + 1) * RET_KEY)
    vs = lambda hh: slice(hh * RET_VAL, (hh + 1) * RET_VAL)
    att = [lax.dot_general(q_ref[bi, :, ks(hh)].astype(BF16), k_ref[bi, :, ks(hh)].astype(BF16), nt,
                           preferred_element_type=F32) * dm_ref[hh] for bi, hh in units]
    cross = [jnp.dot((q_ref[bi, :, ks(hh)] * qd_ref[hh]).astype(BF16), s_ref[bi, hh].astype(BF16),
                     preferred_element_type=F32) for bi, hh in units]
    for (bi, hh), a, x in zip(units, att, cross):
        o = jnp.dot(a.astype(BF16), v_ref[bi, :, vs(hh)], preferred_element_type=F32) + x
        gate = g_ref[bi, :, vs(hh)]
        o_ref[bi, :, vs(hh)] = (_rms_rows(o) * (gate * jax.nn.sigmoid(gate))).astype(BF16)
    for bi, hh in units:
        kd = (k_ref[bi, :, ks(hh)] * kd_ref[hh]).astype(BF16)
        s_ref[bi, hh] = cd_ref[hh] * s_ref[bi, hh] + lax.dot_general(kd, v_ref[bi, :, vs(hh)], tn,
                                                                     preferred_element_type=F32)


def _retention(q, k, v, gate, consts, state0, *, seqs_per_step):
    dm, qd, kd, cd = consts
    nb, t, _ = q.shape
    c = dm.shape[1]
    bb = seqs_per_step
    qk = RET_HEADS * RET_KEY
    vd = RET_HEADS * RET_VAL
    tok = lambda width: pl.BlockSpec((bb, c, width), lambda b, ci: (b, ci, 0))
    st_spec = pl.BlockSpec((bb, RET_HEADS, RET_KEY, RET_VAL), lambda b, ci: (b, 0, 0, 0))
    in_specs = [pl.BlockSpec(memory_space=pltpu.SMEM), tok(qk), tok(qk), tok(vd), tok(vd),
                _const_spec(dm.shape), _const_spec(qd.shape), _const_spec(kd.shape)]
    args = [cd, q, k, v, gate, dm, qd, kd]
    if state0 is not None:
        in_specs.append(st_spec)
        args.append(state0)
    return pl.pallas_call(
        functools.partial(_retention_kernel, has_state0=state0 is not None),
        grid=(nb // bb, t // c),
        in_specs=in_specs,
        out_specs=[tok(vd), st_spec],
        out_shape=[jax.ShapeDtypeStruct((nb, t, vd), BF16),
                   jax.ShapeDtypeStruct((nb, RET_HEADS, RET_KEY, RET_VAL), F32)],
        compiler_params=_params(2),
        name="retention_s" if state0 is not None else "retention_p",
    )(*args)


def _retention_consts(c, c_pad):
    log_g = jnp.log1p(-jnp.exp2(-5.0 - jnp.arange(RET_HEADS, dtype=F32)))
    i = jnp.arange(c, dtype=F32)
    diff = i[:, None] - i[None, :]
    dm = jnp.where(diff >= 0, jnp.exp(log_g[:, None, None] * jnp.maximum(diff, 0.0)), 0.0)
    qd = jnp.exp(log_g[:, None] * (i[None, :] + 1.0))
    kd = jnp.exp(log_g[:, None] * (c - 1.0 - i[None, :]))
    cd = jnp.exp(log_g * c)
    pad = c_pad - c
    dm = jnp.pad(dm, ((0, 0), (0, pad), (0, pad)))
    qd = jnp.broadcast_to(jnp.pad(qd, ((0, 0), (0, pad)))[:, :, None], (RET_HEADS, c_pad, RET_KEY))
    kd = jnp.broadcast_to(jnp.pad(kd, ((0, 0), (0, pad)))[:, :, None], (RET_HEADS, c_pad, RET_KEY))
    return dm, qd, kd, cd


def _rotation_tables(pos):
    half = RET_KEY // 2
    theta = 1.0 / (10000.0 ** jnp.linspace(0.0, 1.0, half, dtype=F32))
    ang = pos.astype(F32)[:, None] * theta[None, :]
    return jnp.cos(ang), jnp.sin(ang)


def kernel(x_prompt, x_sample, cache_k, cache_v, cache_kidx, state_ret, page_table, rel_bias, ln_mix, ln_mlp,
           att_w_in, att_q_gain, att_k_gain, att_w_out, ret_w_in, ret_w_out, mlp_w_in, mlp_w_out):
    bp, tp, _ = x_prompt.shape
    bs, ts, _ = x_sample.shape
    n_phys = cache_k.shape[1]
    assert ln_mix.shape[0] == 2 and att_w_in.shape[0] == 1 and ret_w_in.shape[0] == 1
    assert tp % RET_CHUNK == 0 and ts <= TQ and cache_k.shape[2] == PAGE
    assert (tp // PAGE) % 4 == 0 and bs % SEL_SAMPLES == 0
    kvw = ATT_KV_HEADS * HEAD_DIM
    past = page_table.shape[1] * PAGE

    w_att = jnp.pad(att_w_in[0], ((0, 0), (0, ATT_IN_PAD - ATT_IN))).astype(BF16)
    w_att_out = att_w_out[0].astype(BF16)
    w_ret = ret_w_in[0].astype(BF16)
    w_ret_out = ret_w_out[0].astype(BF16)
    w1 = mlp_w_in.astype(BF16)
    w2 = mlp_w_out.astype(BF16)
    row = lambda a: a.reshape(1, -1)

    xp = x_prompt.reshape(bp * tp, D_MODEL)
    xs = x_sample.reshape(bs * ts, D_MODEL)

    qT, qiT, wiT, vT3, kb, kib, kp, vp, kip = _attn_proj(
        xp, row(ln_mix[0]), w_att, row(att_q_gain[0]), row(att_k_gain[0]), transposed=True)
    op = _dsa_prompt(rel_bias, qT, qiT, wiT, kb, vT3, kib, batch=bp, seq=tp)
    yp = _post_mix(xp, op, w_att_out, row(ln_mlp[0]), w1[0], w2[0])

    qs, qis, zls, ks, vs, kis = _attn_proj(
        xs, row(ln_mix[0]), w_att, row(att_q_gain[0]), row(att_k_gain[0]), transposed=False)
    pad_t = lambda a: jnp.pad(a, ((0, 0), (0, TQ - ts)) + ((0, 0),) * (a.ndim - 2))
    q8 = pad_t(qs.reshape(bs, ts, ATT_HEADS, HEAD_DIM)).transpose(0, 2, 1, 3)
    q8 = q8.reshape(bs, ATT_HEADS * TQ, HEAD_DIM).astype(BF16)
    qi8 = pad_t(qis.reshape(bs, ts, IDX_HEADS, IDX_DIM)).transpose(0, 2, 1, 3)
    qi8 = qi8.reshape(bs, IDX_HEADS * TQ, IDX_DIM).astype(BF16)
    zl8 = pad_t(zls.reshape(bs, ts, LANES))
    kn8 = pad_t(ks.reshape(bs, ts, kvw))
    vn8 = pad_t(vs.reshape(bs, ts, kvw))
    mb = _dsa_select(page_table, qi8, zl8, jnp.swapaxes(cache_kidx[0], 1, 2), n_new=ts)
    os8 = _dsa_sample(page_table, rel_bias, q8, mb, kn8, vn8,
                      cache_k[0].reshape(n_phys * PAGE * ATT_KV_HEADS, HEAD_DIM),
                      cache_v[0].reshape(n_phys * PAGE * ATT_KV_HEADS, HEAD_DIM))
    os_ = os8.reshape(bs, ATT_HEADS, TQ, HEAD_DIM)[:, :, :ts].transpose(0, 2, 1, 3)
    os_ = os_.reshape(bs * ts, ATT_HEADS * HEAD_DIM)
    ys = _post_mix(xs, os_, w_att_out, row(ln_mlp[0]), w1[0], w2[0])

    cos_p, sin_p = _rotation_tables(jnp.arange(tp, dtype=jnp.int32))
    q, k, v, gate = _ret_proj(yp, row(ln_mix[1]), w_ret, cos_p, sin_p)
    r3 = lambda a: a.reshape(bp, tp, a.shape[-1])
    orp, state_p = _retention(r3(q), r3(k), r3(v), r3(gate), _retention_consts(RET_CHUNK, RET_CHUNK), None,
                            seqs_per_step=1)
    yp = _post_mix(yp, orp.reshape(bp * tp, -1), w_ret_out, row(ln_mlp[1]), w1[1], w2[1])

    cos_s, sin_s = _rotation_tables(past + jnp.arange(ts, dtype=jnp.int32))
    tile_s = lambda a: jnp.tile(a, (bs, 1))
    q, k, v, gate = _ret_proj(ys, row(ln_mix[1]), w_ret, tile_s(cos_s), tile_s(sin_s))
    r3s = lambda a: pad_t(a.reshape(bs, ts, a.shape[-1]))
    ors, state_s = _retention(r3s(q), r3s(k), r3s(v), r3s(gate), _retention_consts(ts, TQ), state_ret[0],
                            seqs_per_step=2)
    ys = _post_mix(ys, ors[:, :ts].reshape(bs * ts, -1), w_ret_out, row(ln_mlp[1]), w1[1], w2[1])

    return (yp.reshape(bp, tp, D_MODEL), ys.reshape(bs, ts, D_MODEL),
            kp.reshape(1, bp, tp, ATT_KV_HEADS, HEAD_DIM), vp.reshape(1, bp, tp, ATT_KV_HEADS, HEAD_DIM),
            kip.reshape(1, bp, tp, IDX_DIM), state_p[None],
            ks.reshape(1, bs, ts, ATT_KV_HEADS, HEAD_DIM), vs.reshape(1, bs, ts, ATT_KV_HEADS, HEAD_DIM),
            kis.reshape(1, bs, ts, IDX_DIM), state_s[None])
```

```python
import functools
import math

import jax
import jax.numpy as jnp
import numpy as np
from jax import lax
from jax.experimental import pallas as pl
from jax.experimental.pallas import tpu as pltpu

F32 = jnp.float32
BF16 = jnp.bfloat16
I32 = jnp.int32

D_MODEL = 1024
PAGE = 128
ATT_HEADS = 8
ATT_KV_HEADS = 2
ATT_GROUP = ATT_HEADS // ATT_KV_HEADS
HEAD_DIM = 128
IDX_HEADS = 4
IDX_DIM = 64
TOPK_MAX = 256
NUM_BUCKETS = 32
MAX_DISTANCE = 128
RET_HEADS = 4
RET_KEY = 256
RET_VAL = 512
RET_CHUNK = 256
D_FF = 4 * D_MODEL
EPS = 1e-6

Q_OFF = 0
K_OFF = ATT_HEADS * HEAD_DIM
V_OFF = K_OFF + ATT_KV_HEADS * HEAD_DIM
QI_OFF = V_OFF + ATT_KV_HEADS * HEAD_DIM
KI_OFF = QI_OFF + IDX_HEADS * IDX_DIM
WI_OFF = KI_OFF + IDX_DIM
ATT_IN = WI_OFF + IDX_HEADS
ATT_IN_PAD = 1920

LANES = 128
SUBLANES = 8
ROW_TILE = 512
VMEM_LIMIT = 56 * 1024 * 1024

ATT_SCALE = HEAD_DIM ** -0.5
RET_KSCALE = RET_KEY ** -0.5
INT_MIN = -(2 ** 31)
NEG = -1e30


def _bucket_bounds():
    n = np.arange(0, 8192)
    max_exact = NUM_BUCKETS // 2
    val = np.log(np.maximum(n, 1) / max_exact) / math.log(MAX_DISTANCE / max_exact) * (NUM_BUCKETS - max_exact)
    frac = np.abs(val - np.round(val))
    risky = (n > max_exact) & (n < MAX_DISTANCE) & (frac < 1e-4)
    assert not risky.any()
    large = np.minimum(max_exact + np.floor(val + 1e-9).astype(np.int64), NUM_BUCKETS - 1)
    bucket = np.where(n < max_exact, n, large)
    assert (np.diff(bucket) >= 0).all() and (bucket[MAX_DISTANCE:] == NUM_BUCKETS - 1).all()
    return [int(np.argmax(bucket >= b)) for b in range(NUM_BUCKETS)]


BUCKET_BOUNDS = _bucket_bounds()


def _const_spec(shape):
    zeros = (0,) * len(shape)
    return pl.BlockSpec(shape, lambda *_: zeros, pipeline_mode=pl.Buffered(1))


def _params(n_axes):
    return pltpu.CompilerParams(dimension_semantics=("arbitrary",) * n_axes,
                                vmem_limit_bytes=VMEM_LIMIT)


def _rms_rows(a):
    return a * lax.rsqrt(jnp.mean(a * a, axis=-1, keepdims=True) + EPS)


def _canonical_zero(score):
    return jnp.where(score == 0.0, 0.0, score)


def _sortable_key(score):
    bits = pltpu.bitcast(score, I32)
    return bits ^ ((bits >> 31) & jnp.int32(0x7FFFFFFF))


def _key_to_float(key):
    return pltpu.bitcast(key ^ ((key >> 31) & jnp.int32(0x7FFFFFFF)), F32)


KEY_NEG_INF = -2139095041
KEY_POS_INF = 2139095040
KEY_SUBNORMAL_LO = -8388608
KEY_SUBNORMAL_HI = 8388607


def _bracket_kth_largest(count_ge, kprime, guess):
    def canon(x):
        return jnp.where(jnp.logical_and(x >= KEY_SUBNORMAL_LO, x <= KEY_SUBNORMAL_HI), 0, x)

    def succ(x):
        xc = canon(x)
        return jnp.where(xc == 0, KEY_SUBNORMAL_HI + 1, canon(xc + 1))

    g0 = jnp.clip(guess, KEY_NEG_INF, KEY_POS_INF - 1)
    g1 = succ(g0)
    lo = jnp.full(guess.shape, KEY_NEG_INF, I32)
    hi = jnp.full(guess.shape, KEY_POS_INF, I32)

    def narrow(lo, hi, x, cnt):
        ok = cnt >= kprime
        return jnp.where(ok, jnp.maximum(lo, x), lo), jnp.where(ok, hi, jnp.minimum(hi, x))

    c0, c1 = count_ge([_key_to_float(g0), _key_to_float(g1)])
    lo, hi = narrow(lo, hi, g0, c0)
    lo, hi = narrow(lo, hi, g1, c1)

    def open_bracket(c):
        return jnp.max(jnp.where(canon(c[1]) > succ(c[0]), 1.0, 0.0)) > 0.5

    def bisect(c):
        lo, hi = c
        mid = jnp.where(canon(hi) > succ(lo), (lo >> 1) + (hi >> 1) + (lo & hi & 1), lo)
        return narrow(lo, hi, mid, count_ge([_key_to_float(mid)])[0])

    lo, _ = lax.while_loop(open_bracket, bisect, (lo, hi))
    return canon(lo)


def _bit_transpose32(words):
    a = list(words)
    j, m = 16, 0x0000FFFF
    while j:
        k = 0
        while k < 32:
            t = (a[k] ^ lax.shift_right_logical(a[k + j], jnp.int32(j))) & jnp.int32(m)
            a[k] = a[k] ^ t
            a[k + j] = a[k + j] ^ lax.shift_left(t, jnp.int32(j))
            k = (k + j + 1) & ~j
        j >>= 1
        m ^= (m << j) & 0xFFFFFFFF
    return a


def _bias_from_distance(relb_ref, head, dist):
    val = jnp.full(dist.shape, relb_ref[0, head], F32)
    for b in range(1, NUM_BUCKETS):
        val = jnp.where(dist >= BUCKET_BOUNDS[b], relb_ref[b, head], val)
    return val


def _attn_proj_kernel(x_ref, g_ref, w_ref, qg_ref, kg_ref, *outs, transposed):
    x = x_ref[...]
    h = _rms_rows(x) * g_ref[...]
    z = jnp.dot(h.astype(BF16), w_ref[...], preferred_element_type=F32)
    qg = qg_ref[...]
    kg = kg_ref[...]
    k = jnp.concatenate(
        [_rms_rows(z[:, K_OFF + g * HEAD_DIM:K_OFF + (g + 1) * HEAD_DIM]) * kg for g in range(ATT_KV_HEADS)],
        axis=1)
    v = z[:, V_OFF:QI_OFF]
    qi = z[:, QI_OFF:KI_OFF]
    zl = z[:, KI_OFF:ATT_IN_PAD]
    ki = zl[:, :IDX_DIM]
    if transposed:
        qT_ref, qiT_ref, wiT_ref, vT_ref, kb_ref, kib_ref, k_ref, v_ref, ki_ref = outs
        for hh in range(ATT_HEADS):
            qh = _rms_rows(z[:, hh * HEAD_DIM:(hh + 1) * HEAD_DIM]) * qg * ATT_SCALE
            qT_ref[hh * HEAD_DIM:(hh + 1) * HEAD_DIM, :] = qh.T.astype(BF16)
        for c in range(IDX_HEADS * IDX_DIM // LANES):
            qiT_ref[c * LANES:(c + 1) * LANES, :] = qi[:, c * LANES:(c + 1) * LANES].T.astype(BF16)
        wiT_ref[...] = zl.T[IDX_DIM:IDX_DIM + 8, :]
        for g in range(ATT_KV_HEADS):
            vT = v[:, g * HEAD_DIM:(g + 1) * HEAD_DIM].T
            for c in range(x.shape[0] // PAGE):
                vT_ref[c, g * HEAD_DIM:(g + 1) * HEAD_DIM, :] = vT[:, c * PAGE:(c + 1) * PAGE].astype(BF16)
        kb_ref[...] = k.astype(BF16)
        kib_ref[...] = ki.astype(BF16)
    else:
        q_ref, qi_ref, zl_ref, k_ref, v_ref, ki_ref = outs
        for hh in range(ATT_HEADS):
            q_ref[:, hh * HEAD_DIM:(hh + 1) * HEAD_DIM] = _rms_rows(z[:, hh * HEAD_DIM:(hh + 1) * HEAD_DIM]) * qg * ATT_SCALE
        qi_ref[...] = qi
        zl_ref[...] = zl
    for g in range(ATT_KV_HEADS):
        rows_g = pl.ds(g, x.shape[0], stride=ATT_KV_HEADS)
        k_ref[rows_g, :] = k[:, g * HEAD_DIM:(g + 1) * HEAD_DIM]
        v_ref[rows_g, :] = v[:, g * HEAD_DIM:(g + 1) * HEAD_DIM]
    ki_ref[...] = ki


def _attn_proj(x, g, w, qg, kg, *, transposed):
    rows = x.shape[0]
    tm = ROW_TILE
    nt = rows // tm
    kvw = ATT_KV_HEADS * HEAD_DIM
    row_spec = lambda width: pl.BlockSpec((tm, width), lambda i: (i, 0))
    col_spec = lambda height: pl.BlockSpec((height, tm), lambda i: (0, i))
    kv_rows = ATT_KV_HEADS * rows
    kv_spec = pl.BlockSpec((ATT_KV_HEADS * tm, HEAD_DIM), lambda i: (i, 0))
    leaves_shape = [jax.ShapeDtypeStruct((kv_rows, HEAD_DIM), F32), jax.ShapeDtypeStruct((kv_rows, HEAD_DIM), F32),
                    jax.ShapeDtypeStruct((rows, IDX_DIM), F32)]
    leaves_spec = [kv_spec, kv_spec, row_spec(IDX_DIM)]
    if transposed:
        out_shape = [jax.ShapeDtypeStruct((ATT_HEADS * HEAD_DIM, rows), BF16),
                     jax.ShapeDtypeStruct((IDX_HEADS * IDX_DIM, rows), BF16),
                     jax.ShapeDtypeStruct((8, rows), F32),
                     jax.ShapeDtypeStruct((rows // PAGE, kvw, PAGE), BF16),
                     jax.ShapeDtypeStruct((rows, kvw), BF16),
                     jax.ShapeDtypeStruct((rows, IDX_DIM), BF16)] + leaves_shape
        out_specs = [col_spec(ATT_HEADS * HEAD_DIM), col_spec(IDX_HEADS * IDX_DIM), col_spec(8),
                     pl.BlockSpec((tm // PAGE, kvw, PAGE), lambda i: (i, 0, 0)),
                     row_spec(kvw), row_spec(IDX_DIM)] + leaves_spec
    else:
        out_shape = [jax.ShapeDtypeStruct((rows, ATT_HEADS * HEAD_DIM), F32),
                     jax.ShapeDtypeStruct((rows, IDX_HEADS * IDX_DIM), F32),
                     jax.ShapeDtypeStruct((rows, LANES), F32)] + leaves_shape
        out_specs = [row_spec(ATT_HEADS * HEAD_DIM), row_spec(IDX_HEADS * IDX_DIM), row_spec(LANES)] + leaves_spec
    return pl.pallas_call(
        functools.partial(_attn_proj_kernel, transposed=transposed),
        grid=(nt,),
        in_specs=[row_spec(D_MODEL), _const_spec((1, D_MODEL)), _const_spec((D_MODEL, ATT_IN_PAD)),
                  _const_spec((1, HEAD_DIM)), _const_spec((1, HEAD_DIM))],
        out_specs=out_specs,
        out_shape=out_shape,
        compiler_params=_params(1),
        name="attn_proj_t" if transposed else "attn_proj_r",
    )(x, g, w, qg, kg)


def _dsa_prompt_kernel(relb_ref, qT_ref, qiT_ref, wiT_ref, k_ref, vT_ref, ki_ref, o_ref,
                       keys_sc, scores_sc, planes_sc, live_sc, mb_sc, bias_sc, logit_sc, acc_sc, *, topk):
    b = pl.program_id(0)
    i = pl.program_id(1)
    srow = lax.broadcasted_iota(I32, (PAGE, PAGE), 0)
    tcol = lax.broadcasted_iota(I32, (PAGE, PAGE), 1)

    @pl.when((b == 0) & (i == 0))
    def _init_bias():
        planes_sc[...] = jnp.zeros_like(planes_sc)
        live_sc[...] = jnp.zeros_like(live_sc)
        for hh in range(ATT_HEADS):
            for rel in range(2):
                dist = jnp.maximum(rel * PAGE + tcol - srow, 0)
                bias_sc[hh, rel] = _bias_from_distance(relb_ref, hh, dist)
            bias_sc[hh, 2] = jnp.full((PAGE, PAGE), relb_ref[NUM_BUCKETS - 1, hh], F32)

    wiT = wiT_ref[...]

    npair = (i + 2) // 2

    def score_block(j):
        off = pl.multiple_of(j * PAGE, PAGE)
        kij = ki_ref[pl.ds(off, PAGE), :]
        sc = jnp.zeros((PAGE, PAGE), F32)
        for hh in range(IDX_HEADS):
            idx = jnp.dot(kij, qiT_ref[hh * IDX_DIM:(hh + 1) * IDX_DIM, :], preferred_element_type=F32)
            sc = sc + wiT[hh:hh + 1, :] * jnp.maximum(idx, 0.0)
        sc = _canonical_zero(sc)
        causal = jnp.logical_or(j < i, jnp.logical_and(j == i, srow <= tcol))
        scores_sc[j] = jnp.where(causal, sc, -jnp.inf)
        keys_sc[j] = jnp.where(causal, _sortable_key(sc), INT_MIN)

    nquad = (i + 4) // 4

    def score_body(jj, carry):
        for u in range(4):
            score_block(4 * jj + u)
        return carry

    lax.fori_loop(0, nquad, score_body, 0)

    lane = lax.broadcasted_iota(I32, (1, PAGE), 1)
    kprime = jnp.minimum(topk, i * PAGE + lane + 1).astype(F32)

    def plane_body(jj, carry):
        rows = [keys_sc[2 * jj + blk, k * SUBLANES:(k + 1) * SUBLANES, :] ^ INT_MIN
                for blk in range(2) for k in range(PAGE // SUBLANES)]
        planes = _bit_transpose32(rows)
        live = planes[0]
        for p in range(32):
            planes_sc[jj, p] = planes[p]
            live = live | planes[p]
        live_sc[jj] = live
        return carry

    lax.fori_loop(0, npair, plane_body, 0)

    n_pairs_max = keys_sc.shape[0] // 2

    def bit_body(bi, carry):
        thr_u, alive, above = carry
        cnt = jnp.zeros((SUBLANES, PAGE), I32)
        with_bit, at_least = [], []
        for p in range(n_pairs_max):
            t = alive[p] & planes_sc[p, bi]
            ge = above[p] | t
            cnt = cnt + lax.population_count(ge)
            with_bit.append(t)
            at_least.append(ge)
        take = jnp.sum(cnt.astype(F32), axis=0, keepdims=True) >= kprime
        alive = tuple(jnp.where(take, t, a ^ t) for t, a in zip(with_bit, alive))
        above = tuple(jnp.where(take, g, ge) for g, ge in zip(above, at_least))
        thr_u = thr_u | jnp.where(take, lax.shift_left(jnp.int32(1), 31 - bi), 0)
        return thr_u, alive, above

    zero_words = jnp.zeros((SUBLANES, PAGE), I32)
    thr_u, _, _ = lax.fori_loop(
        0, 32, bit_body,
        (jnp.zeros((1, PAGE), I32),
         tuple(jnp.where(p < npair, live_sc[p], zero_words) for p in range(n_pairs_max)),
         (zero_words,) * n_pairs_max))
    def count_scores(preds):
        def body(jj, cnts):
            for blk in range(2):
                sc = scores_sc[2 * jj + blk]
                cnts = tuple(c + jnp.where(p(sc), 1.0, 0.0) for c, p in zip(cnts, preds))
            return cnts
        cnts = lax.fori_loop(0, npair, body, tuple(jnp.zeros((PAGE, PAGE), F32) for _ in preds))
        return [jnp.sum(c, axis=0, keepdims=True) for c in cnts]

    def count_ge(thresholds):
        return count_scores([lambda sc, x=x: sc >= x for x in thresholds])

    thr = _key_to_float(_bracket_kth_largest(count_ge, kprime, thr_u ^ INT_MIN))
    need = kprime - count_scores([lambda sc: sc > thr])[0]

    tri = jnp.where(srow > tcol, 1.0, 0.0).astype(BF16)

    def mask_block(j, carry):
        sc = scores_sc[j]
        eqf = jnp.where(sc == thr, 1.0, 0.0)
        rank = jnp.dot(tri, eqf.astype(BF16), preferred_element_type=F32) + carry
        tie = jnp.where(jnp.logical_and(sc == thr, rank < need), 0.0, NEG)
        mb_sc[j] = jnp.where(sc > thr, 0.0, tie)
        return carry + jnp.sum(eqf, axis=0, keepdims=True)

    def mask_body(jj, carry):
        for u in range(4):
            carry = mask_block(4 * jj + u, carry)
        return carry

    lax.fori_loop(0, nquad, mask_body, jnp.zeros((1, PAGE), F32))

    groups = range(ATT_KV_HEADS)
    width = ATT_GROUP * PAGE
    qcats = [jnp.concatenate(
        [qT_ref[(ATT_GROUP * g + r) * HEAD_DIM:(ATT_GROUP * g + r + 1) * HEAD_DIM, :] for r in range(ATT_GROUP)],
        axis=1) for g in groups]

    def logit_block(j, ms):
        off = pl.multiple_of(j * PAGE, PAGE)
        mb = mb_sc[j]
        relc = jnp.clip(i - j, 0, 2)
        out = []
        for g in groups:
            kj = k_ref[pl.ds(off, PAGE), g * HEAD_DIM:(g + 1) * HEAD_DIM]
            logits = jnp.dot(kj, qcats[g], preferred_element_type=F32)
            logits = jnp.concatenate(
                [logits[:, r * PAGE:(r + 1) * PAGE] + (mb + bias_sc[ATT_GROUP * g + r, relc])
                 for r in range(ATT_GROUP)], axis=1)
            logit_sc[g, j] = logits
            out.append(jnp.maximum(ms[g], jnp.max(logits, axis=0, keepdims=True)))
        return tuple(out)

    ms = lax.fori_loop(0, npair, lambda jj, c: logit_block(2 * jj + 1, logit_block(2 * jj, c)),
                       tuple(jnp.full((1, width), NEG, F32) for _ in groups))

    def value_block(j, ls):
        contrib, out = [], []
        for g in groups:
            p = jnp.exp(logit_sc[g, j] - ms[g])
            out.append(ls[g] + jnp.sum(p, axis=0, keepdims=True))
            vTj = vT_ref[j, g * HEAD_DIM:(g + 1) * HEAD_DIM, :]
            contrib.append(jnp.dot(vTj, p.astype(BF16), preferred_element_type=F32))
        return contrib, tuple(out)

    def value_body(jj, ls):
        c0, ls = value_block(2 * jj, ls)
        c1, ls = value_block(2 * jj + 1, ls)
        for g in groups:
            acc_sc[g] = acc_sc[g] + (c0[g] + c1[g])
        return ls

    acc_sc[...] = jnp.zeros_like(acc_sc)
    ls = lax.fori_loop(0, npair, value_body, tuple(jnp.zeros((1, width), F32) for _ in groups))
    for g in groups:
        outT = acc_sc[g] * (1.0 / ls[g])
        for r in range(ATT_GROUP):
            hh = ATT_GROUP * g + r
            o_ref[:, hh * HEAD_DIM:(hh + 1) * HEAD_DIM] = outT[:, r * PAGE:(r + 1) * PAGE].T.astype(BF16)


def _dsa_prompt(rel_bias, qT, qiT, wiT, kb, vT3, kib, *, batch, seq):
    nb = seq // PAGE
    rows = batch * seq
    kvw = ATT_KV_HEADS * HEAD_DIM
    qcol = lambda height: pl.BlockSpec((height, PAGE), lambda b, i: (0, b * nb + i))
    return pl.pallas_call(
        functools.partial(_dsa_prompt_kernel, topk=min(TOPK_MAX, seq // 4)),
        grid=(batch, nb),
        in_specs=[pl.BlockSpec(memory_space=pltpu.SMEM),
                  qcol(ATT_HEADS * HEAD_DIM), qcol(IDX_HEADS * IDX_DIM), qcol(8),
                  pl.BlockSpec((seq, kvw), lambda b, i: (b, 0)),
                  pl.BlockSpec((nb, kvw, PAGE), lambda b, i: (b, 0, 0)),
                  pl.BlockSpec((seq, IDX_DIM), lambda b, i: (b, 0))],
        out_specs=pl.BlockSpec((PAGE, ATT_HEADS * HEAD_DIM), lambda b, i: (b * nb + i, 0)),
        out_shape=jax.ShapeDtypeStruct((rows, ATT_HEADS * HEAD_DIM), BF16),
        scratch_shapes=[pltpu.VMEM((nb, PAGE, PAGE), I32), pltpu.VMEM((nb, PAGE, PAGE), F32),
                        pltpu.VMEM((nb // 2, 32, SUBLANES, PAGE), I32), pltpu.VMEM((nb // 2, SUBLANES, PAGE), I32),
                        pltpu.VMEM((nb, PAGE, PAGE), F32),
                        pltpu.VMEM((ATT_HEADS, 3, PAGE, PAGE), F32),
                        pltpu.VMEM((ATT_KV_HEADS, nb, PAGE, ATT_GROUP * PAGE), F32),
                        pltpu.VMEM((ATT_KV_HEADS, HEAD_DIM, ATT_GROUP * PAGE), F32)],
        compiler_params=_params(2),
        name="dsa_prompt",
    )(rel_bias, qT, qiT, wiT, kb, vT3, kib)


TQ = 8
SEL_SAMPLES = 8


def _dsa_select_kernel(pt_ref, qi_ref, zl_ref, cki_hbm, mb_ref, sc_sc, kibuf, sems, *, n_pages, n_new):
    ns = SEL_SAMPLES
    n_fetch = ns * n_pages
    b = pl.program_id(0)
    slot = b % 2

    def page_copies(step, slot):
        return [pltpu.make_async_copy(cki_hbm.at[pt_ref[step * n_fetch + f]], kibuf.at[slot, f], sems.at[slot])
                for f in range(n_fetch)]

    @pl.when(b == 0)
    def _start_first():
        for c in page_copies(0, 0):
            c.start()

    @pl.when(b + 1 < pl.num_programs(0))
    def _start_next():
        for c in page_copies(b + 1, 1 - slot):
            c.start()

    for c in page_copies(b, slot):
        c.wait()
    ki_pages = [kibuf.at[slot, f] for f in range(n_fetch)]
    n_blk = n_pages + 1
    rows = ns * TQ
    trow = lax.broadcasted_iota(I32, (TQ, PAGE), 0)
    scol = lax.broadcasted_iota(I32, (TQ, PAGE), 1)
    pad_rows = lambda a: jnp.concatenate([a, jnp.zeros((PAGE - TQ, a.shape[1]), a.dtype)], axis=0)
    nt = (((1,), (1,)), ((), ()))

    for s in range(ns):
        zl = zl_ref[s]
        qi = qi_ref[s]
        for p in range(n_blk):
            if p == n_pages:
                ki_new = pad_rows(zl[:, :IDX_DIM]).astype(BF16)
                idx = lax.dot_general(qi, ki_new, nt, preferred_element_type=F32)
            else:
                idx = jnp.dot(qi, ki_pages[s * n_pages + p][...].astype(BF16), preferred_element_type=F32)
            sc = jnp.zeros((TQ, PAGE), F32)
            for hh in range(IDX_HEADS):
                sc = sc + zl[:, IDX_DIM + hh:IDX_DIM + hh + 1] * jnp.maximum(idx[hh * TQ:(hh + 1) * TQ], 0.0)
            sc = _canonical_zero(sc)
            if p == n_pages:
                sc = jnp.where(jnp.logical_and(scol <= trow, scol < n_new), sc, -jnp.inf)
            sc_sc[s * TQ:(s + 1) * TQ, p * PAGE:(p + 1) * PAGE] = sc

    kprime = float(min(TOPK_MAX, (n_pages * PAGE + n_new) // 4))

    def count(pred_of_scores):
        return jnp.sum(jnp.where(pred_of_scores(sc_sc[...]), 1.0, 0.0), axis=1, keepdims=True)

    cand0 = jnp.zeros((rows, 1), I32)
    thr0 = jnp.where(count(lambda sc: sc >= _key_to_float(cand0)) >= kprime, cand0, INT_MIN)

    def bit_body(bi, thr_key):
        cand = thr_key | lax.shift_left(jnp.int32(1), 30 - bi)
        return jnp.where(count(lambda sc: sc >= _key_to_float(cand)) >= kprime, cand, thr_key)

    thr = _key_to_float(lax.fori_loop(0, 31, bit_body, thr0))
    need = kprime - count(lambda sc: sc > thr)

    s_r = lax.broadcasted_iota(I32, (PAGE, PAGE), 0)
    s_c = lax.broadcasted_iota(I32, (PAGE, PAGE), 1)
    triu = jnp.where(s_r < s_c, 1.0, 0.0).astype(BF16)
    carry = jnp.zeros((rows, 1), F32)
    for p in range(n_blk):
        sc = sc_sc[:, p * PAGE:(p + 1) * PAGE]
        eqf = jnp.where(sc == thr, 1.0, 0.0)
        rank = jnp.dot(eqf.astype(BF16), triu, preferred_element_type=F32) + carry
        tie = jnp.where(jnp.logical_and(sc == thr, rank < need), 0.0, NEG)
        mb_ref[:, p * PAGE:(p + 1) * PAGE] = jnp.where(sc > thr, 0.0, tie)
        carry = carry + jnp.sum(eqf, axis=1, keepdims=True)


def _dsa_select(page_table, qi8, zl8, cache_ki, *, n_new):
    nsamp, n_pages = page_table.shape
    ns = SEL_SAMPLES
    width = (n_pages + 1) * PAGE

    in_specs = [pl.BlockSpec((ns, IDX_HEADS * TQ, IDX_DIM), lambda b, pt: (b, 0, 0)),
                pl.BlockSpec((ns, TQ, LANES), lambda b, pt: (b, 0, 0)),
                pl.BlockSpec(memory_space=pl.ANY)]
    grid_spec = pltpu.PrefetchScalarGridSpec(
        num_scalar_prefetch=1, grid=(nsamp // ns,), in_specs=in_specs,
        out_specs=pl.BlockSpec((ns * TQ, width), lambda b, pt: (b, 0)),
        scratch_shapes=[pltpu.VMEM((ns * TQ, width), F32),
                        pltpu.VMEM((2, ns * n_pages, IDX_DIM, PAGE), F32), pltpu.SemaphoreType.DMA((2,))])
    return pl.pallas_call(
        functools.partial(_dsa_select_kernel, n_pages=n_pages, n_new=n_new),
        grid_spec=grid_spec,
        out_shape=jax.ShapeDtypeStruct((nsamp * TQ, width), F32),
        compiler_params=_params(1),
        name="dsa_select",
    )(page_table.reshape(-1), qi8, zl8, cache_ki)


def _dsa_sample_kernel(pt_ref, relb_ref, q_ref, mb_ref, kn_ref, vn_ref, ck_hbm, cv_hbm, o_ref,
                       bias_sc, kall_sc, vall_sc, kbuf, vbuf, sems, *, n_pages):
    past = n_pages * PAGE
    rows_g = ATT_GROUP * TQ
    page_rows = PAGE * ATT_KV_HEADS
    b = pl.program_id(0)
    slot = b % 2

    def page_copies(sample, slot):
        copies = []
        for p in range(n_pages):
            src = pl.ds(pl.multiple_of(pt_ref[sample * n_pages + p] * page_rows, page_rows), page_rows)
            dst = pl.ds(p * page_rows, page_rows)
            copies.append(pltpu.make_async_copy(ck_hbm.at[src, :], kbuf.at[slot, dst, :], sems.at[slot, 0]))
            copies.append(pltpu.make_async_copy(cv_hbm.at[src, :], vbuf.at[slot, dst, :], sems.at[slot, 1]))
        return copies

    @pl.when(b == 0)
    def _start_first():
        for c in page_copies(0, 0):
            c.start()

    @pl.when(b + 1 < pl.num_programs(0))
    def _start_next():
        for c in page_copies(b + 1, 1 - slot):
            c.start()

    for c in page_copies(b, slot):
        c.wait()

    trow = lax.broadcasted_iota(I32, (TQ, PAGE), 0)
    scol = lax.broadcasted_iota(I32, (TQ, PAGE), 1)

    @pl.when(pl.program_id(0) == 0)
    def _init_bias():
        for g in range(ATT_KV_HEADS):
            for r in range(ATT_GROUP):
                hh = ATT_GROUP * g + r
                rs = slice(r * TQ, (r + 1) * TQ)
                far = jnp.full((TQ, PAGE), relb_ref[NUM_BUCKETS - 1, hh], F32)
                for p in range(n_pages - 1):
                    bias_sc[g, rs, p * PAGE:(p + 1) * PAGE] = far
                bias_sc[g, rs, past - PAGE:past] = _bias_from_distance(
                    relb_ref, hh, jnp.maximum(PAGE + trow - scol, 0))
                bias_sc[g, rs, past:past + PAGE] = _bias_from_distance(relb_ref, hh, jnp.maximum(trow - scol, 0))

    nt = (((1,), (1,)), ((), ()))
    mb = jnp.concatenate([mb_ref[...]] * ATT_GROUP, axis=0)
    pad_rows = lambda a: jnp.concatenate([a, jnp.zeros((PAGE - TQ, a.shape[1]), a.dtype)], axis=0)
    for g in range(ATT_KV_HEADS):
        gs = slice(g * HEAD_DIM, (g + 1) * HEAD_DIM)
        for p in range(n_pages):
            ps = slice(p * PAGE, (p + 1) * PAGE)
            head_rows = pl.ds(p * page_rows + g, PAGE, stride=ATT_KV_HEADS)
            kall_sc[g, ps, :] = kbuf[slot, head_rows, :].astype(BF16)
            vall_sc[g, ps, :] = vbuf[slot, head_rows, :].astype(BF16)
        kall_sc[g, past:past + PAGE, :] = pad_rows(kn_ref[:, gs]).astype(BF16)
        vall_sc[g, past:past + PAGE, :] = pad_rows(vn_ref[:, gs]).astype(BF16)
    for g in range(ATT_KV_HEADS):
        qg = q_ref[g * rows_g:(g + 1) * rows_g, :]
        logits = lax.dot_general(qg, kall_sc[g], nt, preferred_element_type=F32)
        logits = logits + (mb + bias_sc[g])
        m = jnp.max(logits, axis=1, keepdims=True)
        pr = jnp.exp(logits - m)
        l = jnp.sum(pr, axis=1, keepdims=True)
        acc = jnp.dot(pr.astype(BF16), vall_sc[g], preferred_element_type=F32)
        o_ref[g * rows_g:(g + 1) * rows_g, :] = (acc * (1.0 / l)).astype(BF16)


def _dsa_sample(page_table, rel_bias, q8, mb, kn8, vn8, cache_k, cache_v):
    nsamp, n_pages = page_table.shape
    kvw = ATT_KV_HEADS * HEAD_DIM
    per_sample = lambda *tail: pl.BlockSpec((None,) + tail, lambda b, pt: (b,) + (0,) * len(tail))

    page_buf = pltpu.VMEM((2, n_pages * PAGE * ATT_KV_HEADS, HEAD_DIM), F32)
    in_specs = [pl.BlockSpec(memory_space=pltpu.SMEM),
                per_sample(ATT_HEADS * TQ, HEAD_DIM),
                pl.BlockSpec((TQ, mb.shape[1]), lambda b, pt: (b, 0)),
                per_sample(TQ, kvw), per_sample(TQ, kvw),
                pl.BlockSpec(memory_space=pl.ANY), pl.BlockSpec(memory_space=pl.ANY)]
    grid_spec = pltpu.PrefetchScalarGridSpec(
        num_scalar_prefetch=1, grid=(nsamp,), in_specs=in_specs,
        out_specs=per_sample(ATT_HEADS * TQ, HEAD_DIM),
        scratch_shapes=[pltpu.VMEM((ATT_KV_HEADS, ATT_GROUP * TQ, mb.shape[1]), F32),
                        pltpu.VMEM((ATT_KV_HEADS, mb.shape[1], HEAD_DIM), BF16),
                        pltpu.VMEM((ATT_KV_HEADS, mb.shape[1], HEAD_DIM), BF16),
                        page_buf, page_buf, pltpu.SemaphoreType.DMA((2, 2))])
    return pl.pallas_call(
        functools.partial(_dsa_sample_kernel, n_pages=n_pages),
        grid_spec=grid_spec,
        out_shape=jax.ShapeDtypeStruct((nsamp, ATT_HEADS * TQ, HEAD_DIM), BF16),
        compiler_params=_params(1),
        name="dsa_sample",
    )(page_table.reshape(-1), rel_bias, q8, mb, kn8, vn8, cache_k, cache_v)


def _post_mix_kernel(x_ref, o_ref, wo_ref, g_ref, w1_ref, w2_ref, y_ref):
    y = x_ref[...] + jnp.dot(o_ref[...], wo_ref[...], preferred_element_type=F32)
    h = (_rms_rows(y) * g_ref[...]).astype(BF16)
    acc = y
    for c in range(D_FF // D_MODEL):
        a = jnp.dot(h, w1_ref[:, c * D_MODEL:(c + 1) * D_MODEL], preferred_element_type=F32)
        a = jnp.square(jnp.maximum(a, 0.0)).astype(BF16)
        acc = acc + jnp.dot(a, w2_ref[c * D_MODEL:(c + 1) * D_MODEL, :], preferred_element_type=F32)
    y_ref[...] = acc


def _post_mix(x, o, wo, g, w1, w2):
    rows = x.shape[0]
    tm = ROW_TILE
    ko = o.shape[1]
    return pl.pallas_call(
        _post_mix_kernel,
        grid=(rows // tm,),
        in_specs=[pl.BlockSpec((tm, D_MODEL), lambda i: (i, 0)), pl.BlockSpec((tm, ko), lambda i: (i, 0)),
                  _const_spec((ko, D_MODEL)), _const_spec((1, D_MODEL)),
                  _const_spec((D_MODEL, D_FF)), _const_spec((D_FF, D_MODEL))],
        out_specs=pl.BlockSpec((tm, D_MODEL), lambda i: (i, 0)),
        out_shape=jax.ShapeDtypeStruct((rows, D_MODEL), F32),
        compiler_params=_params(1),
        name="post_mix",
    )(x, o, wo, g, w1, w2)


def _ret_proj_kernel(x_ref, g_ref, w_ref, cos_ref, sin_ref, q_ref, k_ref, v_ref, gate_ref):
    hb = (_rms_rows(x_ref[...]) * g_ref[...]).astype(BF16)
    cos = cos_ref[...]
    sin = sin_ref[...]
    half = RET_KEY // 2
    qk = RET_HEADS * RET_KEY
    vd = RET_HEADS * RET_VAL

    def rotated(col0, scale):
        z = jnp.dot(hb, w_ref[:, col0:col0 + RET_KEY], preferred_element_type=F32)
        x1, x2 = z[:, :half], z[:, half:]
        return (x1 * cos - x2 * sin) * scale, (x1 * sin + x2 * cos) * scale

    for hh in range(RET_HEADS):
        a, b = rotated(hh * RET_KEY, 1.0)
        q_ref[:, hh * RET_KEY:hh * RET_KEY + half] = a
        q_ref[:, hh * RET_KEY + half:(hh + 1) * RET_KEY] = b
        a, b = rotated(qk + hh * RET_KEY, RET_KSCALE)
        k_ref[:, hh * RET_KEY:hh * RET_KEY + half] = a
        k_ref[:, hh * RET_KEY + half:(hh + 1) * RET_KEY] = b
    for hh in range(RET_HEADS):
        cs = slice(hh * RET_VAL, (hh + 1) * RET_VAL)
        v_ref[:, cs] = jnp.dot(hb, w_ref[:, 2 * qk + hh * RET_VAL:2 * qk + (hh + 1) * RET_VAL],
                               preferred_element_type=F32).astype(BF16)
        gate_ref[:, cs] = jnp.dot(hb, w_ref[:, 2 * qk + vd + hh * RET_VAL:2 * qk + vd + (hh + 1) * RET_VAL],
                                  preferred_element_type=F32)


def _ret_proj(x, g, w, cos, sin):
    rows = x.shape[0]
    tm = ROW_TILE
    qk = RET_HEADS * RET_KEY
    vd = RET_HEADS * RET_VAL
    n_pos_tiles = cos.shape[0] // tm
    row_spec = lambda width: pl.BlockSpec((tm, width), lambda i: (i, 0))
    pos_spec = pl.BlockSpec((tm, RET_KEY // 2), lambda i: (i % n_pos_tiles, 0))
    return pl.pallas_call(
        _ret_proj_kernel,
        grid=(rows // tm,),
        in_specs=[row_spec(D_MODEL), _const_spec((1, D_MODEL)), _const_spec((D_MODEL, 2 * qk + 2 * vd)),
                  pos_spec, pos_spec],
        out_specs=[row_spec(qk), row_spec(qk), row_spec(vd), row_spec(vd)],
        out_shape=[jax.ShapeDtypeStruct((rows, qk), F32), jax.ShapeDtypeStruct((rows, qk), F32),
                   jax.ShapeDtypeStruct((rows, vd), BF16), jax.ShapeDtypeStruct((rows, vd), F32)],
        compiler_params=_params(1),
        name="ret_proj",
    )(x, g, w, cos, sin)


def _retention_kernel(cd_ref, q_ref, k_ref, v_ref, g_ref, dm_ref, qd_ref, kd_ref, *rest, has_state0):
    if has_state0:
        s0_ref, o_ref, s_ref = rest
    else:
        o_ref, s_ref = rest

    @pl.when(pl.program_id(1) == 0)
    def _init_state():
        s_ref[...] = s0_ref[...] if has_state0 else jnp.zeros_like(s_ref)

    nt = (((1,), (1,)), ((), ()))
    tn = (((0,), (0,)), ((), ()))
    units = [(bi, hh) for bi in range(q_ref.shape[0]) for hh in range(RET_HEADS)]
    ks = lambda hh: slice(hh * RET_KEY, (hh + 1) * RET_KEY)
    vs = lambda hh: slice(hh * RET_VAL, (hh + 1) * RET_VAL)
    att = [lax.dot_general(q_ref[bi, :, ks(hh)].astype(BF16), k_ref[bi, :, ks(hh)].astype(BF16), nt,
                           preferred_element_type=F32) * dm_ref[hh] for bi, hh in units]
    cross = [jnp.dot((q_ref[bi, :, ks(hh)] * qd_ref[hh]).astype(BF16), s_ref[bi, hh].astype(BF16),
                     preferred_element_type=F32) for bi, hh in units]
    for (bi, hh), a, x in zip(units, att, cross):
        o = jnp.dot(a.astype(BF16), v_ref[bi, :, vs(hh)], preferred_element_type=F32) + x
        gate = g_ref[bi, :, vs(hh)]
        o_ref[bi, :, vs(hh)] = (_rms_rows(o) * (gate * jax.nn.sigmoid(gate))).astype(BF16)
    for bi, hh in units:
        kd = (k_ref[bi, :, ks(hh)] * kd_ref[hh]).astype(BF16)
        s_ref[bi, hh] = cd_ref[hh] * s_ref[bi, hh] + lax.dot_general(kd, v_ref[bi, :, vs(hh)], tn,
                                                                     preferred_element_type=F32)


def _retention(q, k, v, gate, consts, state0, *, seqs_per_step):
    dm, qd, kd, cd = consts
    nb, t, _ = q.shape
    c = dm.shape[1]
    bb = seqs_per_step
    qk = RET_HEADS * RET_KEY
    vd = RET_HEADS * RET_VAL
    tok = lambda width: pl.BlockSpec((bb, c, width), lambda b, ci: (b, ci, 0))
    st_spec = pl.BlockSpec((bb, RET_HEADS, RET_KEY, RET_VAL), lambda b, ci: (b, 0, 0, 0))
    in_specs = [pl.BlockSpec(memory_space=pltpu.SMEM), tok(qk), tok(qk), tok(vd), tok(vd),
                _const_spec(dm.shape), _const_spec(qd.shape), _const_spec(kd.shape)]
    args = [cd, q, k, v, gate, dm, qd, kd]
    if state0 is not None:
        in_specs.append(st_spec)
        args.append(state0)
    return pl.pallas_call(
        functools.partial(_retention_kernel, has_state0=state0 is not None),
        grid=(nb // bb, t // c),
        in_specs=in_specs,
        out_specs=[tok(vd), st_spec],
        out_shape=[jax.ShapeDtypeStruct((nb, t, vd), BF16),
                   jax.ShapeDtypeStruct((nb, RET_HEADS, RET_KEY, RET_VAL), F32)],
        compiler_params=_params(2),
        name="retention_s" if state0 is not None else "retention_p",
    )(*args)


def _retention_consts(c, c_pad):
    log_g = jnp.log1p(-jnp.exp2(-5.0 - jnp.arange(RET_HEADS, dtype=F32)))
    i = jnp.arange(c, dtype=F32)
    diff = i[:, None] - i[None, :]
    dm = jnp.where(diff >= 0, jnp.exp(log_g[:, None, None] * jnp.maximum(diff, 0.0)), 0.0)
    qd = jnp.exp(log_g[:, None] * (i[None, :] + 1.0))
    kd = jnp.exp(log_g[:, None] * (c - 1.0 - i[None, :]))
    cd = jnp.exp(log_g * c)
    pad = c_pad - c
    dm = jnp.pad(dm, ((0, 0), (0, pad), (0, pad)))
    qd = jnp.broadcast_to(jnp.pad(qd, ((0, 0), (0, pad)))[:, :, None], (RET_HEADS, c_pad, RET_KEY))
    kd = jnp.broadcast_to(jnp.pad(kd, ((0, 0), (0, pad)))[:, :, None], (RET_HEADS, c_pad, RET_KEY))
    return dm, qd, kd, cd


def _rotation_tables(pos):
    half = RET_KEY // 2
    theta = 1.0 / (10000.0 ** jnp.linspace(0.0, 1.0, half, dtype=F32))
    ang = pos.astype(F32)[:, None] * theta[None, :]
    return jnp.cos(ang), jnp.sin(ang)


def kernel(x_prompt, x_sample, cache_k, cache_v, cache_kidx, state_ret, page_table, rel_bias, ln_mix, ln_mlp,
           att_w_in, att_q_gain, att_k_gain, att_w_out, ret_w_in, ret_w_out, mlp_w_in, mlp_w_out):
    bp, tp, _ = x_prompt.shape
    bs, ts, _ = x_sample.shape
    n_phys = cache_k.shape[1]
    assert ln_mix.shape[0] == 2 and att_w_in.shape[0] == 1 and ret_w_in.shape[0] == 1
    assert tp % RET_CHUNK == 0 and ts <= TQ and cache_k.shape[2] == PAGE
    assert (tp // PAGE) % 4 == 0 and bs % SEL_SAMPLES == 0
    kvw = ATT_KV_HEADS * HEAD_DIM
    past = page_table.shape[1] * PAGE

    w_att = jnp.pad(att_w_in[0], ((0, 0), (0, ATT_IN_PAD - ATT_IN))).astype(BF16)
    w_att_out = att_w_out[0].astype(BF16)
    w_ret = ret_w_in[0].astype(BF16)
    w_ret_out = ret_w_out[0].astype(BF16)
    w1 = mlp_w_in.astype(BF16)
    w2 = mlp_w_out.astype(BF16)
    row = lambda a: a.reshape(1, -1)

    xp = x_prompt.reshape(bp * tp, D_MODEL)
    xs = x_sample.reshape(bs * ts, D_MODEL)

    qT, qiT, wiT, vT3, kb, kib, kp, vp, kip = _attn_proj(
        xp, row(ln_mix[0]), w_att, row(att_q_gain[0]), row(att_k_gain[0]), transposed=True)
    op = _dsa_prompt(rel_bias, qT, qiT, wiT, kb, vT3, kib, batch=bp, seq=tp)
    yp = _post_mix(xp, op, w_att_out, row(ln_mlp[0]), w1[0], w2[0])

    qs, qis, zls, ks, vs, kis = _attn_proj(
        xs, row(ln_mix[0]), w_att, row(att_q_gain[0]), row(att_k_gain[0]), transposed=False)
    pad_t = lambda a: jnp.pad(a, ((0, 0), (0, TQ - ts)) + ((0, 0),) * (a.ndim - 2))
    q8 = pad_t(qs.reshape(bs, ts, ATT_HEADS, HEAD_DIM)).transpose(0, 2, 1, 3)
    q8 = q8.reshape(bs, ATT_HEADS * TQ, HEAD_DIM).astype(BF16)
    qi8 = pad_t(qis.reshape(bs, ts, IDX_HEADS, IDX_DIM)).transpose(0, 2, 1, 3)
    qi8 = qi8.reshape(bs, IDX_HEADS * TQ, IDX_DIM).astype(BF16)
    zl8 = pad_t(zls.reshape(bs, ts, LANES))
    kn8 = pad_t(ks.reshape(bs, ts, kvw))
    vn8 = pad_t(vs.reshape(bs, ts, kvw))
    mb = _dsa_select(page_table, qi8, zl8, jnp.swapaxes(cache_kidx[0], 1, 2), n_new=ts)
    os8 = _dsa_sample(page_table, rel_bias, q8, mb, kn8, vn8,
                      cache_k[0].reshape(n_phys * PAGE * ATT_KV_HEADS, HEAD_DIM),
                      cache_v[0].reshape(n_phys * PAGE * ATT_KV_HEADS, HEAD_DIM))
    os_ = os8.reshape(bs, ATT_HEADS, TQ, HEAD_DIM)[:, :, :ts].transpose(0, 2, 1, 3)
    os_ = os_.reshape(bs * ts, ATT_HEADS * HEAD_DIM)
    ys = _post_mix(xs, os_, w_att_out, row(ln_mlp[0]), w1[0], w2[0])

    cos_p, sin_p = _rotation_tables(jnp.arange(tp, dtype=jnp.int32))
    q, k, v, gate = _ret_proj(yp, row(ln_mix[1]), w_ret, cos_p, sin_p)
    r3 = lambda a: a.reshape(bp, tp, a.shape[-1])
    orp, state_p = _retention(r3(q), r3(k), r3(v), r3(gate), _retention_consts(RET_CHUNK, RET_CHUNK), None,
                            seqs_per_step=1)
    yp = _post_mix(yp, orp.reshape(bp * tp, -1), w_ret_out, row(ln_mlp[1]), w1[1], w2[1])

    cos_s, sin_s = _rotation_tables(past + jnp.arange(ts, dtype=jnp.int32))
    tile_s = lambda a: jnp.tile(a, (bs, 1))
    q, k, v, gate = _ret_proj(ys, row(ln_mix[1]), w_ret, tile_s(cos_s), tile_s(sin_s))
    r3s = lambda a: pad_t(a.reshape(bs, ts, a.shape[-1]))
    ors, state_s = _retention(r3s(q), r3s(k), r3s(v), r3s(gate), _retention_consts(ts, TQ), state_ret[0],
                            seqs_per_step=2)
    ys = _post_mix(ys, ors[:, :ts].reshape(bs * ts, -1), w_ret_out, row(ln_mlp[1]), w1[1], w2[1])

    return (yp.reshape(bp, tp, D_MODEL), ys.reshape(bs, ts, D_MODEL),
            kp.reshape(1, bp, tp, ATT_KV_HEADS, HEAD_DIM), vp.reshape(1, bp, tp, ATT_KV_HEADS, HEAD_DIM),
            kip.reshape(1, bp, tp, IDX_DIM), state_p[None],
            ks.reshape(1, bs, ts, ATT_KV_HEADS, HEAD_DIM), vs.reshape(1, bs, ts, ATT_KV_HEADS, HEAD_DIM),
            kis.reshape(1, bs, ts, IDX_DIM), state_s[None])
```

```python
import functools
import math

import jax
import jax.numpy as jnp
import numpy as np
from jax import lax
from jax.experimental import pallas as pl
from jax.experimental.pallas import tpu as pltpu

F32 = jnp.float32
BF16 = jnp.bfloat16
I32 = jnp.int32

D_MODEL = 1024
PAGE = 128
ATT_HEADS = 8
ATT_KV_HEADS = 2
ATT_GROUP = ATT_HEADS // ATT_KV_HEADS
HEAD_DIM = 128
IDX_HEADS = 4
IDX_DIM = 64
TOPK_MAX = 256
NUM_BUCKETS = 32
MAX_DISTANCE = 128
RET_HEADS = 4
RET_KEY = 256
RET_VAL = 512
RET_CHUNK = 256
D_FF = 4 * D_MODEL
EPS = 1e-6

Q_OFF = 0
K_OFF = ATT_HEADS * HEAD_DIM
V_OFF = K_OFF + ATT_KV_HEADS * HEAD_DIM
QI_OFF = V_OFF + ATT_KV_HEADS * HEAD_DIM
KI_OFF = QI_OFF + IDX_HEADS * IDX_DIM
WI_OFF = KI_OFF + IDX_DIM
ATT_IN = WI_OFF + IDX_HEADS
ATT_IN_PAD = 1920

LANES = 128
SUBLANES = 8
ROW_TILE = 512
VMEM_LIMIT = 56 * 1024 * 1024

ATT_SCALE = HEAD_DIM ** -0.5
RET_KSCALE = RET_KEY ** -0.5
INT_MIN = -(2 ** 31)
NEG = -1e30


def _bucket_bounds():
    n = np.arange(0, 8192)
    max_exact = NUM_BUCKETS // 2
    val = np.log(np.maximum(n, 1) / max_exact) / math.log(MAX_DISTANCE / max_exact) * (NUM_BUCKETS - max_exact)
    frac = np.abs(val - np.round(val))
    risky = (n > max_exact) & (n < MAX_DISTANCE) & (frac < 1e-4)
    assert not risky.any()
    large = np.minimum(max_exact + np.floor(val + 1e-9).astype(np.int64), NUM_BUCKETS - 1)
    bucket = np.where(n < max_exact, n, large)
    assert (np.diff(bucket) >= 0).all() and (bucket[MAX_DISTANCE:] == NUM_BUCKETS - 1).all()
    return [int(np.argmax(bucket >= b)) for b in range(NUM_BUCKETS)]


BUCKET_BOUNDS = _bucket_bounds()


def _const_spec(shape):
    zeros = (0,) * len(shape)
    return pl.BlockSpec(shape, lambda *_: zeros, pipeline_mode=pl.Buffered(1))


def _params(n_axes):
    return pltpu.CompilerParams(dimension_semantics=("arbitrary",) * n_axes,
                                vmem_limit_bytes=VMEM_LIMIT)


def _rms_rows(a):
    return a * lax.rsqrt(jnp.mean(a * a, axis=-1, keepdims=True) + EPS)


def _canonical_zero(score):
    return jnp.where(score == 0.0, 0.0, score)


def _sortable_key(score):
    bits = pltpu.bitcast(score, I32)
    return bits ^ ((bits >> 31) & jnp.int32(0x7FFFFFFF))


def _key_to_float(key):
    return pltpu.bitcast(key ^ ((key >> 31) & jnp.int32(0x7FFFFFFF)), F32)


KEY_NEG_INF = -2139095041
KEY_POS_INF = 2139095040
KEY_SUBNORMAL_LO = -8388608
KEY_SUBNORMAL_HI = 8388607


def _bracket_kth_largest(count_ge, kprime, guess):
    def canon(x):
        return jnp.where(jnp.logical_and(x >= KEY_SUBNORMAL_LO, x <= KEY_SUBNORMAL_HI), 0, x)

    def succ(x):
        xc = canon(x)
        return jnp.where(xc == 0, KEY_SUBNORMAL_HI + 1, canon(xc + 1))

    g0 = jnp.clip(guess, KEY_NEG_INF, KEY_POS_INF - 1)
    g1 = succ(g0)
    lo = jnp.full(guess.shape, KEY_NEG_INF, I32)
    hi = jnp.full(guess.shape, KEY_POS_INF, I32)

    def narrow(lo, hi, x, cnt):
        ok = cnt >= kprime
        return jnp.where(ok, jnp.maximum(lo, x), lo), jnp.where(ok, hi, jnp.minimum(hi, x))

    c0, c1 = count_ge([_key_to_float(g0), _key_to_float(g1)])
    lo, hi = narrow(lo, hi, g0, c0)
    lo, hi = narrow(lo, hi, g1, c1)

    def open_bracket(c):
        return jnp.max(jnp.where(canon(c[1]) > succ(c[0]), 1.0, 0.0)) > 0.5

    def bisect(c):
        lo, hi = c
        mid = jnp.where(canon(hi) > succ(lo), (lo >> 1) + (hi >> 1) + (lo & hi & 1), lo)
        return narrow(lo, hi, mid, count_ge([_key_to_float(mid)])[0])

    lo, _ = lax.while_loop(open_bracket, bisect, (lo, hi))
    return canon(lo)


def _bit_transpose32(words):
    a = list(words)
    j, m = 16, 0x0000FFFF
    while j:
        k = 0
        while k < 32:
            t = (a[k] ^ lax.shift_right_logical(a[k + j], jnp.int32(j))) & jnp.int32(m)
            a[k] = a[k] ^ t
            a[k + j] = a[k + j] ^ lax.shift_left(t, jnp.int32(j))
            k = (k + j + 1) & ~j
        j >>= 1
        m ^= (m << j) & 0xFFFFFFFF
    return a


def _bias_from_distance(relb_ref, head, dist):
    val = jnp.full(dist.shape, relb_ref[0, head], F32)
    for b in range(1, NUM_BUCKETS):
        val = jnp.where(dist >= BUCKET_BOUNDS[b], relb_ref[b, head], val)
    return val


def _attn_proj_kernel(x_ref, g_ref, w_ref, qg_ref, kg_ref, *outs, transposed):
    x = x_ref[...]
    h = _rms_rows(x) * g_ref[...]
    z = jnp.dot(h.astype(BF16), w_ref[...], preferred_element_type=F32)
    qg = qg_ref[...]
    kg = kg_ref[...]
    k = jnp.concatenate(
        [_rms_rows(z[:, K_OFF + g * HEAD_DIM:K_OFF + (g + 1) * HEAD_DIM]) * kg for g in range(ATT_KV_HEADS)],
        axis=1)
    v = z[:, V_OFF:QI_OFF]
    qi = z[:, QI_OFF:KI_OFF]
    zl = z[:, KI_OFF:ATT_IN_PAD]
    ki = zl[:, :IDX_DIM]
    if transposed:
        qT_ref, qiT_ref, wiT_ref, vT_ref, kb_ref, kib_ref, k_ref, v_ref, ki_ref = outs
        for hh in range(ATT_HEADS):
            qh = _rms_rows(z[:, hh * HEAD_DIM:(hh + 1) * HEAD_DIM]) * qg * ATT_SCALE
            qT_ref[hh * HEAD_DIM:(hh + 1) * HEAD_DIM, :] = qh.T.astype(BF16)
        for c in range(IDX_HEADS * IDX_DIM // LANES):
            qiT_ref[c * LANES:(c + 1) * LANES, :] = qi[:, c * LANES:(c + 1) * LANES].T.astype(BF16)
        wiT_ref[...] = zl.T[IDX_DIM:IDX_DIM + 8, :]
        for g in range(ATT_KV_HEADS):
            vT = v[:, g * HEAD_DIM:(g + 1) * HEAD_DIM].T
            for c in range(x.shape[0] // PAGE):
                vT_ref[c, g * HEAD_DIM:(g + 1) * HEAD_DIM, :] = vT[:, c * PAGE:(c + 1) * PAGE].astype(BF16)
        kb_ref[...] = k.astype(BF16)
        kib_ref[...] = ki.astype(BF16)
    else:
        q_ref, qi_ref, zl_ref, k_ref, v_ref, ki_ref = outs
        for hh in range(ATT_HEADS):
            q_ref[:, hh * HEAD_DIM:(hh + 1) * HEAD_DIM] = _rms_rows(z[:, hh * HEAD_DIM:(hh + 1) * HEAD_DIM]) * qg * ATT_SCALE
        qi_ref[...] = qi
        zl_ref[...] = zl
    for g in range(ATT_KV_HEADS):
        rows_g = pl.ds(g, x.shape[0], stride=ATT_KV_HEADS)
        k_ref[rows_g, :] = k[:, g * HEAD_DIM:(g + 1) * HEAD_DIM]
        v_ref[rows_g, :] = v[:, g * HEAD_DIM:(g + 1) * HEAD_DIM]
    ki_ref[...] = ki


def _attn_proj(x, g, w, qg, kg, *, transposed):
    rows = x.shape[0]
    tm = ROW_TILE
    nt = rows // tm
    kvw = ATT_KV_HEADS * HEAD_DIM
    row_spec = lambda width: pl.BlockSpec((tm, width), lambda i: (i, 0))
    col_spec = lambda height: pl.BlockSpec((height, tm), lambda i: (0, i))
    kv_rows = ATT_KV_HEADS * rows
    kv_spec = pl.BlockSpec((ATT_KV_HEADS * tm, HEAD_DIM), lambda i: (i, 0))
    leaves_shape = [jax.ShapeDtypeStruct((kv_rows, HEAD_DIM), F32), jax.ShapeDtypeStruct((kv_rows, HEAD_DIM), F32),
                    jax.ShapeDtypeStruct((rows, IDX_DIM), F32)]
    leaves_spec = [kv_spec, kv_spec, row_spec(IDX_DIM)]
    if transposed:
        out_shape = [jax.ShapeDtypeStruct((ATT_HEADS * HEAD_DIM, rows), BF16),
                     jax.ShapeDtypeStruct((IDX_HEADS * IDX_DIM, rows), BF16),
                     jax.ShapeDtypeStruct((8, rows), F32),
                     jax.ShapeDtypeStruct((rows // PAGE, kvw, PAGE), BF16),
                     jax.ShapeDtypeStruct((rows, kvw), BF16),
                     jax.ShapeDtypeStruct((rows, IDX_DIM), BF16)] + leaves_shape
        out_specs = [col_spec(ATT_HEADS * HEAD_DIM), col_spec(IDX_HEADS * IDX_DIM), col_spec(8),
                     pl.BlockSpec((tm // PAGE, kvw, PAGE), lambda i: (i, 0, 0)),
                     row_spec(kvw), row_spec(IDX_DIM)] + leaves_spec
    else:
        out_shape = [jax.ShapeDtypeStruct((rows, ATT_HEADS * HEAD_DIM), F32),
                     jax.ShapeDtypeStruct((rows, IDX_HEADS * IDX_DIM), F32),
                     jax.ShapeDtypeStruct((rows, LANES), F32)] + leaves_shape
        out_specs = [row_spec(ATT_HEADS * HEAD_DIM), row_spec(IDX_HEADS * IDX_DIM), row_spec(LANES)] + leaves_spec
    return pl.pallas_call(
        functools.partial(_attn_proj_kernel, transposed=transposed),
        grid=(nt,),
        in_specs=[row_spec(D_MODEL), _const_spec((1, D_MODEL)), _const_spec((D_MODEL, ATT_IN_PAD)),
                  _const_spec((1, HEAD_DIM)), _const_spec((1, HEAD_DIM))],
        out_specs=out_specs,
        out_shape=out_shape,
        compiler_params=_params(1),
        name="attn_proj_t" if transposed else "attn_proj_r",
    )(x, g, w, qg, kg)


def _dsa_prompt_kernel(relb_ref, qT_ref, qiT_ref, wiT_ref, k_ref, vT_ref, ki_ref, o_ref,
                       keys_sc, scores_sc, planes_sc, live_sc, mb_sc, bias_sc, logit_sc, acc_sc, *, topk):
    b = pl.program_id(0)
    i = pl.program_id(1)
    srow = lax.broadcasted_iota(I32, (PAGE, PAGE), 0)
    tcol = lax.broadcasted_iota(I32, (PAGE, PAGE), 1)

    @pl.when((b == 0) & (i == 0))
    def _init_bias():
        planes_sc[...] = jnp.zeros_like(planes_sc)
        live_sc[...] = jnp.zeros_like(live_sc)
        for hh in range(ATT_HEADS):
            for rel in range(2):
                dist = jnp.maximum(rel * PAGE + tcol - srow, 0)
                bias_sc[hh, rel] = _bias_from_distance(relb_ref, hh, dist)
            bias_sc[hh, 2] = jnp.full((PAGE, PAGE), relb_ref[NUM_BUCKETS - 1, hh], F32)

    wiT = wiT_ref[...]

    npair = (i + 2) // 2

    def score_block(j):
        off = pl.multiple_of(j * PAGE, PAGE)
        kij = ki_ref[pl.ds(off, PAGE), :]
        sc = jnp.zeros((PAGE, PAGE), F32)
        for hh in range(IDX_HEADS):
            idx = jnp.dot(kij, qiT_ref[hh * IDX_DIM:(hh + 1) * IDX_DIM, :], preferred_element_type=F32)
            sc = sc + wiT[hh:hh + 1, :] * jnp.maximum(idx, 0.0)
        sc = _canonical_zero(sc)
        causal = jnp.logical_or(j < i, jnp.logical_and(j == i, srow <= tcol))
        scores_sc[j] = jnp.where(causal, sc, -jnp.inf)
        keys_sc[j] = jnp.where(causal, _sortable_key(sc), INT_MIN)

    nquad = (i + 4) // 4

    def score_body(jj, carry):
        for u in range(4):
            score_block(4 * jj + u)
        return carry

    lax.fori_loop(0, nquad, score_body, 0)

    lane = lax.broadcasted_iota(I32, (1, PAGE), 1)
    kprime = jnp.minimum(topk, i * PAGE + lane + 1).astype(F32)

    def plane_body(jj, carry):
        rows = [keys_sc[2 * jj + blk, k * SUBLANES:(k + 1) * SUBLANES, :] ^ INT_MIN
                for blk in range(2) for k in range(PAGE // SUBLANES)]
        planes = _bit_transpose32(rows)
        live = planes[0]
        for p in range(32):
            planes_sc[jj, p] = planes[p]
            live = live | planes[p]
        live_sc[jj] = live
        return carry

    lax.fori_loop(0, npair, plane_body, 0)

    n_pairs_max = keys_sc.shape[0] // 2

    def bit_body(bi, carry):
        thr_u, alive, above = carry
        cnt = jnp.zeros((SUBLANES, PAGE), I32)
        with_bit, at_least = [], []
        for p in range(n_pairs_max):
            t = alive[p] & planes_sc[p, bi]
            ge = above[p] | t
            cnt = cnt + lax.population_count(ge)
            with_bit.append(t)
            at_least.append(ge)
        take = jnp.sum(cnt.astype(F32), axis=0, keepdims=True) >= kprime
        alive = tuple(jnp.where(take, t, a ^ t) for t, a in zip(with_bit, alive))
        above = tuple(jnp.where(take, g, ge) for g, ge in zip(above, at_least))
        thr_u = thr_u | jnp.where(take, lax.shift_left(jnp.int32(1), 31 - bi), 0)
        return thr_u, alive, above

    zero_words = jnp.zeros((SUBLANES, PAGE), I32)
    thr_u, _, _ = lax.fori_loop(
        0, 32, bit_body,
        (jnp.zeros((1, PAGE), I32),
         tuple(jnp.where(p < npair, live_sc[p], zero_words) for p in range(n_pairs_max)),
         (zero_words,) * n_pairs_max))
    def count_scores(preds):
        def body(jj, cnts):
            for blk in range(2):
                sc = scores_sc[2 * jj + blk]
                cnts = tuple(c + jnp.where(p(sc), 1.0, 0.0) for c, p in zip(cnts, preds))
            return cnts
        cnts = lax.fori_loop(0, npair, body, tuple(jnp.zeros((PAGE, PAGE), F32) for _ in preds))
        return [jnp.sum(c, axis=0, keepdims=True) for c in cnts]

    def count_ge(thresholds):
        return count_scores([lambda sc, x=x: sc >= x for x in thresholds])

    thr = _key_to_float(_bracket_kth_largest(count_ge, kprime, thr_u ^ INT_MIN))
    need = kprime - count_scores([lambda sc: sc > thr])[0]

    tri = jnp.where(srow > tcol, 1.0, 0.0).astype(BF16)

    def mask_block(j, carry):
        sc = scores_sc[j]
        eqf = jnp.where(sc == thr, 1.0, 0.0)
        rank = jnp.dot(tri, eqf.astype(BF16), preferred_element_type=F32) + carry
        tie = jnp.where(jnp.logical_and(sc == thr, rank < need), 0.0, NEG)
        mb_sc[j] = jnp.where(sc > thr, 0.0, tie)
        return carry + jnp.sum(eqf, axis=0, keepdims=True)

    def mask_body(jj, carry):
        for u in range(4):
            carry = mask_block(4 * jj + u, carry)
        return carry

    lax.fori_loop(0, nquad, mask_body, jnp.zeros((1, PAGE), F32))

    groups = range(ATT_KV_HEADS)
    width = ATT_GROUP * PAGE
    qcats = [jnp.concatenate(
        [qT_ref[(ATT_GROUP * g + r) * HEAD_DIM:(ATT_GROUP * g + r + 1) * HEAD_DIM, :] for r in range(ATT_GROUP)],
        axis=1) for g in groups]

    def logit_block(j, ms):
        off = pl.multiple_of(j * PAGE, PAGE)
        mb = mb_sc[j]
        relc = jnp.clip(i - j, 0, 2)
        out = []
        for g in groups:
            kj = k_ref[pl.ds(off, PAGE), g * HEAD_DIM:(g + 1) * HEAD_DIM]
            logits = jnp.dot(kj, qcats[g], preferred_element_type=F32)
            logits = jnp.concatenate(
                [logits[:, r * PAGE:(r + 1) * PAGE] + (mb + bias_sc[ATT_GROUP * g + r, relc])
                 for r in range(ATT_GROUP)], axis=1)
            logit_sc[g, j] = logits
            out.append(jnp.maximum(ms[g], jnp.max(logits, axis=0, keepdims=True)))
        return tuple(out)

    def over_block_pairs(pair_step, carry):
        nfull = npair // 2
        carry = lax.fori_loop(0, nfull, lambda t, c: pair_step(4 * t + 2, pair_step(4 * t, c)), carry)
        return lax.cond(npair % 2 == 1, lambda c: pair_step(4 * nfull, c), lambda c: c, carry)

    ms = over_block_pairs(lambda j0, c: logit_block(j0 + 1, logit_block(j0, c)),
                          tuple(jnp.full((1, width), NEG, F32) for _ in groups))

    def value_block(j, ls):
        contrib, out = [], []
        for g in groups:
            p = jnp.exp(logit_sc[g, j] - ms[g])
            out.append(ls[g] + jnp.sum(p, axis=0, keepdims=True))
            vTj = vT_ref[j, g * HEAD_DIM:(g + 1) * HEAD_DIM, :]
            contrib.append(jnp.dot(vTj, p.astype(BF16), preferred_element_type=F32))
        return contrib, tuple(out)

    def value_pair(j0, ls):
        c0, ls = value_block(j0, ls)
        c1, ls = value_block(j0 + 1, ls)
        for g in groups:
            acc_sc[g] = acc_sc[g] + (c0[g] + c1[g])
        return ls

    acc_sc[...] = jnp.zeros_like(acc_sc)
    ls = over_block_pairs(value_pair, tuple(jnp.zeros((1, width), F32) for _ in groups))
    for g in groups:
        outT = acc_sc[g] * (1.0 / ls[g])
        for r in range(ATT_GROUP):
            hh = ATT_GROUP * g + r
            o_ref[:, hh * HEAD_DIM:(hh + 1) * HEAD_DIM] = outT[:, r * PAGE:(r + 1) * PAGE].T.astype(BF16)


def _dsa_prompt(rel_bias, qT, qiT, wiT, kb, vT3, kib, *, batch, seq):
    nb = seq // PAGE
    rows = batch * seq
    kvw = ATT_KV_HEADS * HEAD_DIM
    qcol = lambda height: pl.BlockSpec((height, PAGE), lambda b, i: (0, b * nb + i))
    return pl.pallas_call(
        functools.partial(_dsa_prompt_kernel, topk=min(TOPK_MAX, seq // 4)),
        grid=(batch, nb),
        in_specs=[pl.BlockSpec(memory_space=pltpu.SMEM),
                  qcol(ATT_HEADS * HEAD_DIM), qcol(IDX_HEADS * IDX_DIM), qcol(8),
                  pl.BlockSpec((seq, kvw), lambda b, i: (b, 0)),
                  pl.BlockSpec((nb, kvw, PAGE), lambda b, i: (b, 0, 0)),
                  pl.BlockSpec((seq, IDX_DIM), lambda b, i: (b, 0))],
        out_specs=pl.BlockSpec((PAGE, ATT_HEADS * HEAD_DIM), lambda b, i: (b * nb + i, 0)),
        out_shape=jax.ShapeDtypeStruct((rows, ATT_HEADS * HEAD_DIM), BF16),
        scratch_shapes=[pltpu.VMEM((nb, PAGE, PAGE), I32), pltpu.VMEM((nb, PAGE, PAGE), F32),
                        pltpu.VMEM((nb // 2, 32, SUBLANES, PAGE), I32), pltpu.VMEM((nb // 2, SUBLANES, PAGE), I32),
                        pltpu.VMEM((nb, PAGE, PAGE), F32),
                        pltpu.VMEM((ATT_HEADS, 3, PAGE, PAGE), F32),
                        pltpu.VMEM((ATT_KV_HEADS, nb, PAGE, ATT_GROUP * PAGE), F32),
                        pltpu.VMEM((ATT_KV_HEADS, HEAD_DIM, ATT_GROUP * PAGE), F32)],
        compiler_params=_params(2),
        name="dsa_prompt",
    )(rel_bias, qT, qiT, wiT, kb, vT3, kib)


TQ = 8
SEL_SAMPLES = 8


def _dsa_select_kernel(pt_ref, qi_ref, zl_ref, cki_hbm, mb_ref, sc_sc, kibuf, sems, *, n_pages, n_new):
    ns = SEL_SAMPLES
    n_fetch = ns * n_pages
    b = pl.program_id(0)
    slot = b % 2

    def page_copies(step, slot):
        return [pltpu.make_async_copy(cki_hbm.at[pt_ref[step * n_fetch + f]], kibuf.at[slot, f], sems.at[slot])
                for f in range(n_fetch)]

    @pl.when(b == 0)
    def _start_first():
        for c in page_copies(0, 0):
            c.start()

    @pl.when(b + 1 < pl.num_programs(0))
    def _start_next():
        for c in page_copies(b + 1, 1 - slot):
            c.start()

    for c in page_copies(b, slot):
        c.wait()
    ki_pages = [kibuf.at[slot, f] for f in range(n_fetch)]
    n_blk = n_pages + 1
    rows = ns * TQ
    trow = lax.broadcasted_iota(I32, (TQ, PAGE), 0)
    scol = lax.broadcasted_iota(I32, (TQ, PAGE), 1)
    pad_rows = lambda a: jnp.concatenate([a, jnp.zeros((PAGE - TQ, a.shape[1]), a.dtype)], axis=0)
    nt = (((1,), (1,)), ((), ()))

    for s in range(ns):
        zl = zl_ref[s]
        qi = qi_ref[s]
        for p in range(n_blk):
            if p == n_pages:
                ki_new = pad_rows(zl[:, :IDX_DIM]).astype(BF16)
                idx = lax.dot_general(qi, ki_new, nt, preferred_element_type=F32)
            else:
                idx = jnp.dot(qi, ki_pages[s * n_pages + p][...].astype(BF16), preferred_element_type=F32)
            sc = jnp.zeros((TQ, PAGE), F32)
            for hh in range(IDX_HEADS):
                sc = sc + zl[:, IDX_DIM + hh:IDX_DIM + hh + 1] * jnp.maximum(idx[hh * TQ:(hh + 1) * TQ], 0.0)
            sc = _canonical_zero(sc)
            if p == n_pages:
                sc = jnp.where(jnp.logical_and(scol <= trow, scol < n_new), sc, -jnp.inf)
            sc_sc[s * TQ:(s + 1) * TQ, p * PAGE:(p + 1) * PAGE] = sc

    kprime = float(min(TOPK_MAX, (n_pages * PAGE + n_new) // 4))

    def count(pred_of_scores):
        return jnp.sum(jnp.where(pred_of_scores(sc_sc[...]), 1.0, 0.0), axis=1, keepdims=True)

    cand0 = jnp.zeros((rows, 1), I32)
    thr0 = jnp.where(count(lambda sc: sc >= _key_to_float(cand0)) >= kprime, cand0, INT_MIN)

    def bit_body(bi, thr_key):
        cand = thr_key | lax.shift_left(jnp.int32(1), 30 - bi)
        return jnp.where(count(lambda sc: sc >= _key_to_float(cand)) >= kprime, cand, thr_key)

    thr = _key_to_float(lax.fori_loop(0, 31, bit_body, thr0))
    need = kprime - count(lambda sc: sc > thr)

    s_r = lax.broadcasted_iota(I32, (PAGE, PAGE), 0)
    s_c = lax.broadcasted_iota(I32, (PAGE, PAGE), 1)
    triu = jnp.where(s_r < s_c, 1.0, 0.0).astype(BF16)
    carry = jnp.zeros((rows, 1), F32)
    for p in range(n_blk):
        sc = sc_sc[:, p * PAGE:(p + 1) * PAGE]
        eqf = jnp.where(sc == thr, 1.0, 0.0)
        rank = jnp.dot(eqf.astype(BF16), triu, preferred_element_type=F32) + carry
        tie = jnp.where(jnp.logical_and(sc == thr, rank < need), 0.0, NEG)
        mb_ref[:, p * PAGE:(p + 1) * PAGE] = jnp.where(sc > thr, 0.0, tie)
        carry = carry + jnp.sum(eqf, axis=1, keepdims=True)


def _dsa_select(page_table, qi8, zl8, cache_ki, *, n_new):
    nsamp, n_pages = page_table.shape
    ns = SEL_SAMPLES
    width = (n_pages + 1) * PAGE

    in_specs = [pl.BlockSpec((ns, IDX_HEADS * TQ, IDX_DIM), lambda b, pt: (b, 0, 0)),
                pl.BlockSpec((ns, TQ, LANES), lambda b, pt: (b, 0, 0)),
                pl.BlockSpec(memory_space=pl.ANY)]
    grid_spec = pltpu.PrefetchScalarGridSpec(
        num_scalar_prefetch=1, grid=(nsamp // ns,), in_specs=in_specs,
        out_specs=pl.BlockSpec((ns * TQ, width), lambda b, pt: (b, 0)),
        scratch_shapes=[pltpu.VMEM((ns * TQ, width), F32),
                        pltpu.VMEM((2, ns * n_pages, IDX_DIM, PAGE), F32), pltpu.SemaphoreType.DMA((2,))])
    return pl.pallas_call(
        functools.partial(_dsa_select_kernel, n_pages=n_pages, n_new=n_new),
        grid_spec=grid_spec,
        out_shape=jax.ShapeDtypeStruct((nsamp * TQ, width), F32),
        compiler_params=_params(1),
        name="dsa_select",
    )(page_table.reshape(-1), qi8, zl8, cache_ki)


def _dsa_sample_kernel(pt_ref, relb_ref, q_ref, mb_ref, kn_ref, vn_ref, ck_hbm, cv_hbm, o_ref,
                       bias_sc, kall_sc, vall_sc, kbuf, vbuf, sems, *, n_pages):
    past = n_pages * PAGE
    rows_g = ATT_GROUP * TQ
    page_rows = PAGE * ATT_KV_HEADS
    b = pl.program_id(0)
    slot = b % 2

    def page_copies(sample, slot):
        copies = []
        for p in range(n_pages):
            src = pl.ds(pl.multiple_of(pt_ref[sample * n_pages + p] * page_rows, page_rows), page_rows)
            dst = pl.ds(p * page_rows, page_rows)
            copies.append(pltpu.make_async_copy(ck_hbm.at[src, :], kbuf.at[slot, dst, :], sems.at[slot, 0]))
            copies.append(pltpu.make_async_copy(cv_hbm.at[src, :], vbuf.at[slot, dst, :], sems.at[slot, 1]))
        return copies

    @pl.when(b == 0)
    def _start_first():
        for c in page_copies(0, 0):
            c.start()

    @pl.when(b + 1 < pl.num_programs(0))
    def _start_next():
        for c in page_copies(b + 1, 1 - slot):
            c.start()

    for c in page_copies(b, slot):
        c.wait()

    trow = lax.broadcasted_iota(I32, (TQ, PAGE), 0)
    scol = lax.broadcasted_iota(I32, (TQ, PAGE), 1)

    @pl.when(pl.program_id(0) == 0)
    def _init_bias():
        for g in range(ATT_KV_HEADS):
            for r in range(ATT_GROUP):
                hh = ATT_GROUP * g + r
                rs = slice(r * TQ, (r + 1) * TQ)
                far = jnp.full((TQ, PAGE), relb_ref[NUM_BUCKETS - 1, hh], F32)
                for p in range(n_pages - 1):
                    bias_sc[g, rs, p * PAGE:(p + 1) * PAGE] = far
                bias_sc[g, rs, past - PAGE:past] = _bias_from_distance(
                    relb_ref, hh, jnp.maximum(PAGE + trow - scol, 0))
                bias_sc[g, rs, past:past + PAGE] = _bias_from_distance(relb_ref, hh, jnp.maximum(trow - scol, 0))

    nt = (((1,), (1,)), ((), ()))
    mb = jnp.concatenate([mb_ref[...]] * ATT_GROUP, axis=0)
    pad_rows = lambda a: jnp.concatenate([a, jnp.zeros((PAGE - TQ, a.shape[1]), a.dtype)], axis=0)
    for g in range(ATT_KV_HEADS):
        gs = slice(g * HEAD_DIM, (g + 1) * HEAD_DIM)
        for p in range(n_pages):
            ps = slice(p * PAGE, (p + 1) * PAGE)
            head_rows = pl.ds(p * page_rows + g, PAGE, stride=ATT_KV_HEADS)
            kall_sc[g, ps, :] = kbuf[slot, head_rows, :].astype(BF16)
            vall_sc[g, ps, :] = vbuf[slot, head_rows, :].astype(BF16)
        kall_sc[g, past:past + PAGE, :] = pad_rows(kn_ref[:, gs]).astype(BF16)
        vall_sc[g, past:past + PAGE, :] = pad_rows(vn_ref[:, gs]).astype(BF16)
    for g in range(ATT_KV_HEADS):
        qg = q_ref[g * rows_g:(g + 1) * rows_g, :]
        logits = lax.dot_general(qg, kall_sc[g], nt, preferred_element_type=F32)
        logits = logits + (mb + bias_sc[g])
        m = jnp.max(logits, axis=1, keepdims=True)
        pr = jnp.exp(logits - m)
        l = jnp.sum(pr, axis=1, keepdims=True)
        acc = jnp.dot(pr.astype(BF16), vall_sc[g], preferred_element_type=F32)
        o_ref[g * rows_g:(g + 1) * rows_g, :] = (acc * (1.0 / l)).astype(BF16)


def _dsa_sample(page_table, rel_bias, q8, mb, kn8, vn8, cache_k, cache_v):
    nsamp, n_pages = page_table.shape
    kvw = ATT_KV_HEADS * HEAD_DIM
    per_sample = lambda *tail: pl.BlockSpec((None,) + tail, lambda b, pt: (b,) + (0,) * len(tail))

    page_buf = pltpu.VMEM((2, n_pages * PAGE * ATT_KV_HEADS, HEAD_DIM), F32)
    in_specs = [pl.BlockSpec(memory_space=pltpu.SMEM),
                per_sample(ATT_HEADS * TQ, HEAD_DIM),
                pl.BlockSpec((TQ, mb.shape[1]), lambda b, pt: (b, 0)),
                per_sample(TQ, kvw), per_sample(TQ, kvw),
                pl.BlockSpec(memory_space=pl.ANY), pl.BlockSpec(memory_space=pl.ANY)]
    grid_spec = pltpu.PrefetchScalarGridSpec(
        num_scalar_prefetch=1, grid=(nsamp,), in_specs=in_specs,
        out_specs=per_sample(ATT_HEADS * TQ, HEAD_DIM),
        scratch_shapes=[pltpu.VMEM((ATT_KV_HEADS, ATT_GROUP * TQ, mb.shape[1]), F32),
                        pltpu.VMEM((ATT_KV_HEADS, mb.shape[1], HEAD_DIM), BF16),
                        pltpu.VMEM((ATT_KV_HEADS, mb.shape[1], HEAD_DIM), BF16),
                        page_buf, page_buf, pltpu.SemaphoreType.DMA((2, 2))])
    return pl.pallas_call(
        functools.partial(_dsa_sample_kernel, n_pages=n_pages),
        grid_spec=grid_spec,
        out_shape=jax.ShapeDtypeStruct((nsamp, ATT_HEADS * TQ, HEAD_DIM), BF16),
        compiler_params=_params(1),
        name="dsa_sample",
    )(page_table.reshape(-1), rel_bias, q8, mb, kn8, vn8, cache_k, cache_v)


def _post_mix_kernel(x_ref, o_ref, wo_ref, g_ref, w1_ref, w2_ref, y_ref):
    y = x_ref[...] + jnp.dot(o_ref[...], wo_ref[...], preferred_element_type=F32)
    h = (_rms_rows(y) * g_ref[...]).astype(BF16)
    acc = y
    for c in range(D_FF // D_MODEL):
        a = jnp.dot(h, w1_ref[:, c * D_MODEL:(c + 1) * D_MODEL], preferred_element_type=F32)
        a = jnp.square(jnp.maximum(a, 0.0)).astype(BF16)
        acc = acc + jnp.dot(a, w2_ref[c * D_MODEL:(c + 1) * D_MODEL, :], preferred_element_type=F32)
    y_ref[...] = acc


def _post_mix(x, o, wo, g, w1, w2):
    rows = x.shape[0]
    tm = ROW_TILE
    ko = o.shape[1]
    return pl.pallas_call(
        _post_mix_kernel,
        grid=(rows // tm,),
        in_specs=[pl.BlockSpec((tm, D_MODEL), lambda i: (i, 0)), pl.BlockSpec((tm, ko), lambda i: (i, 0)),
                  _const_spec((ko, D_MODEL)), _const_spec((1, D_MODEL)),
                  _const_spec((D_MODEL, D_FF)), _const_spec((D_FF, D_MODEL))],
        out_specs=pl.BlockSpec((tm, D_MODEL), lambda i: (i, 0)),
        out_shape=jax.ShapeDtypeStruct((rows, D_MODEL), F32),
        compiler_params=_params(1),
        name="post_mix",
    )(x, o, wo, g, w1, w2)


def _ret_proj_kernel(x_ref, g_ref, w_ref, cos_ref, sin_ref, q_ref, k_ref, v_ref, gate_ref):
    hb = (_rms_rows(x_ref[...]) * g_ref[...]).astype(BF16)
    cos = cos_ref[...]
    sin = sin_ref[...]
    half = RET_KEY // 2
    qk = RET_HEADS * RET_KEY
    vd = RET_HEADS * RET_VAL

    def rotated(col0, scale):
        z = jnp.dot(hb, w_ref[:, col0:col0 + RET_KEY], preferred_element_type=F32)
        x1, x2 = z[:, :half], z[:, half:]
        return (x1 * cos - x2 * sin) * scale, (x1 * sin + x2 * cos) * scale

    for hh in range(RET_HEADS):
        a, b = rotated(hh * RET_KEY, 1.0)
        q_ref[:, hh * RET_KEY:hh * RET_KEY + half] = a
        q_ref[:, hh * RET_KEY + half:(hh + 1) * RET_KEY] = b
        a, b = rotated(qk + hh * RET_KEY, RET_KSCALE)
        k_ref[:, hh * RET_KEY:hh * RET_KEY + half] = a
        k_ref[:, hh * RET_KEY + half:(hh + 1) * RET_KEY] = b
    for hh in range(RET_HEADS):
        cs = slice(hh * RET_VAL, (hh + 1) * RET_VAL)
        v_ref[:, cs] = jnp.dot(hb, w_ref[:, 2 * qk + hh * RET_VAL:2 * qk + (hh + 1) * RET_VAL],
                               preferred_element_type=F32).astype(BF16)
        gate_ref[:, cs] = jnp.dot(hb, w_ref[:, 2 * qk + vd + hh * RET_VAL:2 * qk + vd + (hh + 1) * RET_VAL],
                                  preferred_element_type=F32)


def _ret_proj(x, g, w, cos, sin):
    rows = x.shape[0]
    tm = ROW_TILE
    qk = RET_HEADS * RET_KEY
    vd = RET_HEADS * RET_VAL
    n_pos_tiles = cos.shape[0] // tm
    row_spec = lambda width: pl.BlockSpec((tm, width), lambda i: (i, 0))
    pos_spec = pl.BlockSpec((tm, RET_KEY // 2), lambda i: (i % n_pos_tiles, 0))
    return pl.pallas_call(
        _ret_proj_kernel,
        grid=(rows // tm,),
        in_specs=[row_spec(D_MODEL), _const_spec((1, D_MODEL)), _const_spec((D_MODEL, 2 * qk + 2 * vd)),
                  pos_spec, pos_spec],
        out_specs=[row_spec(qk), row_spec(qk), row_spec(vd), row_spec(vd)],
        out_shape=[jax.ShapeDtypeStruct((rows, qk), F32), jax.ShapeDtypeStruct((rows, qk), F32),
                   jax.ShapeDtypeStruct((rows, vd), BF16), jax.ShapeDtypeStruct((rows, vd), F32)],
        compiler_params=_params(1),
        name="ret_proj",
    )(x, g, w, cos, sin)


def _retention_kernel(cd_ref, q_ref, k_ref, v_ref, g_ref, dm_ref, qd_ref, kd_ref, *rest, has_state0):
    if has_state0:
        s0_ref, o_ref, s_ref = rest
    else:
        o_ref, s_ref = rest

    @pl.when(pl.program_id(1) == 0)
    def _init_state():
        s_ref[...] = s0_ref[...] if has_state0 else jnp.zeros_like(s_ref)

    nt = (((1,), (1,)), ((), ()))
    tn = (((0,), (0,)), ((), ()))
    units = [(bi, hh) for bi in range(q_ref.shape[0]) for hh in range(RET_HEADS)]
    ks = lambda hh: slice(hh * RET_KEY, (hh + 1) * RET_KEY)
    vs = lambda hh: slice(hh * RET_VAL, (hh + 1) * RET_VAL)
    att = [lax.dot_general(q_ref[bi, :, ks(hh)].astype(BF16), k_ref[bi, :, ks(hh)].astype(BF16), nt,
                           preferred_element_type=F32) * dm_ref[hh] for bi, hh in units]
    cross = [jnp.dot((q_ref[bi, :, ks(hh)] * qd_ref[hh]).astype(BF16), s_ref[bi, hh].astype(BF16),
                     preferred_element_type=F32) for bi, hh in units]
    for (bi, hh), a, x in zip(units, att, cross):
        o = jnp.dot(a.astype(BF16), v_ref[bi, :, vs(hh)], preferred_element_type=F32) + x
        gate = g_ref[bi, :, vs(hh)]
        o_ref[bi, :, vs(hh)] = (_rms_rows(o) * (gate * jax.nn.sigmoid(gate))).astype(BF16)
    for bi, hh in units:
        kd = (k_ref[bi, :, ks(hh)] * kd_ref[hh]).astype(BF16)
        s_ref[bi, hh] = cd_ref[hh] * s_ref[bi, hh] + lax.dot_general(kd, v_ref[bi, :, vs(hh)], tn,
                                                                     preferred_element_type=F32)


def _retention(q, k, v, gate, consts, state0, *, seqs_per_step):
    dm, qd, kd, cd = consts
    nb, t, _ = q.shape
    c = dm.shape[1]
    bb = seqs_per_step
    qk = RET_HEADS * RET_KEY
    vd = RET_HEADS * RET_VAL
    tok = lambda width: pl.BlockSpec((bb, c, width), lambda b, ci: (b, ci, 0))
    st_spec = pl.BlockSpec((bb, RET_HEADS, RET_KEY, RET_VAL), lambda b, ci: (b, 0, 0, 0))
    in_specs = [pl.BlockSpec(memory_space=pltpu.SMEM), tok(qk), tok(qk), tok(vd), tok(vd),
                _const_spec(dm.shape), _const_spec(qd.shape), _const_spec(kd.shape)]
    args = [cd, q, k, v, gate, dm, qd, kd]
    if state0 is not None:
        in_specs.append(st_spec)
        args.append(state0)
    return pl.pallas_call(
        functools.partial(_retention_kernel, has_state0=state0 is not None),
        grid=(nb // bb, t // c),
        in_specs=in_specs,
        out_specs=[tok(vd), st_spec],
        out_shape=[jax.ShapeDtypeStruct((nb, t, vd), BF16),
                   jax.ShapeDtypeStruct((nb, RET_HEADS, RET_KEY, RET_VAL), F32)],
        compiler_params=_params(2),
        name="retention_s" if state0 is not None else "retention_p",
    )(*args)


def _retention_consts(c, c_pad):
    log_g = jnp.log1p(-jnp.exp2(-5.0 - jnp.arange(RET_HEADS, dtype=F32)))
    i = jnp.arange(c, dtype=F32)
    diff = i[:, None] - i[None, :]
    dm = jnp.where(diff >= 0, jnp.exp(log_g[:, None, None] * jnp.maximum(diff, 0.0)), 0.0)
    qd = jnp.exp(log_g[:, None] * (i[None, :] + 1.0))
    kd = jnp.exp(log_g[:, None] * (c - 1.0 - i[None, :]))
    cd = jnp.exp(log_g * c)
    pad = c_pad - c
    dm = jnp.pad(dm, ((0, 0), (0, pad), (0, pad)))
    qd = jnp.broadcast_to(jnp.pad(qd, ((0, 0), (0, pad)))[:, :, None], (RET_HEADS, c_pad, RET_KEY))
    kd = jnp.broadcast_to(jnp.pad(kd, ((0, 0), (0, pad)))[:, :, None], (RET_HEADS, c_pad, RET_KEY))
    return dm, qd, kd, cd


def _rotation_tables(pos):
    half = RET_KEY // 2
    theta = 1.0 / (10000.0 ** jnp.linspace(0.0, 1.0, half, dtype=F32))
    ang = pos.astype(F32)[:, None] * theta[None, :]
    return jnp.cos(ang), jnp.sin(ang)


def kernel(x_prompt, x_sample, cache_k, cache_v, cache_kidx, state_ret, page_table, rel_bias, ln_mix, ln_mlp,
           att_w_in, att_q_gain, att_k_gain, att_w_out, ret_w_in, ret_w_out, mlp_w_in, mlp_w_out):
    bp, tp, _ = x_prompt.shape
    bs, ts, _ = x_sample.shape
    n_phys = cache_k.shape[1]
    assert ln_mix.shape[0] == 2 and att_w_in.shape[0] == 1 and ret_w_in.shape[0] == 1
    assert tp % RET_CHUNK == 0 and ts <= TQ and cache_k.shape[2] == PAGE
    assert (tp // PAGE) % 4 == 0 and bs % SEL_SAMPLES == 0
    kvw = ATT_KV_HEADS * HEAD_DIM
    past = page_table.shape[1] * PAGE

    w_att = jnp.pad(att_w_in[0], ((0, 0), (0, ATT_IN_PAD - ATT_IN))).astype(BF16)
    w_att_out = att_w_out[0].astype(BF16)
    w_ret = ret_w_in[0].astype(BF16)
    w_ret_out = ret_w_out[0].astype(BF16)
    w1 = mlp_w_in.astype(BF16)
    w2 = mlp_w_out.astype(BF16)
    row = lambda a: a.reshape(1, -1)

    xp = x_prompt.reshape(bp * tp, D_MODEL)
    xs = x_sample.reshape(bs * ts, D_MODEL)

    qT, qiT, wiT, vT3, kb, kib, kp, vp, kip = _attn_proj(
        xp, row(ln_mix[0]), w_att, row(att_q_gain[0]), row(att_k_gain[0]), transposed=True)
    op = _dsa_prompt(rel_bias, qT, qiT, wiT, kb, vT3, kib, batch=bp, seq=tp)
    yp = _post_mix(xp, op, w_att_out, row(ln_mlp[0]), w1[0], w2[0])

    qs, qis, zls, ks, vs, kis = _attn_proj(
        xs, row(ln_mix[0]), w_att, row(att_q_gain[0]), row(att_k_gain[0]), transposed=False)
    pad_t = lambda a: jnp.pad(a, ((0, 0), (0, TQ - ts)) + ((0, 0),) * (a.ndim - 2))
    q8 = pad_t(qs.reshape(bs, ts, ATT_HEADS, HEAD_DIM)).transpose(0, 2, 1, 3)
    q8 = q8.reshape(bs, ATT_HEADS * TQ, HEAD_DIM).astype(BF16)
    qi8 = pad_t(qis.reshape(bs, ts, IDX_HEADS, IDX_DIM)).transpose(0, 2, 1, 3)
    qi8 = qi8.reshape(bs, IDX_HEADS * TQ, IDX_DIM).astype(BF16)
    zl8 = pad_t(zls.reshape(bs, ts, LANES))
    kn8 = pad_t(ks.reshape(bs, ts, kvw))
    vn8 = pad_t(vs.reshape(bs, ts, kvw))
    mb = _dsa_select(page_table, qi8, zl8, jnp.swapaxes(cache_kidx[0], 1, 2), n_new=ts)
    os8 = _dsa_sample(page_table, rel_bias, q8, mb, kn8, vn8,
                      cache_k[0].reshape(n_phys * PAGE * ATT_KV_HEADS, HEAD_DIM),
                      cache_v[0].reshape(n_phys * PAGE * ATT_KV_HEADS, HEAD_DIM))
    os_ = os8.reshape(bs, ATT_HEADS, TQ, HEAD_DIM)[:, :, :ts].transpose(0, 2, 1, 3)
    os_ = os_.reshape(bs * ts, ATT_HEADS * HEAD_DIM)
    ys = _post_mix(xs, os_, w_att_out, row(ln_mlp[0]), w1[0], w2[0])

    cos_p, sin_p = _rotation_tables(jnp.arange(tp, dtype=jnp.int32))
    q, k, v, gate = _ret_proj(yp, row(ln_mix[1]), w_ret, cos_p, sin_p)
    r3 = lambda a: a.reshape(bp, tp, a.shape[-1])
    orp, state_p = _retention(r3(q), r3(k), r3(v), r3(gate), _retention_consts(RET_CHUNK, RET_CHUNK), None,
                            seqs_per_step=1)
    yp = _post_mix(yp, orp.reshape(bp * tp, -1), w_ret_out, row(ln_mlp[1]), w1[1], w2[1])

    cos_s, sin_s = _rotation_tables(past + jnp.arange(ts, dtype=jnp.int32))
    tile_s = lambda a: jnp.tile(a, (bs, 1))
    q, k, v, gate = _ret_proj(ys, row(ln_mix[1]), w_ret, tile_s(cos_s), tile_s(sin_s))
    r3s = lambda a: pad_t(a.reshape(bs, ts, a.shape[-1]))
    ors, state_s = _retention(r3s(q), r3s(k), r3s(v), r3s(gate), _retention_consts(ts, TQ), state_ret[0],
                            seqs_per_step=2)
    ys = _post_mix(ys, ors[:, :ts].reshape(bs * ts, -1), w_ret_out, row(ln_mlp[1]), w1[1], w2[1])

    return (yp.reshape(bp, tp, D_MODEL), ys.reshape(bs, ts, D_MODEL),
            kp.reshape(1, bp, tp, ATT_KV_HEADS, HEAD_DIM), vp.reshape(1, bp, tp, ATT_KV_HEADS, HEAD_DIM),
            kip.reshape(1, bp, tp, IDX_DIM), state_p[None],
            ks.reshape(1, bs, ts, ATT_KV_HEADS, HEAD_DIM), vs.reshape(1, bs, ts, ATT_KV_HEADS, HEAD_DIM),
            kis.reshape(1, bs, ts, IDX_DIM), state_s[None])
```

```python
import functools
import math

import jax
import jax.numpy as jnp
import numpy as np
from jax import lax
from jax.experimental import pallas as pl
from jax.experimental.pallas import tpu as pltpu

F32 = jnp.float32
BF16 = jnp.bfloat16
I32 = jnp.int32

D_MODEL = 1024
PAGE = 128
ATT_HEADS = 8
ATT_KV_HEADS = 2
ATT_GROUP = ATT_HEADS // ATT_KV_HEADS
HEAD_DIM = 128
IDX_HEADS = 4
IDX_DIM = 64
TOPK_MAX = 256
NUM_BUCKETS = 32
MAX_DISTANCE = 128
RET_HEADS = 4
RET_KEY = 256
RET_VAL = 512
RET_CHUNK = 256
D_FF = 4 * D_MODEL
EPS = 1e-6

Q_OFF = 0
K_OFF = ATT_HEADS * HEAD_DIM
V_OFF = K_OFF + ATT_KV_HEADS * HEAD_DIM
QI_OFF = V_OFF + ATT_KV_HEADS * HEAD_DIM
KI_OFF = QI_OFF + IDX_HEADS * IDX_DIM
WI_OFF = KI_OFF + IDX_DIM
ATT_IN = WI_OFF + IDX_HEADS
ATT_IN_PAD = 1920

LANES = 128
SUBLANES = 8
ROW_TILE = 512
VMEM_LIMIT = 56 * 1024 * 1024

LOG2E = math.log2(math.e)
Q_SCALE = HEAD_DIM ** -0.5 * LOG2E
ONES_ROWS = 16
RET_KSCALE = RET_KEY ** -0.5
INT_MIN = -(2 ** 31)
NEG = -1e30


def _bucket_bounds():
    n = np.arange(0, 8192)
    max_exact = NUM_BUCKETS // 2
    val = np.log(np.maximum(n, 1) / max_exact) / math.log(MAX_DISTANCE / max_exact) * (NUM_BUCKETS - max_exact)
    frac = np.abs(val - np.round(val))
    risky = (n > max_exact) & (n < MAX_DISTANCE) & (frac < 1e-4)
    assert not risky.any()
    large = np.minimum(max_exact + np.floor(val + 1e-9).astype(np.int64), NUM_BUCKETS - 1)
    bucket = np.where(n < max_exact, n, large)
    assert (np.diff(bucket) >= 0).all() and (bucket[MAX_DISTANCE:] == NUM_BUCKETS - 1).all()
    return [int(np.argmax(bucket >= b)) for b in range(NUM_BUCKETS)]


BUCKET_BOUNDS = _bucket_bounds()


def _const_spec(shape):
    zeros = (0,) * len(shape)
    return pl.BlockSpec(shape, lambda *_: zeros, pipeline_mode=pl.Buffered(1))


def _params(n_axes):
    return pltpu.CompilerParams(dimension_semantics=("arbitrary",) * n_axes,
                                vmem_limit_bytes=VMEM_LIMIT)


def _rms_rows(a):
    return a * lax.rsqrt(jnp.mean(a * a, axis=-1, keepdims=True) + EPS)


def _canonical_zero(score):
    return jnp.where(score == 0.0, 0.0, score)


def _sortable_key(score):
    bits = pltpu.bitcast(score, I32)
    return bits ^ ((bits >> 31) & jnp.int32(0x7FFFFFFF))


def _key_to_float(key):
    return pltpu.bitcast(key ^ ((key >> 31) & jnp.int32(0x7FFFFFFF)), F32)


KEY_NEG_INF = -2139095041
KEY_POS_INF = 2139095040
KEY_SUBNORMAL_LO = -8388608
KEY_SUBNORMAL_HI = 8388607


def _bracket_kth_largest(count_ge, kprime, guess):
    def canon(x):
        return jnp.where(jnp.logical_and(x >= KEY_SUBNORMAL_LO, x <= KEY_SUBNORMAL_HI), 0, x)

    def succ(x):
        xc = canon(x)
        return jnp.where(xc == 0, KEY_SUBNORMAL_HI + 1, canon(xc + 1))

    g0 = jnp.clip(guess, KEY_NEG_INF, KEY_POS_INF - 1)
    g1 = succ(g0)
    lo = jnp.full(guess.shape, KEY_NEG_INF, I32)
    hi = jnp.full(guess.shape, KEY_POS_INF, I32)

    def narrow(lo, hi, x, cnt):
        ok = cnt >= kprime
        return jnp.where(ok, jnp.maximum(lo, x), lo), jnp.where(ok, hi, jnp.minimum(hi, x))

    c0, c1 = count_ge([_key_to_float(g0), _key_to_float(g1)])
    lo, hi = narrow(lo, hi, g0, c0)
    lo, hi = narrow(lo, hi, g1, c1)

    def open_bracket(c):
        return jnp.max(jnp.where(canon(c[1]) > succ(c[0]), 1.0, 0.0)) > 0.5

    def bisect(c):
        lo, hi = c
        mid = jnp.where(canon(hi) > succ(lo), (lo >> 1) + (hi >> 1) + (lo & hi & 1), lo)
        return narrow(lo, hi, mid, count_ge([_key_to_float(mid)])[0])

    lo, _ = lax.while_loop(open_bracket, bisect, (lo, hi))
    return canon(lo)


def _bit_transpose32(words):
    a = list(words)
    j, m = 16, 0x0000FFFF
    while j:
        k = 0
        while k < 32:
            t = (a[k] ^ lax.shift_right_logical(a[k + j], jnp.int32(j))) & jnp.int32(m)
            a[k] = a[k] ^ t
            a[k + j] = a[k + j] ^ lax.shift_left(t, jnp.int32(j))
            k = (k + j + 1) & ~j
        j >>= 1
        m ^= (m << j) & 0xFFFFFFFF
    return a


def _bias_from_distance(relb_ref, head, dist):
    val = jnp.full(dist.shape, relb_ref[0, head], F32)
    for b in range(1, NUM_BUCKETS):
        val = jnp.where(dist >= BUCKET_BOUNDS[b], relb_ref[b, head], val)
    return val * LOG2E


def _attn_proj_kernel(x_ref, g_ref, w_ref, qg_ref, kg_ref, *outs, transposed):
    x = x_ref[...]
    h = _rms_rows(x) * g_ref[...]
    z = jnp.dot(h.astype(BF16), w_ref[...], preferred_element_type=F32)
    qg = qg_ref[...]
    kg = kg_ref[...]
    k = jnp.concatenate(
        [_rms_rows(z[:, K_OFF + g * HEAD_DIM:K_OFF + (g + 1) * HEAD_DIM]) * kg for g in range(ATT_KV_HEADS)],
        axis=1)
    v = z[:, V_OFF:QI_OFF]
    qi = z[:, QI_OFF:KI_OFF]
    zl = z[:, KI_OFF:ATT_IN_PAD]
    ki = zl[:, :IDX_DIM]
    if transposed:
        qT_ref, qiT_ref, wiT_ref, vT_ref, kb_ref, kib_ref, k_ref, v_ref, ki_ref = outs
        for hh in range(ATT_HEADS):
            qh = _rms_rows(z[:, hh * HEAD_DIM:(hh + 1) * HEAD_DIM]) * qg * Q_SCALE
            qT_ref[hh * HEAD_DIM:(hh + 1) * HEAD_DIM, :] = qh.T.astype(BF16)
        for c in range(IDX_HEADS * IDX_DIM // LANES):
            qiT_ref[c * LANES:(c + 1) * LANES, :] = qi[:, c * LANES:(c + 1) * LANES].T.astype(BF16)
        wiT_ref[...] = zl.T[IDX_DIM:IDX_DIM + 8, :]
        for g in range(ATT_KV_HEADS):
            vT = v[:, g * HEAD_DIM:(g + 1) * HEAD_DIM].T
            for c in range(x.shape[0] // PAGE):
                vT_ref[c, g * HEAD_DIM:(g + 1) * HEAD_DIM, :] = vT[:, c * PAGE:(c + 1) * PAGE].astype(BF16)
        kb_ref[...] = k.astype(BF16)
        kib_ref[...] = ki.astype(BF16)
    else:
        q_ref, qi_ref, zl_ref, k_ref, v_ref, ki_ref = outs
        for hh in range(ATT_HEADS):
            q_ref[:, hh * HEAD_DIM:(hh + 1) * HEAD_DIM] = _rms_rows(z[:, hh * HEAD_DIM:(hh + 1) * HEAD_DIM]) * qg * Q_SCALE
        qi_ref[...] = qi
        zl_ref[...] = zl
    for g in range(ATT_KV_HEADS):
        rows_g = pl.ds(g, x.shape[0], stride=ATT_KV_HEADS)
        k_ref[rows_g, :] = k[:, g * HEAD_DIM:(g + 1) * HEAD_DIM]
        v_ref[rows_g, :] = v[:, g * HEAD_DIM:(g + 1) * HEAD_DIM]
    ki_ref[...] = ki


def _attn_proj(x, g, w, qg, kg, *, transposed):
    rows = x.shape[0]
    tm = ROW_TILE
    nt = rows // tm
    kvw = ATT_KV_HEADS * HEAD_DIM
    row_spec = lambda width: pl.BlockSpec((tm, width), lambda i: (i, 0))
    col_spec = lambda height: pl.BlockSpec((height, tm), lambda i: (0, i))
    kv_rows = ATT_KV_HEADS * rows
    kv_spec = pl.BlockSpec((ATT_KV_HEADS * tm, HEAD_DIM), lambda i: (i, 0))
    leaves_shape = [jax.ShapeDtypeStruct((kv_rows, HEAD_DIM), F32), jax.ShapeDtypeStruct((kv_rows, HEAD_DIM), F32),
                    jax.ShapeDtypeStruct((rows, IDX_DIM), F32)]
    leaves_spec = [kv_spec, kv_spec, row_spec(IDX_DIM)]
    if transposed:
        out_shape = [jax.ShapeDtypeStruct((ATT_HEADS * HEAD_DIM, rows), BF16),
                     jax.ShapeDtypeStruct((IDX_HEADS * IDX_DIM, rows), BF16),
                     jax.ShapeDtypeStruct((8, rows), F32),
                     jax.ShapeDtypeStruct((rows // PAGE, kvw, PAGE), BF16),
                     jax.ShapeDtypeStruct((rows, kvw), BF16),
                     jax.ShapeDtypeStruct((rows, IDX_DIM), BF16)] + leaves_shape
        out_specs = [col_spec(ATT_HEADS * HEAD_DIM), col_spec(IDX_HEADS * IDX_DIM), col_spec(8),
                     pl.BlockSpec((tm // PAGE, kvw, PAGE), lambda i: (i, 0, 0)),
                     row_spec(kvw), row_spec(IDX_DIM)] + leaves_spec
    else:
        out_shape = [jax.ShapeDtypeStruct((rows, ATT_HEADS * HEAD_DIM), F32),
                     jax.ShapeDtypeStruct((rows, IDX_HEADS * IDX_DIM), F32),
                     jax.ShapeDtypeStruct((rows, LANES), F32)] + leaves_shape
        out_specs = [row_spec(ATT_HEADS * HEAD_DIM), row_spec(IDX_HEADS * IDX_DIM), row_spec(LANES)] + leaves_spec
    return pl.pallas_call(
        functools.partial(_attn_proj_kernel, transposed=transposed),
        grid=(nt,),
        in_specs=[row_spec(D_MODEL), _const_spec((1, D_MODEL)), _const_spec((D_MODEL, ATT_IN_PAD)),
                  _const_spec((1, HEAD_DIM)), _const_spec((1, HEAD_DIM))],
        out_specs=out_specs,
        out_shape=out_shape,
        compiler_params=_params(1),
        name="attn_proj_t" if transposed else "attn_proj_r",
    )(x, g, w, qg, kg)


def _dsa_prompt_kernel(relb_ref, qT_ref, qiT_ref, wiT_ref, k_ref, vT_ref, ki_ref, o_ref,
                       keys_sc, scores_sc, planes_sc, live_sc, mb_sc, bias_sc, logit_sc, acc_sc, *, topk):
    b = pl.program_id(0)
    i = pl.program_id(1)
    srow = lax.broadcasted_iota(I32, (PAGE, PAGE), 0)
    tcol = lax.broadcasted_iota(I32, (PAGE, PAGE), 1)

    @pl.when((b == 0) & (i == 0))
    def _init_bias():
        planes_sc[...] = jnp.zeros_like(planes_sc)
        live_sc[...] = jnp.zeros_like(live_sc)
        for hh in range(ATT_HEADS):
            for rel in range(2):
                dist = jnp.maximum(rel * PAGE + tcol - srow, 0)
                bias_sc[hh, rel] = _bias_from_distance(relb_ref, hh, dist)
            bias_sc[hh, 2] = jnp.full((PAGE, PAGE), relb_ref[NUM_BUCKETS - 1, hh] * LOG2E, F32)

    wiT = wiT_ref[...]

    npair = (i + 2) // 2

    def score_block(j):
        off = pl.multiple_of(j * PAGE, PAGE)
        kij = ki_ref[pl.ds(off, PAGE), :]
        sc = jnp.zeros((PAGE, PAGE), F32)
        for hh in range(IDX_HEADS):
            idx = jnp.dot(kij, qiT_ref[hh * IDX_DIM:(hh + 1) * IDX_DIM, :], preferred_element_type=F32)
            sc = sc + wiT[hh:hh + 1, :] * jnp.maximum(idx, 0.0)
        sc = _canonical_zero(sc)
        causal = jnp.logical_or(j < i, jnp.logical_and(j == i, srow <= tcol))
        scores_sc[j] = jnp.where(causal, sc, -jnp.inf)
        keys_sc[j] = jnp.where(causal, _sortable_key(sc), INT_MIN)

    nquad = (i + 4) // 4

    def score_body(jj, carry):
        for u in range(4):
            score_block(4 * jj + u)
        return carry

    lax.fori_loop(0, nquad, score_body, 0)

    lane = lax.broadcasted_iota(I32, (1, PAGE), 1)
    kprime = jnp.minimum(topk, i * PAGE + lane + 1).astype(F32)

    def plane_body(jj, carry):
        rows = [keys_sc[2 * jj + blk, k * SUBLANES:(k + 1) * SUBLANES, :] ^ INT_MIN
                for blk in range(2) for k in range(PAGE // SUBLANES)]
        planes = _bit_transpose32(rows)
        live = planes[0]
        for p in range(32):
            planes_sc[jj, p] = planes[p]
            live = live | planes[p]
        live_sc[jj] = live
        return carry

    lax.fori_loop(0, npair, plane_body, 0)

    n_pairs_max = keys_sc.shape[0] // 2

    def bit_body(bi, carry):
        thr_u, alive, above = carry
        cnt = jnp.zeros((SUBLANES, PAGE), I32)
        with_bit, at_least = [], []
        for p in range(n_pairs_max):
            t = alive[p] & planes_sc[p, bi]
            ge = above[p] | t
            cnt = cnt + lax.population_count(ge)
            with_bit.append(t)
            at_least.append(ge)
        take = jnp.sum(cnt.astype(F32), axis=0, keepdims=True) >= kprime
        alive = tuple(jnp.where(take, t, a ^ t) for t, a in zip(with_bit, alive))
        above = tuple(jnp.where(take, g, ge) for g, ge in zip(above, at_least))
        thr_u = thr_u | jnp.where(take, lax.shift_left(jnp.int32(1), 31 - bi), 0)
        return thr_u, alive, above

    zero_words = jnp.zeros((SUBLANES, PAGE), I32)
    thr_u, _, _ = lax.fori_loop(
        0, 32, bit_body,
        (jnp.zeros((1, PAGE), I32),
         tuple(jnp.where(p < npair, live_sc[p], zero_words) for p in range(n_pairs_max)),
         (zero_words,) * n_pairs_max))
    def count_scores(preds):
        def body(jj, cnts):
            for blk in range(2):
                sc = scores_sc[2 * jj + blk]
                cnts = tuple(c + jnp.where(p(sc), 1.0, 0.0) for c, p in zip(cnts, preds))
            return cnts
        cnts = lax.fori_loop(0, npair, body, tuple(jnp.zeros((PAGE, PAGE), F32) for _ in preds))
        return [jnp.sum(c, axis=0, keepdims=True) for c in cnts]

    def count_ge(thresholds):
        return count_scores([lambda sc, x=x: sc >= x for x in thresholds])

    thr = _key_to_float(_bracket_kth_largest(count_ge, kprime, thr_u ^ INT_MIN))
    need = kprime - count_scores([lambda sc: sc > thr])[0]

    tri = jnp.where(srow > tcol, 1.0, 0.0).astype(BF16)

    def mask_block(j, carry):
        sc = scores_sc[j]
        eqf = jnp.where(sc == thr, 1.0, 0.0)
        rank = jnp.dot(tri, eqf.astype(BF16), preferred_element_type=F32) + carry
        tie = jnp.where(jnp.logical_and(sc == thr, rank < need), 0.0, NEG)
        mb_sc[j] = jnp.where(sc > thr, 0.0, tie)
        return carry + jnp.sum(eqf, axis=0, keepdims=True)

    def mask_body(jj, carry):
        for u in range(4):
            carry = mask_block(4 * jj + u, carry)
        return carry

    lax.fori_loop(0, nquad, mask_body, jnp.zeros((1, PAGE), F32))

    groups = range(ATT_KV_HEADS)
    width = ATT_GROUP * PAGE
    qcats = [jnp.concatenate(
        [qT_ref[(ATT_GROUP * g + r) * HEAD_DIM:(ATT_GROUP * g + r + 1) * HEAD_DIM, :] for r in range(ATT_GROUP)],
        axis=1) for g in groups]

    def logit_block(j, ms):
        off = pl.multiple_of(j * PAGE, PAGE)
        mb = mb_sc[j]
        relc = jnp.clip(i - j, 0, 2)
        out = []
        for g in groups:
            kj = k_ref[pl.ds(off, PAGE), g * HEAD_DIM:(g + 1) * HEAD_DIM]
            logits = jnp.dot(kj, qcats[g], preferred_element_type=F32)
            logits = jnp.concatenate(
                [logits[:, r * PAGE:(r + 1) * PAGE] + (mb + bias_sc[ATT_GROUP * g + r, relc])
                 for r in range(ATT_GROUP)], axis=1)
            logit_sc[g, j] = logits
            out.append(jnp.maximum(ms[g], jnp.max(logits, axis=0, keepdims=True)))
        return tuple(out)

    def over_block_pairs(pair_step, carry):
        nfull = npair // 2
        carry = lax.fori_loop(0, nfull, lambda t, c: pair_step(4 * t + 2, pair_step(4 * t, c)), carry)
        return lax.cond(npair % 2 == 1, lambda c: pair_step(4 * nfull, c), lambda c: c, carry)

    ms = over_block_pairs(lambda j0, c: logit_block(j0 + 1, logit_block(j0, c)),
                          tuple(jnp.full((1, width), NEG, F32) for _ in groups))

    ones_rows = jnp.ones((ONES_ROWS, PAGE), BF16)

    def value_block(j):
        contrib = []
        for g in groups:
            p = jnp.exp2(logit_sc[g, j] - ms[g])
            vTj = jnp.concatenate([vT_ref[j, g * HEAD_DIM:(g + 1) * HEAD_DIM, :], ones_rows], axis=0)
            contrib.append(jnp.dot(vTj, p.astype(BF16), preferred_element_type=F32))
        return contrib

    def value_pair(j0, carry):
        c0 = value_block(j0)
        c1 = value_block(j0 + 1)
        for g in groups:
            acc_sc[g] = acc_sc[g] + (c0[g] + c1[g])
        return carry

    acc_sc[...] = jnp.zeros_like(acc_sc)
    over_block_pairs(value_pair, 0)
    for g in groups:
        acc = acc_sc[g]
        outT = acc[:HEAD_DIM] * (1.0 / acc[HEAD_DIM:HEAD_DIM + 1])
        for r in range(ATT_GROUP):
            hh = ATT_GROUP * g + r
            o_ref[:, hh * HEAD_DIM:(hh + 1) * HEAD_DIM] = outT[:, r * PAGE:(r + 1) * PAGE].T.astype(BF16)


def _dsa_prompt(rel_bias, qT, qiT, wiT, kb, vT3, kib, *, batch, seq):
    nb = seq // PAGE
    rows = batch * seq
    kvw = ATT_KV_HEADS * HEAD_DIM
    qcol = lambda height: pl.BlockSpec((height, PAGE), lambda b, i: (0, b * nb + i))
    return pl.pallas_call(
        functools.partial(_dsa_prompt_kernel, topk=min(TOPK_MAX, seq // 4)),
        grid=(batch, nb),
        in_specs=[pl.BlockSpec(memory_space=pltpu.SMEM),
                  qcol(ATT_HEADS * HEAD_DIM), qcol(IDX_HEADS * IDX_DIM), qcol(8),
                  pl.BlockSpec((seq, kvw), lambda b, i: (b, 0)),
                  pl.BlockSpec((nb, kvw, PAGE), lambda b, i: (b, 0, 0)),
                  pl.BlockSpec((seq, IDX_DIM), lambda b, i: (b, 0))],
        out_specs=pl.BlockSpec((PAGE, ATT_HEADS * HEAD_DIM), lambda b, i: (b * nb + i, 0)),
        out_shape=jax.ShapeDtypeStruct((rows, ATT_HEADS * HEAD_DIM), BF16),
        scratch_shapes=[pltpu.VMEM((nb, PAGE, PAGE), I32), pltpu.VMEM((nb, PAGE, PAGE), F32),
                        pltpu.VMEM((nb // 2, 32, SUBLANES, PAGE), I32), pltpu.VMEM((nb // 2, SUBLANES, PAGE), I32),
                        pltpu.VMEM((nb, PAGE, PAGE), F32),
                        pltpu.VMEM((ATT_HEADS, 3, PAGE, PAGE), F32),
                        pltpu.VMEM((ATT_KV_HEADS, nb, PAGE, ATT_GROUP * PAGE), F32),
                        pltpu.VMEM((ATT_KV_HEADS, HEAD_DIM + ONES_ROWS, ATT_GROUP * PAGE), F32)],
        compiler_params=_params(2),
        name="dsa_prompt",
    )(rel_bias, qT, qiT, wiT, kb, vT3, kib)


TQ = 8
SEL_SAMPLES = 8


def _dsa_select_kernel(pt_ref, qi_ref, zl_ref, cki_hbm, mb_ref, sc_sc, kibuf, sems, *, n_pages, n_new):
    ns = SEL_SAMPLES
    n_fetch = ns * n_pages
    b = pl.program_id(0)
    slot = b % 2

    def page_copies(step, slot):
        return [pltpu.make_async_copy(cki_hbm.at[pt_ref[step * n_fetch + f]], kibuf.at[slot, f], sems.at[slot])
                for f in range(n_fetch)]

    @pl.when(b == 0)
    def _start_first():
        for c in page_copies(0, 0):
            c.start()

    @pl.when(b + 1 < pl.num_programs(0))
    def _start_next():
        for c in page_copies(b + 1, 1 - slot):
            c.start()

    for c in page_copies(b, slot):
        c.wait()
    ki_pages = [kibuf.at[slot, f] for f in range(n_fetch)]
    n_blk = n_pages + 1
    rows = ns * TQ
    trow = lax.broadcasted_iota(I32, (TQ, PAGE), 0)
    scol = lax.broadcasted_iota(I32, (TQ, PAGE), 1)
    pad_rows = lambda a: jnp.concatenate([a, jnp.zeros((PAGE - TQ, a.shape[1]), a.dtype)], axis=0)
    nt = (((1,), (1,)), ((), ()))

    for s in range(ns):
        zl = zl_ref[s]
        qi = qi_ref[s]
        for p in range(n_blk):
            if p == n_pages:
                ki_new = pad_rows(zl[:, :IDX_DIM]).astype(BF16)
                idx = lax.dot_general(qi, ki_new, nt, preferred_element_type=F32)
            else:
                idx = jnp.dot(qi, ki_pages[s * n_pages + p][...].astype(BF16), preferred_element_type=F32)
            sc = jnp.zeros((TQ, PAGE), F32)
            for hh in range(IDX_HEADS):
                sc = sc + zl[:, IDX_DIM + hh:IDX_DIM + hh + 1] * jnp.maximum(idx[hh * TQ:(hh + 1) * TQ], 0.0)
            sc = _canonical_zero(sc)
            if p == n_pages:
                sc = jnp.where(jnp.logical_and(scol <= trow, scol < n_new), sc, -jnp.inf)
            sc_sc[s * TQ:(s + 1) * TQ, p * PAGE:(p + 1) * PAGE] = sc

    kprime = float(min(TOPK_MAX, (n_pages * PAGE + n_new) // 4))

    def count(pred_of_scores):
        return jnp.sum(jnp.where(pred_of_scores(sc_sc[...]), 1.0, 0.0), axis=1, keepdims=True)

    cand0 = jnp.zeros((rows, 1), I32)
    thr0 = jnp.where(count(lambda sc: sc >= _key_to_float(cand0)) >= kprime, cand0, INT_MIN)

    def bit_body(bi, thr_key):
        cand = thr_key | lax.shift_left(jnp.int32(1), 30 - bi)
        return jnp.where(count(lambda sc: sc >= _key_to_float(cand)) >= kprime, cand, thr_key)

    thr = _key_to_float(lax.fori_loop(0, 31, bit_body, thr0))
    need = kprime - count(lambda sc: sc > thr)

    s_r = lax.broadcasted_iota(I32, (PAGE, PAGE), 0)
    s_c = lax.broadcasted_iota(I32, (PAGE, PAGE), 1)
    triu = jnp.where(s_r < s_c, 1.0, 0.0).astype(BF16)
    carry = jnp.zeros((rows, 1), F32)
    for p in range(n_blk):
        sc = sc_sc[:, p * PAGE:(p + 1) * PAGE]
        eqf = jnp.where(sc == thr, 1.0, 0.0)
        rank = jnp.dot(eqf.astype(BF16), triu, preferred_element_type=F32) + carry
        tie = jnp.where(jnp.logical_and(sc == thr, rank < need), 0.0, NEG)
        mb_ref[:, p * PAGE:(p + 1) * PAGE] = jnp.where(sc > thr, 0.0, tie)
        carry = carry + jnp.sum(eqf, axis=1, keepdims=True)


def _dsa_select(page_table, qi8, zl8, cache_ki, *, n_new):
    nsamp, n_pages = page_table.shape
    ns = SEL_SAMPLES
    width = (n_pages + 1) * PAGE

    in_specs = [pl.BlockSpec((ns, IDX_HEADS * TQ, IDX_DIM), lambda b, pt: (b, 0, 0)),
                pl.BlockSpec((ns, TQ, LANES), lambda b, pt: (b, 0, 0)),
                pl.BlockSpec(memory_space=pl.ANY)]
    grid_spec = pltpu.PrefetchScalarGridSpec(
        num_scalar_prefetch=1, grid=(nsamp // ns,), in_specs=in_specs,
        out_specs=pl.BlockSpec((ns * TQ, width), lambda b, pt: (b, 0)),
        scratch_shapes=[pltpu.VMEM((ns * TQ, width), F32),
                        pltpu.VMEM((2, ns * n_pages, IDX_DIM, PAGE), F32), pltpu.SemaphoreType.DMA((2,))])
    return pl.pallas_call(
        functools.partial(_dsa_select_kernel, n_pages=n_pages, n_new=n_new),
        grid_spec=grid_spec,
        out_shape=jax.ShapeDtypeStruct((nsamp * TQ, width), F32),
        compiler_params=_params(1),
        name="dsa_select",
    )(page_table.reshape(-1), qi8, zl8, cache_ki)


def _dsa_sample_kernel(pt_ref, relb_ref, q_ref, mb_ref, kn_ref, vn_ref, ck_hbm, cv_hbm, o_ref,
                       bias_sc, kall_sc, vall_sc, kbuf, vbuf, sems, *, n_pages):
    past = n_pages * PAGE
    rows_g = ATT_GROUP * TQ
    page_rows = PAGE * ATT_KV_HEADS
    b = pl.program_id(0)
    slot = b % 2

    def page_copies(sample, slot):
        copies = []
        for p in range(n_pages):
            src = pl.ds(pl.multiple_of(pt_ref[sample * n_pages + p] * page_rows, page_rows), page_rows)
            dst = pl.ds(p * page_rows, page_rows)
            copies.append(pltpu.make_async_copy(ck_hbm.at[src, :], kbuf.at[slot, dst, :], sems.at[slot, 0]))
            copies.append(pltpu.make_async_copy(cv_hbm.at[src, :], vbuf.at[slot, dst, :], sems.at[slot, 1]))
        return copies

    @pl.when(b == 0)
    def _start_first():
        for c in page_copies(0, 0):
            c.start()

    @pl.when(b + 1 < pl.num_programs(0))
    def _start_next():
        for c in page_copies(b + 1, 1 - slot):
            c.start()

    for c in page_copies(b, slot):
        c.wait()

    trow = lax.broadcasted_iota(I32, (TQ, PAGE), 0)
    scol = lax.broadcasted_iota(I32, (TQ, PAGE), 1)

    @pl.when(pl.program_id(0) == 0)
    def _init_bias():
        for g in range(ATT_KV_HEADS):
            for r in range(ATT_GROUP):
                hh = ATT_GROUP * g + r
                rs = slice(r * TQ, (r + 1) * TQ)
                far = jnp.full((TQ, PAGE), relb_ref[NUM_BUCKETS - 1, hh] * LOG2E, F32)
                for p in range(n_pages - 1):
                    bias_sc[g, rs, p * PAGE:(p + 1) * PAGE] = far
                bias_sc[g, rs, past - PAGE:past] = _bias_from_distance(
                    relb_ref, hh, jnp.maximum(PAGE + trow - scol, 0))
                bias_sc[g, rs, past:past + PAGE] = _bias_from_distance(relb_ref, hh, jnp.maximum(trow - scol, 0))

    nt = (((1,), (1,)), ((), ()))
    mb = jnp.concatenate([mb_ref[...]] * ATT_GROUP, axis=0)
    pad_rows = lambda a: jnp.concatenate([a, jnp.zeros((PAGE - TQ, a.shape[1]), a.dtype)], axis=0)
    for g in range(ATT_KV_HEADS):
        gs = slice(g * HEAD_DIM, (g + 1) * HEAD_DIM)
        for p in range(n_pages):
            ps = slice(p * PAGE, (p + 1) * PAGE)
            head_rows = pl.ds(p * page_rows + g, PAGE, stride=ATT_KV_HEADS)
            kall_sc[g, ps, :] = kbuf[slot, head_rows, :].astype(BF16)
            vall_sc[g, ps, :] = vbuf[slot, head_rows, :].astype(BF16)
        kall_sc[g, past:past + PAGE, :] = pad_rows(kn_ref[:, gs]).astype(BF16)
        vall_sc[g, past:past + PAGE, :] = pad_rows(vn_ref[:, gs]).astype(BF16)
    for g in range(ATT_KV_HEADS):
        qg = q_ref[g * rows_g:(g + 1) * rows_g, :]
        logits = lax.dot_general(qg, kall_sc[g], nt, preferred_element_type=F32)
        logits = logits + (mb + bias_sc[g])
        m = jnp.max(logits, axis=1, keepdims=True)
        pr = jnp.exp2(logits - m)
        l = jnp.sum(pr, axis=1, keepdims=True)
        acc = jnp.dot(pr.astype(BF16), vall_sc[g], preferred_element_type=F32)
        o_ref[g * rows_g:(g + 1) * rows_g, :] = (acc * (1.0 / l)).astype(BF16)


def _dsa_sample(page_table, rel_bias, q8, mb, kn8, vn8, cache_k, cache_v):
    nsamp, n_pages = page_table.shape
    kvw = ATT_KV_HEADS * HEAD_DIM
    per_sample = lambda *tail: pl.BlockSpec((None,) + tail, lambda b, pt: (b,) + (0,) * len(tail))

    page_buf = pltpu.VMEM((2, n_pages * PAGE * ATT_KV_HEADS, HEAD_DIM), F32)
    in_specs = [pl.BlockSpec(memory_space=pltpu.SMEM),
                per_sample(ATT_HEADS * TQ, HEAD_DIM),
                pl.BlockSpec((TQ, mb.shape[1]), lambda b, pt: (b, 0)),
                per_sample(TQ, kvw), per_sample(TQ, kvw),
                pl.BlockSpec(memory_space=pl.ANY), pl.BlockSpec(memory_space=pl.ANY)]
    grid_spec = pltpu.PrefetchScalarGridSpec(
        num_scalar_prefetch=1, grid=(nsamp,), in_specs=in_specs,
        out_specs=per_sample(ATT_HEADS * TQ, HEAD_DIM),
        scratch_shapes=[pltpu.VMEM((ATT_KV_HEADS, ATT_GROUP * TQ, mb.shape[1]), F32),
                        pltpu.VMEM((ATT_KV_HEADS, mb.shape[1], HEAD_DIM), BF16),
                        pltpu.VMEM((ATT_KV_HEADS, mb.shape[1], HEAD_DIM), BF16),
                        page_buf, page_buf, pltpu.SemaphoreType.DMA((2, 2))])
    return pl.pallas_call(
        functools.partial(_dsa_sample_kernel, n_pages=n_pages),
        grid_spec=grid_spec,
        out_shape=jax.ShapeDtypeStruct((nsamp, ATT_HEADS * TQ, HEAD_DIM), BF16),
        compiler_params=_params(1),
        name="dsa_sample",
    )(page_table.reshape(-1), rel_bias, q8, mb, kn8, vn8, cache_k, cache_v)


def _post_mix_kernel(x_ref, o_ref, wo_ref, g_ref, w1_ref, w2_ref, y_ref):
    y = x_ref[...] + jnp.dot(o_ref[...], wo_ref[...], preferred_element_type=F32)
    h = (_rms_rows(y) * g_ref[...]).astype(BF16)
    acc = y
    for c in range(D_FF // D_MODEL):
        a = jnp.dot(h, w1_ref[:, c * D_MODEL:(c + 1) * D_MODEL], preferred_element_type=F32)
        a = jnp.square(jnp.maximum(a, 0.0)).astype(BF16)
        acc = acc + jnp.dot(a, w2_ref[c * D_MODEL:(c + 1) * D_MODEL, :], preferred_element_type=F32)
    y_ref[...] = acc


def _post_mix(x, o, wo, g, w1, w2):
    rows = x.shape[0]
    tm = ROW_TILE
    ko = o.shape[1]
    return pl.pallas_call(
        _post_mix_kernel,
        grid=(rows // tm,),
        in_specs=[pl.BlockSpec((tm, D_MODEL), lambda i: (i, 0)), pl.BlockSpec((tm, ko), lambda i: (i, 0)),
                  _const_spec((ko, D_MODEL)), _const_spec((1, D_MODEL)),
                  _const_spec((D_MODEL, D_FF)), _const_spec((D_FF, D_MODEL))],
        out_specs=pl.BlockSpec((tm, D_MODEL), lambda i: (i, 0)),
        out_shape=jax.ShapeDtypeStruct((rows, D_MODEL), F32),
        compiler_params=_params(1),
        name="post_mix",
    )(x, o, wo, g, w1, w2)


def _ret_proj_kernel(x_ref, g_ref, w_ref, cos_ref, sin_ref, q_ref, k_ref, v_ref, gate_ref):
    hb = (_rms_rows(x_ref[...]) * g_ref[...]).astype(BF16)
    cos = cos_ref[...]
    sin = sin_ref[...]
    half = RET_KEY // 2
    qk = RET_HEADS * RET_KEY
    vd = RET_HEADS * RET_VAL

    def rotated(col0, scale):
        z = jnp.dot(hb, w_ref[:, col0:col0 + RET_KEY], preferred_element_type=F32)
        x1, x2 = z[:, :half], z[:, half:]
        return (x1 * cos - x2 * sin) * scale, (x1 * sin + x2 * cos) * scale

    for hh in range(RET_HEADS):
        a, b = rotated(hh * RET_KEY, 1.0)
        q_ref[:, hh * RET_KEY:hh * RET_KEY + half] = a.astype(BF16)
        q_ref[:, hh * RET_KEY + half:(hh + 1) * RET_KEY] = b.astype(BF16)
        a, b = rotated(qk + hh * RET_KEY, RET_KSCALE)
        k_ref[:, hh * RET_KEY:hh * RET_KEY + half] = a.astype(BF16)
        k_ref[:, hh * RET_KEY + half:(hh + 1) * RET_KEY] = b.astype(BF16)
    for hh in range(RET_HEADS):
        cs = slice(hh * RET_VAL, (hh + 1) * RET_VAL)
        v_ref[:, cs] = jnp.dot(hb, w_ref[:, 2 * qk + hh * RET_VAL:2 * qk + (hh + 1) * RET_VAL],
                               preferred_element_type=F32).astype(BF16)
        gate_ref[:, cs] = jnp.dot(hb, w_ref[:, 2 * qk + vd + hh * RET_VAL:2 * qk + vd + (hh + 1) * RET_VAL],
                                  preferred_element_type=F32).astype(BF16)


def _ret_proj(x, g, w, cos, sin):
    rows = x.shape[0]
    tm = ROW_TILE
    qk = RET_HEADS * RET_KEY
    vd = RET_HEADS * RET_VAL
    n_pos_tiles = cos.shape[0] // tm
    row_spec = lambda width: pl.BlockSpec((tm, width), lambda i: (i, 0))
    pos_spec = pl.BlockSpec((tm, RET_KEY // 2), lambda i: (i % n_pos_tiles, 0))
    return pl.pallas_call(
        _ret_proj_kernel,
        grid=(rows // tm,),
        in_specs=[row_spec(D_MODEL), _const_spec((1, D_MODEL)), _const_spec((D_MODEL, 2 * qk + 2 * vd)),
                  pos_spec, pos_spec],
        out_specs=[row_spec(qk), row_spec(qk), row_spec(vd), row_spec(vd)],
        out_shape=[jax.ShapeDtypeStruct((rows, qk), BF16), jax.ShapeDtypeStruct((rows, qk), BF16),
                   jax.ShapeDtypeStruct((rows, vd), BF16), jax.ShapeDtypeStruct((rows, vd), BF16)],
        compiler_params=_params(1),
        name="ret_proj",
    )(x, g, w, cos, sin)


def _retention_kernel(cd_ref, q_ref, k_ref, v_ref, g_ref, dm_ref, qd_ref, kd_ref, *rest, has_state0):
    if has_state0:
        s0_ref, o_ref, s_ref = rest
    else:
        o_ref, s_ref = rest

    @pl.when(pl.program_id(1) == 0)
    def _init_state():
        s_ref[...] = s0_ref[...] if has_state0 else jnp.zeros_like(s_ref)

    nt = (((1,), (1,)), ((), ()))
    tn = (((0,), (0,)), ((), ()))
    units = [(bi, hh) for bi in range(q_ref.shape[0]) for hh in range(RET_HEADS)]
    ks = lambda hh: slice(hh * RET_KEY, (hh + 1) * RET_KEY)
    vs = lambda hh: slice(hh * RET_VAL, (hh + 1) * RET_VAL)
    att = [lax.dot_general(q_ref[bi, :, ks(hh)].astype(BF16), k_ref[bi, :, ks(hh)].astype(BF16), nt,
                           preferred_element_type=F32) * dm_ref[hh] for bi, hh in units]
    cross = [jnp.dot((q_ref[bi, :, ks(hh)] * qd_ref[hh]).astype(BF16), s_ref[bi, hh].astype(BF16),
                     preferred_element_type=F32) for bi, hh in units]
    for (bi, hh), a, x in zip(units, att, cross):
        o = jnp.dot(a.astype(BF16), v_ref[bi, :, vs(hh)], preferred_element_type=F32) + x
        gate = g_ref[bi, :, vs(hh)].astype(F32)
        o_ref[bi, :, vs(hh)] = (_rms_rows(o) * (gate * jax.nn.sigmoid(gate))).astype(BF16)
    for bi, hh in units:
        kd = (k_ref[bi, :, ks(hh)] * kd_ref[hh]).astype(BF16)
        s_ref[bi, hh] = cd_ref[hh] * s_ref[bi, hh] + lax.dot_general(kd, v_ref[bi, :, vs(hh)], tn,
                                                                     preferred_element_type=F32)


def _retention(q, k, v, gate, consts, state0, *, seqs_per_step):
    dm, qd, kd, cd = consts
    nb, t, _ = q.shape
    c = dm.shape[1]
    bb = seqs_per_step
    qk = RET_HEADS * RET_KEY
    vd = RET_HEADS * RET_VAL
    tok = lambda width: pl.BlockSpec((bb, c, width), lambda b, ci: (b, ci, 0))
    st_spec = pl.BlockSpec((bb, RET_HEADS, RET_KEY, RET_VAL), lambda b, ci: (b, 0, 0, 0))
    in_specs = [pl.BlockSpec(memory_space=pltpu.SMEM), tok(qk), tok(qk), tok(vd), tok(vd),
                _const_spec(dm.shape), _const_spec(qd.shape), _const_spec(kd.shape)]
    args = [cd, q, k, v, gate, dm, qd, kd]
    if state0 is not None:
        in_specs.append(st_spec)
        args.append(state0)
    return pl.pallas_call(
        functools.partial(_retention_kernel, has_state0=state0 is not None),
        grid=(nb // bb, t // c),
        in_specs=in_specs,
        out_specs=[tok(vd), st_spec],
        out_shape=[jax.ShapeDtypeStruct((nb, t, vd), BF16),
                   jax.ShapeDtypeStruct((nb, RET_HEADS, RET_KEY, RET_VAL), F32)],
        compiler_params=_params(2),
        name="retention_s" if state0 is not None else "retention_p",
    )(*args)


def _retention_consts(c, c_pad):
    log_g = jnp.log1p(-jnp.exp2(-5.0 - jnp.arange(RET_HEADS, dtype=F32)))
    i = jnp.arange(c, dtype=F32)
    diff = i[:, None] - i[None, :]
    dm = jnp.where(diff >= 0, jnp.exp(log_g[:, None, None] * jnp.maximum(diff, 0.0)), 0.0)
    qd = jnp.exp(log_g[:, None] * (i[None, :] + 1.0))
    kd = jnp.exp(log_g[:, None] * (c - 1.0 - i[None, :]))
    cd = jnp.exp(log_g * c)
    pad = c_pad - c
    dm = jnp.pad(dm, ((0, 0), (0, pad), (0, pad)))
    qd = jnp.broadcast_to(jnp.pad(qd, ((0, 0), (0, pad)))[:, :, None], (RET_HEADS, c_pad, RET_KEY))
    kd = jnp.broadcast_to(jnp.pad(kd, ((0, 0), (0, pad)))[:, :, None], (RET_HEADS, c_pad, RET_KEY))
    return dm, qd, kd, cd


def _rotation_tables(pos):
    half = RET_KEY // 2
    theta = 1.0 / (10000.0 ** jnp.linspace(0.0, 1.0, half, dtype=F32))
    ang = pos.astype(F32)[:, None] * theta[None, :]
    return jnp.cos(ang), jnp.sin(ang)


def kernel(x_prompt, x_sample, cache_k, cache_v, cache_kidx, state_ret, page_table, rel_bias, ln_mix, ln_mlp,
           att_w_in, att_q_gain, att_k_gain, att_w_out, ret_w_in, ret_w_out, mlp_w_in, mlp_w_out):
    bp, tp, _ = x_prompt.shape
    bs, ts, _ = x_sample.shape
    n_phys = cache_k.shape[1]
    assert ln_mix.shape[0] == 2 and att_w_in.shape[0] == 1 and ret_w_in.shape[0] == 1
    assert tp % RET_CHUNK == 0 and ts <= TQ and cache_k.shape[2] == PAGE
    assert (tp // PAGE) % 4 == 0 and bs % SEL_SAMPLES == 0
    kvw = ATT_KV_HEADS * HEAD_DIM
    past = page_table.shape[1] * PAGE

    w_att = jnp.pad(att_w_in[0], ((0, 0), (0, ATT_IN_PAD - ATT_IN))).astype(BF16)
    w_att_out = att_w_out[0].astype(BF16)
    w_ret = ret_w_in[0].astype(BF16)
    w_ret_out = ret_w_out[0].astype(BF16)
    w1 = mlp_w_in.astype(BF16)
    w2 = mlp_w_out.astype(BF16)
    row = lambda a: a.reshape(1, -1)

    xp = x_prompt.reshape(bp * tp, D_MODEL)
    xs = x_sample.reshape(bs * ts, D_MODEL)

    qT, qiT, wiT, vT3, kb, kib, kp, vp, kip = _attn_proj(
        xp, row(ln_mix[0]), w_att, row(att_q_gain[0]), row(att_k_gain[0]), transposed=True)
    op = _dsa_prompt(rel_bias, qT, qiT, wiT, kb, vT3, kib, batch=bp, seq=tp)
    yp = _post_mix(xp, op, w_att_out, row(ln_mlp[0]), w1[0], w2[0])

    qs, qis, zls, ks, vs, kis = _attn_proj(
        xs, row(ln_mix[0]), w_att, row(att_q_gain[0]), row(att_k_gain[0]), transposed=False)
    pad_t = lambda a: jnp.pad(a, ((0, 0), (0, TQ - ts)) + ((0, 0),) * (a.ndim - 2))
    q8 = pad_t(qs.reshape(bs, ts, ATT_HEADS, HEAD_DIM)).transpose(0, 2, 1, 3)
    q8 = q8.reshape(bs, ATT_HEADS * TQ, HEAD_DIM).astype(BF16)
    qi8 = pad_t(qis.reshape(bs, ts, IDX_HEADS, IDX_DIM)).transpose(0, 2, 1, 3)
    qi8 = qi8.reshape(bs, IDX_HEADS * TQ, IDX_DIM).astype(BF16)
    zl8 = pad_t(zls.reshape(bs, ts, LANES))
    kn8 = pad_t(ks.reshape(bs, ts, kvw))
    vn8 = pad_t(vs.reshape(bs, ts, kvw))
    mb = _dsa_select(page_table, qi8, zl8, jnp.swapaxes(cache_kidx[0], 1, 2), n_new=ts)
    os8 = _dsa_sample(page_table, rel_bias, q8, mb, kn8, vn8,
                      cache_k[0].reshape(n_phys * PAGE * ATT_KV_HEADS, HEAD_DIM),
                      cache_v[0].reshape(n_phys * PAGE * ATT_KV_HEADS, HEAD_DIM))
    os_ = os8.reshape(bs, ATT_HEADS, TQ, HEAD_DIM)[:, :, :ts].transpose(0, 2, 1, 3)
    os_ = os_.reshape(bs * ts, ATT_HEADS * HEAD_DIM)
    ys = _post_mix(xs, os_, w_att_out, row(ln_mlp[0]), w1[0], w2[0])

    cos_p, sin_p = _rotation_tables(jnp.arange(tp, dtype=jnp.int32))
    q, k, v, gate = _ret_proj(yp, row(ln_mix[1]), w_ret, cos_p, sin_p)
    r3 = lambda a: a.reshape(bp, tp, a.shape[-1])
    orp, state_p = _retention(r3(q), r3(k), r3(v), r3(gate), _retention_consts(RET_CHUNK, RET_CHUNK), None,
                            seqs_per_step=1)
    yp = _post_mix(yp, orp.reshape(bp * tp, -1), w_ret_out, row(ln_mlp[1]), w1[1], w2[1])

    cos_s, sin_s = _rotation_tables(past + jnp.arange(ts, dtype=jnp.int32))
    tile_s = lambda a: jnp.tile(a, (bs, 1))
    q, k, v, gate = _ret_proj(ys, row(ln_mix[1]), w_ret, tile_s(cos_s), tile_s(sin_s))
    r3s = lambda a: pad_t(a.reshape(bs, ts, a.shape[-1]))
    ors, state_s = _retention(r3s(q), r3s(k), r3s(v), r3s(gate), _retention_consts(ts, TQ), state_ret[0],
                            seqs_per_step=2)
    ys = _post_mix(ys, ors[:, :ts].reshape(bs * ts, -1), w_ret_out, row(ln_mlp[1]), w1[1], w2[1])

    return (yp.reshape(bp, tp, D_MODEL), ys.reshape(bs, ts, D_MODEL),
            kp.reshape(1, bp, tp, ATT_KV_HEADS, HEAD_DIM), vp.reshape(1, bp, tp, ATT_KV_HEADS, HEAD_DIM),
            kip.reshape(1, bp, tp, IDX_DIM), state_p[None],
            ks.reshape(1, bs, ts, ATT_KV_HEADS, HEAD_DIM), vs.reshape(1, bs, ts, ATT_KV_HEADS, HEAD_DIM),
            kis.reshape(1, bs, ts, IDX_DIM), state_s[None])
```

```python
import functools
import math

import jax
import jax.numpy as jnp
import numpy as np
from jax import lax
from jax.experimental import pallas as pl
from jax.experimental.pallas import tpu as pltpu

F32 = jnp.float32
BF16 = jnp.bfloat16
I32 = jnp.int32

D_MODEL = 1024
PAGE = 128
ATT_HEADS = 8
ATT_KV_HEADS = 2
ATT_GROUP = ATT_HEADS // ATT_KV_HEADS
HEAD_DIM = 128
IDX_HEADS = 4
IDX_DIM = 64
TOPK_MAX = 256
NUM_BUCKETS = 32
MAX_DISTANCE = 128
RET_HEADS = 4
RET_KEY = 256
RET_VAL = 512
RET_CHUNK = 256
D_FF = 4 * D_MODEL
EPS = 1e-6

Q_OFF = 0
K_OFF = ATT_HEADS * HEAD_DIM
V_OFF = K_OFF + ATT_KV_HEADS * HEAD_DIM
QI_OFF = V_OFF + ATT_KV_HEADS * HEAD_DIM
KI_OFF = QI_OFF + IDX_HEADS * IDX_DIM
WI_OFF = KI_OFF + IDX_DIM
ATT_IN = WI_OFF + IDX_HEADS
ATT_IN_PAD = 1920

LANES = 128
SUBLANES = 8
ROW_TILE = 512
VMEM_LIMIT = 56 * 1024 * 1024

LOG2E = math.log2(math.e)
Q_SCALE = HEAD_DIM ** -0.5 * LOG2E
ONES_ROWS = 16
RET_KSCALE = RET_KEY ** -0.5
INT_MIN = -(2 ** 31)
NEG = -1e30


def _bucket_bounds():
    n = np.arange(0, 8192)
    max_exact = NUM_BUCKETS // 2
    val = np.log(np.maximum(n, 1) / max_exact) / math.log(MAX_DISTANCE / max_exact) * (NUM_BUCKETS - max_exact)
    frac = np.abs(val - np.round(val))
    risky = (n > max_exact) & (n < MAX_DISTANCE) & (frac < 1e-4)
    assert not risky.any()
    large = np.minimum(max_exact + np.floor(val + 1e-9).astype(np.int64), NUM_BUCKETS - 1)
    bucket = np.where(n < max_exact, n, large)
    assert (np.diff(bucket) >= 0).all() and (bucket[MAX_DISTANCE:] == NUM_BUCKETS - 1).all()
    return [int(np.argmax(bucket >= b)) for b in range(NUM_BUCKETS)]


BUCKET_BOUNDS = _bucket_bounds()


def _const_spec(shape):
    zeros = (0,) * len(shape)
    return pl.BlockSpec(shape, lambda *_: zeros, pipeline_mode=pl.Buffered(1))


def _params(n_axes):
    return pltpu.CompilerParams(dimension_semantics=("arbitrary",) * n_axes,
                                vmem_limit_bytes=VMEM_LIMIT)


def _rms_rows(a):
    return a * lax.rsqrt(jnp.mean(a * a, axis=-1, keepdims=True) + EPS)


def _canonical_zero(score):
    return jnp.where(score == 0.0, 0.0, score)


def _sortable_key(score):
    bits = pltpu.bitcast(score, I32)
    return bits ^ ((bits >> 31) & jnp.int32(0x7FFFFFFF))


def _key_to_float(key):
    return pltpu.bitcast(key ^ ((key >> 31) & jnp.int32(0x7FFFFFFF)), F32)


KEY_NEG_INF = -2139095041
KEY_POS_INF = 2139095040
KEY_SUBNORMAL_LO = -8388608
KEY_SUBNORMAL_HI = 8388607


def _bracket_kth_largest(count_ge, kprime, guess):
    def canon(x):
        return jnp.where(jnp.logical_and(x >= KEY_SUBNORMAL_LO, x <= KEY_SUBNORMAL_HI), 0, x)

    def succ(x):
        xc = canon(x)
        return jnp.where(xc == 0, KEY_SUBNORMAL_HI + 1, canon(xc + 1))

    g0 = jnp.clip(guess, KEY_NEG_INF, KEY_POS_INF - 1)
    g1 = succ(g0)
    lo = jnp.full(guess.shape, KEY_NEG_INF, I32)
    hi = jnp.full(guess.shape, KEY_POS_INF, I32)

    def narrow(lo, hi, x, cnt):
        ok = cnt >= kprime
        return jnp.where(ok, jnp.maximum(lo, x), lo), jnp.where(ok, hi, jnp.minimum(hi, x))

    c0, c1 = count_ge([_key_to_float(g0), _key_to_float(g1)])
    lo, hi = narrow(lo, hi, g0, c0)
    lo, hi = narrow(lo, hi, g1, c1)

    def open_bracket(c):
        return jnp.max(jnp.where(canon(c[1]) > succ(c[0]), 1.0, 0.0)) > 0.5

    def bisect(c):
        lo, hi = c
        mid = jnp.where(canon(hi) > succ(lo), (lo >> 1) + (hi >> 1) + (lo & hi & 1), lo)
        return narrow(lo, hi, mid, count_ge([_key_to_float(mid)])[0])

    lo, _ = lax.while_loop(open_bracket, bisect, (lo, hi))
    return canon(lo)


def _bit_transpose32(words):
    a = list(words)
    j, m = 16, 0x0000FFFF
    while j:
        k = 0
        while k < 32:
            t = (a[k] ^ lax.shift_right_logical(a[k + j], jnp.int32(j))) & jnp.int32(m)
            a[k] = a[k] ^ t
            a[k + j] = a[k + j] ^ lax.shift_left(t, jnp.int32(j))
            k = (k + j + 1) & ~j
        j >>= 1
        m ^= (m << j) & 0xFFFFFFFF
    return a


def _bias_from_distance(relb_ref, head, dist):
    val = jnp.full(dist.shape, relb_ref[0, head], F32)
    for b in range(1, NUM_BUCKETS):
        val = jnp.where(dist >= BUCKET_BOUNDS[b], relb_ref[b, head], val)
    return val * LOG2E


def _attn_proj_kernel(x_ref, g_ref, w_ref, qg_ref, kg_ref, *outs, transposed):
    x = x_ref[...]
    h = _rms_rows(x) * g_ref[...]
    z = jnp.dot(h.astype(BF16), w_ref[...], preferred_element_type=F32)
    qg = qg_ref[...]
    kg = kg_ref[...]
    k = jnp.concatenate(
        [_rms_rows(z[:, K_OFF + g * HEAD_DIM:K_OFF + (g + 1) * HEAD_DIM]) * kg for g in range(ATT_KV_HEADS)],
        axis=1)
    v = z[:, V_OFF:QI_OFF]
    qi = z[:, QI_OFF:KI_OFF]
    zl = z[:, KI_OFF:ATT_IN_PAD]
    ki = zl[:, :IDX_DIM]
    if transposed:
        qT_ref, qiT_ref, wiT_ref, vT_ref, kb_ref, kib_ref, k_ref, v_ref, ki_ref = outs
        for hh in range(ATT_HEADS):
            qh = _rms_rows(z[:, hh * HEAD_DIM:(hh + 1) * HEAD_DIM]) * qg * Q_SCALE
            qT_ref[hh * HEAD_DIM:(hh + 1) * HEAD_DIM, :] = qh.T.astype(BF16)
        for c in range(IDX_HEADS * IDX_DIM // LANES):
            qiT_ref[c * LANES:(c + 1) * LANES, :] = qi[:, c * LANES:(c + 1) * LANES].T.astype(BF16)
        wiT_ref[...] = zl.T[IDX_DIM:IDX_DIM + 8, :]
        for g in range(ATT_KV_HEADS):
            vT = v[:, g * HEAD_DIM:(g + 1) * HEAD_DIM].T
            for c in range(x.shape[0] // PAGE):
                vT_ref[c, g * HEAD_DIM:(g + 1) * HEAD_DIM, :] = vT[:, c * PAGE:(c + 1) * PAGE].astype(BF16)
        kb_ref[...] = k.astype(BF16)
        kib_ref[...] = ki.astype(BF16)
    else:
        q_ref, qi_ref, zl_ref, k_ref, v_ref, ki_ref = outs
        for hh in range(ATT_HEADS):
            q_ref[:, hh * HEAD_DIM:(hh + 1) * HEAD_DIM] = _rms_rows(z[:, hh * HEAD_DIM:(hh + 1) * HEAD_DIM]) * qg * Q_SCALE
        qi_ref[...] = qi
        zl_ref[...] = zl
    for g in range(ATT_KV_HEADS):
        rows_g = pl.ds(g, x.shape[0], stride=ATT_KV_HEADS)
        k_ref[rows_g, :] = k[:, g * HEAD_DIM:(g + 1) * HEAD_DIM]
        v_ref[rows_g, :] = v[:, g * HEAD_DIM:(g + 1) * HEAD_DIM]
    ki_ref[...] = ki


def _attn_proj(x, g, w, qg, kg, *, transposed):
    rows = x.shape[0]
    tm = ROW_TILE
    nt = rows // tm
    kvw = ATT_KV_HEADS * HEAD_DIM
    row_spec = lambda width: pl.BlockSpec((tm, width), lambda i: (i, 0))
    col_spec = lambda height: pl.BlockSpec((height, tm), lambda i: (0, i))
    kv_rows = ATT_KV_HEADS * rows
    kv_spec = pl.BlockSpec((ATT_KV_HEADS * tm, HEAD_DIM), lambda i: (i, 0))
    leaves_shape = [jax.ShapeDtypeStruct((kv_rows, HEAD_DIM), F32), jax.ShapeDtypeStruct((kv_rows, HEAD_DIM), F32),
                    jax.ShapeDtypeStruct((rows, IDX_DIM), F32)]
    leaves_spec = [kv_spec, kv_spec, row_spec(IDX_DIM)]
    if transposed:
        out_shape = [jax.ShapeDtypeStruct((ATT_HEADS * HEAD_DIM, rows), BF16),
                     jax.ShapeDtypeStruct((IDX_HEADS * IDX_DIM, rows), BF16),
                     jax.ShapeDtypeStruct((8, rows), F32),
                     jax.ShapeDtypeStruct((rows // PAGE, kvw, PAGE), BF16),
                     jax.ShapeDtypeStruct((rows, kvw), BF16),
                     jax.ShapeDtypeStruct((rows, IDX_DIM), BF16)] + leaves_shape
        out_specs = [col_spec(ATT_HEADS * HEAD_DIM), col_spec(IDX_HEADS * IDX_DIM), col_spec(8),
                     pl.BlockSpec((tm // PAGE, kvw, PAGE), lambda i: (i, 0, 0)),
                     row_spec(kvw), row_spec(IDX_DIM)] + leaves_spec
    else:
        out_shape = [jax.ShapeDtypeStruct((rows, ATT_HEADS * HEAD_DIM), F32),
                     jax.ShapeDtypeStruct((rows, IDX_HEADS * IDX_DIM), F32),
                     jax.ShapeDtypeStruct((rows, LANES), F32)] + leaves_shape
        out_specs = [row_spec(ATT_HEADS * HEAD_DIM), row_spec(IDX_HEADS * IDX_DIM), row_spec(LANES)] + leaves_spec
    return pl.pallas_call(
        functools.partial(_attn_proj_kernel, transposed=transposed),
        grid=(nt,),
        in_specs=[row_spec(D_MODEL), _const_spec((1, D_MODEL)), _const_spec((D_MODEL, ATT_IN_PAD)),
                  _const_spec((1, HEAD_DIM)), _const_spec((1, HEAD_DIM))],
        out_specs=out_specs,
        out_shape=out_shape,
        compiler_params=_params(1),
        name="attn_proj_t" if transposed else "attn_proj_r",
    )(x, g, w, qg, kg)


PROMPT_SEQS = 2


def _dsa_prompt_kernel(relb_ref, *refs, topk):
    seqs = range(PROMPT_SEQS)
    n_in = 6
    qT_refs, qiT_refs, wiT_refs, k_refs, vT_refs, ki_refs = zip(*[refs[n_in * u:n_in * (u + 1)] for u in seqs])
    o_ref, keys_sc, scores_sc, planes_sc, live_sc, mb_sc, bias_sc, logit_sc, acc_sc = refs[n_in * PROMPT_SEQS:]
    b = pl.program_id(0)
    i = pl.program_id(1)
    srow = lax.broadcasted_iota(I32, (PAGE, PAGE), 0)
    tcol = lax.broadcasted_iota(I32, (PAGE, PAGE), 1)

    @pl.when((b == 0) & (i == 0))
    def _init_bias():
        planes_sc[...] = jnp.zeros_like(planes_sc)
        live_sc[...] = jnp.zeros_like(live_sc)
        for hh in range(ATT_HEADS):
            for rel in range(2):
                dist = jnp.maximum(rel * PAGE + tcol - srow, 0)
                bias_sc[hh, rel] = _bias_from_distance(relb_ref, hh, dist)
            bias_sc[hh, 2] = jnp.full((PAGE, PAGE), relb_ref[NUM_BUCKETS - 1, hh] * LOG2E, F32)

    wiT = [r[...] for r in wiT_refs]
    qi_cat = [jnp.concatenate([r[hh * IDX_DIM:(hh + 1) * IDX_DIM, :] for hh in range(IDX_HEADS)], axis=1)
              for r in qiT_refs]

    npair = (i + 2) // 2

    def score_block(u, j):
        off = pl.multiple_of(j * PAGE, PAGE)
        kij = ki_refs[u][pl.ds(off, PAGE), :]
        idx = jnp.dot(kij, qi_cat[u], preferred_element_type=F32)
        sc = jnp.zeros((PAGE, PAGE), F32)
        for hh in range(IDX_HEADS):
            sc = sc + wiT[u][hh:hh + 1, :] * jnp.maximum(idx[:, hh * PAGE:(hh + 1) * PAGE], 0.0)
        sc = _canonical_zero(sc)
        causal = jnp.logical_or(j < i, jnp.logical_and(j == i, srow <= tcol))
        scores_sc[u, j] = jnp.where(causal, sc, -jnp.inf)
        keys_sc[u, j] = jnp.where(causal, _sortable_key(sc), INT_MIN)

    nquad = (i + 4) // 4

    def score_body(jj, carry):
        for blk in range(4):
            for u in seqs:
                score_block(u, 4 * jj + blk)
        return carry

    lax.fori_loop(0, nquad, score_body, 0)

    lane = lax.broadcasted_iota(I32, (1, PAGE), 1)
    kprime = jnp.minimum(topk, i * PAGE + lane + 1).astype(F32)

    def plane_body(jj, carry):
        for u in seqs:
            rows = [keys_sc[u, 2 * jj + blk, k * SUBLANES:(k + 1) * SUBLANES, :] ^ INT_MIN
                    for blk in range(2) for k in range(PAGE // SUBLANES)]
            planes = _bit_transpose32(rows)
            live = planes[0]
            for p in range(32):
                planes_sc[u, jj, p] = planes[p]
                live = live | planes[p]
            live_sc[u, jj] = live
        return carry

    lax.fori_loop(0, npair, plane_body, 0)

    n_pairs_max = keys_sc.shape[1] // 2

    def bit_step(u, bi, state):
        thr_u, alive, above = state
        cnt = jnp.zeros((SUBLANES, PAGE), I32)
        with_bit, at_least = [], []
        for p in range(n_pairs_max):
            t = alive[p] & planes_sc[u, p, bi]
            ge = above[p] | t
            cnt = cnt + lax.population_count(ge)
            with_bit.append(t)
            at_least.append(ge)
        take = jnp.sum(cnt.astype(F32), axis=0, keepdims=True) >= kprime
        alive = tuple(jnp.where(take, t, a ^ t) for t, a in zip(with_bit, alive))
        above = tuple(jnp.where(take, g, ge) for g, ge in zip(above, at_least))
        thr_u = thr_u | jnp.where(take, lax.shift_left(jnp.int32(1), 31 - bi), 0)
        return thr_u, alive, above

    zero_words = jnp.zeros((SUBLANES, PAGE), I32)
    states = lax.fori_loop(
        0, 32, lambda bi, states: tuple(bit_step(u, bi, states[u]) for u in seqs),
        tuple((jnp.zeros((1, PAGE), I32),
               tuple(jnp.where(p < npair, live_sc[u, p], zero_words) for p in range(n_pairs_max)),
               (zero_words,) * n_pairs_max) for u in seqs))
    def count_scores(u, preds):
        def body(jj, cnts):
            for blk in range(2):
                sc = scores_sc[u, 2 * jj + blk]
                cnts = tuple(c + jnp.where(p(sc), 1.0, 0.0) for c, p in zip(cnts, preds))
            return cnts
        cnts = lax.fori_loop(0, npair, body, tuple(jnp.zeros((PAGE, PAGE), F32) for _ in preds))
        return [jnp.sum(c, axis=0, keepdims=True) for c in cnts]

    def count_ge(u):
        return lambda thresholds: count_scores(u, [lambda sc, x=x: sc >= x for x in thresholds])

    thr = [_key_to_float(_bracket_kth_largest(count_ge(u), kprime, states[u][0] ^ INT_MIN)) for u in seqs]
    need = [kprime - count_scores(u, [lambda sc, u=u: sc > thr[u]])[0] for u in seqs]

    tri = jnp.where(srow > tcol, 1.0, 0.0).astype(BF16)

    def mask_block(u, j, carry):
        sc = scores_sc[u, j]
        eqf = jnp.where(sc == thr[u], 1.0, 0.0)
        rank = jnp.dot(tri, eqf.astype(BF16), preferred_element_type=F32) + carry
        tie = jnp.where(jnp.logical_and(sc == thr[u], rank < need[u]), 0.0, NEG)
        mb_sc[u, j] = jnp.where(sc > thr[u], 0.0, tie)
        return carry + jnp.sum(eqf, axis=0, keepdims=True)

    def mask_body(jj, carry):
        for blk in range(4):
            carry = tuple(mask_block(u, 4 * jj + blk, carry[u]) for u in seqs)
        return carry

    lax.fori_loop(0, nquad, mask_body, tuple(jnp.zeros((1, PAGE), F32) for _ in seqs))

    groups = range(ATT_KV_HEADS)
    units = [(u, g) for u in seqs for g in groups]
    width = ATT_GROUP * PAGE
    qcats = [jnp.concatenate(
        [qT_refs[u][(ATT_GROUP * g + r) * HEAD_DIM:(ATT_GROUP * g + r + 1) * HEAD_DIM, :]
         for r in range(ATT_GROUP)], axis=1) for u, g in units]

    def logit_block(j, ms):
        off = pl.multiple_of(j * PAGE, PAGE)
        relc = jnp.clip(i - j, 0, 2)
        out = []
        for n, (u, g) in enumerate(units):
            mb = mb_sc[u, j]
            kj = k_refs[u][pl.ds(off, PAGE), g * HEAD_DIM:(g + 1) * HEAD_DIM]
            logits = jnp.dot(kj, qcats[n], preferred_element_type=F32)
            logits = jnp.concatenate(
                [logits[:, r * PAGE:(r + 1) * PAGE] + (mb + bias_sc[ATT_GROUP * g + r, relc])
                 for r in range(ATT_GROUP)], axis=1)
            logit_sc[u, g, j] = logits
            out.append(jnp.maximum(ms[n], jnp.max(logits, axis=0, keepdims=True)))
        return tuple(out)

    def over_block_pairs(pair_step, carry):
        nfull = npair // 2
        carry = lax.fori_loop(0, nfull, lambda t, c: pair_step(4 * t + 2, pair_step(4 * t, c)), carry)
        return lax.cond(npair % 2 == 1, lambda c: pair_step(4 * nfull, c), lambda c: c, carry)

    ms = over_block_pairs(lambda j0, c: logit_block(j0 + 1, logit_block(j0, c)),
                          tuple(jnp.full((1, width), NEG, F32) for _ in units))

    ones_rows = jnp.ones((ONES_ROWS, PAGE), BF16)

    def value_pair(j0, carry):
        for n, (u, g) in enumerate(units):
            p = jnp.concatenate(
                [jnp.exp2(logit_sc[u, g, j0 + blk] - ms[n]).astype(BF16) for blk in range(2)], axis=0)
            vT2 = jnp.concatenate(
                [jnp.concatenate([vT_refs[u][j0 + blk, g * HEAD_DIM:(g + 1) * HEAD_DIM, :], ones_rows], axis=0)
                 for blk in range(2)], axis=1)
            acc_sc[u, g] = acc_sc[u, g] + jnp.dot(vT2, p, preferred_element_type=F32)
        return carry

    acc_sc[...] = jnp.zeros_like(acc_sc)
    over_block_pairs(value_pair, 0)
    for u, g in units:
        acc = acc_sc[u, g]
        outT = acc[:HEAD_DIM] * (1.0 / acc[HEAD_DIM:HEAD_DIM + 1])
        for r in range(ATT_GROUP):
            hh = ATT_GROUP * g + r
            o_ref[u, :, hh * HEAD_DIM:(hh + 1) * HEAD_DIM] = outT[:, r * PAGE:(r + 1) * PAGE].T.astype(BF16)


def _dsa_prompt(rel_bias, qT, qiT, wiT, kb, vT3, kib, *, batch, seq):
    nb = seq // PAGE
    rows = batch * seq
    kvw = ATT_KV_HEADS * HEAD_DIM
    ns = PROMPT_SEQS

    def seq_specs(u):
        qcol = lambda height: pl.BlockSpec((height, PAGE), lambda b, i: (0, (ns * b + u) * nb + i))
        return [qcol(ATT_HEADS * HEAD_DIM), qcol(IDX_HEADS * IDX_DIM), qcol(8),
                pl.BlockSpec((seq, kvw), lambda b, i: (ns * b + u, 0)),
                pl.BlockSpec((nb, kvw, PAGE), lambda b, i: (ns * b + u, 0, 0)),
                pl.BlockSpec((seq, IDX_DIM), lambda b, i: (ns * b + u, 0))]

    out = pl.pallas_call(
        functools.partial(_dsa_prompt_kernel, topk=min(TOPK_MAX, seq // 4)),
        grid=(batch // ns, nb),
        in_specs=[pl.BlockSpec(memory_space=pltpu.SMEM)] + [s for u in range(ns) for s in seq_specs(u)],
        out_specs=pl.BlockSpec((ns, PAGE, ATT_HEADS * HEAD_DIM), lambda b, i: (b, i, 0)),
        out_shape=jax.ShapeDtypeStruct((batch, seq, ATT_HEADS * HEAD_DIM), BF16),
        scratch_shapes=[pltpu.VMEM((ns, nb, PAGE, PAGE), I32), pltpu.VMEM((ns, nb, PAGE, PAGE), F32),
                        pltpu.VMEM((ns, nb // 2, 32, SUBLANES, PAGE), I32),
                        pltpu.VMEM((ns, nb // 2, SUBLANES, PAGE), I32),
                        pltpu.VMEM((ns, nb, PAGE, PAGE), F32),
                        pltpu.VMEM((ATT_HEADS, 3, PAGE, PAGE), F32),
                        pltpu.VMEM((ns, ATT_KV_HEADS, nb, PAGE, ATT_GROUP * PAGE), F32),
                        pltpu.VMEM((ns, ATT_KV_HEADS, HEAD_DIM + ONES_ROWS, ATT_GROUP * PAGE), F32)],
        compiler_params=_params(2),
        name="dsa_prompt",
    )(rel_bias, *([qT, qiT, wiT, kb, vT3, kib] * ns))
    return out.reshape(rows, ATT_HEADS * HEAD_DIM)


TQ = 8
SEL_SAMPLES = 16


def _dsa_select_kernel(pt_ref, qi_ref, zl_ref, cki_hbm, mb_ref, sc_sc, kibuf, sems, *, n_pages, n_new):
    ns = SEL_SAMPLES
    n_fetch = ns * n_pages
    b = pl.program_id(0)
    slot = b % 2

    def page_copies(step, slot):
        return [pltpu.make_async_copy(cki_hbm.at[pt_ref[step * n_fetch + f]], kibuf.at[slot, f], sems.at[slot])
                for f in range(n_fetch)]

    @pl.when(b == 0)
    def _start_first():
        for c in page_copies(0, 0):
            c.start()

    @pl.when(b + 1 < pl.num_programs(0))
    def _start_next():
        for c in page_copies(b + 1, 1 - slot):
            c.start()

    for c in page_copies(b, slot):
        c.wait()
    ki_pages = [kibuf.at[slot, f] for f in range(n_fetch)]
    n_blk = n_pages + 1
    rows = ns * TQ
    trow = lax.broadcasted_iota(I32, (TQ, PAGE), 0)
    scol = lax.broadcasted_iota(I32, (TQ, PAGE), 1)
    pad_rows = lambda a: jnp.concatenate([a, jnp.zeros((PAGE - TQ, a.shape[1]), a.dtype)], axis=0)
    nt = (((1,), (1,)), ((), ()))

    for s in range(ns):
        zl = zl_ref[s]
        qi = qi_ref[s]
        for p in range(n_blk):
            if p == n_pages:
                ki_new = pad_rows(zl[:, :IDX_DIM]).astype(BF16)
                idx = lax.dot_general(qi, ki_new, nt, preferred_element_type=F32)
            else:
                idx = jnp.dot(qi, ki_pages[s * n_pages + p][...].astype(BF16), preferred_element_type=F32)
            sc = jnp.zeros((TQ, PAGE), F32)
            for hh in range(IDX_HEADS):
                sc = sc + zl[:, IDX_DIM + hh:IDX_DIM + hh + 1] * jnp.maximum(idx[hh * TQ:(hh + 1) * TQ], 0.0)
            sc = _canonical_zero(sc)
            if p == n_pages:
                sc = jnp.where(jnp.logical_and(scol <= trow, scol < n_new), sc, -jnp.inf)
            sc_sc[s * TQ:(s + 1) * TQ, p * PAGE:(p + 1) * PAGE] = sc

    kprime = float(min(TOPK_MAX, (n_pages * PAGE + n_new) // 4))

    def count(pred_of_scores):
        return jnp.sum(jnp.where(pred_of_scores(sc_sc[...]), 1.0, 0.0), axis=1, keepdims=True)

    cand0 = jnp.zeros((rows, 1), I32)
    thr0 = jnp.where(count(lambda sc: sc >= _key_to_float(cand0)) >= kprime, cand0, INT_MIN)

    def bit_body(bi, thr_key):
        cand = thr_key | lax.shift_left(jnp.int32(1), 30 - bi)
        return jnp.where(count(lambda sc: sc >= _key_to_float(cand)) >= kprime, cand, thr_key)

    thr = _key_to_float(lax.fori_loop(0, 31, bit_body, thr0))
    need = kprime - count(lambda sc: sc > thr)

    s_r = lax.broadcasted_iota(I32, (PAGE, PAGE), 0)
    s_c = lax.broadcasted_iota(I32, (PAGE, PAGE), 1)
    triu = jnp.where(s_r < s_c, 1.0, 0.0).astype(BF16)
    carry = jnp.zeros((rows, 1), F32)
    for p in range(n_blk):
        sc = sc_sc[:, p * PAGE:(p + 1) * PAGE]
        eqf = jnp.where(sc == thr, 1.0, 0.0)
        rank = jnp.dot(eqf.astype(BF16), triu, preferred_element_type=F32) + carry
        tie = jnp.where(jnp.logical_and(sc == thr, rank < need), 0.0, NEG)
        mb_ref[:, p * PAGE:(p + 1) * PAGE] = jnp.where(sc > thr, 0.0, tie)
        carry = carry + jnp.sum(eqf, axis=1, keepdims=True)


def _dsa_select(page_table, qi8, zl8, cache_ki, *, n_new):
    nsamp, n_pages = page_table.shape
    ns = SEL_SAMPLES
    width = (n_pages + 1) * PAGE

    in_specs = [pl.BlockSpec((ns, IDX_HEADS * TQ, IDX_DIM), lambda b, pt: (b, 0, 0)),
                pl.BlockSpec((ns, TQ, LANES), lambda b, pt: (b, 0, 0)),
                pl.BlockSpec(memory_space=pl.ANY)]
    grid_spec = pltpu.PrefetchScalarGridSpec(
        num_scalar_prefetch=1, grid=(nsamp // ns,), in_specs=in_specs,
        out_specs=pl.BlockSpec((ns * TQ, width), lambda b, pt: (b, 0)),
        scratch_shapes=[pltpu.VMEM((ns * TQ, width), F32),
                        pltpu.VMEM((2, ns * n_pages, IDX_DIM, PAGE), F32), pltpu.SemaphoreType.DMA((2,))])
    return pl.pallas_call(
        functools.partial(_dsa_select_kernel, n_pages=n_pages, n_new=n_new),
        grid_spec=grid_spec,
        out_shape=jax.ShapeDtypeStruct((nsamp * TQ, width), F32),
        compiler_params=_params(1),
        name="dsa_select",
    )(page_table.reshape(-1), qi8, zl8, cache_ki)


def _dsa_sample_kernel(pt_ref, relb_ref, q_ref, mb_ref, kn_ref, vn_ref, ck_hbm, cv_hbm, o_ref,
                       bias_sc, kall_sc, vall_sc, kbuf, vbuf, sems, *, n_pages):
    past = n_pages * PAGE
    rows_g = ATT_GROUP * TQ
    page_rows = PAGE * ATT_KV_HEADS
    b = pl.program_id(0)
    slot = b % 2

    def page_copies(sample, slot):
        copies = []
        for p in range(n_pages):
            src = pl.ds(pl.multiple_of(pt_ref[sample * n_pages + p] * page_rows, page_rows), page_rows)
            dst = pl.ds(p * page_rows, page_rows)
            copies.append(pltpu.make_async_copy(ck_hbm.at[src, :], kbuf.at[slot, dst, :], sems.at[slot, 0]))
            copies.append(pltpu.make_async_copy(cv_hbm.at[src, :], vbuf.at[slot, dst, :], sems.at[slot, 1]))
        return copies

    @pl.when(b == 0)
    def _start_first():
        for c in page_copies(0, 0):
            c.start()

    @pl.when(b + 1 < pl.num_programs(0))
    def _start_next():
        for c in page_copies(b + 1, 1 - slot):
            c.start()

    for c in page_copies(b, slot):
        c.wait()

    trow = lax.broadcasted_iota(I32, (TQ, PAGE), 0)
    scol = lax.broadcasted_iota(I32, (TQ, PAGE), 1)

    @pl.when(pl.program_id(0) == 0)
    def _init_bias():
        for g in range(ATT_KV_HEADS):
            for r in range(ATT_GROUP):
                hh = ATT_GROUP * g + r
                rs = slice(r * TQ, (r + 1) * TQ)
                far = jnp.full((TQ, PAGE), relb_ref[NUM_BUCKETS - 1, hh] * LOG2E, F32)
                for p in range(n_pages - 1):
                    bias_sc[g, rs, p * PAGE:(p + 1) * PAGE] = far
                bias_sc[g, rs, past - PAGE:past] = _bias_from_distance(
                    relb_ref, hh, jnp.maximum(PAGE + trow - scol, 0))
                bias_sc[g, rs, past:past + PAGE] = _bias_from_distance(relb_ref, hh, jnp.maximum(trow - scol, 0))

    nt = (((1,), (1,)), ((), ()))
    mb = jnp.concatenate([mb_ref[...]] * ATT_GROUP, axis=0)
    pad_rows = lambda a: jnp.concatenate([a, jnp.zeros((PAGE - TQ, a.shape[1]), a.dtype)], axis=0)
    for g in range(ATT_KV_HEADS):
        gs = slice(g * HEAD_DIM, (g + 1) * HEAD_DIM)
        for p in range(n_pages):
            ps = slice(p * PAGE, (p + 1) * PAGE)
            head_rows = pl.ds(p * page_rows + g, PAGE, stride=ATT_KV_HEADS)
            kall_sc[g, ps, :] = kbuf[slot, head_rows, :].astype(BF16)
            vall_sc[g, ps, :] = vbuf[slot, head_rows, :].astype(BF16)
        kall_sc[g, past:past + PAGE, :] = pad_rows(kn_ref[:, gs]).astype(BF16)
        vall_sc[g, past:past + PAGE, :] = pad_rows(vn_ref[:, gs]).astype(BF16)
    for g in range(ATT_KV_HEADS):
        qg = q_ref[g * rows_g:(g + 1) * rows_g, :]
        logits = lax.dot_general(qg, kall_sc[g], nt, preferred_element_type=F32)
        logits = logits + (mb + bias_sc[g])
        m = jnp.max(logits, axis=1, keepdims=True)
        pr = jnp.exp2(logits - m)
        l = jnp.sum(pr, axis=1, keepdims=True)
        acc = jnp.dot(pr.astype(BF16), vall_sc[g], preferred_element_type=F32)
        o_ref[g * rows_g:(g + 1) * rows_g, :] = (acc * (1.0 / l)).astype(BF16)


def _dsa_sample(page_table, rel_bias, q8, mb, kn8, vn8, cache_k, cache_v):
    nsamp, n_pages = page_table.shape
    kvw = ATT_KV_HEADS * HEAD_DIM
    per_sample = lambda *tail: pl.BlockSpec((None,) + tail, lambda b, pt: (b,) + (0,) * len(tail))

    page_buf = pltpu.VMEM((2, n_pages * PAGE * ATT_KV_HEADS, HEAD_DIM), F32)
    in_specs = [pl.BlockSpec(memory_space=pltpu.SMEM),
                per_sample(ATT_HEADS * TQ, HEAD_DIM),
                pl.BlockSpec((TQ, mb.shape[1]), lambda b, pt: (b, 0)),
                per_sample(TQ, kvw), per_sample(TQ, kvw),
                pl.BlockSpec(memory_space=pl.ANY), pl.BlockSpec(memory_space=pl.ANY)]
    grid_spec = pltpu.PrefetchScalarGridSpec(
        num_scalar_prefetch=1, grid=(nsamp,), in_specs=in_specs,
        out_specs=per_sample(ATT_HEADS * TQ, HEAD_DIM),
        scratch_shapes=[pltpu.VMEM((ATT_KV_HEADS, ATT_GROUP * TQ, mb.shape[1]), F32),
                        pltpu.VMEM((ATT_KV_HEADS, mb.shape[1], HEAD_DIM), BF16),
                        pltpu.VMEM((ATT_KV_HEADS, mb.shape[1], HEAD_DIM), BF16),
                        page_buf, page_buf, pltpu.SemaphoreType.DMA((2, 2))])
    return pl.pallas_call(
        functools.partial(_dsa_sample_kernel, n_pages=n_pages),
        grid_spec=grid_spec,
        out_shape=jax.ShapeDtypeStruct((nsamp, ATT_HEADS * TQ, HEAD_DIM), BF16),
        compiler_params=_params(1),
        name="dsa_sample",
    )(page_table.reshape(-1), rel_bias, q8, mb, kn8, vn8, cache_k, cache_v)


def _post_mix_kernel(x_ref, o_ref, wo_ref, g_ref, w1_ref, w2_ref, y_ref):
    y = x_ref[...] + jnp.dot(o_ref[...], wo_ref[...], preferred_element_type=F32)
    h = (_rms_rows(y) * g_ref[...]).astype(BF16)
    acc = y
    for c in range(D_FF // D_MODEL):
        a = jnp.dot(h, w1_ref[:, c * D_MODEL:(c + 1) * D_MODEL], preferred_element_type=F32)
        a = jnp.square(jnp.maximum(a, 0.0)).astype(BF16)
        acc = acc + jnp.dot(a, w2_ref[c * D_MODEL:(c + 1) * D_MODEL, :], preferred_element_type=F32)
    y_ref[...] = acc


def _post_mix(x, o, wo, g, w1, w2):
    rows = x.shape[0]
    tm = ROW_TILE
    ko = o.shape[1]
    return pl.pallas_call(
        _post_mix_kernel,
        grid=(rows // tm,),
        in_specs=[pl.BlockSpec((tm, D_MODEL), lambda i: (i, 0)), pl.BlockSpec((tm, ko), lambda i: (i, 0)),
                  _const_spec((ko, D_MODEL)), _const_spec((1, D_MODEL)),
                  _const_spec((D_MODEL, D_FF)), _const_spec((D_FF, D_MODEL))],
        out_specs=pl.BlockSpec((tm, D_MODEL), lambda i: (i, 0)),
        out_shape=jax.ShapeDtypeStruct((rows, D_MODEL), F32),
        compiler_params=_params(1),
        name="post_mix",
    )(x, o, wo, g, w1, w2)


def _ret_proj_kernel(x_ref, g_ref, w_ref, cos_ref, sin_ref, q_ref, k_ref, v_ref, gate_ref):
    hb = (_rms_rows(x_ref[...]) * g_ref[...]).astype(BF16)
    cos = cos_ref[...]
    sin = sin_ref[...]
    half = RET_KEY // 2
    qk = RET_HEADS * RET_KEY
    vd = RET_HEADS * RET_VAL

    def rotated(col0, scale):
        z = jnp.dot(hb, w_ref[:, col0:col0 + RET_KEY], preferred_element_type=F32)
        x1, x2 = z[:, :half], z[:, half:]
        return (x1 * cos - x2 * sin) * scale, (x1 * sin + x2 * cos) * scale

    for hh in range(RET_HEADS):
        a, b = rotated(hh * RET_KEY, 1.0)
        q_ref[:, hh * RET_KEY:hh * RET_KEY + half] = a.astype(BF16)
        q_ref[:, hh * RET_KEY + half:(hh + 1) * RET_KEY] = b.astype(BF16)
        a, b = rotated(qk + hh * RET_KEY, RET_KSCALE)
        k_ref[:, hh * RET_KEY:hh * RET_KEY + half] = a.astype(BF16)
        k_ref[:, hh * RET_KEY + half:(hh + 1) * RET_KEY] = b.astype(BF16)
    for hh in range(RET_HEADS):
        cs = slice(hh * RET_VAL, (hh + 1) * RET_VAL)
        v_ref[:, cs] = jnp.dot(hb, w_ref[:, 2 * qk + hh * RET_VAL:2 * qk + (hh + 1) * RET_VAL],
                               preferred_element_type=F32).astype(BF16)
        gate_ref[:, cs] = jnp.dot(hb, w_ref[:, 2 * qk + vd + hh * RET_VAL:2 * qk + vd + (hh + 1) * RET_VAL],
                                  preferred_element_type=F32).astype(BF16)


def _ret_proj(x, g, w, cos, sin):
    rows = x.shape[0]
    tm = ROW_TILE
    qk = RET_HEADS * RET_KEY
    vd = RET_HEADS * RET_VAL
    n_pos_tiles = cos.shape[0] // tm
    row_spec = lambda width: pl.BlockSpec((tm, width), lambda i: (i, 0))
    pos_spec = pl.BlockSpec((tm, RET_KEY // 2), lambda i: (i % n_pos_tiles, 0))
    return pl.pallas_call(
        _ret_proj_kernel,
        grid=(rows // tm,),
        in_specs=[row_spec(D_MODEL), _const_spec((1, D_MODEL)), _const_spec((D_MODEL, 2 * qk + 2 * vd)),
                  pos_spec, pos_spec],
        out_specs=[row_spec(qk), row_spec(qk), row_spec(vd), row_spec(vd)],
        out_shape=[jax.ShapeDtypeStruct((rows, qk), BF16), jax.ShapeDtypeStruct((rows, qk), BF16),
                   jax.ShapeDtypeStruct((rows, vd), BF16), jax.ShapeDtypeStruct((rows, vd), BF16)],
        compiler_params=_params(1),
        name="ret_proj",
    )(x, g, w, cos, sin)


def _retention_kernel(cd_ref, q_ref, k_ref, v_ref, g_ref, dm_ref, qd_ref, kd_ref, *rest, has_state0):
    if has_state0:
        s0_ref, o_ref, s_ref = rest
    else:
        o_ref, s_ref = rest

    @pl.when(pl.program_id(1) == 0)
    def _init_state():
        s_ref[...] = s0_ref[...] if has_state0 else jnp.zeros_like(s_ref)

    nt = (((1,), (1,)), ((), ()))
    tn = (((0,), (0,)), ((), ()))
    units = [(bi, hh) for bi in range(q_ref.shape[0]) for hh in range(RET_HEADS)]
    ks = lambda hh: slice(hh * RET_KEY, (hh + 1) * RET_KEY)
    vs = lambda hh: slice(hh * RET_VAL, (hh + 1) * RET_VAL)
    att = [lax.dot_general(q_ref[bi, :, ks(hh)].astype(BF16), k_ref[bi, :, ks(hh)].astype(BF16), nt,
                           preferred_element_type=F32) * dm_ref[hh] for bi, hh in units]
    cross = [jnp.dot((q_ref[bi, :, ks(hh)] * qd_ref[hh]).astype(BF16), s_ref[bi, hh].astype(BF16),
                     preferred_element_type=F32) for bi, hh in units]
    for (bi, hh), a, x in zip(units, att, cross):
        o = jnp.dot(a.astype(BF16), v_ref[bi, :, vs(hh)], preferred_element_type=F32) + x
        gate = g_ref[bi, :, vs(hh)].astype(F32)
        o_ref[bi, :, vs(hh)] = (_rms_rows(o) * (gate * jax.nn.sigmoid(gate))).astype(BF16)
    for bi, hh in units:
        kd = (k_ref[bi, :, ks(hh)] * kd_ref[hh]).astype(BF16)
        s_ref[bi, hh] = cd_ref[hh] * s_ref[bi, hh] + lax.dot_general(kd, v_ref[bi, :, vs(hh)], tn,
                                                                     preferred_element_type=F32)


def _retention(q, k, v, gate, consts, state0, *, seqs_per_step):
    dm, qd, kd, cd = consts
    nb, t, _ = q.shape
    c = dm.shape[1]
    bb = seqs_per_step
    qk = RET_HEADS * RET_KEY
    vd = RET_HEADS * RET_VAL
    tok = lambda width: pl.BlockSpec((bb, c, width), lambda b, ci: (b, ci, 0))
    st_spec = pl.BlockSpec((bb, RET_HEADS, RET_KEY, RET_VAL), lambda b, ci: (b, 0, 0, 0))
    in_specs = [pl.BlockSpec(memory_space=pltpu.SMEM), tok(qk), tok(qk), tok(vd), tok(vd),
                _const_spec(dm.shape), _const_spec(qd.shape), _const_spec(kd.shape)]
    args = [cd, q, k, v, gate, dm, qd, kd]
    if state0 is not None:
        in_specs.append(st_spec)
        args.append(state0)
    return pl.pallas_call(
        functools.partial(_retention_kernel, has_state0=state0 is not None),
        grid=(nb // bb, t // c),
        in_specs=in_specs,
        out_specs=[tok(vd), st_spec],
        out_shape=[jax.ShapeDtypeStruct((nb, t, vd), BF16),
                   jax.ShapeDtypeStruct((nb, RET_HEADS, RET_KEY, RET_VAL), F32)],
        compiler_params=_params(2),
        name="retention_s" if state0 is not None else "retention_p",
    )(*args)


def _retention_consts(c, c_pad):
    log_g = jnp.log1p(-jnp.exp2(-5.0 - jnp.arange(RET_HEADS, dtype=F32)))
    i = jnp.arange(c, dtype=F32)
    diff = i[:, None] - i[None, :]
    dm = jnp.where(diff >= 0, jnp.exp(log_g[:, None, None] * jnp.maximum(diff, 0.0)), 0.0)
    qd = jnp.exp(log_g[:, None] * (i[None, :] + 1.0))
    kd = jnp.exp(log_g[:, None] * (c - 1.0 - i[None, :]))
    cd = jnp.exp(log_g * c)
    pad = c_pad - c
    dm = jnp.pad(dm, ((0, 0), (0, pad), (0, pad)))
    qd = jnp.broadcast_to(jnp.pad(qd, ((0, 0), (0, pad)))[:, :, None], (RET_HEADS, c_pad, RET_KEY))
    kd = jnp.broadcast_to(jnp.pad(kd, ((0, 0), (0, pad)))[:, :, None], (RET_HEADS, c_pad, RET_KEY))
    return dm, qd, kd, cd


def _rotation_tables(pos):
    half = RET_KEY // 2
    theta = 1.0 / (10000.0 ** jnp.linspace(0.0, 1.0, half, dtype=F32))
    ang = pos.astype(F32)[:, None] * theta[None, :]
    return jnp.cos(ang), jnp.sin(ang)


def kernel(x_prompt, x_sample, cache_k, cache_v, cache_kidx, state_ret, page_table, rel_bias, ln_mix, ln_mlp,
           att_w_in, att_q_gain, att_k_gain, att_w_out, ret_w_in, ret_w_out, mlp_w_in, mlp_w_out):
    bp, tp, _ = x_prompt.shape
    bs, ts, _ = x_sample.shape
    n_phys = cache_k.shape[1]
    assert ln_mix.shape[0] == 2 and att_w_in.shape[0] == 1 and ret_w_in.shape[0] == 1
    assert tp % RET_CHUNK == 0 and ts <= TQ and cache_k.shape[2] == PAGE
    assert (tp // PAGE) % 4 == 0 and bs % SEL_SAMPLES == 0 and bp % PROMPT_SEQS == 0
    kvw = ATT_KV_HEADS * HEAD_DIM
    past = page_table.shape[1] * PAGE

    w_att = jnp.pad(att_w_in[0], ((0, 0), (0, ATT_IN_PAD - ATT_IN))).astype(BF16)
    w_att_out = att_w_out[0].astype(BF16)
    w_ret = ret_w_in[0].astype(BF16)
    w_ret_out = ret_w_out[0].astype(BF16)
    w1 = mlp_w_in.astype(BF16)
    w2 = mlp_w_out.astype(BF16)
    row = lambda a: a.reshape(1, -1)

    xp = x_prompt.reshape(bp * tp, D_MODEL)
    xs = x_sample.reshape(bs * ts, D_MODEL)

    qT, qiT, wiT, vT3, kb, kib, kp, vp, kip = _attn_proj(
        xp, row(ln_mix[0]), w_att, row(att_q_gain[0]), row(att_k_gain[0]), transposed=True)
    op = _dsa_prompt(rel_bias, qT, qiT, wiT, kb, vT3, kib, batch=bp, seq=tp)
    yp = _post_mix(xp, op, w_att_out, row(ln_mlp[0]), w1[0], w2[0])

    qs, qis, zls, ks, vs, kis = _attn_proj(
        xs, row(ln_mix[0]), w_att, row(att_q_gain[0]), row(att_k_gain[0]), transposed=False)
    pad_t = lambda a: jnp.pad(a, ((0, 0), (0, TQ - ts)) + ((0, 0),) * (a.ndim - 2))
    q8 = pad_t(qs.reshape(bs, ts, ATT_HEADS, HEAD_DIM)).transpose(0, 2, 1, 3)
    q8 = q8.reshape(bs, ATT_HEADS * TQ, HEAD_DIM).astype(BF16)
    qi8 = pad_t(qis.reshape(bs, ts, IDX_HEADS, IDX_DIM)).transpose(0, 2, 1, 3)
    qi8 = qi8.reshape(bs, IDX_HEADS * TQ, IDX_DIM).astype(BF16)
    zl8 = pad_t(zls.reshape(bs, ts, LANES))
    kn8 = pad_t(ks.reshape(bs, ts, kvw))
    vn8 = pad_t(vs.reshape(bs, ts, kvw))
    mb = _dsa_select(page_table, qi8, zl8, jnp.swapaxes(cache_kidx[0], 1, 2), n_new=ts)
    os8 = _dsa_sample(page_table, rel_bias, q8, mb, kn8, vn8,
                      cache_k[0].reshape(n_phys * PAGE * ATT_KV_HEADS, HEAD_DIM),
                      cache_v[0].reshape(n_phys * PAGE * ATT_KV_HEADS, HEAD_DIM))
    os_ = os8.reshape(bs, ATT_HEADS, TQ, HEAD_DIM)[:, :, :ts].transpose(0, 2, 1, 3)
    os_ = os_.reshape(bs * ts, ATT_HEADS * HEAD_DIM)
    ys = _post_mix(xs, os_, w_att_out, row(ln_mlp[0]), w1[0], w2[0])

    cos_p, sin_p = _rotation_tables(jnp.arange(tp, dtype=jnp.int32))
    q, k, v, gate = _ret_proj(yp, row(ln_mix[1]), w_ret, cos_p, sin_p)
    r3 = lambda a: a.reshape(bp, tp, a.shape[-1])
    orp, state_p = _retention(r3(q), r3(k), r3(v), r3(gate), _retention_consts(RET_CHUNK, RET_CHUNK), None,
                            seqs_per_step=1)
    yp = _post_mix(yp, orp.reshape(bp * tp, -1), w_ret_out, row(ln_mlp[1]), w1[1], w2[1])

    cos_s, sin_s = _rotation_tables(past + jnp.arange(ts, dtype=jnp.int32))
    tile_s = lambda a: jnp.tile(a, (bs, 1))
    q, k, v, gate = _ret_proj(ys, row(ln_mix[1]), w_ret, tile_s(cos_s), tile_s(sin_s))
    r3s = lambda a: pad_t(a.reshape(bs, ts, a.shape[-1]))
    ors, state_s = _retention(r3s(q), r3s(k), r3s(v), r3s(gate), _retention_consts(ts, TQ), state_ret[0],
                            seqs_per_step=2)
    ys = _post_mix(ys, ors[:, :ts].reshape(bs * ts, -1), w_ret_out, row(ln_mlp[1]), w1[1], w2[1])

    return (yp.reshape(bp, tp, D_MODEL), ys.reshape(bs, ts, D_MODEL),
            kp.reshape(1, bp, tp, ATT_KV_HEADS, HEAD_DIM), vp.reshape(1, bp, tp, ATT_KV_HEADS, HEAD_DIM),
            kip.reshape(1, bp, tp, IDX_DIM), state_p[None],
            ks.reshape(1, bs, ts, ATT_KV_HEADS, HEAD_DIM), vs.reshape(1, bs, ts, ATT_KV_HEADS, HEAD_DIM),
            kis.reshape(1, bs, ts, IDX_DIM), state_s[None])
```

```python
import functools
import math

import jax
import jax.numpy as jnp
import numpy as np
from jax import lax
from jax.experimental import pallas as pl
from jax.experimental.pallas import tpu as pltpu

F32 = jnp.float32
BF16 = jnp.bfloat16
I32 = jnp.int32

D_MODEL = 1024
PAGE = 128
ATT_HEADS = 8
ATT_KV_HEADS = 2
ATT_GROUP = ATT_HEADS // ATT_KV_HEADS
HEAD_DIM = 128
IDX_HEADS = 4
IDX_DIM = 64
TOPK_MAX = 256
NUM_BUCKETS = 32
MAX_DISTANCE = 128
RET_HEADS = 4
RET_KEY = 256
RET_VAL = 512
RET_CHUNK = 256
D_FF = 4 * D_MODEL
EPS = 1e-6

Q_OFF = 0
K_OFF = ATT_HEADS * HEAD_DIM
V_OFF = K_OFF + ATT_KV_HEADS * HEAD_DIM
QI_OFF = V_OFF + ATT_KV_HEADS * HEAD_DIM
KI_OFF = QI_OFF + IDX_HEADS * IDX_DIM
WI_OFF = KI_OFF + IDX_DIM
ATT_IN = WI_OFF + IDX_HEADS
ATT_IN_PAD = 1920

LANES = 128
SUBLANES = 8
ROW_TILE = 512
VMEM_LIMIT = 56 * 1024 * 1024

LOG2E = math.log2(math.e)
Q_SCALE = HEAD_DIM ** -0.5 * LOG2E
ONES_ROWS = 16
RET_KSCALE = RET_KEY ** -0.5
INT_MIN = -(2 ** 31)
NEG = -1e30


def _bucket_bounds():
    n = np.arange(0, 8192)
    max_exact = NUM_BUCKETS // 2
    val = np.log(np.maximum(n, 1) / max_exact) / math.log(MAX_DISTANCE / max_exact) * (NUM_BUCKETS - max_exact)
    frac = np.abs(val - np.round(val))
    risky = (n > max_exact) & (n < MAX_DISTANCE) & (frac < 1e-4)
    assert not risky.any()
    large = np.minimum(max_exact + np.floor(val + 1e-9).astype(np.int64), NUM_BUCKETS - 1)
    bucket = np.where(n < max_exact, n, large)
    assert (np.diff(bucket) >= 0).all() and (bucket[MAX_DISTANCE:] == NUM_BUCKETS - 1).all()
    return [int(np.argmax(bucket >= b)) for b in range(NUM_BUCKETS)]


BUCKET_BOUNDS = _bucket_bounds()


def _const_spec(shape):
    zeros = (0,) * len(shape)
    return pl.BlockSpec(shape, lambda *_: zeros, pipeline_mode=pl.Buffered(1))


def _params(n_axes):
    return pltpu.CompilerParams(dimension_semantics=("arbitrary",) * n_axes,
                                vmem_limit_bytes=VMEM_LIMIT)


def _rms_rows(a):
    return a * lax.rsqrt(jnp.mean(a * a, axis=-1, keepdims=True) + EPS)


def _canonical_zero(score):
    return jnp.where(score == 0.0, 0.0, score)


def _sortable_key(score):
    bits = pltpu.bitcast(score, I32)
    return bits ^ ((bits >> 31) & jnp.int32(0x7FFFFFFF))


def _key_to_float(key):
    return pltpu.bitcast(key ^ ((key >> 31) & jnp.int32(0x7FFFFFFF)), F32)


KEY_NEG_INF = -2139095041
KEY_POS_INF = 2139095040
KEY_SUBNORMAL_LO = -8388608
KEY_SUBNORMAL_HI = 8388607


def _bracket_kth_largest(count_ge, kprime, guess):
    def canon(x):
        return jnp.where(jnp.logical_and(x >= KEY_SUBNORMAL_LO, x <= KEY_SUBNORMAL_HI), 0, x)

    def succ(x):
        xc = canon(x)
        return jnp.where(xc == 0, KEY_SUBNORMAL_HI + 1, canon(xc + 1))

    g0 = jnp.clip(guess, KEY_NEG_INF, KEY_POS_INF - 1)
    g1 = succ(g0)
    lo = jnp.full(guess.shape, KEY_NEG_INF, I32)
    hi = jnp.full(guess.shape, KEY_POS_INF, I32)

    def narrow(lo, hi, x, cnt):
        ok = cnt >= kprime
        return jnp.where(ok, jnp.maximum(lo, x), lo), jnp.where(ok, hi, jnp.minimum(hi, x))

    c0, c1 = count_ge([_key_to_float(g0), _key_to_float(g1)])
    lo, hi = narrow(lo, hi, g0, c0)
    lo, hi = narrow(lo, hi, g1, c1)

    def open_bracket(c):
        return jnp.max(jnp.where(canon(c[1]) > succ(c[0]), 1.0, 0.0)) > 0.5

    def bisect(c):
        lo, hi = c
        mid = jnp.where(canon(hi) > succ(lo), (lo >> 1) + (hi >> 1) + (lo & hi & 1), lo)
        return narrow(lo, hi, mid, count_ge([_key_to_float(mid)])[0])

    lo, _ = lax.while_loop(open_bracket, bisect, (lo, hi))
    return canon(lo)


def _bit_transpose32(words):
    a = list(words)
    j, m = 16, 0x0000FFFF
    while j:
        k = 0
        while k < 32:
            t = (a[k] ^ lax.shift_right_logical(a[k + j], jnp.int32(j))) & jnp.int32(m)
            a[k] = a[k] ^ t
            a[k + j] = a[k + j] ^ lax.shift_left(t, jnp.int32(j))
            k = (k + j + 1) & ~j
        j >>= 1
        m ^= (m << j) & 0xFFFFFFFF
    return a


def _bias_from_distance(relb_ref, head, dist):
    val = jnp.full(dist.shape, relb_ref[0, head], F32)
    for b in range(1, NUM_BUCKETS):
        val = jnp.where(dist >= BUCKET_BOUNDS[b], relb_ref[b, head], val)
    return val * LOG2E


def _attn_proj_kernel(x_ref, g_ref, w_ref, qg_ref, kg_ref, *outs, transposed):
    x = x_ref[...]
    h = _rms_rows(x) * g_ref[...]
    z = jnp.dot(h.astype(BF16), w_ref[...], preferred_element_type=F32)
    qg = qg_ref[...]
    kg = kg_ref[...]
    k = jnp.concatenate(
        [_rms_rows(z[:, K_OFF + g * HEAD_DIM:K_OFF + (g + 1) * HEAD_DIM]) * kg for g in range(ATT_KV_HEADS)],
        axis=1)
    v = z[:, V_OFF:QI_OFF]
    qi = z[:, QI_OFF:KI_OFF]
    zl = z[:, KI_OFF:ATT_IN_PAD]
    ki = zl[:, :IDX_DIM]
    if transposed:
        qT_ref, qiT_ref, wiT_ref, vT_ref, kb_ref, kib_ref, k_ref, v_ref, ki_ref = outs
        for hh in range(ATT_HEADS):
            qh = _rms_rows(z[:, hh * HEAD_DIM:(hh + 1) * HEAD_DIM]) * qg * Q_SCALE
            qT_ref[hh * HEAD_DIM:(hh + 1) * HEAD_DIM, :] = qh.T.astype(BF16)
        for c in range(IDX_HEADS * IDX_DIM // LANES):
            qiT_ref[c * LANES:(c + 1) * LANES, :] = qi[:, c * LANES:(c + 1) * LANES].T.astype(BF16)
        wiT_ref[...] = zl.T[IDX_DIM:IDX_DIM + 8, :]
        for g in range(ATT_KV_HEADS):
            vT = v[:, g * HEAD_DIM:(g + 1) * HEAD_DIM].T
            for c in range(x.shape[0] // PAGE):
                vT_ref[c, g * HEAD_DIM:(g + 1) * HEAD_DIM, :] = vT[:, c * PAGE:(c + 1) * PAGE].astype(BF16)
        kb_ref[...] = k.astype(BF16)
        kib_ref[...] = ki.astype(BF16)
    else:
        q_ref, qi_ref, zl_ref, k_ref, v_ref, ki_ref = outs
        for hh in range(ATT_HEADS):
            q_ref[:, hh * HEAD_DIM:(hh + 1) * HEAD_DIM] = _rms_rows(z[:, hh * HEAD_DIM:(hh + 1) * HEAD_DIM]) * qg * Q_SCALE
        qi_ref[...] = qi
        zl_ref[...] = zl
    for g in range(ATT_KV_HEADS):
        rows_g = pl.ds(g, x.shape[0], stride=ATT_KV_HEADS)
        k_ref[rows_g, :] = k[:, g * HEAD_DIM:(g + 1) * HEAD_DIM]
        v_ref[rows_g, :] = v[:, g * HEAD_DIM:(g + 1) * HEAD_DIM]
    ki_ref[...] = ki


def _attn_proj(x, g, w, qg, kg, *, transposed):
    rows = x.shape[0]
    tm = ROW_TILE
    nt = rows // tm
    kvw = ATT_KV_HEADS * HEAD_DIM
    row_spec = lambda width: pl.BlockSpec((tm, width), lambda i: (i, 0))
    col_spec = lambda height: pl.BlockSpec((height, tm), lambda i: (0, i))
    kv_rows = ATT_KV_HEADS * rows
    kv_spec = pl.BlockSpec((ATT_KV_HEADS * tm, HEAD_DIM), lambda i: (i, 0))
    leaves_shape = [jax.ShapeDtypeStruct((kv_rows, HEAD_DIM), F32), jax.ShapeDtypeStruct((kv_rows, HEAD_DIM), F32),
                    jax.ShapeDtypeStruct((rows, IDX_DIM), F32)]
    leaves_spec = [kv_spec, kv_spec, row_spec(IDX_DIM)]
    if transposed:
        out_shape = [jax.ShapeDtypeStruct((ATT_HEADS * HEAD_DIM, rows), BF16),
                     jax.ShapeDtypeStruct((IDX_HEADS * IDX_DIM, rows), BF16),
                     jax.ShapeDtypeStruct((8, rows), F32),
                     jax.ShapeDtypeStruct((rows // PAGE, kvw, PAGE), BF16),
                     jax.ShapeDtypeStruct((rows, kvw), BF16),
                     jax.ShapeDtypeStruct((rows, IDX_DIM), BF16)] + leaves_shape
        out_specs = [col_spec(ATT_HEADS * HEAD_DIM), col_spec(IDX_HEADS * IDX_DIM), col_spec(8),
                     pl.BlockSpec((tm // PAGE, kvw, PAGE), lambda i: (i, 0, 0)),
                     row_spec(kvw), row_spec(IDX_DIM)] + leaves_spec
    else:
        out_shape = [jax.ShapeDtypeStruct((rows, ATT_HEADS * HEAD_DIM), F32),
                     jax.ShapeDtypeStruct((rows, IDX_HEADS * IDX_DIM), F32),
                     jax.ShapeDtypeStruct((rows, LANES), F32)] + leaves_shape
        out_specs = [row_spec(ATT_HEADS * HEAD_DIM), row_spec(IDX_HEADS * IDX_DIM), row_spec(LANES)] + leaves_spec
    return pl.pallas_call(
        functools.partial(_attn_proj_kernel, transposed=transposed),
        grid=(nt,),
        in_specs=[row_spec(D_MODEL), _const_spec((1, D_MODEL)), _const_spec((D_MODEL, ATT_IN_PAD)),
                  _const_spec((1, HEAD_DIM)), _const_spec((1, HEAD_DIM))],
        out_specs=out_specs,
        out_shape=out_shape,
        compiler_params=_params(1),
        name="attn_proj_t" if transposed else "attn_proj_r",
    )(x, g, w, qg, kg)


PROMPT_SEQS = 2


def _dsa_prompt_kernel(relb_ref, *refs, topk):
    seqs = range(PROMPT_SEQS)
    n_in = 6
    qT_refs, qiT_refs, wiT_refs, k_refs, vT_refs, ki_refs = zip(*[refs[n_in * u:n_in * (u + 1)] for u in seqs])
    o_ref, keys_sc, scores_sc, planes_sc, live_sc, mb_sc, bias_sc, logit_sc, acc_sc = refs[n_in * PROMPT_SEQS:]
    b = pl.program_id(0)
    i = pl.program_id(1)
    srow = lax.broadcasted_iota(I32, (PAGE, PAGE), 0)
    tcol = lax.broadcasted_iota(I32, (PAGE, PAGE), 1)

    @pl.when((b == 0) & (i == 0))
    def _init_bias():
        planes_sc[...] = jnp.zeros_like(planes_sc)
        live_sc[...] = jnp.zeros_like(live_sc)
        for hh in range(ATT_HEADS):
            for rel in range(2):
                dist = jnp.maximum(rel * PAGE + tcol - srow, 0)
                bias_sc[hh, rel] = _bias_from_distance(relb_ref, hh, dist)
            bias_sc[hh, 2] = jnp.full((PAGE, PAGE), relb_ref[NUM_BUCKETS - 1, hh] * LOG2E, F32)

    wiT = [r[...] for r in wiT_refs]
    qi_cat = [jnp.concatenate([r[hh * IDX_DIM:(hh + 1) * IDX_DIM, :] for hh in range(IDX_HEADS)], axis=1)
              for r in qiT_refs]

    npair = (i + 2) // 2

    def score_block(u, j):
        off = pl.multiple_of(j * PAGE, PAGE)
        kij = ki_refs[u][pl.ds(off, PAGE), :]
        idx = jnp.dot(kij, qi_cat[u], preferred_element_type=F32)
        sc = jnp.zeros((PAGE, PAGE), F32)
        for hh in range(IDX_HEADS):
            sc = sc + wiT[u][hh:hh + 1, :] * jnp.maximum(idx[:, hh * PAGE:(hh + 1) * PAGE], 0.0)
        sc = _canonical_zero(sc)
        causal = jnp.logical_or(j < i, jnp.logical_and(j == i, srow <= tcol))
        scores_sc[u, j] = jnp.where(causal, sc, -jnp.inf)
        keys_sc[u, j] = jnp.where(causal, _sortable_key(sc), INT_MIN)

    nquad = (i + 4) // 4

    def score_body(jj, carry):
        for blk in range(4):
            for u in seqs:
                score_block(u, 4 * jj + blk)
        return carry

    lax.fori_loop(0, nquad, score_body, 0)

    lane = lax.broadcasted_iota(I32, (1, PAGE), 1)
    kprime = jnp.minimum(topk, i * PAGE + lane + 1).astype(F32)

    def plane_body(jj, carry):
        for u in seqs:
            rows = [keys_sc[u, 2 * jj + blk, k * SUBLANES:(k + 1) * SUBLANES, :] ^ INT_MIN
                    for blk in range(2) for k in range(PAGE // SUBLANES)]
            planes = _bit_transpose32(rows)
            live = planes[0]
            for p in range(32):
                planes_sc[u, jj, p] = planes[p]
                live = live | planes[p]
            live_sc[u, jj] = live
        return carry

    lax.fori_loop(0, npair, plane_body, 0)

    n_pairs_max = keys_sc.shape[1] // 2

    def bit_step(u, bi, state):
        thr_u, alive, above = state
        cnt = jnp.zeros((SUBLANES, PAGE), I32)
        with_bit, at_least = [], []
        for p in range(n_pairs_max):
            t = alive[p] & planes_sc[u, p, bi]
            ge = above[p] | t
            cnt = cnt + lax.population_count(ge)
            with_bit.append(t)
            at_least.append(ge)
        take = jnp.sum(cnt.astype(F32), axis=0, keepdims=True) >= kprime
        alive = tuple(jnp.where(take, t, a ^ t) for t, a in zip(with_bit, alive))
        above = tuple(jnp.where(take, g, ge) for g, ge in zip(above, at_least))
        thr_u = thr_u | jnp.where(take, lax.shift_left(jnp.int32(1), 31 - bi), 0)
        return thr_u, alive, above

    zero_words = jnp.zeros((SUBLANES, PAGE), I32)
    states = lax.fori_loop(
        0, 32, lambda bi, states: tuple(bit_step(u, bi, states[u]) for u in seqs),
        tuple((jnp.zeros((1, PAGE), I32),
               tuple(jnp.where(p < npair, live_sc[u, p], zero_words) for p in range(n_pairs_max)),
               (zero_words,) * n_pairs_max) for u in seqs))
    def count_scores(u, preds):
        def body(jj, cnts):
            for blk in range(2):
                sc = scores_sc[u, 2 * jj + blk]
                cnts = tuple(c + jnp.where(p(sc), 1.0, 0.0) for c, p in zip(cnts, preds))
            return cnts
        cnts = lax.fori_loop(0, npair, body, tuple(jnp.zeros((PAGE, PAGE), F32) for _ in preds))
        return [jnp.sum(c, axis=0, keepdims=True) for c in cnts]

    def count_ge(u):
        return lambda thresholds: count_scores(u, [lambda sc, x=x: sc >= x for x in thresholds])

    thr = [_key_to_float(_bracket_kth_largest(count_ge(u), kprime, states[u][0] ^ INT_MIN)) for u in seqs]
    need = [kprime - count_scores(u, [lambda sc, u=u: sc > thr[u]])[0] for u in seqs]

    tri = jnp.where(srow > tcol, 1.0, 0.0).astype(BF16)

    def mask_block(u, j, carry):
        sc = scores_sc[u, j]
        eqf = jnp.where(sc == thr[u], 1.0, 0.0)
        rank = jnp.dot(tri, eqf.astype(BF16), preferred_element_type=F32) + carry
        tie = jnp.where(jnp.logical_and(sc == thr[u], rank < need[u]), 0.0, NEG)
        mb_sc[u, j] = jnp.where(sc > thr[u], 0.0, tie)
        return carry + jnp.sum(eqf, axis=0, keepdims=True)

    def mask_body(jj, carry):
        for blk in range(4):
            carry = tuple(mask_block(u, 4 * jj + blk, carry[u]) for u in seqs)
        return carry

    lax.fori_loop(0, nquad, mask_body, tuple(jnp.zeros((1, PAGE), F32) for _ in seqs))

    groups = range(ATT_KV_HEADS)
    units = [(u, g) for u in seqs for g in groups]
    width = ATT_GROUP * PAGE
    qcats = [jnp.concatenate(
        [qT_refs[u][(ATT_GROUP * g + r) * HEAD_DIM:(ATT_GROUP * g + r + 1) * HEAD_DIM, :]
         for r in range(ATT_GROUP)], axis=1) for u, g in units]

    def logit_block(j, ms):
        off = pl.multiple_of(j * PAGE, PAGE)
        relc = jnp.clip(i - j, 0, 2)
        out = []
        for n, (u, g) in enumerate(units):
            mb = mb_sc[u, j]
            kj = k_refs[u][pl.ds(off, PAGE), g * HEAD_DIM:(g + 1) * HEAD_DIM]
            logits = jnp.dot(kj, qcats[n], preferred_element_type=F32)
            logits = jnp.concatenate(
                [logits[:, r * PAGE:(r + 1) * PAGE] + (mb + bias_sc[ATT_GROUP * g + r, relc])
                 for r in range(ATT_GROUP)], axis=1)
            logit_sc[u, g, j] = logits
            out.append(jnp.maximum(ms[n], jnp.max(logits, axis=0, keepdims=True)))
        return tuple(out)

    def over_block_pairs(pair_step, carry):
        nfull = npair // 2
        carry = lax.fori_loop(0, nfull, lambda t, c: pair_step(4 * t + 2, pair_step(4 * t, c)), carry)
        return lax.cond(npair % 2 == 1, lambda c: pair_step(4 * nfull, c), lambda c: c, carry)

    ms = over_block_pairs(lambda j0, c: logit_block(j0 + 1, logit_block(j0, c)),
                          tuple(jnp.full((1, width), NEG, F32) for _ in units))

    ones_rows = jnp.ones((ONES_ROWS, PAGE), BF16)

    def value_pair(j0, carry):
        for n, (u, g) in enumerate(units):
            p = jnp.concatenate(
                [jnp.exp2(logit_sc[u, g, j0 + blk] - ms[n]).astype(BF16) for blk in range(2)], axis=0)
            vT2 = jnp.concatenate(
                [jnp.concatenate([vT_refs[u][j0 + blk, g * HEAD_DIM:(g + 1) * HEAD_DIM, :], ones_rows], axis=0)
                 for blk in range(2)], axis=1)
            acc_sc[u, g] = acc_sc[u, g] + jnp.dot(vT2, p, preferred_element_type=F32)
        return carry

    acc_sc[...] = jnp.zeros_like(acc_sc)
    over_block_pairs(value_pair, 0)
    for u, g in units:
        acc = acc_sc[u, g]
        outT = acc[:HEAD_DIM] * (1.0 / acc[HEAD_DIM:HEAD_DIM + 1])
        for r in range(ATT_GROUP):
            hh = ATT_GROUP * g + r
            o_ref[u, :, hh * HEAD_DIM:(hh + 1) * HEAD_DIM] = outT[:, r * PAGE:(r + 1) * PAGE].T.astype(BF16)


def _dsa_prompt(rel_bias, qT, qiT, wiT, kb, vT3, kib, *, batch, seq):
    nb = seq // PAGE
    rows = batch * seq
    kvw = ATT_KV_HEADS * HEAD_DIM
    ns = PROMPT_SEQS

    def seq_specs(u):
        qcol = lambda height: pl.BlockSpec((height, PAGE), lambda b, i: (0, (ns * b + u) * nb + i))
        return [qcol(ATT_HEADS * HEAD_DIM), qcol(IDX_HEADS * IDX_DIM), qcol(8),
                pl.BlockSpec((seq, kvw), lambda b, i: (ns * b + u, 0)),
                pl.BlockSpec((nb, kvw, PAGE), lambda b, i: (ns * b + u, 0, 0)),
                pl.BlockSpec((seq, IDX_DIM), lambda b, i: (ns * b + u, 0))]

    out = pl.pallas_call(
        functools.partial(_dsa_prompt_kernel, topk=min(TOPK_MAX, seq // 4)),
        grid=(batch // ns, nb),
        in_specs=[pl.BlockSpec(memory_space=pltpu.SMEM)] + [s for u in range(ns) for s in seq_specs(u)],
        out_specs=pl.BlockSpec((ns, PAGE, ATT_HEADS * HEAD_DIM), lambda b, i: (b, i, 0)),
        out_shape=jax.ShapeDtypeStruct((batch, seq, ATT_HEADS * HEAD_DIM), BF16),
        scratch_shapes=[pltpu.VMEM((ns, nb, PAGE, PAGE), I32), pltpu.VMEM((ns, nb, PAGE, PAGE), F32),
                        pltpu.VMEM((ns, nb // 2, 32, SUBLANES, PAGE), I32),
                        pltpu.VMEM((ns, nb // 2, SUBLANES, PAGE), I32),
                        pltpu.VMEM((ns, nb, PAGE, PAGE), F32),
                        pltpu.VMEM((ATT_HEADS, 3, PAGE, PAGE), F32),
                        pltpu.VMEM((ns, ATT_KV_HEADS, nb, PAGE, ATT_GROUP * PAGE), F32),
                        pltpu.VMEM((ns, ATT_KV_HEADS, HEAD_DIM + ONES_ROWS, ATT_GROUP * PAGE), F32)],
        compiler_params=_params(2),
        name="dsa_prompt",
    )(rel_bias, *([qT, qiT, wiT, kb, vT3, kib] * ns))
    return out.reshape(rows, ATT_HEADS * HEAD_DIM)


TQ = 8
SEL_SAMPLES = 16


def _dsa_select_kernel(pt_ref, qi_ref, zl_ref, cki_hbm, mb_ref, sc_sc, kibuf, sems, *, n_pages, n_new):
    ns = SEL_SAMPLES
    n_fetch = ns * n_pages
    b = pl.program_id(0)
    slot = b % 2

    def page_copies(step, slot):
        return [pltpu.make_async_copy(cki_hbm.at[pt_ref[step * n_fetch + f]], kibuf.at[slot, f], sems.at[slot])
                for f in range(n_fetch)]

    @pl.when(b == 0)
    def _start_first():
        for c in page_copies(0, 0):
            c.start()

    @pl.when(b + 1 < pl.num_programs(0))
    def _start_next():
        for c in page_copies(b + 1, 1 - slot):
            c.start()

    for c in page_copies(b, slot):
        c.wait()
    ki_pages = [kibuf.at[slot, f] for f in range(n_fetch)]
    n_blk = n_pages + 1
    rows = ns * TQ
    trow = lax.broadcasted_iota(I32, (TQ, PAGE), 0)
    scol = lax.broadcasted_iota(I32, (TQ, PAGE), 1)
    pad_rows = lambda a: jnp.concatenate([a, jnp.zeros((PAGE - TQ, a.shape[1]), a.dtype)], axis=0)
    nt = (((1,), (1,)), ((), ()))

    for s in range(ns):
        zl = zl_ref[s]
        qi = qi_ref[s]
        for p in range(n_blk):
            if p == n_pages:
                ki_new = pad_rows(zl[:, :IDX_DIM]).astype(BF16)
                idx = lax.dot_general(qi, ki_new, nt, preferred_element_type=F32)
            else:
                idx = jnp.dot(qi, ki_pages[s * n_pages + p][...].astype(BF16), preferred_element_type=F32)
            sc = jnp.zeros((TQ, PAGE), F32)
            for hh in range(IDX_HEADS):
                sc = sc + zl[:, IDX_DIM + hh:IDX_DIM + hh + 1] * jnp.maximum(idx[hh * TQ:(hh + 1) * TQ], 0.0)
            sc = _canonical_zero(sc)
            if p == n_pages:
                sc = jnp.where(jnp.logical_and(scol <= trow, scol < n_new), sc, -jnp.inf)
            sc_sc[s * TQ:(s + 1) * TQ, p * PAGE:(p + 1) * PAGE] = sc

    kprime = float(min(TOPK_MAX, (n_pages * PAGE + n_new) // 4))

    def count(pred_of_scores):
        return jnp.sum(jnp.where(pred_of_scores(sc_sc[...]), 1.0, 0.0), axis=1, keepdims=True)

    cand0 = jnp.zeros((rows, 1), I32)
    thr0 = jnp.where(count(lambda sc: sc >= _key_to_float(cand0)) >= kprime, cand0, INT_MIN)

    def bit_body(bi, thr_key):
        cand = thr_key | lax.shift_left(jnp.int32(1), 30 - bi)
        return jnp.where(count(lambda sc: sc >= _key_to_float(cand)) >= kprime, cand, thr_key)

    thr = _key_to_float(lax.fori_loop(0, 31, bit_body, thr0))
    need = kprime - count(lambda sc: sc > thr)

    s_r = lax.broadcasted_iota(I32, (PAGE, PAGE), 0)
    s_c = lax.broadcasted_iota(I32, (PAGE, PAGE), 1)
    triu = jnp.where(s_r < s_c, 1.0, 0.0).astype(BF16)
    carry = jnp.zeros((rows, 1), F32)
    for p in range(n_blk):
        sc = sc_sc[:, p * PAGE:(p + 1) * PAGE]
        eqf = jnp.where(sc == thr, 1.0, 0.0)
        rank = jnp.dot(eqf.astype(BF16), triu, preferred_element_type=F32) + carry
        tie = jnp.where(jnp.logical_and(sc == thr, rank < need), 0.0, NEG)
        mb_ref[:, p * PAGE:(p + 1) * PAGE] = jnp.where(sc > thr, 0.0, tie)
        carry = carry + jnp.sum(eqf, axis=1, keepdims=True)


def _dsa_select(page_table, qi8, zl8, cache_ki, *, n_new):
    nsamp, n_pages = page_table.shape
    ns = SEL_SAMPLES
    width = (n_pages + 1) * PAGE

    in_specs = [pl.BlockSpec((ns, IDX_HEADS * TQ, IDX_DIM), lambda b, pt: (b, 0, 0)),
                pl.BlockSpec((ns, TQ, LANES), lambda b, pt: (b, 0, 0)),
                pl.BlockSpec(memory_space=pl.ANY)]
    grid_spec = pltpu.PrefetchScalarGridSpec(
        num_scalar_prefetch=1, grid=(nsamp // ns,), in_specs=in_specs,
        out_specs=pl.BlockSpec((ns * TQ, width), lambda b, pt: (b, 0)),
        scratch_shapes=[pltpu.VMEM((ns * TQ, width), F32),
                        pltpu.VMEM((2, ns * n_pages, IDX_DIM, PAGE), F32), pltpu.SemaphoreType.DMA((2,))])
    return pl.pallas_call(
        functools.partial(_dsa_select_kernel, n_pages=n_pages, n_new=n_new),
        grid_spec=grid_spec,
        out_shape=jax.ShapeDtypeStruct((nsamp * TQ, width), F32),
        compiler_params=_params(1),
        name="dsa_select",
    )(page_table.reshape(-1), qi8, zl8, cache_ki)


def _dsa_sample_kernel(pt_ref, relb_ref, q_ref, mb_ref, kn_ref, vn_ref, ck_hbm, cv_hbm, o_ref,
                       bias_sc, kall_sc, vall_sc, kbuf, vbuf, sems, *, n_pages):
    past = n_pages * PAGE
    rows_g = ATT_GROUP * TQ
    page_rows = PAGE * ATT_KV_HEADS
    b = pl.program_id(0)
    n_slots = kbuf.shape[0]
    ahead = n_slots - 1
    slot = b % n_slots

    def page_copies(sample, slot):
        copies = []
        for p in range(n_pages):
            src = pl.ds(pl.multiple_of(pt_ref[sample * n_pages + p] * page_rows, page_rows), page_rows)
            dst = pl.ds(p * page_rows, page_rows)
            copies.append(pltpu.make_async_copy(ck_hbm.at[src, :], kbuf.at[slot, dst, :], sems.at[slot, 0]))
            copies.append(pltpu.make_async_copy(cv_hbm.at[src, :], vbuf.at[slot, dst, :], sems.at[slot, 1]))
        return copies

    @pl.when(b == 0)
    def _start_first():
        for s in range(ahead):
            for c in page_copies(s, s):
                c.start()

    @pl.when(b + ahead < pl.num_programs(0))
    def _start_next():
        for c in page_copies(b + ahead, (b + ahead) % n_slots):
            c.start()

    for c in page_copies(b, slot):
        c.wait()

    trow = lax.broadcasted_iota(I32, (TQ, PAGE), 0)
    scol = lax.broadcasted_iota(I32, (TQ, PAGE), 1)

    @pl.when(pl.program_id(0) == 0)
    def _init_bias():
        for g in range(ATT_KV_HEADS):
            for r in range(ATT_GROUP):
                hh = ATT_GROUP * g + r
                rs = slice(r * TQ, (r + 1) * TQ)
                far = jnp.full((TQ, PAGE), relb_ref[NUM_BUCKETS - 1, hh] * LOG2E, F32)
                for p in range(n_pages - 1):
                    bias_sc[g, rs, p * PAGE:(p + 1) * PAGE] = far
                bias_sc[g, rs, past - PAGE:past] = _bias_from_distance(
                    relb_ref, hh, jnp.maximum(PAGE + trow - scol, 0))
                bias_sc[g, rs, past:past + PAGE] = _bias_from_distance(relb_ref, hh, jnp.maximum(trow - scol, 0))

    nt = (((1,), (1,)), ((), ()))
    mb = jnp.concatenate([mb_ref[...]] * ATT_GROUP, axis=0)
    pad_rows = lambda a: jnp.concatenate([a, jnp.zeros((PAGE - TQ, a.shape[1]), a.dtype)], axis=0)
    for g in range(ATT_KV_HEADS):
        gs = slice(g * HEAD_DIM, (g + 1) * HEAD_DIM)
        for p in range(n_pages):
            ps = slice(p * PAGE, (p + 1) * PAGE)
            head_rows = pl.ds(p * page_rows + g, PAGE, stride=ATT_KV_HEADS)
            kall_sc[g, ps, :] = kbuf[slot, head_rows, :].astype(BF16)
            vall_sc[g, ps, :] = vbuf[slot, head_rows, :].astype(BF16)
        kall_sc[g, past:past + PAGE, :] = pad_rows(kn_ref[:, gs]).astype(BF16)
        vall_sc[g, past:past + PAGE, :] = pad_rows(vn_ref[:, gs]).astype(BF16)
    for g in range(ATT_KV_HEADS):
        qg = q_ref[g * rows_g:(g + 1) * rows_g, :]
        logits = lax.dot_general(qg, kall_sc[g], nt, preferred_element_type=F32)
        logits = logits + (mb + bias_sc[g])
        m = jnp.max(logits, axis=1, keepdims=True)
        pr = jnp.exp2(logits - m)
        l = jnp.sum(pr, axis=1, keepdims=True)
        acc = jnp.dot(pr.astype(BF16), vall_sc[g], preferred_element_type=F32)
        o_ref[g * rows_g:(g + 1) * rows_g, :] = (acc * (1.0 / l)).astype(BF16)


def _dsa_sample(page_table, rel_bias, q8, mb, kn8, vn8, cache_k, cache_v):
    nsamp, n_pages = page_table.shape
    kvw = ATT_KV_HEADS * HEAD_DIM
    per_sample = lambda *tail: pl.BlockSpec((None,) + tail, lambda b, pt: (b,) + (0,) * len(tail))

    n_slots = 3
    page_buf = pltpu.VMEM((n_slots, n_pages * PAGE * ATT_KV_HEADS, HEAD_DIM), F32)
    in_specs = [pl.BlockSpec(memory_space=pltpu.SMEM),
                per_sample(ATT_HEADS * TQ, HEAD_DIM),
                pl.BlockSpec((TQ, mb.shape[1]), lambda b, pt: (b, 0)),
                per_sample(TQ, kvw), per_sample(TQ, kvw),
                pl.BlockSpec(memory_space=pl.ANY), pl.BlockSpec(memory_space=pl.ANY)]
    grid_spec = pltpu.PrefetchScalarGridSpec(
        num_scalar_prefetch=1, grid=(nsamp,), in_specs=in_specs,
        out_specs=per_sample(ATT_HEADS * TQ, HEAD_DIM),
        scratch_shapes=[pltpu.VMEM((ATT_KV_HEADS, ATT_GROUP * TQ, mb.shape[1]), F32),
                        pltpu.VMEM((ATT_KV_HEADS, mb.shape[1], HEAD_DIM), BF16),
                        pltpu.VMEM((ATT_KV_HEADS, mb.shape[1], HEAD_DIM), BF16),
                        page_buf, page_buf, pltpu.SemaphoreType.DMA((n_slots, 2))])
    return pl.pallas_call(
        functools.partial(_dsa_sample_kernel, n_pages=n_pages),
        grid_spec=grid_spec,
        out_shape=jax.ShapeDtypeStruct((nsamp, ATT_HEADS * TQ, HEAD_DIM), BF16),
        compiler_params=_params(1),
        name="dsa_sample",
    )(page_table.reshape(-1), rel_bias, q8, mb, kn8, vn8, cache_k, cache_v)


def _post_mix_kernel(x_ref, o_ref, wo_ref, g_ref, w1_ref, w2_ref, y_ref):
    y = x_ref[...] + jnp.dot(o_ref[...], wo_ref[...], preferred_element_type=F32)
    h = (_rms_rows(y) * g_ref[...]).astype(BF16)
    acc = y
    for c in range(D_FF // D_MODEL):
        a = jnp.dot(h, w1_ref[:, c * D_MODEL:(c + 1) * D_MODEL], preferred_element_type=F32)
        a = jnp.square(jnp.maximum(a, 0.0)).astype(BF16)
        acc = acc + jnp.dot(a, w2_ref[c * D_MODEL:(c + 1) * D_MODEL, :], preferred_element_type=F32)
    y_ref[...] = acc


def _post_mix(x, o, wo, g, w1, w2):
    rows = x.shape[0]
    tm = ROW_TILE
    ko = o.shape[1]
    return pl.pallas_call(
        _post_mix_kernel,
        grid=(rows // tm,),
        in_specs=[pl.BlockSpec((tm, D_MODEL), lambda i: (i, 0)), pl.BlockSpec((tm, ko), lambda i: (i, 0)),
                  _const_spec((ko, D_MODEL)), _const_spec((1, D_MODEL)),
                  _const_spec((D_MODEL, D_FF)), _const_spec((D_FF, D_MODEL))],
        out_specs=pl.BlockSpec((tm, D_MODEL), lambda i: (i, 0)),
        out_shape=jax.ShapeDtypeStruct((rows, D_MODEL), F32),
        compiler_params=_params(1),
        name="post_mix",
    )(x, o, wo, g, w1, w2)


def _ret_proj_kernel(x_ref, g_ref, w_ref, cos_ref, sin_ref, q_ref, k_ref, v_ref, gate_ref):
    hb = (_rms_rows(x_ref[...]) * g_ref[...]).astype(BF16)
    cos = cos_ref[...]
    sin = sin_ref[...]
    half = RET_KEY // 2
    qk = RET_HEADS * RET_KEY
    vd = RET_HEADS * RET_VAL

    def rotated(col0, scale):
        z = jnp.dot(hb, w_ref[:, col0:col0 + RET_KEY], preferred_element_type=F32)
        x1, x2 = z[:, :half], z[:, half:]
        return (x1 * cos - x2 * sin) * scale, (x1 * sin + x2 * cos) * scale

    for hh in range(RET_HEADS):
        a, b = rotated(hh * RET_KEY, 1.0)
        q_ref[:, hh * RET_KEY:hh * RET_KEY + half] = a.astype(BF16)
        q_ref[:, hh * RET_KEY + half:(hh + 1) * RET_KEY] = b.astype(BF16)
        a, b = rotated(qk + hh * RET_KEY, RET_KSCALE)
        k_ref[:, hh * RET_KEY:hh * RET_KEY + half] = a.astype(BF16)
        k_ref[:, hh * RET_KEY + half:(hh + 1) * RET_KEY] = b.astype(BF16)
    for hh in range(RET_HEADS):
        cs = slice(hh * RET_VAL, (hh + 1) * RET_VAL)
        v_ref[:, cs] = jnp.dot(hb, w_ref[:, 2 * qk + hh * RET_VAL:2 * qk + (hh + 1) * RET_VAL],
                               preferred_element_type=F32).astype(BF16)
        gate_ref[:, cs] = jnp.dot(hb, w_ref[:, 2 * qk + vd + hh * RET_VAL:2 * qk + vd + (hh + 1) * RET_VAL],
                                  preferred_element_type=F32).astype(BF16)


def _ret_proj(x, g, w, cos, sin):
    rows = x.shape[0]
    tm = ROW_TILE
    qk = RET_HEADS * RET_KEY
    vd = RET_HEADS * RET_VAL
    n_pos_tiles = cos.shape[0] // tm
    row_spec = lambda width: pl.BlockSpec((tm, width), lambda i: (i, 0))
    pos_spec = pl.BlockSpec((tm, RET_KEY // 2), lambda i: (i % n_pos_tiles, 0))
    return pl.pallas_call(
        _ret_proj_kernel,
        grid=(rows // tm,),
        in_specs=[row_spec(D_MODEL), _const_spec((1, D_MODEL)), _const_spec((D_MODEL, 2 * qk + 2 * vd)),
                  pos_spec, pos_spec],
        out_specs=[row_spec(qk), row_spec(qk), row_spec(vd), row_spec(vd)],
        out_shape=[jax.ShapeDtypeStruct((rows, qk), BF16), jax.ShapeDtypeStruct((rows, qk), BF16),
                   jax.ShapeDtypeStruct((rows, vd), BF16), jax.ShapeDtypeStruct((rows, vd), BF16)],
        compiler_params=_params(1),
        name="ret_proj",
    )(x, g, w, cos, sin)


def _retention_kernel(cd_ref, q_ref, k_ref, v_ref, g_ref, dm_ref, qd_ref, kd_ref, *rest, has_state0):
    if has_state0:
        s0_ref, o_ref, s_ref = rest
    else:
        o_ref, s_ref = rest

    @pl.when(pl.program_id(1) == 0)
    def _init_state():
        s_ref[...] = s0_ref[...] if has_state0 else jnp.zeros_like(s_ref)

    nt = (((1,), (1,)), ((), ()))
    tn = (((0,), (0,)), ((), ()))
    units = [(bi, hh) for bi in range(q_ref.shape[0]) for hh in range(RET_HEADS)]
    ks = lambda hh: slice(hh * RET_KEY, (hh + 1) * RET_KEY)
    vs = lambda hh: slice(hh * RET_VAL, (hh + 1) * RET_VAL)
    att = [lax.dot_general(q_ref[bi, :, ks(hh)].astype(BF16), k_ref[bi, :, ks(hh)].astype(BF16), nt,
                           preferred_element_type=F32) * dm_ref[hh] for bi, hh in units]
    cross = [jnp.dot((q_ref[bi, :, ks(hh)] * qd_ref[hh]).astype(BF16), s_ref[bi, hh].astype(BF16),
                     preferred_element_type=F32) for bi, hh in units]
    for (bi, hh), a, x in zip(units, att, cross):
        o = jnp.dot(a.astype(BF16), v_ref[bi, :, vs(hh)], preferred_element_type=F32) + x
        gate = g_ref[bi, :, vs(hh)].astype(F32)
        o_ref[bi, :, vs(hh)] = (_rms_rows(o) * (gate * jax.nn.sigmoid(gate))).astype(BF16)
    for bi, hh in units:
        kd = (k_ref[bi, :, ks(hh)] * kd_ref[hh]).astype(BF16)
        s_ref[bi, hh] = cd_ref[hh] * s_ref[bi, hh] + lax.dot_general(kd, v_ref[bi, :, vs(hh)], tn,
                                                                     preferred_element_type=F32)


def _retention(q, k, v, gate, consts, state0, *, seqs_per_step):
    dm, qd, kd, cd = consts
    nb, t, _ = q.shape
    c = dm.shape[1]
    bb = seqs_per_step
    qk = RET_HEADS * RET_KEY
    vd = RET_HEADS * RET_VAL
    tok = lambda width: pl.BlockSpec((bb, c, width), lambda b, ci: (b, ci, 0))
    st_spec = pl.BlockSpec((bb, RET_HEADS, RET_KEY, RET_VAL), lambda b, ci: (b, 0, 0, 0))
    in_specs = [pl.BlockSpec(memory_space=pltpu.SMEM), tok(qk), tok(qk), tok(vd), tok(vd),
                _const_spec(dm.shape), _const_spec(qd.shape), _const_spec(kd.shape)]
    args = [cd, q, k, v, gate, dm, qd, kd]
    if state0 is not None:
        in_specs.append(st_spec)
        args.append(state0)
    return pl.pallas_call(
        functools.partial(_retention_kernel, has_state0=state0 is not None),
        grid=(nb // bb, t // c),
        in_specs=in_specs,
        out_specs=[tok(vd), st_spec],
        out_shape=[jax.ShapeDtypeStruct((nb, t, vd), BF16),
                   jax.ShapeDtypeStruct((nb, RET_HEADS, RET_KEY, RET_VAL), F32)],
        compiler_params=_params(2),
        name="retention_s" if state0 is not None else "retention_p",
    )(*args)


def _retention_consts(c, c_pad):
    log_g = jnp.log1p(-jnp.exp2(-5.0 - jnp.arange(RET_HEADS, dtype=F32)))
    i = jnp.arange(c, dtype=F32)
    diff = i[:, None] - i[None, :]
    dm = jnp.where(diff >= 0, jnp.exp(log_g[:, None, None] * jnp.maximum(diff, 0.0)), 0.0)
    qd = jnp.exp(log_g[:, None] * (i[None, :] + 1.0))
    kd = jnp.exp(log_g[:, None] * (c - 1.0 - i[None, :]))
    cd = jnp.exp(log_g * c)
    pad = c_pad - c
    dm = jnp.pad(dm, ((0, 0), (0, pad), (0, pad)))
    qd = jnp.broadcast_to(jnp.pad(qd, ((0, 0), (0, pad)))[:, :, None], (RET_HEADS, c_pad, RET_KEY))
    kd = jnp.broadcast_to(jnp.pad(kd, ((0, 0), (0, pad)))[:, :, None], (RET_HEADS, c_pad, RET_KEY))
    return dm, qd, kd, cd


def _rotation_tables(pos):
    half = RET_KEY // 2
    theta = 1.0 / (10000.0 ** jnp.linspace(0.0, 1.0, half, dtype=F32))
    ang = pos.astype(F32)[:, None] * theta[None, :]
    return jnp.cos(ang), jnp.sin(ang)


def kernel(x_prompt, x_sample, cache_k, cache_v, cache_kidx, state_ret, page_table, rel_bias, ln_mix, ln_mlp,
           att_w_in, att_q_gain, att_k_gain, att_w_out, ret_w_in, ret_w_out, mlp_w_in, mlp_w_out):
    bp, tp, _ = x_prompt.shape
    bs, ts, _ = x_sample.shape
    n_phys = cache_k.shape[1]
    assert ln_mix.shape[0] == 2 and att_w_in.shape[0] == 1 and ret_w_in.shape[0] == 1
    assert tp % RET_CHUNK == 0 and ts <= TQ and cache_k.shape[2] == PAGE
    assert (tp // PAGE) % 4 == 0 and bs % SEL_SAMPLES == 0 and bp % PROMPT_SEQS == 0
    kvw = ATT_KV_HEADS * HEAD_DIM
    past = page_table.shape[1] * PAGE

    w_att = jnp.pad(att_w_in[0], ((0, 0), (0, ATT_IN_PAD - ATT_IN))).astype(BF16)
    w_att_out = att_w_out[0].astype(BF16)
    w_ret = ret_w_in[0].astype(BF16)
    w_ret_out = ret_w_out[0].astype(BF16)
    w1 = mlp_w_in.astype(BF16)
    w2 = mlp_w_out.astype(BF16)
    row = lambda a: a.reshape(1, -1)

    xp = x_prompt.reshape(bp * tp, D_MODEL)
    xs = x_sample.reshape(bs * ts, D_MODEL)

    qT, qiT, wiT, vT3, kb, kib, kp, vp, kip = _attn_proj(
        xp, row(ln_mix[0]), w_att, row(att_q_gain[0]), row(att_k_gain[0]), transposed=True)
    op = _dsa_prompt(rel_bias, qT, qiT, wiT, kb, vT3, kib, batch=bp, seq=tp)
    yp = _post_mix(xp, op, w_att_out, row(ln_mlp[0]), w1[0], w2[0])

    qs, qis, zls, ks, vs, kis = _attn_proj(
        xs, row(ln_mix[0]), w_att, row(att_q_gain[0]), row(att_k_gain[0]), transposed=False)
    pad_t = lambda a: jnp.pad(a, ((0, 0), (0, TQ - ts)) + ((0, 0),) * (a.ndim - 2))
    q8 = pad_t(qs.reshape(bs, ts, ATT_HEADS, HEAD_DIM)).transpose(0, 2, 1, 3)
    q8 = q8.reshape(bs, ATT_HEADS * TQ, HEAD_DIM).astype(BF16)
    qi8 = pad_t(qis.reshape(bs, ts, IDX_HEADS, IDX_DIM)).transpose(0, 2, 1, 3)
    qi8 = qi8.reshape(bs, IDX_HEADS * TQ, IDX_DIM).astype(BF16)
    zl8 = pad_t(zls.reshape(bs, ts, LANES))
    kn8 = pad_t(ks.reshape(bs, ts, kvw))
    vn8 = pad_t(vs.reshape(bs, ts, kvw))
    mb = _dsa_select(page_table, qi8, zl8, jnp.swapaxes(cache_kidx[0], 1, 2), n_new=ts)
    os8 = _dsa_sample(page_table, rel_bias, q8, mb, kn8, vn8,
                      cache_k[0].reshape(n_phys * PAGE * ATT_KV_HEADS, HEAD_DIM),
                      cache_v[0].reshape(n_phys * PAGE * ATT_KV_HEADS, HEAD_DIM))
    os_ = os8.reshape(bs, ATT_HEADS, TQ, HEAD_DIM)[:, :, :ts].transpose(0, 2, 1, 3)
    os_ = os_.reshape(bs * ts, ATT_HEADS * HEAD_DIM)
    ys = _post_mix(xs, os_, w_att_out, row(ln_mlp[0]), w1[0], w2[0])

    cos_p, sin_p = _rotation_tables(jnp.arange(tp, dtype=jnp.int32))
    q, k, v, gate = _ret_proj(yp, row(ln_mix[1]), w_ret, cos_p, sin_p)
    r3 = lambda a: a.reshape(bp, tp, a.shape[-1])
    orp, state_p = _retention(r3(q), r3(k), r3(v), r3(gate), _retention_consts(RET_CHUNK, RET_CHUNK), None,
                            seqs_per_step=1)
    yp = _post_mix(yp, orp.reshape(bp * tp, -1), w_ret_out, row(ln_mlp[1]), w1[1], w2[1])

    cos_s, sin_s = _rotation_tables(past + jnp.arange(ts, dtype=jnp.int32))
    tile_s = lambda a: jnp.tile(a, (bs, 1))
    q, k, v, gate = _ret_proj(ys, row(ln_mix[1]), w_ret, tile_s(cos_s), tile_s(sin_s))
    r3s = lambda a: pad_t(a.reshape(bs, ts, a.shape[-1]))
    ors, state_s = _retention(r3s(q), r3s(k), r3s(v), r3s(gate), _retention_consts(ts, TQ), state_ret[0],
                            seqs_per_step=2)
    ys = _post_mix(ys, ors[:, :ts].reshape(bs * ts, -1), w_ret_out, row(ln_mlp[1]), w1[1], w2[1])

    return (yp.reshape(bp, tp, D_MODEL), ys.reshape(bs, ts, D_MODEL),
            kp.reshape(1, bp, tp, ATT_KV_HEADS, HEAD_DIM), vp.reshape(1, bp, tp, ATT_KV_HEADS, HEAD_DIM),
            kip.reshape(1, bp, tp, IDX_DIM), state_p[None],
            ks.reshape(1, bs, ts, ATT_KV_HEADS, HEAD_DIM), vs.reshape(1, bs, ts, ATT_KV_HEADS, HEAD_DIM),
            kis.reshape(1, bs, ts, IDX_DIM), state_s[None])
```

```python
import functools
import math

import jax
import jax.numpy as jnp
import numpy as np
from jax import lax
from jax.experimental import pallas as pl
from jax.experimental.pallas import tpu as pltpu

F32 = jnp.float32
BF16 = jnp.bfloat16
I32 = jnp.int32

D_MODEL = 1024
PAGE = 128
ATT_HEADS = 8
ATT_KV_HEADS = 2
ATT_GROUP = ATT_HEADS // ATT_KV_HEADS
HEAD_DIM = 128
IDX_HEADS = 4
IDX_DIM = 64
TOPK_MAX = 256
NUM_BUCKETS = 32
MAX_DISTANCE = 128
RET_HEADS = 4
RET_KEY = 256
RET_VAL = 512
RET_CHUNK = 256
D_FF = 4 * D_MODEL
EPS = 1e-6

Q_OFF = 0
K_OFF = ATT_HEADS * HEAD_DIM
V_OFF = K_OFF + ATT_KV_HEADS * HEAD_DIM
QI_OFF = V_OFF + ATT_KV_HEADS * HEAD_DIM
KI_OFF = QI_OFF + IDX_HEADS * IDX_DIM
WI_OFF = KI_OFF + IDX_DIM
ATT_IN = WI_OFF + IDX_HEADS
ATT_IN_PAD = 1920

LANES = 128
SUBLANES = 8
ROW_TILE = 512
VMEM_LIMIT = 56 * 1024 * 1024

LOG2E = math.log2(math.e)
Q_SCALE = HEAD_DIM ** -0.5 * LOG2E
ONES_ROWS = 16
RET_KSCALE = RET_KEY ** -0.5
INT_MIN = -(2 ** 31)
NEG = -1e30


def _bucket_bounds():
    n = np.arange(0, 8192)
    max_exact = NUM_BUCKETS // 2
    val = np.log(np.maximum(n, 1) / max_exact) / math.log(MAX_DISTANCE / max_exact) * (NUM_BUCKETS - max_exact)
    frac = np.abs(val - np.round(val))
    risky = (n > max_exact) & (n < MAX_DISTANCE) & (frac < 1e-4)
    assert not risky.any()
    large = np.minimum(max_exact + np.floor(val + 1e-9).astype(np.int64), NUM_BUCKETS - 1)
    bucket = np.where(n < max_exact, n, large)
    assert (np.diff(bucket) >= 0).all() and (bucket[MAX_DISTANCE:] == NUM_BUCKETS - 1).all()
    return [int(np.argmax(bucket >= b)) for b in range(NUM_BUCKETS)]


BUCKET_BOUNDS = _bucket_bounds()


def _const_spec(shape):
    zeros = (0,) * len(shape)
    return pl.BlockSpec(shape, lambda *_: zeros, pipeline_mode=pl.Buffered(1))


def _params(n_axes):
    return pltpu.CompilerParams(dimension_semantics=("arbitrary",) * n_axes,
                                vmem_limit_bytes=VMEM_LIMIT)


def _rms_rows(a):
    return a * lax.rsqrt(jnp.mean(a * a, axis=-1, keepdims=True) + EPS)


def _canonical_zero(score):
    return jnp.where(score == 0.0, 0.0, score)


def _sortable_key(score):
    bits = pltpu.bitcast(score, I32)
    return bits ^ ((bits >> 31) & jnp.int32(0x7FFFFFFF))


def _key_to_float(key):
    return pltpu.bitcast(key ^ ((key >> 31) & jnp.int32(0x7FFFFFFF)), F32)


KEY_NEG_INF = -2139095041
KEY_POS_INF = 2139095040
KEY_SUBNORMAL_LO = -8388608
KEY_SUBNORMAL_HI = 8388607


def _bracket_kth_largest(count_ge, kprime, guess):
    def canon(x):
        return jnp.where(jnp.logical_and(x >= KEY_SUBNORMAL_LO, x <= KEY_SUBNORMAL_HI), 0, x)

    def succ(x):
        xc = canon(x)
        return jnp.where(xc == 0, KEY_SUBNORMAL_HI + 1, canon(xc + 1))

    g0 = jnp.clip(guess, KEY_NEG_INF, KEY_POS_INF - 1)
    g1 = succ(g0)
    lo = jnp.full(guess.shape, KEY_NEG_INF, I32)
    hi = jnp.full(guess.shape, KEY_POS_INF, I32)

    def narrow(lo, hi, x, cnt):
        ok = cnt >= kprime
        return jnp.where(ok, jnp.maximum(lo, x), lo), jnp.where(ok, hi, jnp.minimum(hi, x))

    c0, c1 = count_ge([_key_to_float(g0), _key_to_float(g1)])
    lo, hi = narrow(lo, hi, g0, c0)
    lo, hi = narrow(lo, hi, g1, c1)

    def open_bracket(c):
        return jnp.max(jnp.where(canon(c[1]) > succ(c[0]), 1.0, 0.0)) > 0.5

    def bisect(c):
        lo, hi = c
        mid = jnp.where(canon(hi) > succ(lo), (lo >> 1) + (hi >> 1) + (lo & hi & 1), lo)
        return narrow(lo, hi, mid, count_ge([_key_to_float(mid)])[0])

    lo, _ = lax.while_loop(open_bracket, bisect, (lo, hi))
    return canon(lo)


def _bit_transpose32(words):
    a = list(words)
    j, m = 16, 0x0000FFFF
    while j:
        k = 0
        while k < 32:
            t = (a[k] ^ lax.shift_right_logical(a[k + j], jnp.int32(j))) & jnp.int32(m)
            a[k] = a[k] ^ t
            a[k + j] = a[k + j] ^ lax.shift_left(t, jnp.int32(j))
            k = (k + j + 1) & ~j
        j >>= 1
        m ^= (m << j) & 0xFFFFFFFF
    return a


def _bias_from_distance(relb_ref, head, dist):
    val = jnp.full(dist.shape, relb_ref[0, head], F32)
    for b in range(1, NUM_BUCKETS):
        val = jnp.where(dist >= BUCKET_BOUNDS[b], relb_ref[b, head], val)
    return val * LOG2E


def _attn_proj_kernel(x_ref, g_ref, w_ref, qg_ref, kg_ref, *outs, transposed):
    x = x_ref[...]
    h = _rms_rows(x) * g_ref[...]
    z = jnp.dot(h.astype(BF16), w_ref[...], preferred_element_type=F32)
    qg = qg_ref[...]
    kg = kg_ref[...]
    k = jnp.concatenate(
        [_rms_rows(z[:, K_OFF + g * HEAD_DIM:K_OFF + (g + 1) * HEAD_DIM]) * kg for g in range(ATT_KV_HEADS)],
        axis=1)
    v = z[:, V_OFF:QI_OFF]
    qi = z[:, QI_OFF:KI_OFF]
    zl = z[:, KI_OFF:ATT_IN_PAD]
    ki = zl[:, :IDX_DIM]
    if transposed:
        qT_ref, qiT_ref, wiT_ref, vT_ref, kb_ref, kib_ref, k_ref, v_ref, ki_ref = outs
        for hh in range(ATT_HEADS):
            qh = _rms_rows(z[:, hh * HEAD_DIM:(hh + 1) * HEAD_DIM]) * qg * Q_SCALE
            qT_ref[hh * HEAD_DIM:(hh + 1) * HEAD_DIM, :] = qh.T.astype(BF16)
        for c in range(IDX_HEADS * IDX_DIM // LANES):
            qiT_ref[c * LANES:(c + 1) * LANES, :] = qi[:, c * LANES:(c + 1) * LANES].T.astype(BF16)
        wiT_ref[...] = zl.T[IDX_DIM:IDX_DIM + 8, :]
        for g in range(ATT_KV_HEADS):
            vT = v[:, g * HEAD_DIM:(g + 1) * HEAD_DIM].T
            for c in range(x.shape[0] // PAGE):
                vT_ref[c, g * HEAD_DIM:(g + 1) * HEAD_DIM, :] = vT[:, c * PAGE:(c + 1) * PAGE].astype(BF16)
        kb_ref[...] = k.astype(BF16)
        kib_ref[...] = ki.astype(BF16)
    else:
        q_ref, qi_ref, zl_ref, k_ref, v_ref, ki_ref = outs
        for hh in range(ATT_HEADS):
            q_ref[:, hh * HEAD_DIM:(hh + 1) * HEAD_DIM] = _rms_rows(z[:, hh * HEAD_DIM:(hh + 1) * HEAD_DIM]) * qg * Q_SCALE
        qi_ref[...] = qi
        zl_ref[...] = zl
    for g in range(ATT_KV_HEADS):
        rows_g = pl.ds(g, x.shape[0], stride=ATT_KV_HEADS)
        k_ref[rows_g, :] = k[:, g * HEAD_DIM:(g + 1) * HEAD_DIM]
        v_ref[rows_g, :] = v[:, g * HEAD_DIM:(g + 1) * HEAD_DIM]
    ki_ref[...] = ki


def _attn_proj(x, g, w, qg, kg, *, transposed):
    rows = x.shape[0]
    tm = ROW_TILE
    nt = rows // tm
    kvw = ATT_KV_HEADS * HEAD_DIM
    row_spec = lambda width: pl.BlockSpec((tm, width), lambda i: (i, 0))
    col_spec = lambda height: pl.BlockSpec((height, tm), lambda i: (0, i))
    kv_rows = ATT_KV_HEADS * rows
    kv_spec = pl.BlockSpec((ATT_KV_HEADS * tm, HEAD_DIM), lambda i: (i, 0))
    leaves_shape = [jax.ShapeDtypeStruct((kv_rows, HEAD_DIM), F32), jax.ShapeDtypeStruct((kv_rows, HEAD_DIM), F32),
                    jax.ShapeDtypeStruct((rows, IDX_DIM), F32)]
    leaves_spec = [kv_spec, kv_spec, row_spec(IDX_DIM)]
    if transposed:
        out_shape = [jax.ShapeDtypeStruct((ATT_HEADS * HEAD_DIM, rows), BF16),
                     jax.ShapeDtypeStruct((IDX_HEADS * IDX_DIM, rows), BF16),
                     jax.ShapeDtypeStruct((8, rows), F32),
                     jax.ShapeDtypeStruct((rows // PAGE, kvw, PAGE), BF16),
                     jax.ShapeDtypeStruct((rows, kvw), BF16),
                     jax.ShapeDtypeStruct((rows, IDX_DIM), BF16)] + leaves_shape
        out_specs = [col_spec(ATT_HEADS * HEAD_DIM), col_spec(IDX_HEADS * IDX_DIM), col_spec(8),
                     pl.BlockSpec((tm // PAGE, kvw, PAGE), lambda i: (i, 0, 0)),
                     row_spec(kvw), row_spec(IDX_DIM)] + leaves_spec
    else:
        out_shape = [jax.ShapeDtypeStruct((rows, ATT_HEADS * HEAD_DIM), F32),
                     jax.ShapeDtypeStruct((rows, IDX_HEADS * IDX_DIM), F32),
                     jax.ShapeDtypeStruct((rows, LANES), F32)] + leaves_shape
        out_specs = [row_spec(ATT_HEADS * HEAD_DIM), row_spec(IDX_HEADS * IDX_DIM), row_spec(LANES)] + leaves_spec
    return pl.pallas_call(
        functools.partial(_attn_proj_kernel, transposed=transposed),
        grid=(nt,),
        in_specs=[row_spec(D_MODEL), _const_spec((1, D_MODEL)), _const_spec((D_MODEL, ATT_IN_PAD)),
                  _const_spec((1, HEAD_DIM)), _const_spec((1, HEAD_DIM))],
        out_specs=out_specs,
        out_shape=out_shape,
        compiler_params=_params(1),
        name="attn_proj_t" if transposed else "attn_proj_r",
    )(x, g, w, qg, kg)


PROMPT_SEQS = 2


def _dsa_prompt_kernel(relb_ref, *refs, topk):
    seqs = range(PROMPT_SEQS)
    n_in = 6
    qT_refs, qiT_refs, wiT_refs, k_refs, vT_refs, ki_refs = zip(*[refs[n_in * u:n_in * (u + 1)] for u in seqs])
    o_ref, keys_sc, scores_sc, planes_sc, live_sc, mb_sc, bias_sc, logit_sc, acc_sc = refs[n_in * PROMPT_SEQS:]
    b = pl.program_id(0)
    i = pl.program_id(1)
    srow = lax.broadcasted_iota(I32, (PAGE, PAGE), 0)
    tcol = lax.broadcasted_iota(I32, (PAGE, PAGE), 1)

    @pl.when((b == 0) & (i == 0))
    def _init_bias():
        planes_sc[...] = jnp.zeros_like(planes_sc)
        live_sc[...] = jnp.zeros_like(live_sc)
        for hh in range(ATT_HEADS):
            for rel in range(2):
                dist = jnp.maximum(rel * PAGE + tcol - srow, 0)
                bias_sc[hh, rel] = _bias_from_distance(relb_ref, hh, dist)
            bias_sc[hh, 2] = jnp.full((PAGE, PAGE), relb_ref[NUM_BUCKETS - 1, hh] * LOG2E, F32)

    wiT = [r[...] for r in wiT_refs]
    qi_cat = [jnp.concatenate([r[hh * IDX_DIM:(hh + 1) * IDX_DIM, :] for hh in range(IDX_HEADS)], axis=1)
              for r in qiT_refs]

    npair = (i + 2) // 2

    def score_block(u, j):
        off = pl.multiple_of(j * PAGE, PAGE)
        kij = ki_refs[u][pl.ds(off, PAGE), :]
        idx = jnp.dot(kij, qi_cat[u], preferred_element_type=F32)
        sc = jnp.zeros((PAGE, PAGE), F32)
        for hh in range(IDX_HEADS):
            sc = sc + wiT[u][hh:hh + 1, :] * jnp.maximum(idx[:, hh * PAGE:(hh + 1) * PAGE], 0.0)
        sc = _canonical_zero(sc)
        causal = jnp.logical_or(j < i, jnp.logical_and(j == i, srow <= tcol))
        scores_sc[u, j] = jnp.where(causal, sc, -jnp.inf)
        keys_sc[u, j] = jnp.where(causal, _sortable_key(sc), INT_MIN)

    nquad = (i + 4) // 4

    def score_body(jj, carry):
        for blk in range(4):
            for u in seqs:
                score_block(u, 4 * jj + blk)
        return carry

    lax.fori_loop(0, nquad, score_body, 0)

    lane = lax.broadcasted_iota(I32, (1, PAGE), 1)
    kprime = jnp.minimum(topk, i * PAGE + lane + 1).astype(F32)

    def plane_body(jj, carry):
        for u in seqs:
            rows = [keys_sc[u, 2 * jj + blk, k * SUBLANES:(k + 1) * SUBLANES, :] ^ INT_MIN
                    for blk in range(2) for k in range(PAGE // SUBLANES)]
            planes = _bit_transpose32(rows)
            live = planes[0]
            for p in range(32):
                planes_sc[u, jj, p] = planes[p]
                live = live | planes[p]
            live_sc[u, jj] = live
        return carry

    lax.fori_loop(0, npair, plane_body, 0)

    n_pairs_max = keys_sc.shape[1] // 2

    def bit_step(u, bi, state):
        thr_u, alive, above = state
        cnt = jnp.zeros((SUBLANES, PAGE), I32)
        with_bit, at_least = [], []
        for p in range(n_pairs_max):
            t = alive[p] & planes_sc[u, p, bi]
            ge = above[p] | t
            cnt = cnt + lax.population_count(ge)
            with_bit.append(t)
            at_least.append(ge)
        take = jnp.sum(cnt.astype(F32), axis=0, keepdims=True) >= kprime
        alive = tuple(jnp.where(take, t, a ^ t) for t, a in zip(with_bit, alive))
        above = tuple(jnp.where(take, g, ge) for g, ge in zip(above, at_least))
        thr_u = thr_u | jnp.where(take, lax.shift_left(jnp.int32(1), 31 - bi), 0)
        return thr_u, alive, above

    zero_words = jnp.zeros((SUBLANES, PAGE), I32)
    states = lax.fori_loop(
        0, 32, lambda bi, states: tuple(bit_step(u, bi, states[u]) for u in seqs),
        tuple((jnp.zeros((1, PAGE), I32),
               tuple(jnp.where(p < npair, live_sc[u, p], zero_words) for p in range(n_pairs_max)),
               (zero_words,) * n_pairs_max) for u in seqs))
    def count_scores(u, preds):
        def body(jj, cnts):
            for blk in range(2):
                sc = scores_sc[u, 2 * jj + blk]
                cnts = tuple(c + jnp.where(p(sc), 1.0, 0.0) for c, p in zip(cnts, preds))
            return cnts
        cnts = lax.fori_loop(0, npair, body, tuple(jnp.zeros((PAGE, PAGE), F32) for _ in preds))
        return [jnp.sum(c, axis=0, keepdims=True) for c in cnts]

    def count_ge(u):
        return lambda thresholds: count_scores(u, [lambda sc, x=x: sc >= x for x in thresholds])

    thr = [_key_to_float(_bracket_kth_largest(count_ge(u), kprime, states[u][0] ^ INT_MIN)) for u in seqs]
    need = [kprime - count_scores(u, [lambda sc, u=u: sc > thr[u]])[0] for u in seqs]

    tri = jnp.where(srow > tcol, 1.0, 0.0).astype(BF16)

    def mask_block(u, j, carry):
        sc = scores_sc[u, j]
        eqf = jnp.where(sc == thr[u], 1.0, 0.0)
        rank = jnp.dot(tri, eqf.astype(BF16), preferred_element_type=F32) + carry
        tie = jnp.where(jnp.logical_and(sc == thr[u], rank < need[u]), 0.0, NEG)
        mb_sc[u, j] = jnp.where(sc > thr[u], 0.0, tie)
        return carry + jnp.sum(eqf, axis=0, keepdims=True)

    def mask_body(jj, carry):
        for blk in range(4):
            carry = tuple(mask_block(u, 4 * jj + blk, carry[u]) for u in seqs)
        return carry

    lax.fori_loop(0, nquad, mask_body, tuple(jnp.zeros((1, PAGE), F32) for _ in seqs))

    groups = range(ATT_KV_HEADS)
    units = [(u, g) for u in seqs for g in groups]
    width = ATT_GROUP * PAGE
    qcats = [jnp.concatenate(
        [qT_refs[u][(ATT_GROUP * g + r) * HEAD_DIM:(ATT_GROUP * g + r + 1) * HEAD_DIM, :]
         for r in range(ATT_GROUP)], axis=1) for u, g in units]

    def logit_block(j, ms):
        off = pl.multiple_of(j * PAGE, PAGE)
        relc = jnp.clip(i - j, 0, 2)
        out = []
        for n, (u, g) in enumerate(units):
            mb = mb_sc[u, j]
            kj = k_refs[u][pl.ds(off, PAGE), g * HEAD_DIM:(g + 1) * HEAD_DIM]
            logits = jnp.dot(kj, qcats[n], preferred_element_type=F32)
            logits = jnp.concatenate(
                [logits[:, r * PAGE:(r + 1) * PAGE] + (mb + bias_sc[ATT_GROUP * g + r, relc])
                 for r in range(ATT_GROUP)], axis=1)
            logit_sc[u, g, j] = logits
            out.append(jnp.maximum(ms[n], jnp.max(logits, axis=0, keepdims=True)))
        return tuple(out)

    def over_block_pairs(pair_step, carry):
        nfull = npair // 2
        carry = lax.fori_loop(0, nfull, lambda t, c: pair_step(4 * t + 2, pair_step(4 * t, c)), carry)
        return lax.cond(npair % 2 == 1, lambda c: pair_step(4 * nfull, c), lambda c: c, carry)

    ms = over_block_pairs(lambda j0, c: logit_block(j0 + 1, logit_block(j0, c)),
                          tuple(jnp.full((1, width), NEG, F32) for _ in units))

    ones_rows = jnp.ones((ONES_ROWS, PAGE), BF16)

    def value_pair(j0, carry):
        for n, (u, g) in enumerate(units):
            p = jnp.concatenate(
                [jnp.exp2(logit_sc[u, g, j0 + blk] - ms[n]).astype(BF16) for blk in range(2)], axis=0)
            vT2 = jnp.concatenate(
                [jnp.concatenate([vT_refs[u][j0 + blk, g * HEAD_DIM:(g + 1) * HEAD_DIM, :], ones_rows], axis=0)
                 for blk in range(2)], axis=1)
            acc_sc[u, g] = acc_sc[u, g] + jnp.dot(vT2, p, preferred_element_type=F32)
        return carry

    acc_sc[...] = jnp.zeros_like(acc_sc)
    over_block_pairs(value_pair, 0)
    for u, g in units:
        acc = acc_sc[u, g]
        outT = acc[:HEAD_DIM] * (1.0 / acc[HEAD_DIM:HEAD_DIM + 1])
        for r in range(ATT_GROUP):
            hh = ATT_GROUP * g + r
            o_ref[u, :, hh * HEAD_DIM:(hh + 1) * HEAD_DIM] = outT[:, r * PAGE:(r + 1) * PAGE].T.astype(BF16)


def _dsa_prompt(rel_bias, qT, qiT, wiT, kb, vT3, kib, *, batch, seq):
    nb = seq // PAGE
    rows = batch * seq
    kvw = ATT_KV_HEADS * HEAD_DIM
    ns = PROMPT_SEQS

    def seq_specs(u):
        qcol = lambda height: pl.BlockSpec((height, PAGE), lambda b, i: (0, (ns * b + u) * nb + i))
        return [qcol(ATT_HEADS * HEAD_DIM), qcol(IDX_HEADS * IDX_DIM), qcol(8),
                pl.BlockSpec((seq, kvw), lambda b, i: (ns * b + u, 0)),
                pl.BlockSpec((nb, kvw, PAGE), lambda b, i: (ns * b + u, 0, 0)),
                pl.BlockSpec((seq, IDX_DIM), lambda b, i: (ns * b + u, 0))]

    out = pl.pallas_call(
        functools.partial(_dsa_prompt_kernel, topk=min(TOPK_MAX, seq // 4)),
        grid=(batch // ns, nb),
        in_specs=[pl.BlockSpec(memory_space=pltpu.SMEM)] + [s for u in range(ns) for s in seq_specs(u)],
        out_specs=pl.BlockSpec((ns, PAGE, ATT_HEADS * HEAD_DIM), lambda b, i: (b, i, 0)),
        out_shape=jax.ShapeDtypeStruct((batch, seq, ATT_HEADS * HEAD_DIM), BF16),
        scratch_shapes=[pltpu.VMEM((ns, nb, PAGE, PAGE), I32), pltpu.VMEM((ns, nb, PAGE, PAGE), F32),
                        pltpu.VMEM((ns, nb // 2, 32, SUBLANES, PAGE), I32),
                        pltpu.VMEM((ns, nb // 2, SUBLANES, PAGE), I32),
                        pltpu.VMEM((ns, nb, PAGE, PAGE), F32),
                        pltpu.VMEM((ATT_HEADS, 3, PAGE, PAGE), F32),
                        pltpu.VMEM((ns, ATT_KV_HEADS, nb, PAGE, ATT_GROUP * PAGE), F32),
                        pltpu.VMEM((ns, ATT_KV_HEADS, HEAD_DIM + ONES_ROWS, ATT_GROUP * PAGE), F32)],
        compiler_params=_params(2),
        name="dsa_prompt",
    )(rel_bias, *([qT, qiT, wiT, kb, vT3, kib] * ns))
    return out.reshape(rows, ATT_HEADS * HEAD_DIM)


TQ = 8
ATT_SAMPLES = 2
SEL_SAMPLES = 16


def _dsa_select_kernel(pt_ref, qi_ref, zl_ref, cki_hbm, mb_ref, sc_sc, kibuf, sems, *, n_pages, n_new):
    ns = SEL_SAMPLES
    n_fetch = ns * n_pages
    b = pl.program_id(0)
    slot = b % 2

    def page_copies(step, slot):
        return [pltpu.make_async_copy(cki_hbm.at[pt_ref[step * n_fetch + f]], kibuf.at[slot, f], sems.at[slot])
                for f in range(n_fetch)]

    @pl.when(b == 0)
    def _start_first():
        for c in page_copies(0, 0):
            c.start()

    @pl.when(b + 1 < pl.num_programs(0))
    def _start_next():
        for c in page_copies(b + 1, 1 - slot):
            c.start()

    for c in page_copies(b, slot):
        c.wait()
    ki_pages = [kibuf.at[slot, f] for f in range(n_fetch)]
    n_blk = n_pages + 1
    rows = ns * TQ
    trow = lax.broadcasted_iota(I32, (TQ, PAGE), 0)
    scol = lax.broadcasted_iota(I32, (TQ, PAGE), 1)
    pad_rows = lambda a: jnp.concatenate([a, jnp.zeros((PAGE - TQ, a.shape[1]), a.dtype)], axis=0)
    nt = (((1,), (1,)), ((), ()))

    for s in range(ns):
        zl = zl_ref[s]
        qi = qi_ref[s]
        for p in range(n_blk):
            if p == n_pages:
                ki_new = pad_rows(zl[:, :IDX_DIM]).astype(BF16)
                idx = lax.dot_general(qi, ki_new, nt, preferred_element_type=F32)
            else:
                idx = jnp.dot(qi, ki_pages[s * n_pages + p][...].astype(BF16), preferred_element_type=F32)
            sc = jnp.zeros((TQ, PAGE), F32)
            for hh in range(IDX_HEADS):
                sc = sc + zl[:, IDX_DIM + hh:IDX_DIM + hh + 1] * jnp.maximum(idx[hh * TQ:(hh + 1) * TQ], 0.0)
            sc = _canonical_zero(sc)
            if p == n_pages:
                sc = jnp.where(jnp.logical_and(scol <= trow, scol < n_new), sc, -jnp.inf)
            sc_sc[s * TQ:(s + 1) * TQ, p * PAGE:(p + 1) * PAGE] = sc

    kprime = float(min(TOPK_MAX, (n_pages * PAGE + n_new) // 4))

    def count(pred_of_scores):
        return jnp.sum(jnp.where(pred_of_scores(sc_sc[...]), 1.0, 0.0), axis=1, keepdims=True)

    cand0 = jnp.zeros((rows, 1), I32)
    thr0 = jnp.where(count(lambda sc: sc >= _key_to_float(cand0)) >= kprime, cand0, INT_MIN)

    def bit_body(bi, thr_key):
        cand = thr_key | lax.shift_left(jnp.int32(1), 30 - bi)
        return jnp.where(count(lambda sc: sc >= _key_to_float(cand)) >= kprime, cand, thr_key)

    thr = _key_to_float(lax.fori_loop(0, 31, bit_body, thr0))
    need = kprime - count(lambda sc: sc > thr)

    s_r = lax.broadcasted_iota(I32, (PAGE, PAGE), 0)
    s_c = lax.broadcasted_iota(I32, (PAGE, PAGE), 1)
    triu = jnp.where(s_r < s_c, 1.0, 0.0).astype(BF16)
    carry = jnp.zeros((rows, 1), F32)
    for p in range(n_blk):
        sc = sc_sc[:, p * PAGE:(p + 1) * PAGE]
        eqf = jnp.where(sc == thr, 1.0, 0.0)
        rank = jnp.dot(eqf.astype(BF16), triu, preferred_element_type=F32) + carry
        tie = jnp.where(jnp.logical_and(sc == thr, rank < need), 0.0, NEG)
        mb_ref[:, p * PAGE:(p + 1) * PAGE] = jnp.where(sc > thr, 0.0, tie)
        carry = carry + jnp.sum(eqf, axis=1, keepdims=True)


def _dsa_select(page_table, qi8, zl8, cache_ki, *, n_new):
    nsamp, n_pages = page_table.shape
    ns = SEL_SAMPLES
    width = (n_pages + 1) * PAGE

    in_specs = [pl.BlockSpec((ns, IDX_HEADS * TQ, IDX_DIM), lambda b, pt: (b, 0, 0)),
                pl.BlockSpec((ns, TQ, LANES), lambda b, pt: (b, 0, 0)),
                pl.BlockSpec(memory_space=pl.ANY)]
    grid_spec = pltpu.PrefetchScalarGridSpec(
        num_scalar_prefetch=1, grid=(nsamp // ns,), in_specs=in_specs,
        out_specs=pl.BlockSpec((ns * TQ, width), lambda b, pt: (b, 0)),
        scratch_shapes=[pltpu.VMEM((ns * TQ, width), F32),
                        pltpu.VMEM((2, ns * n_pages, IDX_DIM, PAGE), F32), pltpu.SemaphoreType.DMA((2,))])
    return pl.pallas_call(
        functools.partial(_dsa_select_kernel, n_pages=n_pages, n_new=n_new),
        grid_spec=grid_spec,
        out_shape=jax.ShapeDtypeStruct((nsamp * TQ, width), F32),
        compiler_params=_params(1),
        name="dsa_select",
    )(page_table.reshape(-1), qi8, zl8, cache_ki)


def _dsa_sample_kernel(pt_ref, relb_ref, q_ref, mb_ref, kn_ref, vn_ref, ck_hbm, cv_hbm, o_ref,
                       bias_sc, kbuf, vbuf, sems, *, n_pages):
    past = n_pages * PAGE
    rows_g = ATT_GROUP * TQ
    page_rows = PAGE * ATT_KV_HEADS
    ns = ATT_SAMPLES
    b = pl.program_id(0)
    n_slots = kbuf.shape[0]
    ahead = n_slots - 1
    slot = b % n_slots

    def page_copies(step, slot):
        copies = []
        for f in range(ns * n_pages):
            src = pl.ds(pl.multiple_of(pt_ref[step * ns * n_pages + f] * page_rows, page_rows), page_rows)
            dst = pl.ds(f * page_rows, page_rows)
            copies.append(pltpu.make_async_copy(ck_hbm.at[src, :], kbuf.at[slot, dst, :], sems.at[slot, 0]))
            copies.append(pltpu.make_async_copy(cv_hbm.at[src, :], vbuf.at[slot, dst, :], sems.at[slot, 1]))
        return copies

    @pl.when(b == 0)
    def _start_first():
        for s in range(ahead):
            for c in page_copies(s, s):
                c.start()

    @pl.when(b + ahead < pl.num_programs(0))
    def _start_next():
        for c in page_copies(b + ahead, (b + ahead) % n_slots):
            c.start()

    for c in page_copies(b, slot):
        c.wait()

    trow = lax.broadcasted_iota(I32, (TQ, PAGE), 0)
    scol = lax.broadcasted_iota(I32, (TQ, PAGE), 1)

    @pl.when(pl.program_id(0) == 0)
    def _init_bias():
        for g in range(ATT_KV_HEADS):
            for r in range(ATT_GROUP):
                hh = ATT_GROUP * g + r
                rs = slice(r * TQ, (r + 1) * TQ)
                far = jnp.full((TQ, PAGE), relb_ref[NUM_BUCKETS - 1, hh] * LOG2E, F32)
                for p in range(n_pages - 1):
                    bias_sc[g, rs, p * PAGE:(p + 1) * PAGE] = far
                bias_sc[g, rs, past - PAGE:past] = _bias_from_distance(
                    relb_ref, hh, jnp.maximum(PAGE + trow - scol, 0))
                bias_sc[g, rs, past:past + PAGE] = _bias_from_distance(relb_ref, hh, jnp.maximum(trow - scol, 0))

    nt = (((1,), (1,)), ((), ()))
    pad_rows = lambda a: jnp.concatenate([a, jnp.zeros((PAGE - TQ, a.shape[1]), a.dtype)], axis=0)
    units = [(s, g) for s in range(ns) for g in range(ATT_KV_HEADS)]
    for s, g in units:
        gs = slice(g * HEAD_DIM, (g + 1) * HEAD_DIM)
        head_rows = pl.ds(s * n_pages * page_rows + g, past, stride=ATT_KV_HEADS)
        k_all = jnp.concatenate([kbuf[slot, head_rows, :].astype(BF16), pad_rows(kn_ref[s, :, gs]).astype(BF16)], axis=0)
        v_all = jnp.concatenate([vbuf[slot, head_rows, :].astype(BF16), pad_rows(vn_ref[s, :, gs]).astype(BF16)], axis=0)
        mb = jnp.concatenate([mb_ref[s * TQ:(s + 1) * TQ, :]] * ATT_GROUP, axis=0)
        qg = q_ref[s, g * rows_g:(g + 1) * rows_g, :]
        logits = lax.dot_general(qg, k_all, nt, preferred_element_type=F32)
        logits = logits + (mb + bias_sc[g])
        m = jnp.max(logits, axis=1, keepdims=True)
        pr = jnp.exp2(logits - m)
        l = jnp.sum(pr, axis=1, keepdims=True)
        acc = jnp.dot(pr.astype(BF16), v_all, preferred_element_type=F32)
        o_ref[s, g * rows_g:(g + 1) * rows_g, :] = (acc * (1.0 / l)).astype(BF16)


def _dsa_sample(page_table, rel_bias, q8, mb, kn8, vn8, cache_k, cache_v):
    nsamp, n_pages = page_table.shape
    kvw = ATT_KV_HEADS * HEAD_DIM
    ns = ATT_SAMPLES
    per_step = lambda *tail: pl.BlockSpec((ns,) + tail, lambda b, pt: (b,) + (0,) * len(tail))

    n_slots = 3
    page_buf = pltpu.VMEM((n_slots, ns * n_pages * PAGE * ATT_KV_HEADS, HEAD_DIM), F32)
    in_specs = [pl.BlockSpec(memory_space=pltpu.SMEM),
                per_step(ATT_HEADS * TQ, HEAD_DIM),
                pl.BlockSpec((ns * TQ, mb.shape[1]), lambda b, pt: (b, 0)),
                per_step(TQ, kvw), per_step(TQ, kvw),
                pl.BlockSpec(memory_space=pl.ANY), pl.BlockSpec(memory_space=pl.ANY)]
    grid_spec = pltpu.PrefetchScalarGridSpec(
        num_scalar_prefetch=1, grid=(nsamp // ns,), in_specs=in_specs,
        out_specs=per_step(ATT_HEADS * TQ, HEAD_DIM),
        scratch_shapes=[pltpu.VMEM((ATT_KV_HEADS, ATT_GROUP * TQ, mb.shape[1]), F32),
                        page_buf, page_buf, pltpu.SemaphoreType.DMA((n_slots, 2))])
    return pl.pallas_call(
        functools.partial(_dsa_sample_kernel, n_pages=n_pages),
        grid_spec=grid_spec,
        out_shape=jax.ShapeDtypeStruct((nsamp, ATT_HEADS * TQ, HEAD_DIM), BF16),
        compiler_params=_params(1),
        name="dsa_sample",
    )(page_table.reshape(-1), rel_bias, q8, mb, kn8, vn8, cache_k, cache_v)


def _post_mix_kernel(x_ref, o_ref, wo_ref, g_ref, w1_ref, w2_ref, y_ref):
    y = x_ref[...] + jnp.dot(o_ref[...], wo_ref[...], preferred_element_type=F32)
    h = (_rms_rows(y) * g_ref[...]).astype(BF16)
    acc = y
    for c in range(D_FF // D_MODEL):
        a = jnp.dot(h, w1_ref[:, c * D_MODEL:(c + 1) * D_MODEL], preferred_element_type=F32)
        a = jnp.square(jnp.maximum(a, 0.0)).astype(BF16)
        acc = acc + jnp.dot(a, w2_ref[c * D_MODEL:(c + 1) * D_MODEL, :], preferred_element_type=F32)
    y_ref[...] = acc


def _post_mix(x, o, wo, g, w1, w2):
    rows = x.shape[0]
    tm = ROW_TILE
    ko = o.shape[1]
    return pl.pallas_call(
        _post_mix_kernel,
        grid=(rows // tm,),
        in_specs=[pl.BlockSpec((tm, D_MODEL), lambda i: (i, 0)), pl.BlockSpec((tm, ko), lambda i: (i, 0)),
                  _const_spec((ko, D_MODEL)), _const_spec((1, D_MODEL)),
                  _const_spec((D_MODEL, D_FF)), _const_spec((D_FF, D_MODEL))],
        out_specs=pl.BlockSpec((tm, D_MODEL), lambda i: (i, 0)),
        out_shape=jax.ShapeDtypeStruct((rows, D_MODEL), F32),
        compiler_params=_params(1),
        name="post_mix",
    )(x, o, wo, g, w1, w2)


def _ret_proj_kernel(x_ref, g_ref, w_ref, cos_ref, sin_ref, q_ref, k_ref, v_ref, gate_ref):
    hb = (_rms_rows(x_ref[...]) * g_ref[...]).astype(BF16)
    cos = cos_ref[...]
    sin = sin_ref[...]
    half = RET_KEY // 2
    qk = RET_HEADS * RET_KEY
    vd = RET_HEADS * RET_VAL

    def rotated(col0, scale):
        z = jnp.dot(hb, w_ref[:, col0:col0 + RET_KEY], preferred_element_type=F32)
        x1, x2 = z[:, :half], z[:, half:]
        return (x1 * cos - x2 * sin) * scale, (x1 * sin + x2 * cos) * scale

    for hh in range(RET_HEADS):
        a, b = rotated(hh * RET_KEY, 1.0)
        q_ref[:, hh * RET_KEY:hh * RET_KEY + half] = a.astype(BF16)
        q_ref[:, hh * RET_KEY + half:(hh + 1) * RET_KEY] = b.astype(BF16)
        a, b = rotated(qk + hh * RET_KEY, RET_KSCALE)
        k_ref[:, hh * RET_KEY:hh * RET_KEY + half] = a.astype(BF16)
        k_ref[:, hh * RET_KEY + half:(hh + 1) * RET_KEY] = b.astype(BF16)
    for hh in range(RET_HEADS):
        cs = slice(hh * RET_VAL, (hh + 1) * RET_VAL)
        v_ref[:, cs] = jnp.dot(hb, w_ref[:, 2 * qk + hh * RET_VAL:2 * qk + (hh + 1) * RET_VAL],
                               preferred_element_type=F32).astype(BF16)
        gate_ref[:, cs] = jnp.dot(hb, w_ref[:, 2 * qk + vd + hh * RET_VAL:2 * qk + vd + (hh + 1) * RET_VAL],
                                  preferred_element_type=F32).astype(BF16)


def _ret_proj(x, g, w, cos, sin):
    rows = x.shape[0]
    tm = ROW_TILE
    qk = RET_HEADS * RET_KEY
    vd = RET_HEADS * RET_VAL
    n_pos_tiles = cos.shape[0] // tm
    row_spec = lambda width: pl.BlockSpec((tm, width), lambda i: (i, 0))
    pos_spec = pl.BlockSpec((tm, RET_KEY // 2), lambda i: (i % n_pos_tiles, 0))
    return pl.pallas_call(
        _ret_proj_kernel,
        grid=(rows // tm,),
        in_specs=[row_spec(D_MODEL), _const_spec((1, D_MODEL)), _const_spec((D_MODEL, 2 * qk + 2 * vd)),
                  pos_spec, pos_spec],
        out_specs=[row_spec(qk), row_spec(qk), row_spec(vd), row_spec(vd)],
        out_shape=[jax.ShapeDtypeStruct((rows, qk), BF16), jax.ShapeDtypeStruct((rows, qk), BF16),
                   jax.ShapeDtypeStruct((rows, vd), BF16), jax.ShapeDtypeStruct((rows, vd), BF16)],
        compiler_params=_params(1),
        name="ret_proj",
    )(x, g, w, cos, sin)


def _retention_kernel(cd_ref, q_ref, k_ref, v_ref, g_ref, dm_ref, qd_ref, kd_ref, *rest, has_state0):
    if has_state0:
        s0_ref, o_ref, s_ref = rest
    else:
        o_ref, s_ref = rest

    @pl.when(pl.program_id(1) == 0)
    def _init_state():
        s_ref[...] = s0_ref[...] if has_state0 else jnp.zeros_like(s_ref)

    nt = (((1,), (1,)), ((), ()))
    tn = (((0,), (0,)), ((), ()))
    units = [(bi, hh) for bi in range(q_ref.shape[0]) for hh in range(RET_HEADS)]
    ks = lambda hh: slice(hh * RET_KEY, (hh + 1) * RET_KEY)
    vs = lambda hh: slice(hh * RET_VAL, (hh + 1) * RET_VAL)
    att = [lax.dot_general(q_ref[bi, :, ks(hh)].astype(BF16), k_ref[bi, :, ks(hh)].astype(BF16), nt,
                           preferred_element_type=F32) * dm_ref[hh] for bi, hh in units]
    cross = [jnp.dot((q_ref[bi, :, ks(hh)] * qd_ref[hh]).astype(BF16), s_ref[bi, hh].astype(BF16),
                     preferred_element_type=F32) for bi, hh in units]
    for (bi, hh), a, x in zip(units, att, cross):
        o = jnp.dot(a.astype(BF16), v_ref[bi, :, vs(hh)], preferred_element_type=F32) + x
        gate = g_ref[bi, :, vs(hh)].astype(F32)
        o_ref[bi, :, vs(hh)] = (_rms_rows(o) * (gate * jax.nn.sigmoid(gate))).astype(BF16)
    for bi, hh in units:
        kd = (k_ref[bi, :, ks(hh)] * kd_ref[hh]).astype(BF16)
        s_ref[bi, hh] = cd_ref[hh] * s_ref[bi, hh] + lax.dot_general(kd, v_ref[bi, :, vs(hh)], tn,
                                                                     preferred_element_type=F32)


def _retention(q, k, v, gate, consts, state0, *, seqs_per_step):
    dm, qd, kd, cd = consts
    nb, t, _ = q.shape
    c = dm.shape[1]
    bb = seqs_per_step
    qk = RET_HEADS * RET_KEY
    vd = RET_HEADS * RET_VAL
    tok = lambda width: pl.BlockSpec((bb, c, width), lambda b, ci: (b, ci, 0))
    st_spec = pl.BlockSpec((bb, RET_HEADS, RET_KEY, RET_VAL), lambda b, ci: (b, 0, 0, 0))
    in_specs = [pl.BlockSpec(memory_space=pltpu.SMEM), tok(qk), tok(qk), tok(vd), tok(vd),
                _const_spec(dm.shape), _const_spec(qd.shape), _const_spec(kd.shape)]
    args = [cd, q, k, v, gate, dm, qd, kd]
    if state0 is not None:
        in_specs.append(st_spec)
        args.append(state0)
    return pl.pallas_call(
        functools.partial(_retention_kernel, has_state0=state0 is not None),
        grid=(nb // bb, t // c),
        in_specs=in_specs,
        out_specs=[tok(vd), st_spec],
        out_shape=[jax.ShapeDtypeStruct((nb, t, vd), BF16),
                   jax.ShapeDtypeStruct((nb, RET_HEADS, RET_KEY, RET_VAL), F32)],
        compiler_params=_params(2),
        name="retention_s" if state0 is not None else "retention_p",
    )(*args)


def _retention_consts(c, c_pad):
    log_g = jnp.log1p(-jnp.exp2(-5.0 - jnp.arange(RET_HEADS, dtype=F32)))
    i = jnp.arange(c, dtype=F32)
    diff = i[:, None] - i[None, :]
    dm = jnp.where(diff >= 0, jnp.exp(log_g[:, None, None] * jnp.maximum(diff, 0.0)), 0.0)
    qd = jnp.exp(log_g[:, None] * (i[None, :] + 1.0))
    kd = jnp.exp(log_g[:, None] * (c - 1.0 - i[None, :]))
    cd = jnp.exp(log_g * c)
    pad = c_pad - c
    dm = jnp.pad(dm, ((0, 0), (0, pad), (0, pad)))
    qd = jnp.broadcast_to(jnp.pad(qd, ((0, 0), (0, pad)))[:, :, None], (RET_HEADS, c_pad, RET_KEY))
    kd = jnp.broadcast_to(jnp.pad(kd, ((0, 0), (0, pad)))[:, :, None], (RET_HEADS, c_pad, RET_KEY))
    return dm, qd, kd, cd


def _rotation_tables(pos):
    half = RET_KEY // 2
    theta = 1.0 / (10000.0 ** jnp.linspace(0.0, 1.0, half, dtype=F32))
    ang = pos.astype(F32)[:, None] * theta[None, :]
    return jnp.cos(ang), jnp.sin(ang)


def kernel(x_prompt, x_sample, cache_k, cache_v, cache_kidx, state_ret, page_table, rel_bias, ln_mix, ln_mlp,
           att_w_in, att_q_gain, att_k_gain, att_w_out, ret_w_in, ret_w_out, mlp_w_in, mlp_w_out):
    bp, tp, _ = x_prompt.shape
    bs, ts, _ = x_sample.shape
    n_phys = cache_k.shape[1]
    assert ln_mix.shape[0] == 2 and att_w_in.shape[0] == 1 and ret_w_in.shape[0] == 1
    assert tp % RET_CHUNK == 0 and ts <= TQ and cache_k.shape[2] == PAGE
    assert (tp // PAGE) % 4 == 0 and bs % SEL_SAMPLES == 0 and bs % ATT_SAMPLES == 0 and bp % PROMPT_SEQS == 0
    kvw = ATT_KV_HEADS * HEAD_DIM
    past = page_table.shape[1] * PAGE

    w_att = jnp.pad(att_w_in[0], ((0, 0), (0, ATT_IN_PAD - ATT_IN))).astype(BF16)
    w_att_out = att_w_out[0].astype(BF16)
    w_ret = ret_w_in[0].astype(BF16)
    w_ret_out = ret_w_out[0].astype(BF16)
    w1 = mlp_w_in.astype(BF16)
    w2 = mlp_w_out.astype(BF16)
    row = lambda a: a.reshape(1, -1)

    xp = x_prompt.reshape(bp * tp, D_MODEL)
    xs = x_sample.reshape(bs * ts, D_MODEL)

    qT, qiT, wiT, vT3, kb, kib, kp, vp, kip = _attn_proj(
        xp, row(ln_mix[0]), w_att, row(att_q_gain[0]), row(att_k_gain[0]), transposed=True)
    op = _dsa_prompt(rel_bias, qT, qiT, wiT, kb, vT3, kib, batch=bp, seq=tp)
    yp = _post_mix(xp, op, w_att_out, row(ln_mlp[0]), w1[0], w2[0])

    qs, qis, zls, ks, vs, kis = _attn_proj(
        xs, row(ln_mix[0]), w_att, row(att_q_gain[0]), row(att_k_gain[0]), transposed=False)
    pad_t = lambda a: jnp.pad(a, ((0, 0), (0, TQ - ts)) + ((0, 0),) * (a.ndim - 2))
    q8 = pad_t(qs.reshape(bs, ts, ATT_HEADS, HEAD_DIM)).transpose(0, 2, 1, 3)
    q8 = q8.reshape(bs, ATT_HEADS * TQ, HEAD_DIM).astype(BF16)
    qi8 = pad_t(qis.reshape(bs, ts, IDX_HEADS, IDX_DIM)).transpose(0, 2, 1, 3)
    qi8 = qi8.reshape(bs, IDX_HEADS * TQ, IDX_DIM).astype(BF16)
    zl8 = pad_t(zls.reshape(bs, ts, LANES))
    kn8 = pad_t(ks.reshape(bs, ts, kvw))
    vn8 = pad_t(vs.reshape(bs, ts, kvw))
    mb = _dsa_select(page_table, qi8, zl8, jnp.swapaxes(cache_kidx[0], 1, 2), n_new=ts)
    os8 = _dsa_sample(page_table, rel_bias, q8, mb, kn8, vn8,
                      cache_k[0].reshape(n_phys * PAGE * ATT_KV_HEADS, HEAD_DIM),
                      cache_v[0].reshape(n_phys * PAGE * ATT_KV_HEADS, HEAD_DIM))
    os_ = os8.reshape(bs, ATT_HEADS, TQ, HEAD_DIM)[:, :, :ts].transpose(0, 2, 1, 3)
    os_ = os_.reshape(bs * ts, ATT_HEADS * HEAD_DIM)
    ys = _post_mix(xs, os_, w_att_out, row(ln_mlp[0]), w1[0], w2[0])

    cos_p, sin_p = _rotation_tables(jnp.arange(tp, dtype=jnp.int32))
    q, k, v, gate = _ret_proj(yp, row(ln_mix[1]), w_ret, cos_p, sin_p)
    r3 = lambda a: a.reshape(bp, tp, a.shape[-1])
    orp, state_p = _retention(r3(q), r3(k), r3(v), r3(gate), _retention_consts(RET_CHUNK, RET_CHUNK), None,
                            seqs_per_step=1)
    yp = _post_mix(yp, orp.reshape(bp * tp, -1), w_ret_out, row(ln_mlp[1]), w1[1], w2[1])

    cos_s, sin_s = _rotation_tables(past + jnp.arange(ts, dtype=jnp.int32))
    tile_s = lambda a: jnp.tile(a, (bs, 1))
    q, k, v, gate = _ret_proj(ys, row(ln_mix[1]), w_ret, tile_s(cos_s), tile_s(sin_s))
    r3s = lambda a: pad_t(a.reshape(bs, ts, a.shape[-1]))
    ors, state_s = _retention(r3s(q), r3s(k), r3s(v), r3s(gate), _retention_consts(ts, TQ), state_ret[0],
                            seqs_per_step=2)
    ys = _post_mix(ys, ors[:, :ts].reshape(bs * ts, -1), w_ret_out, row(ln_mlp[1]), w1[1], w2[1])

    return (yp.reshape(bp, tp, D_MODEL), ys.reshape(bs, ts, D_MODEL),
            kp.reshape(1, bp, tp, ATT_KV_HEADS, HEAD_DIM), vp.reshape(1, bp, tp, ATT_KV_HEADS, HEAD_DIM),
            kip.reshape(1, bp, tp, IDX_DIM), state_p[None],
            ks.reshape(1, bs, ts, ATT_KV_HEADS, HEAD_DIM), vs.reshape(1, bs, ts, ATT_KV_HEADS, HEAD_DIM),
            kis.reshape(1, bs, ts, IDX_DIM), state_s[None])
```

```python
import functools
import math

import jax
import jax.numpy as jnp
import numpy as np
from jax import lax
from jax.experimental import pallas as pl
from jax.experimental.pallas import tpu as pltpu

F32 = jnp.float32
BF16 = jnp.bfloat16
I32 = jnp.int32

D_MODEL = 1024
PAGE = 128
ATT_HEADS = 8
ATT_KV_HEADS = 2
ATT_GROUP = ATT_HEADS // ATT_KV_HEADS
HEAD_DIM = 128
IDX_HEADS = 4
IDX_DIM = 64
TOPK_MAX = 256
NUM_BUCKETS = 32
MAX_DISTANCE = 128
RET_HEADS = 4
RET_KEY = 256
RET_VAL = 512
RET_CHUNK = 256
D_FF = 4 * D_MODEL
EPS = 1e-6

Q_OFF = 0
K_OFF = ATT_HEADS * HEAD_DIM
V_OFF = K_OFF + ATT_KV_HEADS * HEAD_DIM
QI_OFF = V_OFF + ATT_KV_HEADS * HEAD_DIM
KI_OFF = QI_OFF + IDX_HEADS * IDX_DIM
WI_OFF = KI_OFF + IDX_DIM
ATT_IN = WI_OFF + IDX_HEADS
ATT_IN_PAD = 1920

LANES = 128
SUBLANES = 8
ROW_TILE = 512
VMEM_LIMIT = 56 * 1024 * 1024

LOG2E = math.log2(math.e)
Q_SCALE = HEAD_DIM ** -0.5 * LOG2E
ONES_ROWS = 16
RET_KSCALE = RET_KEY ** -0.5
INT_MIN = -(2 ** 31)
NEG = -1e30


def _bucket_bounds():
    n = np.arange(0, 8192)
    max_exact = NUM_BUCKETS // 2
    val = np.log(np.maximum(n, 1) / max_exact) / math.log(MAX_DISTANCE / max_exact) * (NUM_BUCKETS - max_exact)
    frac = np.abs(val - np.round(val))
    risky = (n > max_exact) & (n < MAX_DISTANCE) & (frac < 1e-4)
    assert not risky.any()
    large = np.minimum(max_exact + np.floor(val + 1e-9).astype(np.int64), NUM_BUCKETS - 1)
    bucket = np.where(n < max_exact, n, large)
    assert (np.diff(bucket) >= 0).all() and (bucket[MAX_DISTANCE:] == NUM_BUCKETS - 1).all()
    return [int(np.argmax(bucket >= b)) for b in range(NUM_BUCKETS)]


BUCKET_BOUNDS = _bucket_bounds()


def _const_spec(shape):
    zeros = (0,) * len(shape)
    return pl.BlockSpec(shape, lambda *_: zeros, pipeline_mode=pl.Buffered(1))


def _params(n_axes):
    return pltpu.CompilerParams(dimension_semantics=("arbitrary",) * n_axes,
                                vmem_limit_bytes=VMEM_LIMIT)


def _rms_rows(a):
    return a * lax.rsqrt(jnp.mean(a * a, axis=-1, keepdims=True) + EPS)


def _canonical_zero(score):
    return jnp.where(score == 0.0, 0.0, score)


def _sortable_key(score):
    bits = pltpu.bitcast(score, I32)
    return bits ^ ((bits >> 31) & jnp.int32(0x7FFFFFFF))


def _key_to_float(key):
    return pltpu.bitcast(key ^ ((key >> 31) & jnp.int32(0x7FFFFFFF)), F32)


KEY_NEG_INF = -2139095041
KEY_POS_INF = 2139095040
KEY_SUBNORMAL_LO = -8388608
KEY_SUBNORMAL_HI = 8388607


def _bracket_kth_largest(count_ge, kprime, guess):
    def canon(x):
        return jnp.where(jnp.logical_and(x >= KEY_SUBNORMAL_LO, x <= KEY_SUBNORMAL_HI), 0, x)

    def succ(x):
        xc = canon(x)
        return jnp.where(xc == 0, KEY_SUBNORMAL_HI + 1, canon(xc + 1))

    g0 = jnp.clip(guess, KEY_NEG_INF, KEY_POS_INF - 1)
    g1 = succ(g0)
    lo = jnp.full(guess.shape, KEY_NEG_INF, I32)
    hi = jnp.full(guess.shape, KEY_POS_INF, I32)

    def narrow(lo, hi, x, cnt):
        ok = cnt >= kprime
        return jnp.where(ok, jnp.maximum(lo, x), lo), jnp.where(ok, hi, jnp.minimum(hi, x))

    c0, c1 = count_ge([_key_to_float(g0), _key_to_float(g1)])
    lo, hi = narrow(lo, hi, g0, c0)
    lo, hi = narrow(lo, hi, g1, c1)

    def open_bracket(c):
        return jnp.max(jnp.where(canon(c[1]) > succ(c[0]), 1.0, 0.0)) > 0.5

    def bisect(c):
        lo, hi = c
        mid = jnp.where(canon(hi) > succ(lo), (lo >> 1) + (hi >> 1) + (lo & hi & 1), lo)
        return narrow(lo, hi, mid, count_ge([_key_to_float(mid)])[0])

    lo, _ = lax.while_loop(open_bracket, bisect, (lo, hi))
    return canon(lo)


def _bit_transpose32(words):
    a = list(words)
    j, m = 16, 0x0000FFFF
    while j:
        k = 0
        while k < 32:
            t = (a[k] ^ lax.shift_right_logical(a[k + j], jnp.int32(j))) & jnp.int32(m)
            a[k] = a[k] ^ t
            a[k + j] = a[k + j] ^ lax.shift_left(t, jnp.int32(j))
            k = (k + j + 1) & ~j
        j >>= 1
        m ^= (m << j) & 0xFFFFFFFF
    return a


def _bias_from_distance(relb_ref, head, dist):
    val = jnp.full(dist.shape, relb_ref[0, head], F32)
    for b in range(1, NUM_BUCKETS):
        val = jnp.where(dist >= BUCKET_BOUNDS[b], relb_ref[b, head], val)
    return val * LOG2E


def _attn_proj_kernel(x_ref, g_ref, w_ref, qg_ref, kg_ref, *outs, transposed):
    x = x_ref[...]
    h = _rms_rows(x) * g_ref[...]
    z = jnp.dot(h.astype(BF16), w_ref[...], preferred_element_type=F32)
    qg = qg_ref[...]
    kg = kg_ref[...]
    k = jnp.concatenate(
        [_rms_rows(z[:, K_OFF + g * HEAD_DIM:K_OFF + (g + 1) * HEAD_DIM]) * kg for g in range(ATT_KV_HEADS)],
        axis=1)
    v = z[:, V_OFF:QI_OFF]
    qi = z[:, QI_OFF:KI_OFF]
    zl = z[:, KI_OFF:ATT_IN_PAD]
    ki = zl[:, :IDX_DIM]
    if transposed:
        qT_ref, qiT_ref, wiT_ref, vT_ref, kb_ref, kib_ref, k_ref, v_ref, ki_ref = outs
        for hh in range(ATT_HEADS):
            qh = _rms_rows(z[:, hh * HEAD_DIM:(hh + 1) * HEAD_DIM]) * qg * Q_SCALE
            qT_ref[hh * HEAD_DIM:(hh + 1) * HEAD_DIM, :] = qh.T.astype(BF16)
        for c in range(IDX_HEADS * IDX_DIM // LANES):
            qiT_ref[c * LANES:(c + 1) * LANES, :] = qi[:, c * LANES:(c + 1) * LANES].T.astype(BF16)
        wiT_ref[...] = zl.T[IDX_DIM:IDX_DIM + 8, :]
        for g in range(ATT_KV_HEADS):
            vT = v[:, g * HEAD_DIM:(g + 1) * HEAD_DIM].T
            for c in range(x.shape[0] // PAGE):
                vT_ref[c, g * HEAD_DIM:(g + 1) * HEAD_DIM, :] = vT[:, c * PAGE:(c + 1) * PAGE].astype(BF16)
        kb_ref[...] = k.astype(BF16)
        kib_ref[...] = ki.astype(BF16)
    else:
        q_ref, qi_ref, zl_ref, k_ref, v_ref, ki_ref = outs
        for hh in range(ATT_HEADS):
            q_ref[:, hh * HEAD_DIM:(hh + 1) * HEAD_DIM] = _rms_rows(z[:, hh * HEAD_DIM:(hh + 1) * HEAD_DIM]) * qg * Q_SCALE
        qi_ref[...] = qi
        zl_ref[...] = zl
    for g in range(ATT_KV_HEADS):
        rows_g = pl.ds(g, x.shape[0], stride=ATT_KV_HEADS)
        k_ref[rows_g, :] = k[:, g * HEAD_DIM:(g + 1) * HEAD_DIM]
        v_ref[rows_g, :] = v[:, g * HEAD_DIM:(g + 1) * HEAD_DIM]
    ki_ref[...] = ki


def _attn_proj(x, g, w, qg, kg, *, transposed):
    rows = x.shape[0]
    tm = ROW_TILE
    nt = rows // tm
    kvw = ATT_KV_HEADS * HEAD_DIM
    row_spec = lambda width: pl.BlockSpec((tm, width), lambda i: (i, 0))
    col_spec = lambda height: pl.BlockSpec((height, tm), lambda i: (0, i))
    kv_rows = ATT_KV_HEADS * rows
    kv_spec = pl.BlockSpec((ATT_KV_HEADS * tm, HEAD_DIM), lambda i: (i, 0))
    leaves_shape = [jax.ShapeDtypeStruct((kv_rows, HEAD_DIM), F32), jax.ShapeDtypeStruct((kv_rows, HEAD_DIM), F32),
                    jax.ShapeDtypeStruct((rows, IDX_DIM), F32)]
    leaves_spec = [kv_spec, kv_spec, row_spec(IDX_DIM)]
    if transposed:
        out_shape = [jax.ShapeDtypeStruct((ATT_HEADS * HEAD_DIM, rows), BF16),
                     jax.ShapeDtypeStruct((IDX_HEADS * IDX_DIM, rows), BF16),
                     jax.ShapeDtypeStruct((8, rows), F32),
                     jax.ShapeDtypeStruct((rows // PAGE, kvw, PAGE), BF16),
                     jax.ShapeDtypeStruct((rows, kvw), BF16),
                     jax.ShapeDtypeStruct((rows, IDX_DIM), BF16)] + leaves_shape
        out_specs = [col_spec(ATT_HEADS * HEAD_DIM), col_spec(IDX_HEADS * IDX_DIM), col_spec(8),
                     pl.BlockSpec((tm // PAGE, kvw, PAGE), lambda i: (i, 0, 0)),
                     row_spec(kvw), row_spec(IDX_DIM)] + leaves_spec
    else:
        out_shape = [jax.ShapeDtypeStruct((rows, ATT_HEADS * HEAD_DIM), F32),
                     jax.ShapeDtypeStruct((rows, IDX_HEADS * IDX_DIM), F32),
                     jax.ShapeDtypeStruct((rows, LANES), F32)] + leaves_shape
        out_specs = [row_spec(ATT_HEADS * HEAD_DIM), row_spec(IDX_HEADS * IDX_DIM), row_spec(LANES)] + leaves_spec
    return pl.pallas_call(
        functools.partial(_attn_proj_kernel, transposed=transposed),
        grid=(nt,),
        in_specs=[row_spec(D_MODEL), _const_spec((1, D_MODEL)), _const_spec((D_MODEL, ATT_IN_PAD)),
                  _const_spec((1, HEAD_DIM)), _const_spec((1, HEAD_DIM))],
        out_specs=out_specs,
        out_shape=out_shape,
        compiler_params=_params(1),
        name="attn_proj_t" if transposed else "attn_proj_r",
    )(x, g, w, qg, kg)


PROMPT_SEQS = 2


def _dsa_prompt_kernel(relb_ref, *refs, topk):
    seqs = range(PROMPT_SEQS)
    n_in = 6
    qT_refs, qiT_refs, wiT_refs, k_refs, vT_refs, ki_refs = zip(*[refs[n_in * u:n_in * (u + 1)] for u in seqs])
    o_ref, keys_sc, scores_sc, planes_sc, live_sc, mb_sc, bias_sc, logit_sc, acc_sc = refs[n_in * PROMPT_SEQS:]
    b = pl.program_id(0)
    i = pl.program_id(1)
    srow = lax.broadcasted_iota(I32, (PAGE, PAGE), 0)
    tcol = lax.broadcasted_iota(I32, (PAGE, PAGE), 1)

    @pl.when((b == 0) & (i == 0))
    def _init_bias():
        planes_sc[...] = jnp.zeros_like(planes_sc)
        live_sc[...] = jnp.zeros_like(live_sc)
        for hh in range(ATT_HEADS):
            for rel in range(2):
                dist = jnp.maximum(rel * PAGE + tcol - srow, 0)
                bias_sc[hh, rel] = _bias_from_distance(relb_ref, hh, dist)
            bias_sc[hh, 2] = jnp.full((PAGE, PAGE), relb_ref[NUM_BUCKETS - 1, hh] * LOG2E, F32)

    wiT = [r[...] for r in wiT_refs]
    qi_cat = [jnp.concatenate([r[hh * IDX_DIM:(hh + 1) * IDX_DIM, :] for hh in range(IDX_HEADS)], axis=1)
              for r in qiT_refs]

    npair = (i + 2) // 2

    def score_block(u, j):
        off = pl.multiple_of(j * PAGE, PAGE)
        kij = ki_refs[u][pl.ds(off, PAGE), :]
        idx = jnp.dot(kij, qi_cat[u], preferred_element_type=F32)
        sc = jnp.zeros((PAGE, PAGE), F32)
        for hh in range(IDX_HEADS):
            sc = sc + wiT[u][hh:hh + 1, :] * jnp.maximum(idx[:, hh * PAGE:(hh + 1) * PAGE], 0.0)
        sc = _canonical_zero(sc)
        causal = jnp.logical_or(j < i, jnp.logical_and(j == i, srow <= tcol))
        scores_sc[u, j] = jnp.where(causal, sc, -jnp.inf)
        keys_sc[u, j] = jnp.where(causal, _sortable_key(sc), INT_MIN)

    nquad = (i + 4) // 4

    def score_body(jj, carry):
        for blk in range(4):
            for u in seqs:
                score_block(u, 4 * jj + blk)
        return carry

    lax.fori_loop(0, nquad, score_body, 0)

    lane = lax.broadcasted_iota(I32, (1, PAGE), 1)
    kprime = jnp.minimum(topk, i * PAGE + lane + 1).astype(F32)

    def plane_body(jj, carry):
        for u in seqs:
            rows = [keys_sc[u, 2 * jj + blk, k * SUBLANES:(k + 1) * SUBLANES, :] ^ INT_MIN
                    for blk in range(2) for k in range(PAGE // SUBLANES)]
            planes = _bit_transpose32(rows)
            live = planes[0]
            for p in range(32):
                planes_sc[u, jj, p] = planes[p]
                live = live | planes[p]
            live_sc[u, jj] = live
        return carry

    lax.fori_loop(0, npair, plane_body, 0)

    n_pairs_max = keys_sc.shape[1] // 2

    def bit_step(u, bi, state):
        thr_u, alive, above = state
        cnt = jnp.zeros((SUBLANES, PAGE), I32)
        with_bit, at_least = [], []
        for p in range(n_pairs_max):
            t = alive[p] & planes_sc[u, p, bi]
            ge = above[p] | t
            cnt = cnt + lax.population_count(ge)
            with_bit.append(t)
            at_least.append(ge)
        take = jnp.sum(cnt.astype(F32), axis=0, keepdims=True) >= kprime
        alive = tuple(jnp.where(take, t, a ^ t) for t, a in zip(with_bit, alive))
        above = tuple(jnp.where(take, g, ge) for g, ge in zip(above, at_least))
        thr_u = thr_u | jnp.where(take, lax.shift_left(jnp.int32(1), 31 - bi), 0)
        return thr_u, alive, above

    zero_words = jnp.zeros((SUBLANES, PAGE), I32)
    states = lax.fori_loop(
        0, 32, lambda bi, states: tuple(bit_step(u, bi, states[u]) for u in seqs),
        tuple((jnp.zeros((1, PAGE), I32),
               tuple(jnp.where(p < npair, live_sc[u, p], zero_words) for p in range(n_pairs_max)),
               (zero_words,) * n_pairs_max) for u in seqs))
    def count_scores(u, preds):
        def body(jj, cnts):
            for blk in range(2):
                sc = scores_sc[u, 2 * jj + blk]
                cnts = tuple(c + jnp.where(p(sc), 1.0, 0.0) for c, p in zip(cnts, preds))
            return cnts
        cnts = lax.fori_loop(0, npair, body, tuple(jnp.zeros((PAGE, PAGE), F32) for _ in preds))
        return [jnp.sum(c, axis=0, keepdims=True) for c in cnts]

    def count_ge(u):
        return lambda thresholds: count_scores(u, [lambda sc, x=x: sc >= x for x in thresholds])

    thr = [_key_to_float(_bracket_kth_largest(count_ge(u), kprime, states[u][0] ^ INT_MIN)) for u in seqs]
    need = [kprime - count_scores(u, [lambda sc, u=u: sc > thr[u]])[0] for u in seqs]

    tri = jnp.where(srow > tcol, 1.0, 0.0).astype(BF16)

    def mask_block(u, j, carry):
        sc = scores_sc[u, j]
        eqf = jnp.where(sc == thr[u], 1.0, 0.0)
        rank = jnp.dot(tri, eqf.astype(BF16), preferred_element_type=F32) + carry
        tie = jnp.where(jnp.logical_and(sc == thr[u], rank < need[u]), 0.0, NEG)
        mb_sc[u, j] = jnp.where(sc > thr[u], 0.0, tie)
        return carry + jnp.sum(eqf, axis=0, keepdims=True)

    def mask_body(jj, carry):
        for blk in range(4):
            carry = tuple(mask_block(u, 4 * jj + blk, carry[u]) for u in seqs)
        return carry

    lax.fori_loop(0, nquad, mask_body, tuple(jnp.zeros((1, PAGE), F32) for _ in seqs))

    groups = range(ATT_KV_HEADS)
    units = [(u, g) for u in seqs for g in groups]
    width = ATT_GROUP * PAGE
    qcats = [jnp.concatenate(
        [qT_refs[u][(ATT_GROUP * g + r) * HEAD_DIM:(ATT_GROUP * g + r + 1) * HEAD_DIM, :]
         for r in range(ATT_GROUP)], axis=1) for u, g in units]

    def logit_block(j, ms):
        off = pl.multiple_of(j * PAGE, PAGE)
        relc = jnp.clip(i - j, 0, 2)
        out = []
        for n, (u, g) in enumerate(units):
            mb = mb_sc[u, j]
            kj = k_refs[u][pl.ds(off, PAGE), g * HEAD_DIM:(g + 1) * HEAD_DIM]
            logits = jnp.dot(kj, qcats[n], preferred_element_type=F32)
            logits = jnp.concatenate(
                [logits[:, r * PAGE:(r + 1) * PAGE] + (mb + bias_sc[ATT_GROUP * g + r, relc])
                 for r in range(ATT_GROUP)], axis=1)
            logit_sc[u, g, j] = logits
            out.append(jnp.maximum(ms[n], jnp.max(logits, axis=0, keepdims=True)))
        return tuple(out)

    def over_block_pairs(pair_step, carry):
        nfull = npair // 2
        carry = lax.fori_loop(0, nfull, lambda t, c: pair_step(4 * t + 2, pair_step(4 * t, c)), carry)
        return lax.cond(npair % 2 == 1, lambda c: pair_step(4 * nfull, c), lambda c: c, carry)

    ms = over_block_pairs(lambda j0, c: logit_block(j0 + 1, logit_block(j0, c)),
                          tuple(jnp.full((1, width), NEG, F32) for _ in units))

    ones_rows = jnp.ones((ONES_ROWS, PAGE), BF16)

    def value_pair(j0, carry):
        for n, (u, g) in enumerate(units):
            p = jnp.concatenate(
                [jnp.exp2(logit_sc[u, g, j0 + blk] - ms[n]).astype(BF16) for blk in range(2)], axis=0)
            vT2 = jnp.concatenate(
                [jnp.concatenate([vT_refs[u][j0 + blk, g * HEAD_DIM:(g + 1) * HEAD_DIM, :], ones_rows], axis=0)
                 for blk in range(2)], axis=1)
            acc_sc[u, g] = acc_sc[u, g] + jnp.dot(vT2, p, preferred_element_type=F32)
        return carry

    acc_sc[...] = jnp.zeros_like(acc_sc)
    over_block_pairs(value_pair, 0)
    for u, g in units:
        acc = acc_sc[u, g]
        outT = acc[:HEAD_DIM] * (1.0 / acc[HEAD_DIM:HEAD_DIM + 1])
        for r in range(ATT_GROUP):
            hh = ATT_GROUP * g + r
            o_ref[u, :, hh * HEAD_DIM:(hh + 1) * HEAD_DIM] = outT[:, r * PAGE:(r + 1) * PAGE].T.astype(BF16)


def _dsa_prompt(rel_bias, qT, qiT, wiT, kb, vT3, kib, *, batch, seq):
    nb = seq // PAGE
    rows = batch * seq
    kvw = ATT_KV_HEADS * HEAD_DIM
    ns = PROMPT_SEQS

    def seq_specs(u):
        qcol = lambda height: pl.BlockSpec((height, PAGE), lambda b, i: (0, (ns * b + u) * nb + i))
        return [qcol(ATT_HEADS * HEAD_DIM), qcol(IDX_HEADS * IDX_DIM), qcol(8),
                pl.BlockSpec((seq, kvw), lambda b, i: (ns * b + u, 0)),
                pl.BlockSpec((nb, kvw, PAGE), lambda b, i: (ns * b + u, 0, 0)),
                pl.BlockSpec((seq, IDX_DIM), lambda b, i: (ns * b + u, 0))]

    out = pl.pallas_call(
        functools.partial(_dsa_prompt_kernel, topk=min(TOPK_MAX, seq // 4)),
        grid=(batch // ns, nb),
        in_specs=[pl.BlockSpec(memory_space=pltpu.SMEM)] + [s for u in range(ns) for s in seq_specs(u)],
        out_specs=pl.BlockSpec((ns, PAGE, ATT_HEADS * HEAD_DIM), lambda b, i: (b, i, 0)),
        out_shape=jax.ShapeDtypeStruct((batch, seq, ATT_HEADS * HEAD_DIM), BF16),
        scratch_shapes=[pltpu.VMEM((ns, nb, PAGE, PAGE), I32), pltpu.VMEM((ns, nb, PAGE, PAGE), F32),
                        pltpu.VMEM((ns, nb // 2, 32, SUBLANES, PAGE), I32),
                        pltpu.VMEM((ns, nb // 2, SUBLANES, PAGE), I32),
                        pltpu.VMEM((ns, nb, PAGE, PAGE), F32),
                        pltpu.VMEM((ATT_HEADS, 3, PAGE, PAGE), F32),
                        pltpu.VMEM((ns, ATT_KV_HEADS, nb, PAGE, ATT_GROUP * PAGE), F32),
                        pltpu.VMEM((ns, ATT_KV_HEADS, HEAD_DIM + ONES_ROWS, ATT_GROUP * PAGE), F32)],
        compiler_params=_params(2),
        name="dsa_prompt",
    )(rel_bias, *([qT, qiT, wiT, kb, vT3, kib] * ns))
    return out.reshape(rows, ATT_HEADS * HEAD_DIM)


TQ = 8
ATT_SAMPLES = 2
SEL_SAMPLES = 16


def _dsa_select_kernel(pt_ref, qi_ref, zl_ref, cki_hbm, mb_ref, sc_sc, kibuf, sems, *, n_pages, n_new):
    ns = SEL_SAMPLES
    n_fetch = ns * n_pages
    b = pl.program_id(0)
    slot = b % 2

    def page_copies(step, slot):
        return [pltpu.make_async_copy(cki_hbm.at[pt_ref[step * n_fetch + f]], kibuf.at[slot, f], sems.at[slot])
                for f in range(n_fetch)]

    @pl.when(b == 0)
    def _start_first():
        for c in page_copies(0, 0):
            c.start()

    @pl.when(b + 1 < pl.num_programs(0))
    def _start_next():
        for c in page_copies(b + 1, 1 - slot):
            c.start()

    for c in page_copies(b, slot):
        c.wait()
    ki_pages = [kibuf.at[slot, f] for f in range(n_fetch)]
    n_blk = n_pages + 1
    rows = ns * TQ
    trow = lax.broadcasted_iota(I32, (TQ, PAGE), 0)
    scol = lax.broadcasted_iota(I32, (TQ, PAGE), 1)
    pad_rows = lambda a: jnp.concatenate([a, jnp.zeros((PAGE - TQ, a.shape[1]), a.dtype)], axis=0)
    nt = (((1,), (1,)), ((), ()))

    for s in range(ns):
        zl = zl_ref[s]
        qi = qi_ref[s]
        for p in range(n_blk):
            if p == n_pages:
                ki_new = pad_rows(zl[:, :IDX_DIM]).astype(BF16)
                idx = lax.dot_general(qi, ki_new, nt, preferred_element_type=F32)
            else:
                idx = jnp.dot(qi, ki_pages[s * n_pages + p][...].astype(BF16), preferred_element_type=F32)
            sc = jnp.zeros((TQ, PAGE), F32)
            for hh in range(IDX_HEADS):
                sc = sc + zl[:, IDX_DIM + hh:IDX_DIM + hh + 1] * jnp.maximum(idx[hh * TQ:(hh + 1) * TQ], 0.0)
            sc = _canonical_zero(sc)
            if p == n_pages:
                sc = jnp.where(jnp.logical_and(scol <= trow, scol < n_new), sc, -jnp.inf)
            sc_sc[s * TQ:(s + 1) * TQ, p * PAGE:(p + 1) * PAGE] = sc

    kprime = float(min(TOPK_MAX, (n_pages * PAGE + n_new) // 4))

    def count(pred_of_scores):
        return jnp.sum(jnp.where(pred_of_scores(sc_sc[...]), 1.0, 0.0), axis=1, keepdims=True)

    cand0 = jnp.zeros((rows, 1), I32)
    thr0 = jnp.where(count(lambda sc: sc >= _key_to_float(cand0)) >= kprime, cand0, INT_MIN)

    def bit_body(bi, thr_key):
        cand = thr_key | lax.shift_left(jnp.int32(1), 30 - bi)
        return jnp.where(count(lambda sc: sc >= _key_to_float(cand)) >= kprime, cand, thr_key)

    thr = _key_to_float(lax.fori_loop(0, 31, bit_body, thr0))
    need = kprime - count(lambda sc: sc > thr)

    s_r = lax.broadcasted_iota(I32, (PAGE, PAGE), 0)
    s_c = lax.broadcasted_iota(I32, (PAGE, PAGE), 1)
    triu = jnp.where(s_r < s_c, 1.0, 0.0).astype(BF16)
    carry = jnp.zeros((rows, 1), F32)
    for p in range(n_blk):
        sc = sc_sc[:, p * PAGE:(p + 1) * PAGE]
        eqf = jnp.where(sc == thr, 1.0, 0.0)
        rank = jnp.dot(eqf.astype(BF16), triu, preferred_element_type=F32) + carry
        tie = jnp.where(jnp.logical_and(sc == thr, rank < need), 0.0, NEG)
        mb_ref[:, p * PAGE:(p + 1) * PAGE] = jnp.where(sc > thr, 0.0, tie)
        carry = carry + jnp.sum(eqf, axis=1, keepdims=True)


def _dsa_select(page_table, qi8, zl8, cache_ki, *, n_new):
    nsamp, n_pages = page_table.shape
    ns = SEL_SAMPLES
    width = (n_pages + 1) * PAGE

    in_specs = [pl.BlockSpec((ns, IDX_HEADS * TQ, IDX_DIM), lambda b, pt: (b, 0, 0)),
                pl.BlockSpec((ns, TQ, LANES), lambda b, pt: (b, 0, 0)),
                pl.BlockSpec(memory_space=pl.ANY)]
    grid_spec = pltpu.PrefetchScalarGridSpec(
        num_scalar_prefetch=1, grid=(nsamp // ns,), in_specs=in_specs,
        out_specs=pl.BlockSpec((ns * TQ, width), lambda b, pt: (b, 0)),
        scratch_shapes=[pltpu.VMEM((ns * TQ, width), F32),
                        pltpu.VMEM((2, ns * n_pages, IDX_DIM, PAGE), F32), pltpu.SemaphoreType.DMA((2,))])
    return pl.pallas_call(
        functools.partial(_dsa_select_kernel, n_pages=n_pages, n_new=n_new),
        grid_spec=grid_spec,
        out_shape=jax.ShapeDtypeStruct((nsamp * TQ, width), F32),
        compiler_params=_params(1),
        name="dsa_select",
    )(page_table.reshape(-1), qi8, zl8, cache_ki)


def _dsa_sample_kernel(pt_ref, relb_ref, q_ref, mb_ref, kn_ref, vn_ref, ck_hbm, cv_hbm, o_ref,
                       bias_sc, kbuf, vbuf, sems, *, n_pages):
    past = n_pages * PAGE
    rows_g = ATT_GROUP * TQ
    page_rows = PAGE * ATT_KV_HEADS
    ns = ATT_SAMPLES
    b = pl.program_id(0)
    n_slots = kbuf.shape[0]
    ahead = n_slots - 1
    slot = b % n_slots

    def page_copies(step, slot):
        copies = []
        for f in range(ns * n_pages):
            src = pl.ds(pl.multiple_of(pt_ref[step * ns * n_pages + f] * page_rows, page_rows), page_rows)
            dst = pl.ds(f * page_rows, page_rows)
            copies.append(pltpu.make_async_copy(ck_hbm.at[src, :], kbuf.at[slot, dst, :], sems.at[slot, 0]))
            copies.append(pltpu.make_async_copy(cv_hbm.at[src, :], vbuf.at[slot, dst, :], sems.at[slot, 1]))
        return copies

    @pl.when(b == 0)
    def _start_first():
        for s in range(ahead):
            for c in page_copies(s, s):
                c.start()

    @pl.when(b + ahead < pl.num_programs(0))
    def _start_next():
        for c in page_copies(b + ahead, (b + ahead) % n_slots):
            c.start()

    for c in page_copies(b, slot):
        c.wait()

    trow = lax.broadcasted_iota(I32, (TQ, PAGE), 0)
    scol = lax.broadcasted_iota(I32, (TQ, PAGE), 1)

    @pl.when(pl.program_id(0) == 0)
    def _init_bias():
        for g in range(ATT_KV_HEADS):
            for r in range(ATT_GROUP):
                hh = ATT_GROUP * g + r
                rs = slice(r * TQ, (r + 1) * TQ)
                far = jnp.full((TQ, PAGE), relb_ref[NUM_BUCKETS - 1, hh] * LOG2E, F32)
                for p in range(n_pages - 1):
                    bias_sc[g, rs, p * PAGE:(p + 1) * PAGE] = far
                bias_sc[g, rs, past - PAGE:past] = _bias_from_distance(
                    relb_ref, hh, jnp.maximum(PAGE + trow - scol, 0))
                bias_sc[g, rs, past:past + PAGE] = _bias_from_distance(relb_ref, hh, jnp.maximum(trow - scol, 0))

    nt = (((1,), (1,)), ((), ()))
    pad_rows = lambda a: jnp.concatenate([a, jnp.zeros((PAGE - TQ, a.shape[1]), a.dtype)], axis=0)
    units = [(s, g) for s in range(ns) for g in range(ATT_KV_HEADS)]
    for s, g in units:
        gs = slice(g * HEAD_DIM, (g + 1) * HEAD_DIM)
        head_rows = pl.ds(s * n_pages * page_rows + g, past, stride=ATT_KV_HEADS)
        k_all = jnp.concatenate([kbuf[slot, head_rows, :].astype(BF16), pad_rows(kn_ref[s, :, gs]).astype(BF16)], axis=0)
        v_all = jnp.concatenate([vbuf[slot, head_rows, :].astype(BF16), pad_rows(vn_ref[s, :, gs]).astype(BF16)], axis=0)
        mb = jnp.concatenate([mb_ref[s * TQ:(s + 1) * TQ, :]] * ATT_GROUP, axis=0)
        qg = q_ref[s, g * rows_g:(g + 1) * rows_g, :]
        logits = lax.dot_general(qg, k_all, nt, preferred_element_type=F32)
        logits = logits + (mb + bias_sc[g])
        m = jnp.max(logits, axis=1, keepdims=True)
        pr = jnp.exp2(logits - m)
        l = jnp.sum(pr, axis=1, keepdims=True)
        acc = jnp.dot(pr.astype(BF16), v_all, preferred_element_type=F32)
        o_ref[s, g * rows_g:(g + 1) * rows_g, :] = (acc * (1.0 / l)).astype(BF16)


def _dsa_sample(page_table, rel_bias, q8, mb, kn8, vn8, cache_k, cache_v):
    nsamp, n_pages = page_table.shape
    kvw = ATT_KV_HEADS * HEAD_DIM
    ns = ATT_SAMPLES
    per_step = lambda *tail: pl.BlockSpec((ns,) + tail, lambda b, pt: (b,) + (0,) * len(tail))

    n_slots = 3
    page_buf = pltpu.VMEM((n_slots, ns * n_pages * PAGE * ATT_KV_HEADS, HEAD_DIM), F32)
    in_specs = [pl.BlockSpec(memory_space=pltpu.SMEM),
                per_step(ATT_HEADS * TQ, HEAD_DIM),
                pl.BlockSpec((ns * TQ, mb.shape[1]), lambda b, pt: (b, 0)),
                per_step(TQ, kvw), per_step(TQ, kvw),
                pl.BlockSpec(memory_space=pl.ANY), pl.BlockSpec(memory_space=pl.ANY)]
    grid_spec = pltpu.PrefetchScalarGridSpec(
        num_scalar_prefetch=1, grid=(nsamp // ns,), in_specs=in_specs,
        out_specs=per_step(ATT_HEADS * TQ, HEAD_DIM),
        scratch_shapes=[pltpu.VMEM((ATT_KV_HEADS, ATT_GROUP * TQ, mb.shape[1]), F32),
                        page_buf, page_buf, pltpu.SemaphoreType.DMA((n_slots, 2))])
    return pl.pallas_call(
        functools.partial(_dsa_sample_kernel, n_pages=n_pages),
        grid_spec=grid_spec,
        out_shape=jax.ShapeDtypeStruct((nsamp, ATT_HEADS * TQ, HEAD_DIM), BF16),
        compiler_params=_params(1),
        name="dsa_sample",
    )(page_table.reshape(-1), rel_bias, q8, mb, kn8, vn8, cache_k, cache_v)


def _post_mix_kernel(x_ref, o_ref, wo_ref, g_ref, w1_ref, w2_ref, y_ref):
    y = x_ref[...] + jnp.dot(o_ref[...], wo_ref[...], preferred_element_type=F32)
    h = (_rms_rows(y) * g_ref[...]).astype(BF16)
    acc = y
    for c in range(D_FF // D_MODEL):
        a = jnp.dot(h, w1_ref[:, c * D_MODEL:(c + 1) * D_MODEL], preferred_element_type=F32)
        a = jnp.square(jnp.maximum(a, 0.0)).astype(BF16)
        acc = acc + jnp.dot(a, w2_ref[c * D_MODEL:(c + 1) * D_MODEL, :], preferred_element_type=F32)
    y_ref[...] = acc


def _post_mix(x, o, wo, g, w1, w2):
    rows = x.shape[0]
    tm = ROW_TILE
    ko = o.shape[1]
    return pl.pallas_call(
        _post_mix_kernel,
        grid=(rows // tm,),
        in_specs=[pl.BlockSpec((tm, D_MODEL), lambda i: (i, 0)), pl.BlockSpec((tm, ko), lambda i: (i, 0)),
                  _const_spec((ko, D_MODEL)), _const_spec((1, D_MODEL)),
                  _const_spec((D_MODEL, D_FF)), _const_spec((D_FF, D_MODEL))],
        out_specs=pl.BlockSpec((tm, D_MODEL), lambda i: (i, 0)),
        out_shape=jax.ShapeDtypeStruct((rows, D_MODEL), F32),
        compiler_params=_params(1),
        name="post_mix",
    )(x, o, wo, g, w1, w2)


def _ret_proj_kernel(x_ref, g_ref, w_ref, cos_ref, sin_ref, q_ref, k_ref, v_ref, gate_ref):
    hb = (_rms_rows(x_ref[...]) * g_ref[...]).astype(BF16)
    cos = cos_ref[...]
    sin = sin_ref[...]
    half = RET_KEY // 2
    qk = RET_HEADS * RET_KEY
    vd = RET_HEADS * RET_VAL

    def rotated(col0, scale):
        z = jnp.dot(hb, w_ref[:, col0:col0 + RET_KEY], preferred_element_type=F32)
        x1, x2 = z[:, :half], z[:, half:]
        return (x1 * cos - x2 * sin) * scale, (x1 * sin + x2 * cos) * scale

    for hh in range(RET_HEADS):
        a, b = rotated(hh * RET_KEY, 1.0)
        q_ref[:, hh * RET_KEY:hh * RET_KEY + half] = a.astype(BF16)
        q_ref[:, hh * RET_KEY + half:(hh + 1) * RET_KEY] = b.astype(BF16)
        a, b = rotated(qk + hh * RET_KEY, RET_KSCALE)
        k_ref[:, hh * RET_KEY:hh * RET_KEY + half] = a.astype(BF16)
        k_ref[:, hh * RET_KEY + half:(hh + 1) * RET_KEY] = b.astype(BF16)
    for hh in range(RET_HEADS):
        cs = slice(hh * RET_VAL, (hh + 1) * RET_VAL)
        v_ref[:, cs] = jnp.dot(hb, w_ref[:, 2 * qk + hh * RET_VAL:2 * qk + (hh + 1) * RET_VAL],
                               preferred_element_type=F32).astype(BF16)
        gate_ref[:, cs] = jnp.dot(hb, w_ref[:, 2 * qk + vd + hh * RET_VAL:2 * qk + vd + (hh + 1) * RET_VAL],
                                  preferred_element_type=F32).astype(BF16)


def _ret_proj(x, g, w, cos, sin):
    rows = x.shape[0]
    tm = ROW_TILE
    qk = RET_HEADS * RET_KEY
    vd = RET_HEADS * RET_VAL
    n_pos_tiles = cos.shape[0] // tm
    row_spec = lambda width: pl.BlockSpec((tm, width), lambda i: (i, 0))
    pos_spec = pl.BlockSpec((tm, RET_KEY // 2), lambda i: (i % n_pos_tiles, 0))
    return pl.pallas_call(
        _ret_proj_kernel,
        grid=(rows // tm,),
        in_specs=[row_spec(D_MODEL), _const_spec((1, D_MODEL)), _const_spec((D_MODEL, 2 * qk + 2 * vd)),
                  pos_spec, pos_spec],
        out_specs=[row_spec(qk), row_spec(qk), row_spec(vd), row_spec(vd)],
        out_shape=[jax.ShapeDtypeStruct((rows, qk), BF16), jax.ShapeDtypeStruct((rows, qk), BF16),
                   jax.ShapeDtypeStruct((rows, vd), BF16), jax.ShapeDtypeStruct((rows, vd), BF16)],
        compiler_params=_params(1),
        name="ret_proj",
    )(x, g, w, cos, sin)


def _retention_kernel(cd_ref, q_ref, k_ref, v_ref, g_ref, dm_ref, qd_ref, kd_ref, *rest, has_state0):
    if has_state0:
        s0_ref, o_ref, s_ref = rest
    else:
        o_ref, s_ref = rest

    @pl.when(pl.program_id(1) == 0)
    def _init_state():
        s_ref[...] = s0_ref[...] if has_state0 else jnp.zeros_like(s_ref)

    nt = (((1,), (1,)), ((), ()))
    tn = (((0,), (0,)), ((), ()))
    units = [(bi, hh) for bi in range(q_ref.shape[0]) for hh in range(RET_HEADS)]
    ks = lambda hh: slice(hh * RET_KEY, (hh + 1) * RET_KEY)
    vs = lambda hh: slice(hh * RET_VAL, (hh + 1) * RET_VAL)
    att = [lax.dot_general(q_ref[bi, :, ks(hh)].astype(BF16), k_ref[bi, :, ks(hh)].astype(BF16), nt,
                           preferred_element_type=F32) * dm_ref[hh] for bi, hh in units]
    cross = [jnp.dot((q_ref[bi, :, ks(hh)] * qd_ref[hh]).astype(BF16), s_ref[bi, hh].astype(BF16),
                     preferred_element_type=F32) for bi, hh in units]
    for (bi, hh), a, x in zip(units, att, cross):
        o = jnp.dot(a.astype(BF16), v_ref[bi, :, vs(hh)], preferred_element_type=F32) + x
        gate = g_ref[bi, :, vs(hh)].astype(F32)
        o_ref[bi, :, vs(hh)] = (_rms_rows(o) * (gate * jax.nn.sigmoid(gate))).astype(BF16)
    for bi, hh in units:
        kd = (k_ref[bi, :, ks(hh)] * kd_ref[hh]).astype(BF16)
        s_ref[bi, hh] = cd_ref[hh] * s_ref[bi, hh] + lax.dot_general(kd, v_ref[bi, :, vs(hh)], tn,
                                                                     preferred_element_type=F32)


def _retention(q, k, v, gate, consts, state0, *, seqs_per_step):
    dm, qd, kd, cd = consts
    nb, t, _ = q.shape
    c = dm.shape[1]
    bb = seqs_per_step
    qk = RET_HEADS * RET_KEY
    vd = RET_HEADS * RET_VAL
    tok = lambda width: pl.BlockSpec((bb, c, width), lambda b, ci: (b, ci, 0))
    st_spec = pl.BlockSpec((bb, RET_HEADS, RET_KEY, RET_VAL), lambda b, ci: (b, 0, 0, 0))
    in_specs = [pl.BlockSpec(memory_space=pltpu.SMEM), tok(qk), tok(qk), tok(vd), tok(vd),
                _const_spec(dm.shape), _const_spec(qd.shape), _const_spec(kd.shape)]
    args = [cd, q, k, v, gate, dm, qd, kd]
    if state0 is not None:
        in_specs.append(st_spec)
        args.append(state0)
    return pl.pallas_call(
        functools.partial(_retention_kernel, has_state0=state0 is not None),
        grid=(nb // bb, t // c),
        in_specs=in_specs,
        out_specs=[tok(vd), st_spec],
        out_shape=[jax.ShapeDtypeStruct((nb, t, vd), BF16),
                   jax.ShapeDtypeStruct((nb, RET_HEADS, RET_KEY, RET_VAL), F32)],
        compiler_params=_params(2),
        name="retention_s" if state0 is not None else "retention_p",
    )(*args)


def _retention_consts(c, c_pad):
    log_g = np.log1p(-np.exp2(-5.0 - np.arange(RET_HEADS, dtype=np.float64)))
    i = np.arange(c, dtype=np.float64)
    diff = i[:, None] - i[None, :]
    dm = np.where(diff >= 0, np.exp(log_g[:, None, None] * np.maximum(diff, 0.0)), 0.0)
    qd = np.exp(log_g[:, None] * (i[None, :] + 1.0))
    kd = np.exp(log_g[:, None] * (c - 1.0 - i[None, :]))
    cd = np.exp(log_g * c)
    pad = c_pad - c
    dm = np.pad(dm, ((0, 0), (0, pad), (0, pad)))
    qd = np.broadcast_to(np.pad(qd, ((0, 0), (0, pad)))[:, :, None], (RET_HEADS, c_pad, RET_KEY))
    kd = np.broadcast_to(np.pad(kd, ((0, 0), (0, pad)))[:, :, None], (RET_HEADS, c_pad, RET_KEY))
    return tuple(jnp.asarray(np.ascontiguousarray(a), F32) for a in (dm, qd, kd, cd))


def _rotation_tables(pos):
    half = RET_KEY // 2
    theta = 1.0 / (10000.0 ** np.linspace(0.0, 1.0, half, dtype=np.float32).astype(np.float64))
    ang = np.asarray(pos, np.float64)[:, None] * theta[None, :]
    return jnp.asarray(np.cos(ang), F32), jnp.asarray(np.sin(ang), F32)


def kernel(x_prompt, x_sample, cache_k, cache_v, cache_kidx, state_ret, page_table, rel_bias, ln_mix, ln_mlp,
           att_w_in, att_q_gain, att_k_gain, att_w_out, ret_w_in, ret_w_out, mlp_w_in, mlp_w_out):
    bp, tp, _ = x_prompt.shape
    bs, ts, _ = x_sample.shape
    n_phys = cache_k.shape[1]
    assert ln_mix.shape[0] == 2 and att_w_in.shape[0] == 1 and ret_w_in.shape[0] == 1
    assert tp % RET_CHUNK == 0 and ts <= TQ and cache_k.shape[2] == PAGE
    assert (tp // PAGE) % 4 == 0 and bs % SEL_SAMPLES == 0 and bs % ATT_SAMPLES == 0 and bp % PROMPT_SEQS == 0
    kvw = ATT_KV_HEADS * HEAD_DIM
    past = page_table.shape[1] * PAGE

    w_att = jnp.pad(att_w_in[0], ((0, 0), (0, ATT_IN_PAD - ATT_IN))).astype(BF16)
    w_att_out = att_w_out[0].astype(BF16)
    w_ret = ret_w_in[0].astype(BF16)
    w_ret_out = ret_w_out[0].astype(BF16)
    w1 = mlp_w_in.astype(BF16)
    w2 = mlp_w_out.astype(BF16)
    row = lambda a: a.reshape(1, -1)

    xp = x_prompt.reshape(bp * tp, D_MODEL)
    xs = x_sample.reshape(bs * ts, D_MODEL)

    qT, qiT, wiT, vT3, kb, kib, kp, vp, kip = _attn_proj(
        xp, row(ln_mix[0]), w_att, row(att_q_gain[0]), row(att_k_gain[0]), transposed=True)
    op = _dsa_prompt(rel_bias, qT, qiT, wiT, kb, vT3, kib, batch=bp, seq=tp)
    yp = _post_mix(xp, op, w_att_out, row(ln_mlp[0]), w1[0], w2[0])

    qs, qis, zls, ks, vs, kis = _attn_proj(
        xs, row(ln_mix[0]), w_att, row(att_q_gain[0]), row(att_k_gain[0]), transposed=False)
    pad_t = lambda a: jnp.pad(a, ((0, 0), (0, TQ - ts)) + ((0, 0),) * (a.ndim - 2))
    q8 = pad_t(qs.reshape(bs, ts, ATT_HEADS, HEAD_DIM)).transpose(0, 2, 1, 3)
    q8 = q8.reshape(bs, ATT_HEADS * TQ, HEAD_DIM).astype(BF16)
    qi8 = pad_t(qis.reshape(bs, ts, IDX_HEADS, IDX_DIM)).transpose(0, 2, 1, 3)
    qi8 = qi8.reshape(bs, IDX_HEADS * TQ, IDX_DIM).astype(BF16)
    zl8 = pad_t(zls.reshape(bs, ts, LANES))
    kn8 = pad_t(ks.reshape(bs, ts, kvw))
    vn8 = pad_t(vs.reshape(bs, ts, kvw))
    mb = _dsa_select(page_table, qi8, zl8, jnp.swapaxes(cache_kidx[0], 1, 2), n_new=ts)
    os8 = _dsa_sample(page_table, rel_bias, q8, mb, kn8, vn8,
                      cache_k[0].reshape(n_phys * PAGE * ATT_KV_HEADS, HEAD_DIM),
                      cache_v[0].reshape(n_phys * PAGE * ATT_KV_HEADS, HEAD_DIM))
    os_ = os8.reshape(bs, ATT_HEADS, TQ, HEAD_DIM)[:, :, :ts].transpose(0, 2, 1, 3)
    os_ = os_.reshape(bs * ts, ATT_HEADS * HEAD_DIM)
    ys = _post_mix(xs, os_, w_att_out, row(ln_mlp[0]), w1[0], w2[0])

    cos_p, sin_p = _rotation_tables(np.arange(tp))
    q, k, v, gate = _ret_proj(yp, row(ln_mix[1]), w_ret, cos_p, sin_p)
    r3 = lambda a: a.reshape(bp, tp, a.shape[-1])
    orp, state_p = _retention(r3(q), r3(k), r3(v), r3(gate), _retention_consts(RET_CHUNK, RET_CHUNK), None,
                            seqs_per_step=1)
    yp = _post_mix(yp, orp.reshape(bp * tp, -1), w_ret_out, row(ln_mlp[1]), w1[1], w2[1])

    cos_s, sin_s = _rotation_tables(np.tile(past + np.arange(ts), bs))
    q, k, v, gate = _ret_proj(ys, row(ln_mix[1]), w_ret, cos_s, sin_s)
    r3s = lambda a: pad_t(a.reshape(bs, ts, a.shape[-1]))
    ors, state_s = _retention(r3s(q), r3s(k), r3s(v), r3s(gate), _retention_consts(ts, TQ), state_ret[0],
                            seqs_per_step=2)
    ys = _post_mix(ys, ors[:, :ts].reshape(bs * ts, -1), w_ret_out, row(ln_mlp[1]), w1[1], w2[1])

    return (yp.reshape(bp, tp, D_MODEL), ys.reshape(bs, ts, D_MODEL),
            kp.reshape(1, bp, tp, ATT_KV_HEADS, HEAD_DIM), vp.reshape(1, bp, tp, ATT_KV_HEADS, HEAD_DIM),
            kip.reshape(1, bp, tp, IDX_DIM), state_p[None],
            ks.reshape(1, bs, ts, ATT_KV_HEADS, HEAD_DIM), vs.reshape(1, bs, ts, ATT_KV_HEADS, HEAD_DIM),
            kis.reshape(1, bs, ts, IDX_DIM), state_s[None])
```

```python
import functools
import math

import jax
import jax.numpy as jnp
import numpy as np
from jax import lax
from jax.experimental import pallas as pl
from jax.experimental.pallas import tpu as pltpu

F32 = jnp.float32
BF16 = jnp.bfloat16
I32 = jnp.int32

D_MODEL = 1024
PAGE = 128
ATT_HEADS = 8
ATT_KV_HEADS = 2
ATT_GROUP = ATT_HEADS // ATT_KV_HEADS
HEAD_DIM = 128
IDX_HEADS = 4
IDX_DIM = 64
TOPK_MAX = 256
NUM_BUCKETS = 32
MAX_DISTANCE = 128
RET_HEADS = 4
RET_KEY = 256
RET_VAL = 512
RET_CHUNK = 256
D_FF = 4 * D_MODEL
EPS = 1e-6

Q_OFF = 0
K_OFF = ATT_HEADS * HEAD_DIM
V_OFF = K_OFF + ATT_KV_HEADS * HEAD_DIM
QI_OFF = V_OFF + ATT_KV_HEADS * HEAD_DIM
KI_OFF = QI_OFF + IDX_HEADS * IDX_DIM
WI_OFF = KI_OFF + IDX_DIM
ATT_IN = WI_OFF + IDX_HEADS
ATT_IN_PAD = 1920

LANES = 128
SUBLANES = 8
ROW_TILE = 512
VMEM_LIMIT = 56 * 1024 * 1024

LOG2E = math.log2(math.e)
Q_SCALE = HEAD_DIM ** -0.5 * LOG2E
ONES_ROWS = 16
RET_KSCALE = RET_KEY ** -0.5
INT_MIN = -(2 ** 31)
NEG = -1e30


def _bucket_bounds():
    n = np.arange(0, 8192)
    max_exact = NUM_BUCKETS // 2
    val = np.log(np.maximum(n, 1) / max_exact) / math.log(MAX_DISTANCE / max_exact) * (NUM_BUCKETS - max_exact)
    frac = np.abs(val - np.round(val))
    risky = (n > max_exact) & (n < MAX_DISTANCE) & (frac < 1e-4)
    assert not risky.any()
    large = np.minimum(max_exact + np.floor(val + 1e-9).astype(np.int64), NUM_BUCKETS - 1)
    bucket = np.where(n < max_exact, n, large)
    assert (np.diff(bucket) >= 0).all() and (bucket[MAX_DISTANCE:] == NUM_BUCKETS - 1).all()
    return [int(np.argmax(bucket >= b)) for b in range(NUM_BUCKETS)]


BUCKET_BOUNDS = _bucket_bounds()


def _const_spec(shape):
    zeros = (0,) * len(shape)
    return pl.BlockSpec(shape, lambda *_: zeros, pipeline_mode=pl.Buffered(1))


def _params(n_axes):
    return pltpu.CompilerParams(dimension_semantics=("arbitrary",) * n_axes,
                                vmem_limit_bytes=VMEM_LIMIT)


def _rms_rows(a):
    return a * lax.rsqrt(jnp.mean(a * a, axis=-1, keepdims=True) + EPS)


def _canonical_zero(score):
    return jnp.where(score == 0.0, 0.0, score)


def _sortable_key(score):
    bits = pltpu.bitcast(score, I32)
    return bits ^ ((bits >> 31) & jnp.int32(0x7FFFFFFF))


def _key_to_float(key):
    return pltpu.bitcast(key ^ ((key >> 31) & jnp.int32(0x7FFFFFFF)), F32)


KEY_NEG_INF = -2139095041
KEY_POS_INF = 2139095040
KEY_SUBNORMAL_LO = -8388608
KEY_SUBNORMAL_HI = 8388607


def _bracket_kth_largest(count_ge, kprime, guess):
    def canon(x):
        return jnp.where(jnp.logical_and(x >= KEY_SUBNORMAL_LO, x <= KEY_SUBNORMAL_HI), 0, x)

    def succ(x):
        xc = canon(x)
        return jnp.where(xc == 0, KEY_SUBNORMAL_HI + 1, canon(xc + 1))

    g0 = jnp.clip(guess, KEY_NEG_INF, KEY_POS_INF - 1)
    g1 = succ(g0)
    lo = jnp.full(guess.shape, KEY_NEG_INF, I32)
    hi = jnp.full(guess.shape, KEY_POS_INF, I32)

    def narrow(lo, hi, x, cnt):
        ok = cnt >= kprime
        return jnp.where(ok, jnp.maximum(lo, x), lo), jnp.where(ok, hi, jnp.minimum(hi, x))

    c0, c1 = count_ge([_key_to_float(g0), _key_to_float(g1)])
    lo, hi = narrow(lo, hi, g0, c0)
    lo, hi = narrow(lo, hi, g1, c1)

    def open_bracket(c):
        return jnp.max(jnp.where(canon(c[1]) > succ(c[0]), 1.0, 0.0)) > 0.5

    def bisect(c):
        lo, hi = c
        mid = jnp.where(canon(hi) > succ(lo), (lo >> 1) + (hi >> 1) + (lo & hi & 1), lo)
        return narrow(lo, hi, mid, count_ge([_key_to_float(mid)])[0])

    lo, _ = lax.while_loop(open_bracket, bisect, (lo, hi))
    return canon(lo)


def _bit_transpose32(words):
    a = list(words)
    j, m = 16, 0x0000FFFF
    while j:
        k = 0
        while k < 32:
            t = (a[k] ^ lax.shift_right_logical(a[k + j], jnp.int32(j))) & jnp.int32(m)
            a[k] = a[k] ^ t
            a[k + j] = a[k + j] ^ lax.shift_left(t, jnp.int32(j))
            k = (k + j + 1) & ~j
        j >>= 1
        m ^= (m << j) & 0xFFFFFFFF
    return a


def _bias_from_distance(relb_ref, head, dist):
    val = jnp.full(dist.shape, relb_ref[0, head], F32)
    for b in range(1, NUM_BUCKETS):
        val = jnp.where(dist >= BUCKET_BOUNDS[b], relb_ref[b, head], val)
    return val * LOG2E


def _attn_proj_kernel(x_ref, g_ref, w_ref, qg_ref, kg_ref, *outs, transposed):
    x = x_ref[...]
    h = _rms_rows(x) * g_ref[...]
    z = lax.dot_general(h.astype(BF16), w_ref[...], (((1,), (1,)), ((), ())),
                        preferred_element_type=F32)
    qg = qg_ref[...]
    kg = kg_ref[...]
    k = jnp.concatenate(
        [_rms_rows(z[:, K_OFF + g * HEAD_DIM:K_OFF + (g + 1) * HEAD_DIM]) * kg for g in range(ATT_KV_HEADS)],
        axis=1)
    v = z[:, V_OFF:QI_OFF]
    qi = z[:, QI_OFF:KI_OFF]
    zl = z[:, KI_OFF:ATT_IN_PAD]
    ki = zl[:, :IDX_DIM]
    if transposed:
        qT_ref, qiT_ref, wiT_ref, vT_ref, kb_ref, kib_ref, k_ref, v_ref, ki_ref = outs
        for hh in range(ATT_HEADS):
            qh = _rms_rows(z[:, hh * HEAD_DIM:(hh + 1) * HEAD_DIM]) * qg * Q_SCALE
            qT_ref[hh * HEAD_DIM:(hh + 1) * HEAD_DIM, :] = qh.T.astype(BF16)
        for c in range(IDX_HEADS * IDX_DIM // LANES):
            qiT_ref[c * LANES:(c + 1) * LANES, :] = qi[:, c * LANES:(c + 1) * LANES].T.astype(BF16)
        wiT_ref[...] = zl.T[IDX_DIM:IDX_DIM + 8, :]
        for g in range(ATT_KV_HEADS):
            vT = v[:, g * HEAD_DIM:(g + 1) * HEAD_DIM].T
            for c in range(x.shape[0] // PAGE):
                vT_ref[c, g * HEAD_DIM:(g + 1) * HEAD_DIM, :] = vT[:, c * PAGE:(c + 1) * PAGE].astype(BF16)
        kb_ref[...] = k.astype(BF16)
        kib_ref[...] = ki.astype(BF16)
    else:
        q_ref, qi_ref, zl_ref, k_ref, v_ref, ki_ref = outs
        for hh in range(ATT_HEADS):
            q_ref[:, hh * HEAD_DIM:(hh + 1) * HEAD_DIM] = _rms_rows(z[:, hh * HEAD_DIM:(hh + 1) * HEAD_DIM]) * qg * Q_SCALE
        qi_ref[...] = qi
        zl_ref[...] = zl
    for g in range(ATT_KV_HEADS):
        rows_g = pl.ds(g, x.shape[0], stride=ATT_KV_HEADS)
        k_ref[rows_g, :] = k[:, g * HEAD_DIM:(g + 1) * HEAD_DIM]
        v_ref[rows_g, :] = v[:, g * HEAD_DIM:(g + 1) * HEAD_DIM]
    ki_ref[...] = ki


def _attn_proj(x, g, w, qg, kg, *, transposed):
    rows = x.shape[0]
    tm = ROW_TILE
    nt = rows // tm
    kvw = ATT_KV_HEADS * HEAD_DIM
    row_spec = lambda width: pl.BlockSpec((tm, width), lambda i: (i, 0))
    col_spec = lambda height: pl.BlockSpec((height, tm), lambda i: (0, i))
    kv_rows = ATT_KV_HEADS * rows
    kv_spec = pl.BlockSpec((ATT_KV_HEADS * tm, HEAD_DIM), lambda i: (i, 0))
    leaves_shape = [jax.ShapeDtypeStruct((kv_rows, HEAD_DIM), F32), jax.ShapeDtypeStruct((kv_rows, HEAD_DIM), F32),
                    jax.ShapeDtypeStruct((rows, IDX_DIM), F32)]
    leaves_spec = [kv_spec, kv_spec, row_spec(IDX_DIM)]
    if transposed:
        out_shape = [jax.ShapeDtypeStruct((ATT_HEADS * HEAD_DIM, rows), BF16),
                     jax.ShapeDtypeStruct((IDX_HEADS * IDX_DIM, rows), BF16),
                     jax.ShapeDtypeStruct((8, rows), F32),
                     jax.ShapeDtypeStruct((rows // PAGE, kvw, PAGE), BF16),
                     jax.ShapeDtypeStruct((rows, kvw), BF16),
                     jax.ShapeDtypeStruct((rows, IDX_DIM), BF16)] + leaves_shape
        out_specs = [col_spec(ATT_HEADS * HEAD_DIM), col_spec(IDX_HEADS * IDX_DIM), col_spec(8),
                     pl.BlockSpec((tm // PAGE, kvw, PAGE), lambda i: (i, 0, 0)),
                     row_spec(kvw), row_spec(IDX_DIM)] + leaves_spec
    else:
        out_shape = [jax.ShapeDtypeStruct((rows, ATT_HEADS * HEAD_DIM), F32),
                     jax.ShapeDtypeStruct((rows, IDX_HEADS * IDX_DIM), F32),
                     jax.ShapeDtypeStruct((rows, LANES), F32)] + leaves_shape
        out_specs = [row_spec(ATT_HEADS * HEAD_DIM), row_spec(IDX_HEADS * IDX_DIM), row_spec(LANES)] + leaves_spec
    return pl.pallas_call(
        functools.partial(_attn_proj_kernel, transposed=transposed),
        grid=(nt,),
        in_specs=[row_spec(D_MODEL), _const_spec((1, D_MODEL)), _const_spec((ATT_IN_PAD, D_MODEL)),
                  _const_spec((1, HEAD_DIM)), _const_spec((1, HEAD_DIM))],
        out_specs=out_specs,
        out_shape=out_shape,
        compiler_params=_params(1),
        name="attn_proj_t" if transposed else "attn_proj_r",
    )(x, g, w, qg, kg)


PROMPT_SEQS = 2


def _dsa_prompt_kernel(relb_ref, *refs, topk):
    seqs = range(PROMPT_SEQS)
    n_in = 6
    qT_refs, qiT_refs, wiT_refs, k_refs, vT_refs, ki_refs = zip(*[refs[n_in * u:n_in * (u + 1)] for u in seqs])
    o_ref, keys_sc, scores_sc, planes_sc, live_sc, mb_sc, bias_sc, logit_sc, acc_sc = refs[n_in * PROMPT_SEQS:]
    b = pl.program_id(0)
    i = pl.program_id(1)
    srow = lax.broadcasted_iota(I32, (PAGE, PAGE), 0)
    tcol = lax.broadcasted_iota(I32, (PAGE, PAGE), 1)

    @pl.when((b == 0) & (i == 0))
    def _init_bias():
        planes_sc[...] = jnp.zeros_like(planes_sc)
        live_sc[...] = jnp.zeros_like(live_sc)
        for hh in range(ATT_HEADS):
            for rel in range(2):
                dist = jnp.maximum(rel * PAGE + tcol - srow, 0)
                bias_sc[hh, rel] = _bias_from_distance(relb_ref, hh, dist)
            bias_sc[hh, 2] = jnp.full((PAGE, PAGE), relb_ref[NUM_BUCKETS - 1, hh] * LOG2E, F32)

    wiT = [r[...] for r in wiT_refs]
    qi_cat = [jnp.concatenate([r[hh * IDX_DIM:(hh + 1) * IDX_DIM, :] for hh in range(IDX_HEADS)], axis=1)
              for r in qiT_refs]

    npair = (i + 2) // 2

    def score_block(u, j):
        off = pl.multiple_of(j * PAGE, PAGE)
        kij = ki_refs[u][pl.ds(off, PAGE), :]
        idx = jnp.dot(kij, qi_cat[u], preferred_element_type=F32)
        sc = jnp.zeros((PAGE, PAGE), F32)
        for hh in range(IDX_HEADS):
            sc = sc + wiT[u][hh:hh + 1, :] * jnp.maximum(idx[:, hh * PAGE:(hh + 1) * PAGE], 0.0)
        sc = _canonical_zero(sc)
        causal = jnp.logical_or(j < i, jnp.logical_and(j == i, srow <= tcol))
        scores_sc[u, j] = jnp.where(causal, sc, -jnp.inf)
        keys_sc[u, j] = jnp.where(causal, _sortable_key(sc), INT_MIN)

    nquad = (i + 4) // 4

    def score_body(jj, carry):
        for blk in range(4):
            for u in seqs:
                score_block(u, 4 * jj + blk)
        return carry

    lax.fori_loop(0, nquad, score_body, 0)

    lane = lax.broadcasted_iota(I32, (1, PAGE), 1)
    kprime = jnp.minimum(topk, i * PAGE + lane + 1).astype(F32)

    def plane_body(jj, carry):
        for u in seqs:
            rows = [keys_sc[u, 2 * jj + blk, k * SUBLANES:(k + 1) * SUBLANES, :] ^ INT_MIN
                    for blk in range(2) for k in range(PAGE // SUBLANES)]
            planes = _bit_transpose32(rows)
            live = planes[0]
            for p in range(32):
                planes_sc[u, jj, p] = planes[p]
                live = live | planes[p]
            live_sc[u, jj] = live
        return carry

    lax.fori_loop(0, npair, plane_body, 0)

    n_pairs_max = keys_sc.shape[1] // 2

    def bit_step(u, bi, state):
        thr_u, alive, above = state
        cnt = jnp.zeros((SUBLANES, PAGE), I32)
        with_bit, at_least = [], []
        for p in range(n_pairs_max):
            t = alive[p] & planes_sc[u, p, bi]
            ge = above[p] | t
            cnt = cnt + lax.population_count(ge)
            with_bit.append(t)
            at_least.append(ge)
        take = jnp.sum(cnt.astype(F32), axis=0, keepdims=True) >= kprime
        alive = tuple(jnp.where(take, t, a ^ t) for t, a in zip(with_bit, alive))
        above = tuple(jnp.where(take, g, ge) for g, ge in zip(above, at_least))
        thr_u = thr_u | jnp.where(take, lax.shift_left(jnp.int32(1), 31 - bi), 0)
        return thr_u, alive, above

    zero_words = jnp.zeros((SUBLANES, PAGE), I32)
    states = lax.fori_loop(
        0, 32, lambda bi, states: tuple(bit_step(u, bi, states[u]) for u in seqs),
        tuple((jnp.zeros((1, PAGE), I32),
               tuple(jnp.where(p < npair, live_sc[u, p], zero_words) for p in range(n_pairs_max)),
               (zero_words,) * n_pairs_max) for u in seqs))
    def count_scores(u, preds):
        def body(jj, cnts):
            for blk in range(2):
                sc = scores_sc[u, 2 * jj + blk]
                cnts = tuple(c + jnp.where(p(sc), 1.0, 0.0) for c, p in zip(cnts, preds))
            return cnts
        cnts = lax.fori_loop(0, npair, body, tuple(jnp.zeros((PAGE, PAGE), F32) for _ in preds))
        return [jnp.sum(c, axis=0, keepdims=True) for c in cnts]

    def count_ge(u):
        return lambda thresholds: count_scores(u, [lambda sc, x=x: sc >= x for x in thresholds])

    thr = [_key_to_float(_bracket_kth_largest(count_ge(u), kprime, states[u][0] ^ INT_MIN)) for u in seqs]
    need = [kprime - count_scores(u, [lambda sc, u=u: sc > thr[u]])[0] for u in seqs]

    tri = jnp.where(srow > tcol, 1.0, 0.0).astype(BF16)

    def mask_block(u, j, carry):
        sc = scores_sc[u, j]
        eqf = jnp.where(sc == thr[u], 1.0, 0.0)
        rank = jnp.dot(tri, eqf.astype(BF16), preferred_element_type=F32) + carry
        tie = jnp.where(jnp.logical_and(sc == thr[u], rank < need[u]), 0.0, NEG)
        mb_sc[u, j] = jnp.where(sc > thr[u], 0.0, tie)
        return carry + jnp.sum(eqf, axis=0, keepdims=True)

    def mask_body(jj, carry):
        for blk in range(4):
            carry = tuple(mask_block(u, 4 * jj + blk, carry[u]) for u in seqs)
        return carry

    lax.fori_loop(0, nquad, mask_body, tuple(jnp.zeros((1, PAGE), F32) for _ in seqs))

    groups = range(ATT_KV_HEADS)
    units = [(u, g) for u in seqs for g in groups]
    width = ATT_GROUP * PAGE
    qcats = [jnp.concatenate(
        [qT_refs[u][(ATT_GROUP * g + r) * HEAD_DIM:(ATT_GROUP * g + r + 1) * HEAD_DIM, :]
         for r in range(ATT_GROUP)], axis=1) for u, g in units]

    def logit_block(j, ms):
        off = pl.multiple_of(j * PAGE, PAGE)
        relc = jnp.clip(i - j, 0, 2)
        out = []
        for n, (u, g) in enumerate(units):
            mb = mb_sc[u, j]
            kj = k_refs[u][pl.ds(off, PAGE), g * HEAD_DIM:(g + 1) * HEAD_DIM]
            logits = jnp.dot(kj, qcats[n], preferred_element_type=F32)
            logits = jnp.concatenate(
                [logits[:, r * PAGE:(r + 1) * PAGE] + (mb + bias_sc[ATT_GROUP * g + r, relc])
                 for r in range(ATT_GROUP)], axis=1)
            logit_sc[u, g, j] = logits
            out.append(jnp.maximum(ms[n], jnp.max(logits, axis=0, keepdims=True)))
        return tuple(out)

    def over_block_pairs(pair_step, carry):
        nfull = npair // 2
        carry = lax.fori_loop(0, nfull, lambda t, c: pair_step(4 * t + 2, pair_step(4 * t, c)), carry)
        return lax.cond(npair % 2 == 1, lambda c: pair_step(4 * nfull, c), lambda c: c, carry)

    ms = over_block_pairs(lambda j0, c: logit_block(j0 + 1, logit_block(j0, c)),
                          tuple(jnp.full((1, width), NEG, F32) for _ in units))

    ones_rows = jnp.ones((ONES_ROWS, PAGE), BF16)

    def value_pair(j0, carry):
        for n, (u, g) in enumerate(units):
            p = jnp.concatenate(
                [jnp.exp2(logit_sc[u, g, j0 + blk] - ms[n]).astype(BF16) for blk in range(2)], axis=0)
            vT2 = jnp.concatenate(
                [jnp.concatenate([vT_refs[u][j0 + blk, g * HEAD_DIM:(g + 1) * HEAD_DIM, :], ones_rows], axis=0)
                 for blk in range(2)], axis=1)
            acc_sc[u, g] = acc_sc[u, g] + jnp.dot(vT2, p, preferred_element_type=F32)
        return carry

    acc_sc[...] = jnp.zeros_like(acc_sc)
    over_block_pairs(value_pair, 0)
    for u, g in units:
        acc = acc_sc[u, g]
        outT = acc[:HEAD_DIM] * (1.0 / acc[HEAD_DIM:HEAD_DIM + 1])
        for r in range(ATT_GROUP):
            hh = ATT_GROUP * g + r
            o_ref[u, :, hh * HEAD_DIM:(hh + 1) * HEAD_DIM] = outT[:, r * PAGE:(r + 1) * PAGE].T.astype(BF16)


def _dsa_prompt(rel_bias, qT, qiT, wiT, kb, vT3, kib, *, batch, seq):
    nb = seq // PAGE
    rows = batch * seq
    kvw = ATT_KV_HEADS * HEAD_DIM
    ns = PROMPT_SEQS

    def seq_specs(u):
        qcol = lambda height: pl.BlockSpec((height, PAGE), lambda b, i: (0, (ns * b + u) * nb + i))
        return [qcol(ATT_HEADS * HEAD_DIM), qcol(IDX_HEADS * IDX_DIM), qcol(8),
                pl.BlockSpec((seq, kvw), lambda b, i: (ns * b + u, 0)),
                pl.BlockSpec((nb, kvw, PAGE), lambda b, i: (ns * b + u, 0, 0)),
                pl.BlockSpec((seq, IDX_DIM), lambda b, i: (ns * b + u, 0))]

    out = pl.pallas_call(
        functools.partial(_dsa_prompt_kernel, topk=min(TOPK_MAX, seq // 4)),
        grid=(batch // ns, nb),
        in_specs=[pl.BlockSpec(memory_space=pltpu.SMEM)] + [s for u in range(ns) for s in seq_specs(u)],
        out_specs=pl.BlockSpec((ns, PAGE, ATT_HEADS * HEAD_DIM), lambda b, i: (b, i, 0)),
        out_shape=jax.ShapeDtypeStruct((batch, seq, ATT_HEADS * HEAD_DIM), BF16),
        scratch_shapes=[pltpu.VMEM((ns, nb, PAGE, PAGE), I32), pltpu.VMEM((ns, nb, PAGE, PAGE), F32),
                        pltpu.VMEM((ns, nb // 2, 32, SUBLANES, PAGE), I32),
                        pltpu.VMEM((ns, nb // 2, SUBLANES, PAGE), I32),
                        pltpu.VMEM((ns, nb, PAGE, PAGE), F32),
                        pltpu.VMEM((ATT_HEADS, 3, PAGE, PAGE), F32),
                        pltpu.VMEM((ns, ATT_KV_HEADS, nb, PAGE, ATT_GROUP * PAGE), F32),
                        pltpu.VMEM((ns, ATT_KV_HEADS, HEAD_DIM + ONES_ROWS, ATT_GROUP * PAGE), F32)],
        compiler_params=_params(2),
        name="dsa_prompt",
    )(rel_bias, *([qT, qiT, wiT, kb, vT3, kib] * ns))
    return out.reshape(rows, ATT_HEADS * HEAD_DIM)


TQ = 8
ATT_SAMPLES = 2
SEL_SAMPLES = 16


def _dsa_select_kernel(pt_ref, qi_ref, zl_ref, cki_hbm, mb_ref, sc_sc, kibuf, sems, *, n_pages, n_new):
    ns = SEL_SAMPLES
    n_fetch = ns * n_pages
    b = pl.program_id(0)
    slot = b % 2

    def page_copies(step, slot):
        return [pltpu.make_async_copy(cki_hbm.at[pt_ref[step * n_fetch + f]], kibuf.at[slot, f], sems.at[slot])
                for f in range(n_fetch)]

    @pl.when(b == 0)
    def _start_first():
        for c in page_copies(0, 0):
            c.start()

    @pl.when(b + 1 < pl.num_programs(0))
    def _start_next():
        for c in page_copies(b + 1, 1 - slot):
            c.start()

    for c in page_copies(b, slot):
        c.wait()
    ki_pages = [kibuf.at[slot, f] for f in range(n_fetch)]
    n_blk = n_pages + 1
    rows = ns * TQ
    trow = lax.broadcasted_iota(I32, (TQ, PAGE), 0)
    scol = lax.broadcasted_iota(I32, (TQ, PAGE), 1)
    pad_rows = lambda a: jnp.concatenate([a, jnp.zeros((PAGE - TQ, a.shape[1]), a.dtype)], axis=0)
    nt = (((1,), (1,)), ((), ()))

    for s in range(ns):
        zl = zl_ref[s]
        qi = qi_ref[s]
        for p in range(n_blk):
            if p == n_pages:
                ki_new = pad_rows(zl[:, :IDX_DIM]).astype(BF16)
                idx = lax.dot_general(qi, ki_new, nt, preferred_element_type=F32)
            else:
                idx = jnp.dot(qi, ki_pages[s * n_pages + p][...].astype(BF16), preferred_element_type=F32)
            sc = jnp.zeros((TQ, PAGE), F32)
            for hh in range(IDX_HEADS):
                sc = sc + zl[:, IDX_DIM + hh:IDX_DIM + hh + 1] * jnp.maximum(idx[hh * TQ:(hh + 1) * TQ], 0.0)
            sc = _canonical_zero(sc)
            if p == n_pages:
                sc = jnp.where(jnp.logical_and(scol <= trow, scol < n_new), sc, -jnp.inf)
            sc_sc[s * TQ:(s + 1) * TQ, p * PAGE:(p + 1) * PAGE] = sc

    kprime = float(min(TOPK_MAX, (n_pages * PAGE + n_new) // 4))

    def count(pred_of_scores):
        return jnp.sum(jnp.where(pred_of_scores(sc_sc[...]), 1.0, 0.0), axis=1, keepdims=True)

    cand0 = jnp.zeros((rows, 1), I32)
    thr0 = jnp.where(count(lambda sc: sc >= _key_to_float(cand0)) >= kprime, cand0, INT_MIN)

    def bit_body(bi, thr_key):
        cand = thr_key | lax.shift_left(jnp.int32(1), 30 - bi)
        return jnp.where(count(lambda sc: sc >= _key_to_float(cand)) >= kprime, cand, thr_key)

    thr = _key_to_float(lax.fori_loop(0, 31, bit_body, thr0))
    need = kprime - count(lambda sc: sc > thr)

    s_r = lax.broadcasted_iota(I32, (PAGE, PAGE), 0)
    s_c = lax.broadcasted_iota(I32, (PAGE, PAGE), 1)
    triu = jnp.where(s_r < s_c, 1.0, 0.0).astype(BF16)
    carry = jnp.zeros((rows, 1), F32)
    for p in range(n_blk):
        sc = sc_sc[:, p * PAGE:(p + 1) * PAGE]
        eqf = jnp.where(sc == thr, 1.0, 0.0)
        rank = jnp.dot(eqf.astype(BF16), triu, preferred_element_type=F32) + carry
        tie = jnp.where(jnp.logical_and(sc == thr, rank < need), 0.0, NEG)
        mb_ref[:, p * PAGE:(p + 1) * PAGE] = jnp.where(sc > thr, 0.0, tie)
        carry = carry + jnp.sum(eqf, axis=1, keepdims=True)


def _dsa_select(page_table, qi8, zl8, cache_ki, *, n_new):
    nsamp, n_pages = page_table.shape
    ns = SEL_SAMPLES
    width = (n_pages + 1) * PAGE

    in_specs = [pl.BlockSpec((ns, IDX_HEADS * TQ, IDX_DIM), lambda b, pt: (b, 0, 0)),
                pl.BlockSpec((ns, TQ, LANES), lambda b, pt: (b, 0, 0)),
                pl.BlockSpec(memory_space=pl.ANY)]
    grid_spec = pltpu.PrefetchScalarGridSpec(
        num_scalar_prefetch=1, grid=(nsamp // ns,), in_specs=in_specs,
        out_specs=pl.BlockSpec((ns * TQ, width), lambda b, pt: (b, 0)),
        scratch_shapes=[pltpu.VMEM((ns * TQ, width), F32),
                        pltpu.VMEM((2, ns * n_pages, IDX_DIM, PAGE), F32), pltpu.SemaphoreType.DMA((2,))])
    return pl.pallas_call(
        functools.partial(_dsa_select_kernel, n_pages=n_pages, n_new=n_new),
        grid_spec=grid_spec,
        out_shape=jax.ShapeDtypeStruct((nsamp * TQ, width), F32),
        compiler_params=_params(1),
        name="dsa_select",
    )(page_table.reshape(-1), qi8, zl8, cache_ki)


def _dsa_sample_kernel(pt_ref, relb_ref, q_ref, mb_ref, kn_ref, vn_ref, ck_hbm, cv_hbm, o_ref,
                       bias_sc, kbuf, vbuf, sems, *, n_pages):
    past = n_pages * PAGE
    rows_g = ATT_GROUP * TQ
    page_rows = PAGE * ATT_KV_HEADS
    ns = ATT_SAMPLES
    b = pl.program_id(0)
    n_slots = kbuf.shape[0]
    ahead = n_slots - 1
    slot = b % n_slots

    def page_copies(step, slot):
        copies = []
        for f in range(ns * n_pages):
            src = pl.ds(pl.multiple_of(pt_ref[step * ns * n_pages + f] * page_rows, page_rows), page_rows)
            dst = pl.ds(f * page_rows, page_rows)
            copies.append(pltpu.make_async_copy(ck_hbm.at[src, :], kbuf.at[slot, dst, :], sems.at[slot, 0]))
            copies.append(pltpu.make_async_copy(cv_hbm.at[src, :], vbuf.at[slot, dst, :], sems.at[slot, 1]))
        return copies

    @pl.when(b == 0)
    def _start_first():
        for s in range(ahead):
            for c in page_copies(s, s):
                c.start()

    @pl.when(b + ahead < pl.num_programs(0))
    def _start_next():
        for c in page_copies(b + ahead, (b + ahead) % n_slots):
            c.start()

    for c in page_copies(b, slot):
        c.wait()

    trow = lax.broadcasted_iota(I32, (TQ, PAGE), 0)
    scol = lax.broadcasted_iota(I32, (TQ, PAGE), 1)

    @pl.when(pl.program_id(0) == 0)
    def _init_bias():
        for g in range(ATT_KV_HEADS):
            for r in range(ATT_GROUP):
                hh = ATT_GROUP * g + r
                rs = slice(r * TQ, (r + 1) * TQ)
                far = jnp.full((TQ, PAGE), relb_ref[NUM_BUCKETS - 1, hh] * LOG2E, F32)
                for p in range(n_pages - 1):
                    bias_sc[g, rs, p * PAGE:(p + 1) * PAGE] = far
                bias_sc[g, rs, past - PAGE:past] = _bias_from_distance(
                    relb_ref, hh, jnp.maximum(PAGE + trow - scol, 0))
                bias_sc[g, rs, past:past + PAGE] = _bias_from_distance(relb_ref, hh, jnp.maximum(trow - scol, 0))

    nt = (((1,), (1,)), ((), ()))
    pad_rows = lambda a: jnp.concatenate([a, jnp.zeros((PAGE - TQ, a.shape[1]), a.dtype)], axis=0)
    units = [(s, g) for s in range(ns) for g in range(ATT_KV_HEADS)]
    for s, g in units:
        gs = slice(g * HEAD_DIM, (g + 1) * HEAD_DIM)
        head_rows = pl.ds(s * n_pages * page_rows + g, past, stride=ATT_KV_HEADS)
        k_all = jnp.concatenate([kbuf[slot, head_rows, :].astype(BF16), pad_rows(kn_ref[s, :, gs]).astype(BF16)], axis=0)
        v_all = jnp.concatenate([vbuf[slot, head_rows, :].astype(BF16), pad_rows(vn_ref[s, :, gs]).astype(BF16)], axis=0)
        mb = jnp.concatenate([mb_ref[s * TQ:(s + 1) * TQ, :]] * ATT_GROUP, axis=0)
        qg = q_ref[s, g * rows_g:(g + 1) * rows_g, :]
        logits = lax.dot_general(qg, k_all, nt, preferred_element_type=F32)
        logits = logits + (mb + bias_sc[g])
        m = jnp.max(logits, axis=1, keepdims=True)
        pr = jnp.exp2(logits - m)
        l = jnp.sum(pr, axis=1, keepdims=True)
        acc = jnp.dot(pr.astype(BF16), v_all, preferred_element_type=F32)
        o_ref[s, g * rows_g:(g + 1) * rows_g, :] = (acc * (1.0 / l)).astype(BF16)


def _dsa_sample(page_table, rel_bias, q8, mb, kn8, vn8, cache_k, cache_v):
    nsamp, n_pages = page_table.shape
    kvw = ATT_KV_HEADS * HEAD_DIM
    ns = ATT_SAMPLES
    per_step = lambda *tail: pl.BlockSpec((ns,) + tail, lambda b, pt: (b,) + (0,) * len(tail))

    n_slots = 3
    page_buf = pltpu.VMEM((n_slots, ns * n_pages * PAGE * ATT_KV_HEADS, HEAD_DIM), F32)
    in_specs = [pl.BlockSpec(memory_space=pltpu.SMEM),
                per_step(ATT_HEADS * TQ, HEAD_DIM),
                pl.BlockSpec((ns * TQ, mb.shape[1]), lambda b, pt: (b, 0)),
                per_step(TQ, kvw), per_step(TQ, kvw),
                pl.BlockSpec(memory_space=pl.ANY), pl.BlockSpec(memory_space=pl.ANY)]
    grid_spec = pltpu.PrefetchScalarGridSpec(
        num_scalar_prefetch=1, grid=(nsamp // ns,), in_specs=in_specs,
        out_specs=per_step(ATT_HEADS * TQ, HEAD_DIM),
        scratch_shapes=[pltpu.VMEM((ATT_KV_HEADS, ATT_GROUP * TQ, mb.shape[1]), F32),
                        page_buf, page_buf, pltpu.SemaphoreType.DMA((n_slots, 2))])
    return pl.pallas_call(
        functools.partial(_dsa_sample_kernel, n_pages=n_pages),
        grid_spec=grid_spec,
        out_shape=jax.ShapeDtypeStruct((nsamp, ATT_HEADS * TQ, HEAD_DIM), BF16),
        compiler_params=_params(1),
        name="dsa_sample",
    )(page_table.reshape(-1), rel_bias, q8, mb, kn8, vn8, cache_k, cache_v)


def _post_mix_kernel(x_ref, o_ref, wo_ref, g_ref, w1_ref, w2_ref, y_ref):
    y = x_ref[...] + jnp.dot(o_ref[...], wo_ref[...], preferred_element_type=F32)
    h = (_rms_rows(y) * g_ref[...]).astype(BF16)
    acc = y
    for c in range(D_FF // D_MODEL):
        a = jnp.dot(h, w1_ref[:, c * D_MODEL:(c + 1) * D_MODEL], preferred_element_type=F32)
        a = jnp.square(jnp.maximum(a, 0.0)).astype(BF16)
        acc = acc + jnp.dot(a, w2_ref[c * D_MODEL:(c + 1) * D_MODEL, :], preferred_element_type=F32)
    y_ref[...] = acc


def _post_mix(x, o, wo, g, w1, w2):
    rows = x.shape[0]
    tm = ROW_TILE
    ko = o.shape[1]
    return pl.pallas_call(
        _post_mix_kernel,
        grid=(rows // tm,),
        in_specs=[pl.BlockSpec((tm, D_MODEL), lambda i: (i, 0)), pl.BlockSpec((tm, ko), lambda i: (i, 0)),
                  _const_spec((ko, D_MODEL)), _const_spec((1, D_MODEL)),
                  _const_spec((D_MODEL, D_FF)), _const_spec((D_FF, D_MODEL))],
        out_specs=pl.BlockSpec((tm, D_MODEL), lambda i: (i, 0)),
        out_shape=jax.ShapeDtypeStruct((rows, D_MODEL), F32),
        compiler_params=_params(1),
        name="post_mix",
    )(x, o, wo, g, w1, w2)


def _ret_proj_kernel(x_ref, g_ref, w_ref, cos_ref, sin_ref, q_ref, k_ref, v_ref, gate_ref):
    hb = (_rms_rows(x_ref[...]) * g_ref[...]).astype(BF16)
    cos = cos_ref[...]
    sin = sin_ref[...]
    half = RET_KEY // 2
    qk = RET_HEADS * RET_KEY
    vd = RET_HEADS * RET_VAL

    def rotated(col0, scale):
        z = jnp.dot(hb, w_ref[:, col0:col0 + RET_KEY], preferred_element_type=F32)
        x1, x2 = z[:, :half], z[:, half:]
        return (x1 * cos - x2 * sin) * scale, (x1 * sin + x2 * cos) * scale

    for hh in range(RET_HEADS):
        a, b = rotated(hh * RET_KEY, 1.0)
        q_ref[:, hh * RET_KEY:hh * RET_KEY + half] = a.astype(BF16)
        q_ref[:, hh * RET_KEY + half:(hh + 1) * RET_KEY] = b.astype(BF16)
        a, b = rotated(qk + hh * RET_KEY, RET_KSCALE)
        k_ref[:, hh * RET_KEY:hh * RET_KEY + half] = a.astype(BF16)
        k_ref[:, hh * RET_KEY + half:(hh + 1) * RET_KEY] = b.astype(BF16)
    for hh in range(RET_HEADS):
        cs = slice(hh * RET_VAL, (hh + 1) * RET_VAL)
        v_ref[:, cs] = jnp.dot(hb, w_ref[:, 2 * qk + hh * RET_VAL:2 * qk + (hh + 1) * RET_VAL],
                               preferred_element_type=F32).astype(BF16)
        gate_ref[:, cs] = jnp.dot(hb, w_ref[:, 2 * qk + vd + hh * RET_VAL:2 * qk + vd + (hh + 1) * RET_VAL],
                                  preferred_element_type=F32).astype(BF16)


def _ret_proj(x, g, w, cos, sin):
    rows = x.shape[0]
    tm = ROW_TILE
    qk = RET_HEADS * RET_KEY
    vd = RET_HEADS * RET_VAL
    n_pos_tiles = cos.shape[0] // tm
    row_spec = lambda width: pl.BlockSpec((tm, width), lambda i: (i, 0))
    pos_spec = pl.BlockSpec((tm, RET_KEY // 2), lambda i: (i % n_pos_tiles, 0))
    return pl.pallas_call(
        _ret_proj_kernel,
        grid=(rows // tm,),
        in_specs=[row_spec(D_MODEL), _const_spec((1, D_MODEL)), _const_spec((D_MODEL, 2 * qk + 2 * vd)),
                  pos_spec, pos_spec],
        out_specs=[row_spec(qk), row_spec(qk), row_spec(vd), row_spec(vd)],
        out_shape=[jax.ShapeDtypeStruct((rows, qk), BF16), jax.ShapeDtypeStruct((rows, qk), BF16),
                   jax.ShapeDtypeStruct((rows, vd), BF16), jax.ShapeDtypeStruct((rows, vd), BF16)],
        compiler_params=_params(1),
        name="ret_proj",
    )(x, g, w, cos, sin)


def _retention_kernel(cd_ref, q_ref, k_ref, v_ref, g_ref, dm_ref, qd_ref, kd_ref, *rest, has_state0):
    if has_state0:
        s0_ref, o_ref, s_ref = rest
    else:
        o_ref, s_ref = rest

    @pl.when(pl.program_id(1) == 0)
    def _init_state():
        s_ref[...] = s0_ref[...] if has_state0 else jnp.zeros_like(s_ref)

    nt = (((1,), (1,)), ((), ()))
    tn = (((0,), (0,)), ((), ()))
    units = [(bi, hh) for bi in range(q_ref.shape[0]) for hh in range(RET_HEADS)]
    ks = lambda hh: slice(hh * RET_KEY, (hh + 1) * RET_KEY)
    vs = lambda hh: slice(hh * RET_VAL, (hh + 1) * RET_VAL)
    att = [lax.dot_general(q_ref[bi, :, ks(hh)].astype(BF16), k_ref[bi, :, ks(hh)].astype(BF16), nt,
                           preferred_element_type=F32) * dm_ref[hh] for bi, hh in units]
    cross = [jnp.dot((q_ref[bi, :, ks(hh)] * qd_ref[hh]).astype(BF16), s_ref[bi, hh].astype(BF16),
                     preferred_element_type=F32) for bi, hh in units]
    for (bi, hh), a, x in zip(units, att, cross):
        o = jnp.dot(a.astype(BF16), v_ref[bi, :, vs(hh)], preferred_element_type=F32) + x
        gate = g_ref[bi, :, vs(hh)].astype(F32)
        o_ref[bi, :, vs(hh)] = (_rms_rows(o) * (gate * jax.nn.sigmoid(gate))).astype(BF16)
    for bi, hh in units:
        kd = (k_ref[bi, :, ks(hh)] * kd_ref[hh]).astype(BF16)
        s_ref[bi, hh] = cd_ref[hh] * s_ref[bi, hh] + lax.dot_general(kd, v_ref[bi, :, vs(hh)], tn,
                                                                     preferred_element_type=F32)


def _retention(q, k, v, gate, consts, state0, *, seqs_per_step):
    dm, qd, kd, cd = consts
    nb, t, _ = q.shape
    c = dm.shape[1]
    bb = seqs_per_step
    qk = RET_HEADS * RET_KEY
    vd = RET_HEADS * RET_VAL
    tok = lambda width: pl.BlockSpec((bb, c, width), lambda b, ci: (b, ci, 0))
    st_spec = pl.BlockSpec((bb, RET_HEADS, RET_KEY, RET_VAL), lambda b, ci: (b, 0, 0, 0))
    in_specs = [pl.BlockSpec(memory_space=pltpu.SMEM), tok(qk), tok(qk), tok(vd), tok(vd),
                _const_spec(dm.shape), _const_spec(qd.shape), _const_spec(kd.shape)]
    args = [cd, q, k, v, gate, dm, qd, kd]
    if state0 is not None:
        in_specs.append(st_spec)
        args.append(state0)
    return pl.pallas_call(
        functools.partial(_retention_kernel, has_state0=state0 is not None),
        grid=(nb // bb, t // c),
        in_specs=in_specs,
        out_specs=[tok(vd), st_spec],
        out_shape=[jax.ShapeDtypeStruct((nb, t, vd), BF16),
                   jax.ShapeDtypeStruct((nb, RET_HEADS, RET_KEY, RET_VAL), F32)],
        compiler_params=_params(2),
        name="retention_s" if state0 is not None else "retention_p",
    )(*args)


def _retention_consts(c, c_pad):
    log_g = np.log1p(-np.exp2(-5.0 - np.arange(RET_HEADS, dtype=np.float64)))
    i = np.arange(c, dtype=np.float64)
    diff = i[:, None] - i[None, :]
    dm = np.where(diff >= 0, np.exp(log_g[:, None, None] * np.maximum(diff, 0.0)), 0.0)
    qd = np.exp(log_g[:, None] * (i[None, :] + 1.0))
    kd = np.exp(log_g[:, None] * (c - 1.0 - i[None, :]))
    cd = np.exp(log_g * c)
    pad = c_pad - c
    dm = np.pad(dm, ((0, 0), (0, pad), (0, pad)))
    qd = np.broadcast_to(np.pad(qd, ((0, 0), (0, pad)))[:, :, None], (RET_HEADS, c_pad, RET_KEY))
    kd = np.broadcast_to(np.pad(kd, ((0, 0), (0, pad)))[:, :, None], (RET_HEADS, c_pad, RET_KEY))
    return tuple(jnp.asarray(np.ascontiguousarray(a), F32) for a in (dm, qd, kd, cd))


def _rotation_tables(pos):
    half = RET_KEY // 2
    theta = 1.0 / (10000.0 ** np.linspace(0.0, 1.0, half, dtype=np.float32).astype(np.float64))
    ang = np.asarray(pos, np.float64)[:, None] * theta[None, :]
    return jnp.asarray(np.cos(ang), F32), jnp.asarray(np.sin(ang), F32)


def kernel(x_prompt, x_sample, cache_k, cache_v, cache_kidx, state_ret, page_table, rel_bias, ln_mix, ln_mlp,
           att_w_in, att_q_gain, att_k_gain, att_w_out, ret_w_in, ret_w_out, mlp_w_in, mlp_w_out):
    bp, tp, _ = x_prompt.shape
    bs, ts, _ = x_sample.shape
    n_phys = cache_k.shape[1]
    assert ln_mix.shape[0] == 2 and att_w_in.shape[0] == 1 and ret_w_in.shape[0] == 1
    assert tp % RET_CHUNK == 0 and ts <= TQ and cache_k.shape[2] == PAGE
    assert (tp // PAGE) % 4 == 0 and bs % SEL_SAMPLES == 0 and bs % ATT_SAMPLES == 0 and bp % PROMPT_SEQS == 0
    kvw = ATT_KV_HEADS * HEAD_DIM
    past = page_table.shape[1] * PAGE

    w_att = jnp.pad(jnp.swapaxes(att_w_in[0], 0, 1), ((0, ATT_IN_PAD - ATT_IN), (0, 0))).astype(BF16)
    w_att_out = att_w_out[0].astype(BF16)
    w_ret = ret_w_in[0].astype(BF16)
    w_ret_out = ret_w_out[0].astype(BF16)
    w1 = mlp_w_in.astype(BF16)
    w2 = mlp_w_out.astype(BF16)
    row = lambda a: a.reshape(1, -1)

    xp = x_prompt.reshape(bp * tp, D_MODEL)
    xs = x_sample.reshape(bs * ts, D_MODEL)

    qT, qiT, wiT, vT3, kb, kib, kp, vp, kip = _attn_proj(
        xp, row(ln_mix[0]), w_att, row(att_q_gain[0]), row(att_k_gain[0]), transposed=True)
    op = _dsa_prompt(rel_bias, qT, qiT, wiT, kb, vT3, kib, batch=bp, seq=tp)
    yp = _post_mix(xp, op, w_att_out, row(ln_mlp[0]), w1[0], w2[0])

    qs, qis, zls, ks, vs, kis = _attn_proj(
        xs, row(ln_mix[0]), w_att, row(att_q_gain[0]), row(att_k_gain[0]), transposed=False)
    pad_t = lambda a: jnp.pad(a, ((0, 0), (0, TQ - ts)) + ((0, 0),) * (a.ndim - 2))
    q8 = pad_t(qs.reshape(bs, ts, ATT_HEADS, HEAD_DIM)).transpose(0, 2, 1, 3)
    q8 = q8.reshape(bs, ATT_HEADS * TQ, HEAD_DIM).astype(BF16)
    qi8 = pad_t(qis.reshape(bs, ts, IDX_HEADS, IDX_DIM)).transpose(0, 2, 1, 3)
    qi8 = qi8.reshape(bs, IDX_HEADS * TQ, IDX_DIM).astype(BF16)
    zl8 = pad_t(zls.reshape(bs, ts, LANES))
    kn8 = pad_t(ks.reshape(bs, ts, kvw))
    vn8 = pad_t(vs.reshape(bs, ts, kvw))
    mb = _dsa_select(page_table, qi8, zl8, jnp.swapaxes(cache_kidx[0], 1, 2), n_new=ts)
    os8 = _dsa_sample(page_table, rel_bias, q8, mb, kn8, vn8,
                      cache_k[0].reshape(n_phys * PAGE * ATT_KV_HEADS, HEAD_DIM),
                      cache_v[0].reshape(n_phys * PAGE * ATT_KV_HEADS, HEAD_DIM))
    os_ = os8.reshape(bs, ATT_HEADS, TQ, HEAD_DIM)[:, :, :ts].transpose(0, 2, 1, 3)
    os_ = os_.reshape(bs * ts, ATT_HEADS * HEAD_DIM)
    ys = _post_mix(xs, os_, w_att_out, row(ln_mlp[0]), w1[0], w2[0])

    cos_p, sin_p = _rotation_tables(np.arange(tp))
    q, k, v, gate = _ret_proj(yp, row(ln_mix[1]), w_ret, cos_p, sin_p)
    r3 = lambda a: a.reshape(bp, tp, a.shape[-1])
    orp, state_p = _retention(r3(q), r3(k), r3(v), r3(gate), _retention_consts(RET_CHUNK, RET_CHUNK), None,
                            seqs_per_step=1)
    yp = _post_mix(yp, orp.reshape(bp * tp, -1), w_ret_out, row(ln_mlp[1]), w1[1], w2[1])

    cos_s, sin_s = _rotation_tables(np.tile(past + np.arange(ts), bs))
    q, k, v, gate = _ret_proj(ys, row(ln_mix[1]), w_ret, cos_s, sin_s)
    r3s = lambda a: pad_t(a.reshape(bs, ts, a.shape[-1]))
    ors, state_s = _retention(r3s(q), r3s(k), r3s(v), r3s(gate), _retention_consts(ts, TQ), state_ret[0],
                            seqs_per_step=2)
    ys = _post_mix(ys, ors[:, :ts].reshape(bs * ts, -1), w_ret_out, row(ln_mlp[1]), w1[1], w2[1])

    return (yp.reshape(bp, tp, D_MODEL), ys.reshape(bs, ts, D_MODEL),
            kp.reshape(1, bp, tp, ATT_KV_HEADS, HEAD_DIM), vp.reshape(1, bp, tp, ATT_KV_HEADS, HEAD_DIM),
            kip.reshape(1, bp, tp, IDX_DIM), state_p[None],
            ks.reshape(1, bs, ts, ATT_KV_HEADS, HEAD_DIM), vs.reshape(1, bs, ts, ATT_KV_HEADS, HEAD_DIM),
            kis.reshape(1, bs, ts, IDX_DIM), state_s[None])
```

```python
import functools
import math

import jax
import jax.numpy as jnp
import numpy as np
from jax import lax
from jax.experimental import pallas as pl
from jax.experimental.pallas import tpu as pltpu

F32 = jnp.float32
BF16 = jnp.bfloat16
I32 = jnp.int32

D_MODEL = 1024
PAGE = 128
ATT_HEADS = 8
ATT_KV_HEADS = 2
ATT_GROUP = ATT_HEADS // ATT_KV_HEADS
HEAD_DIM = 128
IDX_HEADS = 4
IDX_DIM = 64
TOPK_MAX = 256
NUM_BUCKETS = 32
MAX_DISTANCE = 128
RET_HEADS = 4
RET_KEY = 256
RET_VAL = 512
RET_CHUNK = 256
D_FF = 4 * D_MODEL
EPS = 1e-6

Q_OFF = 0
K_OFF = ATT_HEADS * HEAD_DIM
V_OFF = K_OFF + ATT_KV_HEADS * HEAD_DIM
QI_OFF = V_OFF + ATT_KV_HEADS * HEAD_DIM
KI_OFF = QI_OFF + IDX_HEADS * IDX_DIM
WI_OFF = KI_OFF + IDX_DIM
ATT_IN = WI_OFF + IDX_HEADS
ATT_IN_PAD = 1920

LANES = 128
SUBLANES = 8
ROW_TILE = 512
VMEM_LIMIT = 56 * 1024 * 1024

LOG2E = math.log2(math.e)
Q_SCALE = HEAD_DIM ** -0.5 * LOG2E
ONES_ROWS = 16
RET_KSCALE = RET_KEY ** -0.5
INT_MIN = -(2 ** 31)
NEG = -1e30


def _bucket_bounds():
    n = np.arange(0, 8192)
    max_exact = NUM_BUCKETS // 2
    val = np.log(np.maximum(n, 1) / max_exact) / math.log(MAX_DISTANCE / max_exact) * (NUM_BUCKETS - max_exact)
    frac = np.abs(val - np.round(val))
    risky = (n > max_exact) & (n < MAX_DISTANCE) & (frac < 1e-4)
    assert not risky.any()
    large = np.minimum(max_exact + np.floor(val + 1e-9).astype(np.int64), NUM_BUCKETS - 1)
    bucket = np.where(n < max_exact, n, large)
    assert (np.diff(bucket) >= 0).all() and (bucket[MAX_DISTANCE:] == NUM_BUCKETS - 1).all()
    return [int(np.argmax(bucket >= b)) for b in range(NUM_BUCKETS)]


BUCKET_BOUNDS = _bucket_bounds()


def _const_spec(shape):
    zeros = (0,) * len(shape)
    return pl.BlockSpec(shape, lambda *_: zeros, pipeline_mode=pl.Buffered(1))


def _params(n_axes):
    return pltpu.CompilerParams(dimension_semantics=("arbitrary",) * n_axes,
                                vmem_limit_bytes=VMEM_LIMIT)


def _rms_rows(a):
    return a * lax.rsqrt(jnp.mean(a * a, axis=-1, keepdims=True) + EPS)


def _canonical_zero(score):
    return jnp.where(score == 0.0, 0.0, score)


def _sortable_key(score):
    bits = pltpu.bitcast(score, I32)
    return bits ^ ((bits >> 31) & jnp.int32(0x7FFFFFFF))


def _key_to_float(key):
    return pltpu.bitcast(key ^ ((key >> 31) & jnp.int32(0x7FFFFFFF)), F32)


KEY_NEG_INF = -2139095041
KEY_POS_INF = 2139095040
KEY_SUBNORMAL_LO = -8388608
KEY_SUBNORMAL_HI = 8388607


def _bracket_kth_largest(count_ge, kprime, guess):
    def canon(x):
        return jnp.where(jnp.logical_and(x >= KEY_SUBNORMAL_LO, x <= KEY_SUBNORMAL_HI), 0, x)

    def succ(x):
        xc = canon(x)
        return jnp.where(xc == 0, KEY_SUBNORMAL_HI + 1, canon(xc + 1))

    g0 = jnp.clip(guess, KEY_NEG_INF, KEY_POS_INF - 1)
    g1 = succ(g0)
    lo = jnp.full(guess.shape, KEY_NEG_INF, I32)
    hi = jnp.full(guess.shape, KEY_POS_INF, I32)

    def narrow(lo, hi, x, cnt):
        ok = cnt >= kprime
        return jnp.where(ok, jnp.maximum(lo, x), lo), jnp.where(ok, hi, jnp.minimum(hi, x))

    c0, c1 = count_ge([_key_to_float(g0), _key_to_float(g1)])
    lo, hi = narrow(lo, hi, g0, c0)
    lo, hi = narrow(lo, hi, g1, c1)

    def open_bracket(c):
        return jnp.max(jnp.where(canon(c[1]) > succ(c[0]), 1.0, 0.0)) > 0.5

    def bisect(c):
        lo, hi = c
        mid = jnp.where(canon(hi) > succ(lo), (lo >> 1) + (hi >> 1) + (lo & hi & 1), lo)
        return narrow(lo, hi, mid, count_ge([_key_to_float(mid)])[0])

    lo, _ = lax.while_loop(open_bracket, bisect, (lo, hi))
    return canon(lo)


def _bit_transpose32(words):
    a = list(words)
    j, m = 16, 0x0000FFFF
    while j:
        k = 0
        while k < 32:
            t = (a[k] ^ lax.shift_right_logical(a[k + j], jnp.int32(j))) & jnp.int32(m)
            a[k] = a[k] ^ t
            a[k + j] = a[k + j] ^ lax.shift_left(t, jnp.int32(j))
            k = (k + j + 1) & ~j
        j >>= 1
        m ^= (m << j) & 0xFFFFFFFF
    return a


def _bias_from_distance(relb_ref, head, dist):
    val = jnp.full(dist.shape, relb_ref[0, head], F32)
    for b in range(1, NUM_BUCKETS):
        val = jnp.where(dist >= BUCKET_BOUNDS[b], relb_ref[b, head], val)
    return val * LOG2E


def _attn_proj_kernel(x_ref, g_ref, w_ref, qg_ref, kg_ref, *outs, transposed):
    x = x_ref[...]
    h = _rms_rows(x) * g_ref[...]
    z = lax.dot_general(h.astype(BF16), w_ref[...], (((1,), (1,)), ((), ())),
                        preferred_element_type=F32)
    qg = qg_ref[...]
    kg = kg_ref[...]
    k = jnp.concatenate(
        [_rms_rows(z[:, K_OFF + g * HEAD_DIM:K_OFF + (g + 1) * HEAD_DIM]) * kg for g in range(ATT_KV_HEADS)],
        axis=1)
    v = z[:, V_OFF:QI_OFF]
    qi = z[:, QI_OFF:KI_OFF]
    zl = z[:, KI_OFF:ATT_IN_PAD]
    ki = zl[:, :IDX_DIM]
    if transposed:
        qT_ref, qiT_ref, wiT_ref, vT_ref, kb_ref, kib_ref, k_ref, v_ref, ki_ref = outs
        for hh in range(ATT_HEADS):
            qh = _rms_rows(z[:, hh * HEAD_DIM:(hh + 1) * HEAD_DIM]) * qg * Q_SCALE
            qT_ref[hh * HEAD_DIM:(hh + 1) * HEAD_DIM, :] = qh.T.astype(BF16)
        for c in range(IDX_HEADS * IDX_DIM // LANES):
            qiT_ref[c * LANES:(c + 1) * LANES, :] = qi[:, c * LANES:(c + 1) * LANES].T.astype(BF16)
        wiT_ref[...] = zl.T[IDX_DIM:IDX_DIM + 8, :]
        for g in range(ATT_KV_HEADS):
            vT = v[:, g * HEAD_DIM:(g + 1) * HEAD_DIM].T
            for c in range(x.shape[0] // PAGE):
                vT_ref[c, g * HEAD_DIM:(g + 1) * HEAD_DIM, :] = vT[:, c * PAGE:(c + 1) * PAGE].astype(BF16)
        kb_ref[...] = k.astype(BF16)
        kib_ref[...] = ki.astype(BF16)
    else:
        q_ref, qi_ref, zl_ref, k_ref, v_ref, ki_ref = outs
        for hh in range(ATT_HEADS):
            q_ref[:, hh * HEAD_DIM:(hh + 1) * HEAD_DIM] = _rms_rows(z[:, hh * HEAD_DIM:(hh + 1) * HEAD_DIM]) * qg * Q_SCALE
        qi_ref[...] = qi
        zl_ref[...] = zl
    for g in range(ATT_KV_HEADS):
        rows_g = pl.ds(g, x.shape[0], stride=ATT_KV_HEADS)
        k_ref[rows_g, :] = k[:, g * HEAD_DIM:(g + 1) * HEAD_DIM]
        v_ref[rows_g, :] = v[:, g * HEAD_DIM:(g + 1) * HEAD_DIM]
    ki_ref[...] = ki


def _attn_proj(x, g, w, qg, kg, *, transposed):
    rows = x.shape[0]
    tm = ROW_TILE
    nt = rows // tm
    kvw = ATT_KV_HEADS * HEAD_DIM
    row_spec = lambda width: pl.BlockSpec((tm, width), lambda i: (i, 0))
    col_spec = lambda height: pl.BlockSpec((height, tm), lambda i: (0, i))
    kv_rows = ATT_KV_HEADS * rows
    kv_spec = pl.BlockSpec((ATT_KV_HEADS * tm, HEAD_DIM), lambda i: (i, 0))
    leaves_shape = [jax.ShapeDtypeStruct((kv_rows, HEAD_DIM), F32), jax.ShapeDtypeStruct((kv_rows, HEAD_DIM), F32),
                    jax.ShapeDtypeStruct((rows, IDX_DIM), F32)]
    leaves_spec = [kv_spec, kv_spec, row_spec(IDX_DIM)]
    if transposed:
        out_shape = [jax.ShapeDtypeStruct((ATT_HEADS * HEAD_DIM, rows), BF16),
                     jax.ShapeDtypeStruct((IDX_HEADS * IDX_DIM, rows), BF16),
                     jax.ShapeDtypeStruct((8, rows), F32),
                     jax.ShapeDtypeStruct((rows // PAGE, kvw, PAGE), BF16),
                     jax.ShapeDtypeStruct((rows, kvw), BF16),
                     jax.ShapeDtypeStruct((rows, IDX_DIM), BF16)] + leaves_shape
        out_specs = [col_spec(ATT_HEADS * HEAD_DIM), col_spec(IDX_HEADS * IDX_DIM), col_spec(8),
                     pl.BlockSpec((tm // PAGE, kvw, PAGE), lambda i: (i, 0, 0)),
                     row_spec(kvw), row_spec(IDX_DIM)] + leaves_spec
    else:
        out_shape = [jax.ShapeDtypeStruct((rows, ATT_HEADS * HEAD_DIM), F32),
                     jax.ShapeDtypeStruct((rows, IDX_HEADS * IDX_DIM), F32),
                     jax.ShapeDtypeStruct((rows, LANES), F32)] + leaves_shape
        out_specs = [row_spec(ATT_HEADS * HEAD_DIM), row_spec(IDX_HEADS * IDX_DIM), row_spec(LANES)] + leaves_spec
    return pl.pallas_call(
        functools.partial(_attn_proj_kernel, transposed=transposed),
        grid=(nt,),
        in_specs=[row_spec(D_MODEL), _const_spec((1, D_MODEL)), _const_spec((ATT_IN_PAD, D_MODEL)),
                  _const_spec((1, HEAD_DIM)), _const_spec((1, HEAD_DIM))],
        out_specs=out_specs,
        out_shape=out_shape,
        compiler_params=_params(1),
        name="attn_proj_t" if transposed else "attn_proj_r",
    )(x, g, w, qg, kg)


PROMPT_SEQS = 2


def _dsa_prompt_kernel(relb_ref, *refs, topk):
    seqs = range(PROMPT_SEQS)
    n_in = 6
    qT_refs, qiT_refs, wiT_refs, k_refs, vT_refs, ki_refs = zip(*[refs[n_in * u:n_in * (u + 1)] for u in seqs])
    o_ref, keys_sc, scores_sc, planes_sc, mb_sc, bias_sc, logit_sc, acc_sc = refs[n_in * PROMPT_SEQS:]
    b = pl.program_id(0)
    i = pl.program_id(1)
    srow = lax.broadcasted_iota(I32, (PAGE, PAGE), 0)
    tcol = lax.broadcasted_iota(I32, (PAGE, PAGE), 1)

    @pl.when((b == 0) & (i == 0))
    def _init_bias():
        planes_sc[...] = jnp.zeros_like(planes_sc)
        for hh in range(ATT_HEADS):
            for rel in range(2):
                dist = jnp.maximum(rel * PAGE + tcol - srow, 0)
                bias_sc[hh, rel] = _bias_from_distance(relb_ref, hh, dist)
            bias_sc[hh, 2] = jnp.full((PAGE, PAGE), relb_ref[NUM_BUCKETS - 1, hh] * LOG2E, F32)

    wiT = [r[...] for r in wiT_refs]
    qi_cat = [jnp.concatenate([r[hh * IDX_DIM:(hh + 1) * IDX_DIM, :] for hh in range(IDX_HEADS)], axis=1)
              for r in qiT_refs]

    npair = (i + 2) // 2

    def score_block(u, j):
        off = pl.multiple_of(j * PAGE, PAGE)
        kij = ki_refs[u][pl.ds(off, PAGE), :]
        idx = jnp.dot(kij, qi_cat[u], preferred_element_type=F32)
        sc = jnp.zeros((PAGE, PAGE), F32)
        for hh in range(IDX_HEADS):
            sc = sc + wiT[u][hh:hh + 1, :] * jnp.maximum(idx[:, hh * PAGE:(hh + 1) * PAGE], 0.0)
        sc = _canonical_zero(sc)
        causal = jnp.logical_or(j < i, jnp.logical_and(j == i, srow <= tcol))
        scores_sc[u, j] = jnp.where(causal, sc, -jnp.inf)
        keys_sc[u, j] = jnp.where(causal, _sortable_key(sc), INT_MIN)

    nquad = (i + 4) // 4

    def score_body(jj, carry):
        for blk in range(4):
            for u in seqs:
                score_block(u, 4 * jj + blk)
        return carry

    lax.fori_loop(0, nquad, score_body, 0)

    lane = lax.broadcasted_iota(I32, (1, PAGE), 1)
    kprime = jnp.minimum(topk, i * PAGE + lane + 1).astype(F32)

    def plane_body(jj, carry):
        for u in seqs:
            rows = [keys_sc[u, 2 * jj + blk, k * SUBLANES:(k + 1) * SUBLANES, :] ^ INT_MIN
                    for blk in range(2) for k in range(PAGE // SUBLANES)]
            planes = _bit_transpose32(rows)
            for p in range(32):
                planes_sc[u, jj, p] = planes[p]
        return carry

    lax.fori_loop(0, npair, plane_body, 0)

    n_pairs_max = keys_sc.shape[1] // 2

    def bit_step(u, bi, state):
        thr_u, alive, above = state
        cnt = jnp.zeros((SUBLANES, PAGE), I32)
        with_bit, at_least = [], []
        for p in range(n_pairs_max):
            t = alive[p] & planes_sc[u, p, bi]
            ge = above[p] | t
            cnt = cnt + lax.population_count(ge)
            with_bit.append(t)
            at_least.append(ge)
        take = jnp.sum(cnt.astype(F32), axis=0, keepdims=True) >= kprime
        alive = tuple(jnp.where(take, t, a ^ t) for t, a in zip(with_bit, alive))
        above = tuple(jnp.where(take, g, ge) for g, ge in zip(above, at_least))
        thr_u = thr_u | jnp.where(take, lax.shift_left(jnp.int32(1), 31 - bi), 0)
        return thr_u, alive, above

    zero_words = jnp.zeros((SUBLANES, PAGE), I32)
    states = lax.fori_loop(
        0, 32, lambda bi, states: tuple(bit_step(u, bi, states[u]) for u in seqs),
        tuple((jnp.zeros((1, PAGE), I32),
               tuple(jnp.where(p < npair, jnp.int32(-1), zero_words) for p in range(n_pairs_max)),
               (zero_words,) * n_pairs_max) for u in seqs))
    def count_scores(u, preds):
        def body(jj, cnts):
            for blk in range(2):
                sc = scores_sc[u, 2 * jj + blk]
                cnts = tuple(c + jnp.where(p(sc), 1.0, 0.0) for c, p in zip(cnts, preds))
            return cnts
        cnts = lax.fori_loop(0, npair, body, tuple(jnp.zeros((PAGE, PAGE), F32) for _ in preds))
        return [jnp.sum(c, axis=0, keepdims=True) for c in cnts]

    def count_ge(u):
        return lambda thresholds: count_scores(u, [lambda sc, x=x: sc >= x for x in thresholds])

    thr = [_key_to_float(_bracket_kth_largest(count_ge(u), kprime, states[u][0] ^ INT_MIN)) for u in seqs]
    need = [kprime - count_scores(u, [lambda sc, u=u: sc > thr[u]])[0] for u in seqs]

    tri = jnp.where(srow > tcol, 1.0, 0.0).astype(BF16)

    def mask_block(u, j, carry):
        sc = scores_sc[u, j]
        eqf = jnp.where(sc == thr[u], 1.0, 0.0)
        rank = jnp.dot(tri, eqf.astype(BF16), preferred_element_type=F32) + carry
        tie = jnp.where(jnp.logical_and(sc == thr[u], rank < need[u]), 0.0, NEG)
        mb_sc[u, j] = jnp.where(sc > thr[u], 0.0, tie)
        return carry + jnp.sum(eqf, axis=0, keepdims=True)

    def mask_body(jj, carry):
        for blk in range(4):
            carry = tuple(mask_block(u, 4 * jj + blk, carry[u]) for u in seqs)
        return carry

    lax.fori_loop(0, nquad, mask_body, tuple(jnp.zeros((1, PAGE), F32) for _ in seqs))

    groups = range(ATT_KV_HEADS)
    units = [(u, g) for u in seqs for g in groups]
    width = ATT_GROUP * PAGE
    qcats = [jnp.concatenate(
        [qT_refs[u][(ATT_GROUP * g + r) * HEAD_DIM:(ATT_GROUP * g + r + 1) * HEAD_DIM, :]
         for r in range(ATT_GROUP)], axis=1) for u, g in units]

    def logit_block(j, ms):
        off = pl.multiple_of(j * PAGE, PAGE)
        relc = jnp.clip(i - j, 0, 2)
        out = []
        for n, (u, g) in enumerate(units):
            mb = mb_sc[u, j]
            kj = k_refs[u][pl.ds(off, PAGE), g * HEAD_DIM:(g + 1) * HEAD_DIM]
            logits = jnp.dot(kj, qcats[n], preferred_element_type=F32)
            logits = jnp.concatenate(
                [logits[:, r * PAGE:(r + 1) * PAGE] + (mb + bias_sc[ATT_GROUP * g + r, relc])
                 for r in range(ATT_GROUP)], axis=1)
            logit_sc[u, g, j] = logits
            out.append(jnp.maximum(ms[n], jnp.max(logits, axis=0, keepdims=True)))
        return tuple(out)

    def over_block_pairs(pair_step, carry):
        nfull = npair // 2
        carry = lax.fori_loop(0, nfull, lambda t, c: pair_step(4 * t + 2, pair_step(4 * t, c)), carry)
        return lax.cond(npair % 2 == 1, lambda c: pair_step(4 * nfull, c), lambda c: c, carry)

    ms = over_block_pairs(lambda j0, c: logit_block(j0 + 1, logit_block(j0, c)),
                          tuple(jnp.full((1, width), NEG, F32) for _ in units))

    ones_rows = jnp.ones((ONES_ROWS, PAGE), BF16)

    def value_pair(j0, carry):
        for n, (u, g) in enumerate(units):
            p = jnp.concatenate(
                [jnp.exp2(logit_sc[u, g, j0 + blk] - ms[n]).astype(BF16) for blk in range(2)], axis=0)
            vT2 = jnp.concatenate(
                [jnp.concatenate([vT_refs[u][j0 + blk, g * HEAD_DIM:(g + 1) * HEAD_DIM, :], ones_rows], axis=0)
                 for blk in range(2)], axis=1)
            acc_sc[u, g] = acc_sc[u, g] + jnp.dot(vT2, p, preferred_element_type=F32)
        return carry

    acc_sc[...] = jnp.zeros_like(acc_sc)
    over_block_pairs(value_pair, 0)
    for u, g in units:
        acc = acc_sc[u, g]
        outT = acc[:HEAD_DIM] * (1.0 / acc[HEAD_DIM:HEAD_DIM + 1])
        for r in range(ATT_GROUP):
            hh = ATT_GROUP * g + r
            o_ref[u, :, hh * HEAD_DIM:(hh + 1) * HEAD_DIM] = outT[:, r * PAGE:(r + 1) * PAGE].T.astype(BF16)


def _dsa_prompt(rel_bias, qT, qiT, wiT, kb, vT3, kib, *, batch, seq):
    nb = seq // PAGE
    rows = batch * seq
    kvw = ATT_KV_HEADS * HEAD_DIM
    ns = PROMPT_SEQS

    def seq_specs(u):
        qcol = lambda height: pl.BlockSpec((height, PAGE), lambda b, i: (0, (ns * b + u) * nb + i))
        return [qcol(ATT_HEADS * HEAD_DIM), qcol(IDX_HEADS * IDX_DIM), qcol(8),
                pl.BlockSpec((seq, kvw), lambda b, i: (ns * b + u, 0)),
                pl.BlockSpec((nb, kvw, PAGE), lambda b, i: (ns * b + u, 0, 0)),
                pl.BlockSpec((seq, IDX_DIM), lambda b, i: (ns * b + u, 0))]

    out = pl.pallas_call(
        functools.partial(_dsa_prompt_kernel, topk=min(TOPK_MAX, seq // 4)),
        grid=(batch // ns, nb),
        in_specs=[pl.BlockSpec(memory_space=pltpu.SMEM)] + [s for u in range(ns) for s in seq_specs(u)],
        out_specs=pl.BlockSpec((ns, PAGE, ATT_HEADS * HEAD_DIM), lambda b, i: (b, i, 0)),
        out_shape=jax.ShapeDtypeStruct((batch, seq, ATT_HEADS * HEAD_DIM), BF16),
        scratch_shapes=[pltpu.VMEM((ns, nb, PAGE, PAGE), I32), pltpu.VMEM((ns, nb, PAGE, PAGE), F32),
                        pltpu.VMEM((ns, nb // 2, 32, SUBLANES, PAGE), I32),
                        pltpu.VMEM((ns, nb, PAGE, PAGE), F32),
                        pltpu.VMEM((ATT_HEADS, 3, PAGE, PAGE), F32),
                        pltpu.VMEM((ns, ATT_KV_HEADS, nb, PAGE, ATT_GROUP * PAGE), F32),
                        pltpu.VMEM((ns, ATT_KV_HEADS, HEAD_DIM + ONES_ROWS, ATT_GROUP * PAGE), F32)],
        compiler_params=_params(2),
        name="dsa_prompt",
    )(rel_bias, *([qT, qiT, wiT, kb, vT3, kib] * ns))
    return out.reshape(rows, ATT_HEADS * HEAD_DIM)


TQ = 8
ATT_SAMPLES = 2
SEL_SAMPLES = 16


def _dsa_select_kernel(pt_ref, qi_ref, zl_ref, cki_hbm, mb_ref, sc_sc, kibuf, sems, *, n_pages, n_new):
    ns = SEL_SAMPLES
    n_fetch = ns * n_pages
    b = pl.program_id(0)
    slot = b % 2

    def page_copies(step, slot):
        return [pltpu.make_async_copy(cki_hbm.at[pt_ref[step * n_fetch + f]], kibuf.at[slot, f], sems.at[slot])
                for f in range(n_fetch)]

    @pl.when(b == 0)
    def _start_first():
        for c in page_copies(0, 0):
            c.start()

    @pl.when(b + 1 < pl.num_programs(0))
    def _start_next():
        for c in page_copies(b + 1, 1 - slot):
            c.start()

    for c in page_copies(b, slot):
        c.wait()
    ki_pages = [kibuf.at[slot, f] for f in range(n_fetch)]
    n_blk = n_pages + 1
    rows = ns * TQ
    trow = lax.broadcasted_iota(I32, (TQ, PAGE), 0)
    scol = lax.broadcasted_iota(I32, (TQ, PAGE), 1)
    pad_rows = lambda a: jnp.concatenate([a, jnp.zeros((PAGE - TQ, a.shape[1]), a.dtype)], axis=0)
    nt = (((1,), (1,)), ((), ()))

    for s in range(ns):
        zl = zl_ref[s]
        qi = qi_ref[s]
        for p in range(n_blk):
            if p == n_pages:
                ki_new = pad_rows(zl[:, :IDX_DIM]).astype(BF16)
                idx = lax.dot_general(qi, ki_new, nt, preferred_element_type=F32)
            else:
                idx = jnp.dot(qi, ki_pages[s * n_pages + p][...].astype(BF16), preferred_element_type=F32)
            sc = jnp.zeros((TQ, PAGE), F32)
            for hh in range(IDX_HEADS):
                sc = sc + zl[:, IDX_DIM + hh:IDX_DIM + hh + 1] * jnp.maximum(idx[hh * TQ:(hh + 1) * TQ], 0.0)
            sc = _canonical_zero(sc)
            if p == n_pages:
                sc = jnp.where(jnp.logical_and(scol <= trow, scol < n_new), sc, -jnp.inf)
            sc_sc[s * TQ:(s + 1) * TQ, p * PAGE:(p + 1) * PAGE] = sc

    kprime = float(min(TOPK_MAX, (n_pages * PAGE + n_new) // 4))

    def count(pred_of_scores):
        return jnp.sum(jnp.where(pred_of_scores(sc_sc[...]), 1.0, 0.0), axis=1, keepdims=True)

    cand0 = jnp.zeros((rows, 1), I32)
    thr0 = jnp.where(count(lambda sc: sc >= _key_to_float(cand0)) >= kprime, cand0, INT_MIN)

    def bit_body(bi, thr_key):
        cand = thr_key | lax.shift_left(jnp.int32(1), 30 - bi)
        return jnp.where(count(lambda sc: sc >= _key_to_float(cand)) >= kprime, cand, thr_key)

    thr = _key_to_float(lax.fori_loop(0, 31, bit_body, thr0))
    need = kprime - count(lambda sc: sc > thr)

    s_r = lax.broadcasted_iota(I32, (PAGE, PAGE), 0)
    s_c = lax.broadcasted_iota(I32, (PAGE, PAGE), 1)
    triu = jnp.where(s_r < s_c, 1.0, 0.0).astype(BF16)
    carry = jnp.zeros((rows, 1), F32)
    for p in range(n_blk):
        sc = sc_sc[:, p * PAGE:(p + 1) * PAGE]
        eqf = jnp.where(sc == thr, 1.0, 0.0)
        rank = jnp.dot(eqf.astype(BF16), triu, preferred_element_type=F32) + carry
        tie = jnp.where(jnp.logical_and(sc == thr, rank < need), 0.0, NEG)
        mb_ref[:, p * PAGE:(p + 1) * PAGE] = jnp.where(sc > thr, 0.0, tie)
        carry = carry + jnp.sum(eqf, axis=1, keepdims=True)


def _dsa_select(page_table, qi8, zl8, cache_ki, *, n_new):
    nsamp, n_pages = page_table.shape
    ns = SEL_SAMPLES
    width = (n_pages + 1) * PAGE

    in_specs = [pl.BlockSpec((ns, IDX_HEADS * TQ, IDX_DIM), lambda b, pt: (b, 0, 0)),
                pl.BlockSpec((ns, TQ, LANES), lambda b, pt: (b, 0, 0)),
                pl.BlockSpec(memory_space=pl.ANY)]
    grid_spec = pltpu.PrefetchScalarGridSpec(
        num_scalar_prefetch=1, grid=(nsamp // ns,), in_specs=in_specs,
        out_specs=pl.BlockSpec((ns * TQ, width), lambda b, pt: (b, 0)),
        scratch_shapes=[pltpu.VMEM((ns * TQ, width), F32),
                        pltpu.VMEM((2, ns * n_pages, IDX_DIM, PAGE), F32), pltpu.SemaphoreType.DMA((2,))])
    return pl.pallas_call(
        functools.partial(_dsa_select_kernel, n_pages=n_pages, n_new=n_new),
        grid_spec=grid_spec,
        out_shape=jax.ShapeDtypeStruct((nsamp * TQ, width), F32),
        compiler_params=_params(1),
        name="dsa_select",
    )(page_table.reshape(-1), qi8, zl8, cache_ki)


def _dsa_sample_kernel(pt_ref, relb_ref, q_ref, mb_ref, kn_ref, vn_ref, ck_hbm, cv_hbm, o_ref,
                       bias_sc, kbuf, vbuf, sems, *, n_pages):
    past = n_pages * PAGE
    rows_g = ATT_GROUP * TQ
    page_rows = PAGE * ATT_KV_HEADS
    ns = ATT_SAMPLES
    b = pl.program_id(0)
    n_slots = kbuf.shape[0]
    ahead = n_slots - 1
    slot = b % n_slots

    def page_copies(step, slot):
        copies = []
        for f in range(ns * n_pages):
            src = pl.ds(pl.multiple_of(pt_ref[step * ns * n_pages + f] * page_rows, page_rows), page_rows)
            dst = pl.ds(f * page_rows, page_rows)
            copies.append(pltpu.make_async_copy(ck_hbm.at[src, :], kbuf.at[slot, dst, :], sems.at[slot, 0]))
            copies.append(pltpu.make_async_copy(cv_hbm.at[src, :], vbuf.at[slot, dst, :], sems.at[slot, 1]))
        return copies

    @pl.when(b == 0)
    def _start_first():
        for s in range(ahead):
            for c in page_copies(s, s):
                c.start()

    @pl.when(b + ahead < pl.num_programs(0))
    def _start_next():
        for c in page_copies(b + ahead, (b + ahead) % n_slots):
            c.start()

    for c in page_copies(b, slot):
        c.wait()

    trow = lax.broadcasted_iota(I32, (TQ, PAGE), 0)
    scol = lax.broadcasted_iota(I32, (TQ, PAGE), 1)

    @pl.when(pl.program_id(0) == 0)
    def _init_bias():
        for g in range(ATT_KV_HEADS):
            for r in range(ATT_GROUP):
                hh = ATT_GROUP * g + r
                rs = slice(r * TQ, (r + 1) * TQ)
                far = jnp.full((TQ, PAGE), relb_ref[NUM_BUCKETS - 1, hh] * LOG2E, F32)
                for p in range(n_pages - 1):
                    bias_sc[g, rs, p * PAGE:(p + 1) * PAGE] = far
                bias_sc[g, rs, past - PAGE:past] = _bias_from_distance(
                    relb_ref, hh, jnp.maximum(PAGE + trow - scol, 0))
                bias_sc[g, rs, past:past + PAGE] = _bias_from_distance(relb_ref, hh, jnp.maximum(trow - scol, 0))

    nt = (((1,), (1,)), ((), ()))
    pad_rows = lambda a: jnp.concatenate([a, jnp.zeros((PAGE - TQ, a.shape[1]), a.dtype)], axis=0)
    units = [(s, g) for s in range(ns) for g in range(ATT_KV_HEADS)]
    for s, g in units:
        gs = slice(g * HEAD_DIM, (g + 1) * HEAD_DIM)
        head_rows = pl.ds(s * n_pages * page_rows + g, past, stride=ATT_KV_HEADS)
        k_all = jnp.concatenate([kbuf[slot, head_rows, :].astype(BF16), pad_rows(kn_ref[s, :, gs]).astype(BF16)], axis=0)
        v_all = jnp.concatenate([vbuf[slot, head_rows, :].astype(BF16), pad_rows(vn_ref[s, :, gs]).astype(BF16)], axis=0)
        mb = jnp.concatenate([mb_ref[s * TQ:(s + 1) * TQ, :]] * ATT_GROUP, axis=0)
        qg = q_ref[s, g * rows_g:(g + 1) * rows_g, :]
        logits = lax.dot_general(qg, k_all, nt, preferred_element_type=F32)
        logits = logits + (mb + bias_sc[g])
        m = jnp.max(logits, axis=1, keepdims=True)
        pr = jnp.exp2(logits - m)
        l = jnp.sum(pr, axis=1, keepdims=True)
        acc = jnp.dot(pr.astype(BF16), v_all, preferred_element_type=F32)
        o_ref[s, g * rows_g:(g + 1) * rows_g, :] = (acc * (1.0 / l)).astype(BF16)


def _dsa_sample(page_table, rel_bias, q8, mb, kn8, vn8, cache_k, cache_v):
    nsamp, n_pages = page_table.shape
    kvw = ATT_KV_HEADS * HEAD_DIM
    ns = ATT_SAMPLES
    per_step = lambda *tail: pl.BlockSpec((ns,) + tail, lambda b, pt: (b,) + (0,) * len(tail))

    n_slots = 3
    page_buf = pltpu.VMEM((n_slots, ns * n_pages * PAGE * ATT_KV_HEADS, HEAD_DIM), F32)
    in_specs = [pl.BlockSpec(memory_space=pltpu.SMEM),
                per_step(ATT_HEADS * TQ, HEAD_DIM),
                pl.BlockSpec((ns * TQ, mb.shape[1]), lambda b, pt: (b, 0)),
                per_step(TQ, kvw), per_step(TQ, kvw),
                pl.BlockSpec(memory_space=pl.ANY), pl.BlockSpec(memory_space=pl.ANY)]
    grid_spec = pltpu.PrefetchScalarGridSpec(
        num_scalar_prefetch=1, grid=(nsamp // ns,), in_specs=in_specs,
        out_specs=per_step(ATT_HEADS * TQ, HEAD_DIM),
        scratch_shapes=[pltpu.VMEM((ATT_KV_HEADS, ATT_GROUP * TQ, mb.shape[1]), F32),
                        page_buf, page_buf, pltpu.SemaphoreType.DMA((n_slots, 2))])
    return pl.pallas_call(
        functools.partial(_dsa_sample_kernel, n_pages=n_pages),
        grid_spec=grid_spec,
        out_shape=jax.ShapeDtypeStruct((nsamp, ATT_HEADS * TQ, HEAD_DIM), BF16),
        compiler_params=_params(1),
        name="dsa_sample",
    )(page_table.reshape(-1), rel_bias, q8, mb, kn8, vn8, cache_k, cache_v)


def _post_mix_kernel(x_ref, o_ref, wo_ref, g_ref, w1_ref, w2_ref, y_ref):
    y = x_ref[...] + jnp.dot(o_ref[...], wo_ref[...], preferred_element_type=F32)
    h = (_rms_rows(y) * g_ref[...]).astype(BF16)
    acc = y
    for c in range(D_FF // D_MODEL):
        a = jnp.dot(h, w1_ref[:, c * D_MODEL:(c + 1) * D_MODEL], preferred_element_type=F32)
        a = jnp.square(jnp.maximum(a, 0.0)).astype(BF16)
        acc = acc + jnp.dot(a, w2_ref[c * D_MODEL:(c + 1) * D_MODEL, :], preferred_element_type=F32)
    y_ref[...] = acc


def _post_mix(x, o, wo, g, w1, w2):
    rows = x.shape[0]
    tm = ROW_TILE
    ko = o.shape[1]
    return pl.pallas_call(
        _post_mix_kernel,
        grid=(rows // tm,),
        in_specs=[pl.BlockSpec((tm, D_MODEL), lambda i: (i, 0)), pl.BlockSpec((tm, ko), lambda i: (i, 0)),
                  _const_spec((ko, D_MODEL)), _const_spec((1, D_MODEL)),
                  _const_spec((D_MODEL, D_FF)), _const_spec((D_FF, D_MODEL))],
        out_specs=pl.BlockSpec((tm, D_MODEL), lambda i: (i, 0)),
        out_shape=jax.ShapeDtypeStruct((rows, D_MODEL), F32),
        compiler_params=_params(1),
        name="post_mix",
    )(x, o, wo, g, w1, w2)


def _ret_proj_kernel(x_ref, g_ref, w_ref, cos_ref, sin_ref, q_ref, k_ref, v_ref, gate_ref):
    hb = (_rms_rows(x_ref[...]) * g_ref[...]).astype(BF16)
    cos = cos_ref[...]
    sin = sin_ref[...]
    half = RET_KEY // 2
    qk = RET_HEADS * RET_KEY
    vd = RET_HEADS * RET_VAL

    def rotated(col0, scale):
        z = jnp.dot(hb, w_ref[:, col0:col0 + RET_KEY], preferred_element_type=F32)
        x1, x2 = z[:, :half], z[:, half:]
        return (x1 * cos - x2 * sin) * scale, (x1 * sin + x2 * cos) * scale

    for hh in range(RET_HEADS):
        a, b = rotated(hh * RET_KEY, 1.0)
        q_ref[:, hh * RET_KEY:hh * RET_KEY + half] = a.astype(BF16)
        q_ref[:, hh * RET_KEY + half:(hh + 1) * RET_KEY] = b.astype(BF16)
        a, b = rotated(qk + hh * RET_KEY, RET_KSCALE)
        k_ref[:, hh * RET_KEY:hh * RET_KEY + half] = a.astype(BF16)
        k_ref[:, hh * RET_KEY + half:(hh + 1) * RET_KEY] = b.astype(BF16)
    for hh in range(RET_HEADS):
        cs = slice(hh * RET_VAL, (hh + 1) * RET_VAL)
        v_ref[:, cs] = jnp.dot(hb, w_ref[:, 2 * qk + hh * RET_VAL:2 * qk + (hh + 1) * RET_VAL],
                               preferred_element_type=F32).astype(BF16)
        gate_ref[:, cs] = jnp.dot(hb, w_ref[:, 2 * qk + vd + hh * RET_VAL:2 * qk + vd + (hh + 1) * RET_VAL],
                                  preferred_element_type=F32).astype(BF16)


def _ret_proj(x, g, w, cos, sin):
    rows = x.shape[0]
    tm = ROW_TILE
    qk = RET_HEADS * RET_KEY
    vd = RET_HEADS * RET_VAL
    n_pos_tiles = cos.shape[0] // tm
    row_spec = lambda width: pl.BlockSpec((tm, width), lambda i: (i, 0))
    pos_spec = pl.BlockSpec((tm, RET_KEY // 2), lambda i: (i % n_pos_tiles, 0))
    return pl.pallas_call(
        _ret_proj_kernel,
        grid=(rows // tm,),
        in_specs=[row_spec(D_MODEL), _const_spec((1, D_MODEL)), _const_spec((D_MODEL, 2 * qk + 2 * vd)),
                  pos_spec, pos_spec],
        out_specs=[row_spec(qk), row_spec(qk), row_spec(vd), row_spec(vd)],
        out_shape=[jax.ShapeDtypeStruct((rows, qk), BF16), jax.ShapeDtypeStruct((rows, qk), BF16),
                   jax.ShapeDtypeStruct((rows, vd), BF16), jax.ShapeDtypeStruct((rows, vd), BF16)],
        compiler_params=_params(1),
        name="ret_proj",
    )(x, g, w, cos, sin)


def _retention_kernel(cd_ref, q_ref, k_ref, v_ref, g_ref, dm_ref, qd_ref, kd_ref, *rest, has_state0):
    if has_state0:
        s0_ref, o_ref, s_ref = rest
    else:
        o_ref, s_ref = rest

    @pl.when(pl.program_id(1) == 0)
    def _init_state():
        s_ref[...] = s0_ref[...] if has_state0 else jnp.zeros_like(s_ref)

    nt = (((1,), (1,)), ((), ()))
    tn = (((0,), (0,)), ((), ()))
    units = [(bi, hh) for bi in range(q_ref.shape[0]) for hh in range(RET_HEADS)]
    ks = lambda hh: slice(hh * RET_KEY, (hh + 1) * RET_KEY)
    vs = lambda hh: slice(hh * RET_VAL, (hh + 1) * RET_VAL)
    att = [lax.dot_general(q_ref[bi, :, ks(hh)].astype(BF16), k_ref[bi, :, ks(hh)].astype(BF16), nt,
                           preferred_element_type=F32) * dm_ref[hh] for bi, hh in units]
    cross = [jnp.dot((q_ref[bi, :, ks(hh)] * qd_ref[hh]).astype(BF16), s_ref[bi, hh].astype(BF16),
                     preferred_element_type=F32) for bi, hh in units]
    for (bi, hh), a, x in zip(units, att, cross):
        o = jnp.dot(a.astype(BF16), v_ref[bi, :, vs(hh)], preferred_element_type=F32) + x
        gate = g_ref[bi, :, vs(hh)].astype(F32)
        o_ref[bi, :, vs(hh)] = (_rms_rows(o) * (gate * jax.nn.sigmoid(gate))).astype(BF16)
    for bi, hh in units:
        kd = (k_ref[bi, :, ks(hh)] * kd_ref[hh]).astype(BF16)
        s_ref[bi, hh] = cd_ref[hh] * s_ref[bi, hh] + lax.dot_general(kd, v_ref[bi, :, vs(hh)], tn,
                                                                     preferred_element_type=F32)


def _retention(q, k, v, gate, consts, state0, *, seqs_per_step):
    dm, qd, kd, cd = consts
    nb, t, _ = q.shape
    c = dm.shape[1]
    bb = seqs_per_step
    qk = RET_HEADS * RET_KEY
    vd = RET_HEADS * RET_VAL
    tok = lambda width: pl.BlockSpec((bb, c, width), lambda b, ci: (b, ci, 0))
    st_spec = pl.BlockSpec((bb, RET_HEADS, RET_KEY, RET_VAL), lambda b, ci: (b, 0, 0, 0))
    in_specs = [pl.BlockSpec(memory_space=pltpu.SMEM), tok(qk), tok(qk), tok(vd), tok(vd),
                _const_spec(dm.shape), _const_spec(qd.shape), _const_spec(kd.shape)]
    args = [cd, q, k, v, gate, dm, qd, kd]
    if state0 is not None:
        in_specs.append(st_spec)
        args.append(state0)
    return pl.pallas_call(
        functools.partial(_retention_kernel, has_state0=state0 is not None),
        grid=(nb // bb, t // c),
        in_specs=in_specs,
        out_specs=[tok(vd), st_spec],
        out_shape=[jax.ShapeDtypeStruct((nb, t, vd), BF16),
                   jax.ShapeDtypeStruct((nb, RET_HEADS, RET_KEY, RET_VAL), F32)],
        compiler_params=_params(2),
        name="retention_s" if state0 is not None else "retention_p",
    )(*args)


def _retention_consts(c, c_pad):
    log_g = np.log1p(-np.exp2(-5.0 - np.arange(RET_HEADS, dtype=np.float64)))
    i = np.arange(c, dtype=np.float64)
    diff = i[:, None] - i[None, :]
    dm = np.where(diff >= 0, np.exp(log_g[:, None, None] * np.maximum(diff, 0.0)), 0.0)
    qd = np.exp(log_g[:, None] * (i[None, :] + 1.0))
    kd = np.exp(log_g[:, None] * (c - 1.0 - i[None, :]))
    cd = np.exp(log_g * c)
    pad = c_pad - c
    dm = np.pad(dm, ((0, 0), (0, pad), (0, pad)))
    qd = np.broadcast_to(np.pad(qd, ((0, 0), (0, pad)))[:, :, None], (RET_HEADS, c_pad, RET_KEY))
    kd = np.broadcast_to(np.pad(kd, ((0, 0), (0, pad)))[:, :, None], (RET_HEADS, c_pad, RET_KEY))
    return tuple(jnp.asarray(np.ascontiguousarray(a), F32) for a in (dm, qd, kd, cd))


def _rotation_tables(pos):
    half = RET_KEY // 2
    theta = 1.0 / (10000.0 ** np.linspace(0.0, 1.0, half, dtype=np.float32).astype(np.float64))
    ang = np.asarray(pos, np.float64)[:, None] * theta[None, :]
    return jnp.asarray(np.cos(ang), F32), jnp.asarray(np.sin(ang), F32)


def kernel(x_prompt, x_sample, cache_k, cache_v, cache_kidx, state_ret, page_table, rel_bias, ln_mix, ln_mlp,
           att_w_in, att_q_gain, att_k_gain, att_w_out, ret_w_in, ret_w_out, mlp_w_in, mlp_w_out):
    bp, tp, _ = x_prompt.shape
    bs, ts, _ = x_sample.shape
    n_phys = cache_k.shape[1]
    assert ln_mix.shape[0] == 2 and att_w_in.shape[0] == 1 and ret_w_in.shape[0] == 1
    assert tp % RET_CHUNK == 0 and ts <= TQ and cache_k.shape[2] == PAGE
    assert (tp // PAGE) % 4 == 0 and bs % SEL_SAMPLES == 0 and bs % ATT_SAMPLES == 0 and bp % PROMPT_SEQS == 0
    kvw = ATT_KV_HEADS * HEAD_DIM
    past = page_table.shape[1] * PAGE

    w_att = jnp.pad(jnp.swapaxes(att_w_in[0], 0, 1), ((0, ATT_IN_PAD - ATT_IN), (0, 0))).astype(BF16)
    w_att_out = att_w_out[0].astype(BF16)
    w_ret = ret_w_in[0].astype(BF16)
    w_ret_out = ret_w_out[0].astype(BF16)
    w1 = mlp_w_in.astype(BF16)
    w2 = mlp_w_out.astype(BF16)
    row = lambda a: a.reshape(1, -1)

    xp = x_prompt.reshape(bp * tp, D_MODEL)
    xs = x_sample.reshape(bs * ts, D_MODEL)

    qT, qiT, wiT, vT3, kb, kib, kp, vp, kip = _attn_proj(
        xp, row(ln_mix[0]), w_att, row(att_q_gain[0]), row(att_k_gain[0]), transposed=True)
    op = _dsa_prompt(rel_bias, qT, qiT, wiT, kb, vT3, kib, batch=bp, seq=tp)
    yp = _post_mix(xp, op, w_att_out, row(ln_mlp[0]), w1[0], w2[0])

    qs, qis, zls, ks, vs, kis = _attn_proj(
        xs, row(ln_mix[0]), w_att, row(att_q_gain[0]), row(att_k_gain[0]), transposed=False)
    pad_t = lambda a: jnp.pad(a, ((0, 0), (0, TQ - ts)) + ((0, 0),) * (a.ndim - 2))
    q8 = pad_t(qs.reshape(bs, ts, ATT_HEADS, HEAD_DIM)).transpose(0, 2, 1, 3)
    q8 = q8.reshape(bs, ATT_HEADS * TQ, HEAD_DIM).astype(BF16)
    qi8 = pad_t(qis.reshape(bs, ts, IDX_HEADS, IDX_DIM)).transpose(0, 2, 1, 3)
    qi8 = qi8.reshape(bs, IDX_HEADS * TQ, IDX_DIM).astype(BF16)
    zl8 = pad_t(zls.reshape(bs, ts, LANES))
    kn8 = pad_t(ks.reshape(bs, ts, kvw))
    vn8 = pad_t(vs.reshape(bs, ts, kvw))
    mb = _dsa_select(page_table, qi8, zl8, jnp.swapaxes(cache_kidx[0], 1, 2), n_new=ts)
    os8 = _dsa_sample(page_table, rel_bias, q8, mb, kn8, vn8,
                      cache_k[0].reshape(n_phys * PAGE * ATT_KV_HEADS, HEAD_DIM),
                      cache_v[0].reshape(n_phys * PAGE * ATT_KV_HEADS, HEAD_DIM))
    os_ = os8.reshape(bs, ATT_HEADS, TQ, HEAD_DIM)[:, :, :ts].transpose(0, 2, 1, 3)
    os_ = os_.reshape(bs * ts, ATT_HEADS * HEAD_DIM)
    ys = _post_mix(xs, os_, w_att_out, row(ln_mlp[0]), w1[0], w2[0])

    cos_p, sin_p = _rotation_tables(np.arange(tp))
    q, k, v, gate = _ret_proj(yp, row(ln_mix[1]), w_ret, cos_p, sin_p)
    r3 = lambda a: a.reshape(bp, tp, a.shape[-1])
    orp, state_p = _retention(r3(q), r3(k), r3(v), r3(gate), _retention_consts(RET_CHUNK, RET_CHUNK), None,
                            seqs_per_step=1)
    yp = _post_mix(yp, orp.reshape(bp * tp, -1), w_ret_out, row(ln_mlp[1]), w1[1], w2[1])

    cos_s, sin_s = _rotation_tables(np.tile(past + np.arange(ts), bs))
    q, k, v, gate = _ret_proj(ys, row(ln_mix[1]), w_ret, cos_s, sin_s)
    r3s = lambda a: pad_t(a.reshape(bs, ts, a.shape[-1]))
    ors, state_s = _retention(r3s(q), r3s(k), r3s(v), r3s(gate), _retention_consts(ts, TQ), state_ret[0],
                            seqs_per_step=2)
    ys = _post_mix(ys, ors[:, :ts].reshape(bs * ts, -1), w_ret_out, row(ln_mlp[1]), w1[1], w2[1])

    return (yp.reshape(bp, tp, D_MODEL), ys.reshape(bs, ts, D_MODEL),
            kp.reshape(1, bp, tp, ATT_KV_HEADS, HEAD_DIM), vp.reshape(1, bp, tp, ATT_KV_HEADS, HEAD_DIM),
            kip.reshape(1, bp, tp, IDX_DIM), state_p[None],
            ks.reshape(1, bs, ts, ATT_KV_HEADS, HEAD_DIM), vs.reshape(1, bs, ts, ATT_KV_HEADS, HEAD_DIM),
            kis.reshape(1, bs, ts, IDX_DIM), state_s[None])
```

```python
import functools
import math

import jax
import jax.numpy as jnp
import numpy as np
from jax import lax
from jax.experimental import pallas as pl
from jax.experimental.pallas import tpu as pltpu

F32 = jnp.float32
BF16 = jnp.bfloat16
I32 = jnp.int32

D_MODEL = 1024
PAGE = 128
ATT_HEADS = 8
ATT_KV_HEADS = 2
ATT_GROUP = ATT_HEADS // ATT_KV_HEADS
HEAD_DIM = 128
IDX_HEADS = 4
IDX_DIM = 64
TOPK_MAX = 256
NUM_BUCKETS = 32
MAX_DISTANCE = 128
RET_HEADS = 4
RET_KEY = 256
RET_VAL = 512
RET_CHUNK = 256
D_FF = 4 * D_MODEL
EPS = 1e-6

Q_OFF = 0
K_OFF = ATT_HEADS * HEAD_DIM
V_OFF = K_OFF + ATT_KV_HEADS * HEAD_DIM
QI_OFF = V_OFF + ATT_KV_HEADS * HEAD_DIM
KI_OFF = QI_OFF + IDX_HEADS * IDX_DIM
WI_OFF = KI_OFF + IDX_DIM
ATT_IN = WI_OFF + IDX_HEADS
ATT_IN_PAD = 1920

LANES = 128
SUBLANES = 8
ROW_TILE = 512
VMEM_LIMIT = 56 * 1024 * 1024

LOG2E = math.log2(math.e)
Q_SCALE = HEAD_DIM ** -0.5 * LOG2E
ONES_ROWS = 16
RET_KSCALE = RET_KEY ** -0.5
INT_MIN = -(2 ** 31)
NEG = -1e30


def _bucket_bounds():
    n = np.arange(0, 8192)
    max_exact = NUM_BUCKETS // 2
    val = np.log(np.maximum(n, 1) / max_exact) / math.log(MAX_DISTANCE / max_exact) * (NUM_BUCKETS - max_exact)
    frac = np.abs(val - np.round(val))
    risky = (n > max_exact) & (n < MAX_DISTANCE) & (frac < 1e-4)
    assert not risky.any()
    large = np.minimum(max_exact + np.floor(val + 1e-9).astype(np.int64), NUM_BUCKETS - 1)
    bucket = np.where(n < max_exact, n, large)
    assert (np.diff(bucket) >= 0).all() and (bucket[MAX_DISTANCE:] == NUM_BUCKETS - 1).all()
    return [int(np.argmax(bucket >= b)) for b in range(NUM_BUCKETS)]


BUCKET_BOUNDS = _bucket_bounds()


def _const_spec(shape):
    zeros = (0,) * len(shape)
    return pl.BlockSpec(shape, lambda *_: zeros, pipeline_mode=pl.Buffered(1))


def _params(n_axes):
    return pltpu.CompilerParams(dimension_semantics=("arbitrary",) * n_axes,
                                vmem_limit_bytes=VMEM_LIMIT)


def _rms_rows(a):
    return a * lax.rsqrt(jnp.mean(a * a, axis=-1, keepdims=True) + EPS)


def _canonical_zero(score):
    return jnp.where(score == 0.0, 0.0, score)


def _sortable_key(score):
    bits = pltpu.bitcast(score, I32)
    return bits ^ ((bits >> 31) & jnp.int32(0x7FFFFFFF))


def _key_to_float(key):
    return pltpu.bitcast(key ^ ((key >> 31) & jnp.int32(0x7FFFFFFF)), F32)


KEY_NEG_INF = -2139095041
KEY_POS_INF = 2139095040
KEY_SUBNORMAL_LO = -8388608
KEY_SUBNORMAL_HI = 8388607


def _bracket_kth_largest(count_ge, kprime, guess):
    def canon(x):
        return jnp.where(jnp.logical_and(x >= KEY_SUBNORMAL_LO, x <= KEY_SUBNORMAL_HI), 0, x)

    def succ(x):
        xc = canon(x)
        return jnp.where(xc == 0, KEY_SUBNORMAL_HI + 1, canon(xc + 1))

    g0 = jnp.clip(guess, KEY_NEG_INF, KEY_POS_INF - 1)
    g1 = succ(g0)
    lo = jnp.full(guess.shape, KEY_NEG_INF, I32)
    hi = jnp.full(guess.shape, KEY_POS_INF, I32)

    def narrow(lo, hi, x, cnt):
        ok = cnt >= kprime
        return jnp.where(ok, jnp.maximum(lo, x), lo), jnp.where(ok, hi, jnp.minimum(hi, x))

    c0, c1 = count_ge([_key_to_float(g0), _key_to_float(g1)])
    lo, hi = narrow(lo, hi, g0, c0)
    lo, hi = narrow(lo, hi, g1, c1)

    def open_bracket(c):
        return jnp.max(jnp.where(canon(c[1]) > succ(c[0]), 1.0, 0.0)) > 0.5

    def bisect(c):
        lo, hi = c
        mid = jnp.where(canon(hi) > succ(lo), (lo >> 1) + (hi >> 1) + (lo & hi & 1), lo)
        return narrow(lo, hi, mid, count_ge([_key_to_float(mid)])[0])

    lo, _ = lax.while_loop(open_bracket, bisect, (lo, hi))
    return canon(lo)


def _bit_transpose32(words):
    a = list(words)
    j, m = 16, 0x0000FFFF
    while j:
        k = 0
        while k < 32:
            t = (a[k] ^ lax.shift_right_logical(a[k + j], jnp.int32(j))) & jnp.int32(m)
            a[k] = a[k] ^ t
            a[k + j] = a[k + j] ^ lax.shift_left(t, jnp.int32(j))
            k = (k + j + 1) & ~j
        j >>= 1
        m ^= (m << j) & 0xFFFFFFFF
    return a


def _bias_from_distance(relb_ref, head, dist):
    val = jnp.full(dist.shape, relb_ref[0, head], F32)
    for b in range(1, NUM_BUCKETS):
        val = jnp.where(dist >= BUCKET_BOUNDS[b], relb_ref[b, head], val)
    return val * LOG2E


def _attn_proj_kernel(x_ref, g_ref, w_ref, qg_ref, kg_ref, *outs, transposed):
    x = x_ref[...]
    h = _rms_rows(x) * g_ref[...]
    z = lax.dot_general(h.astype(BF16), w_ref[...], (((1,), (1,)), ((), ())),
                        preferred_element_type=F32)
    qg = qg_ref[...]
    kg = kg_ref[...]
    k = jnp.concatenate(
        [_rms_rows(z[:, K_OFF + g * HEAD_DIM:K_OFF + (g + 1) * HEAD_DIM]) * kg for g in range(ATT_KV_HEADS)],
        axis=1)
    v = z[:, V_OFF:QI_OFF]
    qi = z[:, QI_OFF:KI_OFF]
    zl = z[:, KI_OFF:ATT_IN_PAD]
    ki = zl[:, :IDX_DIM]
    if transposed:
        qT_ref, qiT_ref, wiT_ref, vT_ref, kb_ref, kib_ref, k_ref, v_ref, ki_ref = outs
        for hh in range(ATT_HEADS):
            qh = _rms_rows(z[:, hh * HEAD_DIM:(hh + 1) * HEAD_DIM]) * qg * Q_SCALE
            qT_ref[hh * HEAD_DIM:(hh + 1) * HEAD_DIM, :] = qh.T.astype(BF16)
        for c in range(IDX_HEADS * IDX_DIM // LANES):
            qiT_ref[c * LANES:(c + 1) * LANES, :] = qi[:, c * LANES:(c + 1) * LANES].T.astype(BF16)
        wiT_ref[...] = zl.T[IDX_DIM:IDX_DIM + 8, :]
        for g in range(ATT_KV_HEADS):
            vT = v[:, g * HEAD_DIM:(g + 1) * HEAD_DIM].T
            for c in range(x.shape[0] // PAGE):
                vT_ref[c, g * HEAD_DIM:(g + 1) * HEAD_DIM, :] = vT[:, c * PAGE:(c + 1) * PAGE].astype(BF16)
        kb_ref[...] = k.astype(BF16)
        kib_ref[...] = ki.astype(BF16)
    else:
        q_ref, qi_ref, zl_ref, k_ref, v_ref, ki_ref = outs
        for hh in range(ATT_HEADS):
            q_ref[:, hh * HEAD_DIM:(hh + 1) * HEAD_DIM] = _rms_rows(z[:, hh * HEAD_DIM:(hh + 1) * HEAD_DIM]) * qg * Q_SCALE
        qi_ref[...] = qi
        zl_ref[...] = zl
    for g in range(ATT_KV_HEADS):
        rows_g = pl.ds(g, x.shape[0], stride=ATT_KV_HEADS)
        k_ref[rows_g, :] = k[:, g * HEAD_DIM:(g + 1) * HEAD_DIM]
        v_ref[rows_g, :] = v[:, g * HEAD_DIM:(g + 1) * HEAD_DIM]
    ki_ref[...] = ki


def _attn_proj(x, g, w, qg, kg, *, transposed):
    rows = x.shape[0]
    tm = ROW_TILE
    nt = rows // tm
    kvw = ATT_KV_HEADS * HEAD_DIM
    row_spec = lambda width: pl.BlockSpec((tm, width), lambda i: (i, 0))
    col_spec = lambda height: pl.BlockSpec((height, tm), lambda i: (0, i))
    kv_rows = ATT_KV_HEADS * rows
    kv_spec = pl.BlockSpec((ATT_KV_HEADS * tm, HEAD_DIM), lambda i: (i, 0))
    leaves_shape = [jax.ShapeDtypeStruct((kv_rows, HEAD_DIM), F32), jax.ShapeDtypeStruct((kv_rows, HEAD_DIM), F32),
                    jax.ShapeDtypeStruct((rows, IDX_DIM), F32)]
    leaves_spec = [kv_spec, kv_spec, row_spec(IDX_DIM)]
    if transposed:
        out_shape = [jax.ShapeDtypeStruct((ATT_HEADS * HEAD_DIM, rows), BF16),
                     jax.ShapeDtypeStruct((IDX_HEADS * IDX_DIM, rows), BF16),
                     jax.ShapeDtypeStruct((8, rows), F32),
                     jax.ShapeDtypeStruct((rows // PAGE, kvw, PAGE), BF16),
                     jax.ShapeDtypeStruct((rows, kvw), BF16),
                     jax.ShapeDtypeStruct((rows, IDX_DIM), BF16)] + leaves_shape
        out_specs = [col_spec(ATT_HEADS * HEAD_DIM), col_spec(IDX_HEADS * IDX_DIM), col_spec(8),
                     pl.BlockSpec((tm // PAGE, kvw, PAGE), lambda i: (i, 0, 0)),
                     row_spec(kvw), row_spec(IDX_DIM)] + leaves_spec
    else:
        out_shape = [jax.ShapeDtypeStruct((rows, ATT_HEADS * HEAD_DIM), F32),
                     jax.ShapeDtypeStruct((rows, IDX_HEADS * IDX_DIM), F32),
                     jax.ShapeDtypeStruct((rows, LANES), F32)] + leaves_shape
        out_specs = [row_spec(ATT_HEADS * HEAD_DIM), row_spec(IDX_HEADS * IDX_DIM), row_spec(LANES)] + leaves_spec
    return pl.pallas_call(
        functools.partial(_attn_proj_kernel, transposed=transposed),
        grid=(nt,),
        in_specs=[row_spec(D_MODEL), _const_spec((1, D_MODEL)), _const_spec((ATT_IN_PAD, D_MODEL)),
                  _const_spec((1, HEAD_DIM)), _const_spec((1, HEAD_DIM))],
        out_specs=out_specs,
        out_shape=out_shape,
        compiler_params=_params(1),
        name="attn_proj_t" if transposed else "attn_proj_r",
    )(x, g, w, qg, kg)


PROMPT_SEQS = 2


def _dsa_prompt_kernel(relb_ref, *refs, topk):
    seqs = range(PROMPT_SEQS)
    n_in = 6
    qT_refs, qiT_refs, wiT_refs, k_refs, vT_refs, ki_refs = zip(*[refs[n_in * u:n_in * (u + 1)] for u in seqs])
    o_ref, keys_sc, scores_sc, planes_sc, mb_sc, bias_sc, logit_sc, acc_sc = refs[n_in * PROMPT_SEQS:]
    b = pl.program_id(0)
    i = pl.program_id(1)
    srow = lax.broadcasted_iota(I32, (PAGE, PAGE), 0)
    tcol = lax.broadcasted_iota(I32, (PAGE, PAGE), 1)

    @pl.when((b == 0) & (i == 0))
    def _init_bias():
        planes_sc[...] = jnp.zeros_like(planes_sc)
        for hh in range(ATT_HEADS):
            for rel in range(2):
                dist = jnp.maximum(rel * PAGE + tcol - srow, 0)
                bias_sc[hh, rel] = _bias_from_distance(relb_ref, hh, dist)
            bias_sc[hh, 2] = jnp.full((PAGE, PAGE), relb_ref[NUM_BUCKETS - 1, hh] * LOG2E, F32)

    wiT = [r[...] for r in wiT_refs]
    qi_cat = [jnp.concatenate([r[hh * IDX_DIM:(hh + 1) * IDX_DIM, :] for hh in range(IDX_HEADS)], axis=1)
              for r in qiT_refs]

    npair = (i + 2) // 2

    def score_block(u, j):
        off = pl.multiple_of(j * PAGE, PAGE)
        kij = ki_refs[u][pl.ds(off, PAGE), :]
        idx = jnp.dot(kij, qi_cat[u], preferred_element_type=F32)
        sc = jnp.zeros((PAGE, PAGE), F32)
        for hh in range(IDX_HEADS):
            sc = sc + wiT[u][hh:hh + 1, :] * jnp.maximum(idx[:, hh * PAGE:(hh + 1) * PAGE], 0.0)
        sc = _canonical_zero(sc)
        causal = jnp.logical_or(j < i, jnp.logical_and(j == i, srow <= tcol))
        scores_sc[u, j] = jnp.where(causal, sc, -jnp.inf)
        keys_sc[u, j] = jnp.where(causal, _sortable_key(sc), INT_MIN)

    nquad = (i + 4) // 4

    def score_body(jj, carry):
        for blk in range(4):
            for u in seqs:
                score_block(u, 4 * jj + blk)
        return carry

    lax.fori_loop(0, nquad, score_body, 0)

    lane = lax.broadcasted_iota(I32, (1, PAGE), 1)
    kprime = jnp.minimum(topk, i * PAGE + lane + 1).astype(F32)

    def plane_body(jj, carry):
        for u in seqs:
            rows = [keys_sc[u, 2 * jj + blk, k * SUBLANES:(k + 1) * SUBLANES, :] ^ INT_MIN
                    for blk in range(2) for k in range(PAGE // SUBLANES)]
            planes = _bit_transpose32(rows)
            for p in range(32):
                planes_sc[u, jj, p] = planes[p]
        return carry

    lax.fori_loop(0, npair, plane_body, 0)

    n_pairs_max = keys_sc.shape[1] // 2

    def bit_step(u, bi, state):
        thr_u, alive, above = state
        cnt = jnp.zeros((SUBLANES, PAGE), I32)
        with_bit, at_least = [], []
        for p in range(n_pairs_max):
            t = alive[p] & planes_sc[u, p, bi]
            ge = above[p] | t
            cnt = cnt + lax.population_count(ge)
            with_bit.append(t)
            at_least.append(ge)
        take = jnp.sum(cnt.astype(F32), axis=0, keepdims=True) >= kprime
        alive = tuple(jnp.where(take, t, a ^ t) for t, a in zip(with_bit, alive))
        above = tuple(jnp.where(take, g, ge) for g, ge in zip(above, at_least))
        thr_u = thr_u | jnp.where(take, lax.shift_left(jnp.int32(1), 31 - bi), 0)
        return thr_u, alive, above

    zero_words = jnp.zeros((SUBLANES, PAGE), I32)
    states = lax.fori_loop(
        0, 32, lambda bi, states: tuple(bit_step(u, bi, states[u]) for u in seqs),
        tuple((jnp.zeros((1, PAGE), I32),
               tuple(jnp.where(p < npair, jnp.int32(-1), zero_words) for p in range(n_pairs_max)),
               (zero_words,) * n_pairs_max) for u in seqs))
    def count_scores(u, preds):
        def body(jj, cnts):
            for blk in range(2):
                sc = scores_sc[u, 2 * jj + blk]
                cnts = tuple(c + jnp.where(p(sc), 1.0, 0.0) for c, p in zip(cnts, preds))
            return cnts
        cnts = lax.fori_loop(0, npair, body, tuple(jnp.zeros((PAGE, PAGE), F32) for _ in preds))
        return [jnp.sum(c, axis=0, keepdims=True) for c in cnts]

    def count_ge(u):
        return lambda thresholds: count_scores(u, [lambda sc, x=x: sc >= x for x in thresholds])

    thr = [_key_to_float(_bracket_kth_largest(count_ge(u), kprime, states[u][0] ^ INT_MIN)) for u in seqs]
    need = [kprime - count_scores(u, [lambda sc, u=u: sc > thr[u]])[0] for u in seqs]

    tri = jnp.where(srow > tcol, 1.0, 0.0).astype(BF16)

    def mask_block(u, j, carry):
        sc = scores_sc[u, j]
        eqf = jnp.where(sc == thr[u], 1.0, 0.0)
        rank = jnp.dot(tri, eqf.astype(BF16), preferred_element_type=F32) + carry
        tie = jnp.where(jnp.logical_and(sc == thr[u], rank < need[u]), 0.0, NEG)
        mb_sc[u, j] = jnp.where(sc > thr[u], 0.0, tie)
        return carry + jnp.sum(eqf, axis=0, keepdims=True)

    def mask_body(jj, carry):
        for blk in range(4):
            carry = tuple(mask_block(u, 4 * jj + blk, carry[u]) for u in seqs)
        return carry

    lax.fori_loop(0, nquad, mask_body, tuple(jnp.zeros((1, PAGE), F32) for _ in seqs))

    groups = range(ATT_KV_HEADS)
    units = [(u, g) for u in seqs for g in groups]
    width = ATT_GROUP * PAGE
    qcats = [jnp.concatenate(
        [qT_refs[u][(ATT_GROUP * g + r) * HEAD_DIM:(ATT_GROUP * g + r + 1) * HEAD_DIM, :]
         for r in range(ATT_GROUP)], axis=1) for u, g in units]

    def logit_block(j, ms):
        off = pl.multiple_of(j * PAGE, PAGE)
        relc = jnp.clip(i - j, 0, 2)
        out = []
        for n, (u, g) in enumerate(units):
            mb = mb_sc[u, j]
            kj = k_refs[u][pl.ds(off, PAGE), g * HEAD_DIM:(g + 1) * HEAD_DIM]
            logits = jnp.dot(kj, qcats[n], preferred_element_type=F32)
            logits = jnp.concatenate(
                [logits[:, r * PAGE:(r + 1) * PAGE] + (mb + bias_sc[ATT_GROUP * g + r, relc])
                 for r in range(ATT_GROUP)], axis=1)
            logit_sc[u, g, j] = logits
            out.append(jnp.maximum(ms[n], jnp.max(logits, axis=0, keepdims=True)))
        return tuple(out)

    def over_block_pairs(pair_step, carry):
        nfull = npair // 2
        carry = lax.fori_loop(0, nfull, lambda t, c: pair_step(4 * t + 2, pair_step(4 * t, c)), carry)
        return lax.cond(npair % 2 == 1, lambda c: pair_step(4 * nfull, c), lambda c: c, carry)

    ms = over_block_pairs(lambda j0, c: logit_block(j0 + 1, logit_block(j0, c)),
                          tuple(jnp.full((1, width), NEG, F32) for _ in units))

    ones_rows = jnp.ones((ONES_ROWS, PAGE), BF16)

    def value_pair(j0, carry):
        for n, (u, g) in enumerate(units):
            p = jnp.concatenate(
                [jnp.exp2(logit_sc[u, g, j0 + blk] - ms[n]).astype(BF16) for blk in range(2)], axis=0)
            vT2 = jnp.concatenate(
                [jnp.concatenate([vT_refs[u][j0 + blk, g * HEAD_DIM:(g + 1) * HEAD_DIM, :], ones_rows], axis=0)
                 for blk in range(2)], axis=1)
            acc_sc[u, g] = acc_sc[u, g] + jnp.dot(vT2, p, preferred_element_type=F32)
        return carry

    acc_sc[...] = jnp.zeros_like(acc_sc)
    over_block_pairs(value_pair, 0)
    for u, g in units:
        acc = acc_sc[u, g]
        outT = acc[:HEAD_DIM] * (1.0 / acc[HEAD_DIM:HEAD_DIM + 1])
        for r in range(ATT_GROUP):
            hh = ATT_GROUP * g + r
            o_ref[u, :, hh * HEAD_DIM:(hh + 1) * HEAD_DIM] = outT[:, r * PAGE:(r + 1) * PAGE].T.astype(BF16)


def _dsa_prompt(rel_bias, qT, qiT, wiT, kb, vT3, kib, *, batch, seq):
    nb = seq // PAGE
    rows = batch * seq
    kvw = ATT_KV_HEADS * HEAD_DIM
    ns = PROMPT_SEQS

    def seq_specs(u):
        qcol = lambda height: pl.BlockSpec((height, PAGE), lambda b, i: (0, (ns * b + u) * nb + i))
        return [qcol(ATT_HEADS * HEAD_DIM), qcol(IDX_HEADS * IDX_DIM), qcol(8),
                pl.BlockSpec((seq, kvw), lambda b, i: (ns * b + u, 0)),
                pl.BlockSpec((nb, kvw, PAGE), lambda b, i: (ns * b + u, 0, 0)),
                pl.BlockSpec((seq, IDX_DIM), lambda b, i: (ns * b + u, 0))]

    out = pl.pallas_call(
        functools.partial(_dsa_prompt_kernel, topk=min(TOPK_MAX, seq // 4)),
        grid=(batch // ns, nb),
        in_specs=[pl.BlockSpec(memory_space=pltpu.SMEM)] + [s for u in range(ns) for s in seq_specs(u)],
        out_specs=pl.BlockSpec((ns, PAGE, ATT_HEADS * HEAD_DIM), lambda b, i: (b, i, 0)),
        out_shape=jax.ShapeDtypeStruct((batch, seq, ATT_HEADS * HEAD_DIM), BF16),
        scratch_shapes=[pltpu.VMEM((ns, nb, PAGE, PAGE), I32), pltpu.VMEM((ns, nb, PAGE, PAGE), F32),
                        pltpu.VMEM((ns, nb // 2, 32, SUBLANES, PAGE), I32),
                        pltpu.VMEM((ns, nb, PAGE, PAGE), F32),
                        pltpu.VMEM((ATT_HEADS, 3, PAGE, PAGE), F32),
                        pltpu.VMEM((ns, ATT_KV_HEADS, nb, PAGE, ATT_GROUP * PAGE), F32),
                        pltpu.VMEM((ns, ATT_KV_HEADS, HEAD_DIM + ONES_ROWS, ATT_GROUP * PAGE), F32)],
        compiler_params=_params(2),
        name="dsa_prompt",
    )(rel_bias, *([qT, qiT, wiT, kb, vT3, kib] * ns))
    return out.reshape(rows, ATT_HEADS * HEAD_DIM)


TQ = 8
ATT_SAMPLES = 2
SEL_SAMPLES = 16


def _dsa_select_kernel(pt_ref, qi_ref, zl_ref, cki_hbm, mb_ref, sc_sc, kibuf, sems, *, n_pages, n_new):
    ns = SEL_SAMPLES
    n_fetch = ns * n_pages
    b = pl.program_id(0)
    slot = b % 2

    def page_copies(step, slot):
        return [pltpu.make_async_copy(cki_hbm.at[pt_ref[step * n_fetch + f]], kibuf.at[slot, f], sems.at[slot])
                for f in range(n_fetch)]

    @pl.when(b == 0)
    def _start_first():
        for n, c in enumerate(page_copies(0, 0)):
            c.start(priority=n % 2)

    @pl.when(b + 1 < pl.num_programs(0))
    def _start_next():
        for n, c in enumerate(page_copies(b + 1, 1 - slot)):
            c.start(priority=n % 2)

    for c in page_copies(b, slot):
        c.wait()
    ki_pages = [kibuf.at[slot, f] for f in range(n_fetch)]
    n_blk = n_pages + 1
    rows = ns * TQ
    trow = lax.broadcasted_iota(I32, (TQ, PAGE), 0)
    scol = lax.broadcasted_iota(I32, (TQ, PAGE), 1)
    pad_rows = lambda a: jnp.concatenate([a, jnp.zeros((PAGE - TQ, a.shape[1]), a.dtype)], axis=0)
    nt = (((1,), (1,)), ((), ()))

    for s in range(ns):
        zl = zl_ref[s]
        qi = qi_ref[s]
        for p in range(n_blk):
            if p == n_pages:
                ki_new = pad_rows(zl[:, :IDX_DIM]).astype(BF16)
                idx = lax.dot_general(qi, ki_new, nt, preferred_element_type=F32)
            else:
                idx = jnp.dot(qi, ki_pages[s * n_pages + p][...].astype(BF16), preferred_element_type=F32)
            sc = jnp.zeros((TQ, PAGE), F32)
            for hh in range(IDX_HEADS):
                sc = sc + zl[:, IDX_DIM + hh:IDX_DIM + hh + 1] * jnp.maximum(idx[hh * TQ:(hh + 1) * TQ], 0.0)
            sc = _canonical_zero(sc)
            if p == n_pages:
                sc = jnp.where(jnp.logical_and(scol <= trow, scol < n_new), sc, -jnp.inf)
            sc_sc[s * TQ:(s + 1) * TQ, p * PAGE:(p + 1) * PAGE] = sc

    kprime = float(min(TOPK_MAX, (n_pages * PAGE + n_new) // 4))

    def count(pred_of_scores):
        return jnp.sum(jnp.where(pred_of_scores(sc_sc[...]), 1.0, 0.0), axis=1, keepdims=True)

    cand0 = jnp.zeros((rows, 1), I32)
    thr0 = jnp.where(count(lambda sc: sc >= _key_to_float(cand0)) >= kprime, cand0, INT_MIN)

    def bit_body(bi, thr_key):
        cand = thr_key | lax.shift_left(jnp.int32(1), 30 - bi)
        return jnp.where(count(lambda sc: sc >= _key_to_float(cand)) >= kprime, cand, thr_key)

    thr = _key_to_float(lax.fori_loop(0, 31, bit_body, thr0))
    need = kprime - count(lambda sc: sc > thr)

    s_r = lax.broadcasted_iota(I32, (PAGE, PAGE), 0)
    s_c = lax.broadcasted_iota(I32, (PAGE, PAGE), 1)
    triu = jnp.where(s_r < s_c, 1.0, 0.0).astype(BF16)
    carry = jnp.zeros((rows, 1), F32)
    for p in range(n_blk):
        sc = sc_sc[:, p * PAGE:(p + 1) * PAGE]
        eqf = jnp.where(sc == thr, 1.0, 0.0)
        rank = jnp.dot(eqf.astype(BF16), triu, preferred_element_type=F32) + carry
        tie = jnp.where(jnp.logical_and(sc == thr, rank < need), 0.0, NEG)
        mb_ref[:, p * PAGE:(p + 1) * PAGE] = jnp.where(sc > thr, 0.0, tie)
        carry = carry + jnp.sum(eqf, axis=1, keepdims=True)


def _dsa_select(page_table, qi8, zl8, cache_ki, *, n_new):
    nsamp, n_pages = page_table.shape
    ns = SEL_SAMPLES
    width = (n_pages + 1) * PAGE

    in_specs = [pl.BlockSpec((ns, IDX_HEADS * TQ, IDX_DIM), lambda b, pt: (b, 0, 0)),
                pl.BlockSpec((ns, TQ, LANES), lambda b, pt: (b, 0, 0)),
                pl.BlockSpec(memory_space=pl.ANY)]
    grid_spec = pltpu.PrefetchScalarGridSpec(
        num_scalar_prefetch=1, grid=(nsamp // ns,), in_specs=in_specs,
        out_specs=pl.BlockSpec((ns * TQ, width), lambda b, pt: (b, 0)),
        scratch_shapes=[pltpu.VMEM((ns * TQ, width), F32),
                        pltpu.VMEM((2, ns * n_pages, IDX_DIM, PAGE), F32), pltpu.SemaphoreType.DMA((2,))])
    return pl.pallas_call(
        functools.partial(_dsa_select_kernel, n_pages=n_pages, n_new=n_new),
        grid_spec=grid_spec,
        out_shape=jax.ShapeDtypeStruct((nsamp * TQ, width), F32),
        compiler_params=_params(1),
        name="dsa_select",
    )(page_table.reshape(-1), qi8, zl8, cache_ki)


def _dsa_sample_kernel(pt_ref, relb_ref, q_ref, mb_ref, kn_ref, vn_ref, ck_hbm, cv_hbm, o_ref,
                       bias_sc, kbuf, vbuf, sems, *, n_pages):
    past = n_pages * PAGE
    rows_g = ATT_GROUP * TQ
    page_rows = PAGE * ATT_KV_HEADS
    ns = ATT_SAMPLES
    b = pl.program_id(0)
    n_slots = kbuf.shape[0]
    ahead = n_slots - 1
    slot = b % n_slots

    def page_copies(step, slot):
        copies = []
        for f in range(ns * n_pages):
            src = pl.ds(pl.multiple_of(pt_ref[step * ns * n_pages + f] * page_rows, page_rows), page_rows)
            dst = pl.ds(f * page_rows, page_rows)
            copies.append(pltpu.make_async_copy(ck_hbm.at[src, :], kbuf.at[slot, dst, :], sems.at[slot, 0]))
            copies.append(pltpu.make_async_copy(cv_hbm.at[src, :], vbuf.at[slot, dst, :], sems.at[slot, 1]))
        return copies

    @pl.when(b == 0)
    def _start_first():
        for s in range(ahead):
            for n, c in enumerate(page_copies(s, s)):
                c.start(priority=n % 2)

    @pl.when(b + ahead < pl.num_programs(0))
    def _start_next():
        for n, c in enumerate(page_copies(b + ahead, (b + ahead) % n_slots)):
            c.start(priority=n % 2)

    for c in page_copies(b, slot):
        c.wait()

    trow = lax.broadcasted_iota(I32, (TQ, PAGE), 0)
    scol = lax.broadcasted_iota(I32, (TQ, PAGE), 1)

    @pl.when(pl.program_id(0) == 0)
    def _init_bias():
        for g in range(ATT_KV_HEADS):
            for r in range(ATT_GROUP):
                hh = ATT_GROUP * g + r
                rs = slice(r * TQ, (r + 1) * TQ)
                far = jnp.full((TQ, PAGE), relb_ref[NUM_BUCKETS - 1, hh] * LOG2E, F32)
                for p in range(n_pages - 1):
                    bias_sc[g, rs, p * PAGE:(p + 1) * PAGE] = far
                bias_sc[g, rs, past - PAGE:past] = _bias_from_distance(
                    relb_ref, hh, jnp.maximum(PAGE + trow - scol, 0))
                bias_sc[g, rs, past:past + PAGE] = _bias_from_distance(relb_ref, hh, jnp.maximum(trow - scol, 0))

    nt = (((1,), (1,)), ((), ()))
    pad_rows = lambda a: jnp.concatenate([a, jnp.zeros((PAGE - TQ, a.shape[1]), a.dtype)], axis=0)
    units = [(s, g) for s in range(ns) for g in range(ATT_KV_HEADS)]
    for s, g in units:
        gs = slice(g * HEAD_DIM, (g + 1) * HEAD_DIM)
        head_rows = pl.ds(s * n_pages * page_rows + g, past, stride=ATT_KV_HEADS)
        k_all = jnp.concatenate([kbuf[slot, head_rows, :].astype(BF16), pad_rows(kn_ref[s, :, gs]).astype(BF16)], axis=0)
        v_all = jnp.concatenate([vbuf[slot, head_rows, :].astype(BF16), pad_rows(vn_ref[s, :, gs]).astype(BF16)], axis=0)
        mb = jnp.concatenate([mb_ref[s * TQ:(s + 1) * TQ, :]] * ATT_GROUP, axis=0)
        qg = q_ref[s, g * rows_g:(g + 1) * rows_g, :]
        logits = lax.dot_general(qg, k_all, nt, preferred_element_type=F32)
        logits = logits + (mb + bias_sc[g])
        m = jnp.max(logits, axis=1, keepdims=True)
        pr = jnp.exp2(logits - m)
        l = jnp.sum(pr, axis=1, keepdims=True)
        acc = jnp.dot(pr.astype(BF16), v_all, preferred_element_type=F32)
        o_ref[s, g * rows_g:(g + 1) * rows_g, :] = (acc * (1.0 / l)).astype(BF16)


def _dsa_sample(page_table, rel_bias, q8, mb, kn8, vn8, cache_k, cache_v):
    nsamp, n_pages = page_table.shape
    kvw = ATT_KV_HEADS * HEAD_DIM
    ns = ATT_SAMPLES
    per_step = lambda *tail: pl.BlockSpec((ns,) + tail, lambda b, pt: (b,) + (0,) * len(tail))

    n_slots = 3
    page_buf = pltpu.VMEM((n_slots, ns * n_pages * PAGE * ATT_KV_HEADS, HEAD_DIM), F32)
    in_specs = [pl.BlockSpec(memory_space=pltpu.SMEM),
                per_step(ATT_HEADS * TQ, HEAD_DIM),
                pl.BlockSpec((ns * TQ, mb.shape[1]), lambda b, pt: (b, 0)),
                per_step(TQ, kvw), per_step(TQ, kvw),
                pl.BlockSpec(memory_space=pl.ANY), pl.BlockSpec(memory_space=pl.ANY)]
    grid_spec = pltpu.PrefetchScalarGridSpec(
        num_scalar_prefetch=1, grid=(nsamp // ns,), in_specs=in_specs,
        out_specs=per_step(ATT_HEADS * TQ, HEAD_DIM),
        scratch_shapes=[pltpu.VMEM((ATT_KV_HEADS, ATT_GROUP * TQ, mb.shape[1]), F32),
                        page_buf, page_buf, pltpu.SemaphoreType.DMA((n_slots, 2))])
    return pl.pallas_call(
        functools.partial(_dsa_sample_kernel, n_pages=n_pages),
        grid_spec=grid_spec,
        out_shape=jax.ShapeDtypeStruct((nsamp, ATT_HEADS * TQ, HEAD_DIM), BF16),
        compiler_params=_params(1),
        name="dsa_sample",
    )(page_table.reshape(-1), rel_bias, q8, mb, kn8, vn8, cache_k, cache_v)


def _post_mix_kernel(x_ref, o_ref, wo_ref, g_ref, w1_ref, w2_ref, y_ref):
    y = x_ref[...] + jnp.dot(o_ref[...], wo_ref[...], preferred_element_type=F32)
    h = (_rms_rows(y) * g_ref[...]).astype(BF16)
    acc = y
    for c in range(D_FF // D_MODEL):
        a = jnp.dot(h, w1_ref[:, c * D_MODEL:(c + 1) * D_MODEL], preferred_element_type=F32)
        a = jnp.square(jnp.maximum(a, 0.0)).astype(BF16)
        acc = acc + jnp.dot(a, w2_ref[c * D_MODEL:(c + 1) * D_MODEL, :], preferred_element_type=F32)
    y_ref[...] = acc


def _post_mix(x, o, wo, g, w1, w2):
    rows = x.shape[0]
    tm = ROW_TILE
    ko = o.shape[1]
    return pl.pallas_call(
        _post_mix_kernel,
        grid=(rows // tm,),
        in_specs=[pl.BlockSpec((tm, D_MODEL), lambda i: (i, 0)), pl.BlockSpec((tm, ko), lambda i: (i, 0)),
                  _const_spec((ko, D_MODEL)), _const_spec((1, D_MODEL)),
                  _const_spec((D_MODEL, D_FF)), _const_spec((D_FF, D_MODEL))],
        out_specs=pl.BlockSpec((tm, D_MODEL), lambda i: (i, 0)),
        out_shape=jax.ShapeDtypeStruct((rows, D_MODEL), F32),
        compiler_params=_params(1),
        name="post_mix",
    )(x, o, wo, g, w1, w2)


def _ret_proj_kernel(x_ref, g_ref, w_ref, cos_ref, sin_ref, q_ref, k_ref, v_ref, gate_ref):
    hb = (_rms_rows(x_ref[...]) * g_ref[...]).astype(BF16)
    cos = cos_ref[...]
    sin = sin_ref[...]
    half = RET_KEY // 2
    qk = RET_HEADS * RET_KEY
    vd = RET_HEADS * RET_VAL

    def rotated(col0, scale):
        z = jnp.dot(hb, w_ref[:, col0:col0 + RET_KEY], preferred_element_type=F32)
        x1, x2 = z[:, :half], z[:, half:]
        return (x1 * cos - x2 * sin) * scale, (x1 * sin + x2 * cos) * scale

    for hh in range(RET_HEADS):
        a, b = rotated(hh * RET_KEY, 1.0)
        q_ref[:, hh * RET_KEY:hh * RET_KEY + half] = a.astype(BF16)
        q_ref[:, hh * RET_KEY + half:(hh + 1) * RET_KEY] = b.astype(BF16)
        a, b = rotated(qk + hh * RET_KEY, RET_KSCALE)
        k_ref[:, hh * RET_KEY:hh * RET_KEY + half] = a.astype(BF16)
        k_ref[:, hh * RET_KEY + half:(hh + 1) * RET_KEY] = b.astype(BF16)
    for hh in range(RET_HEADS):
        cs = slice(hh * RET_VAL, (hh + 1) * RET_VAL)
        v_ref[:, cs] = jnp.dot(hb, w_ref[:, 2 * qk + hh * RET_VAL:2 * qk + (hh + 1) * RET_VAL],
                               preferred_element_type=F32).astype(BF16)
        gate_ref[:, cs] = jnp.dot(hb, w_ref[:, 2 * qk + vd + hh * RET_VAL:2 * qk + vd + (hh + 1) * RET_VAL],
                                  preferred_element_type=F32).astype(BF16)


def _ret_proj(x, g, w, cos, sin):
    rows = x.shape[0]
    tm = ROW_TILE
    qk = RET_HEADS * RET_KEY
    vd = RET_HEADS * RET_VAL
    n_pos_tiles = cos.shape[0] // tm
    row_spec = lambda width: pl.BlockSpec((tm, width), lambda i: (i, 0))
    pos_spec = pl.BlockSpec((tm, RET_KEY // 2), lambda i: (i % n_pos_tiles, 0))
    return pl.pallas_call(
        _ret_proj_kernel,
        grid=(rows // tm,),
        in_specs=[row_spec(D_MODEL), _const_spec((1, D_MODEL)), _const_spec((D_MODEL, 2 * qk + 2 * vd)),
                  pos_spec, pos_spec],
        out_specs=[row_spec(qk), row_spec(qk), row_spec(vd), row_spec(vd)],
        out_shape=[jax.ShapeDtypeStruct((rows, qk), BF16), jax.ShapeDtypeStruct((rows, qk), BF16),
                   jax.ShapeDtypeStruct((rows, vd), BF16), jax.ShapeDtypeStruct((rows, vd), BF16)],
        compiler_params=_params(1),
        name="ret_proj",
    )(x, g, w, cos, sin)


def _retention_kernel(cd_ref, q_ref, k_ref, v_ref, g_ref, dm_ref, qd_ref, kd_ref, *rest, has_state0):
    if has_state0:
        s0_ref, o_ref, s_ref = rest
    else:
        o_ref, s_ref = rest

    @pl.when(pl.program_id(1) == 0)
    def _init_state():
        s_ref[...] = s0_ref[...] if has_state0 else jnp.zeros_like(s_ref)

    nt = (((1,), (1,)), ((), ()))
    tn = (((0,), (0,)), ((), ()))
    units = [(bi, hh) for bi in range(q_ref.shape[0]) for hh in range(RET_HEADS)]
    ks = lambda hh: slice(hh * RET_KEY, (hh + 1) * RET_KEY)
    vs = lambda hh: slice(hh * RET_VAL, (hh + 1) * RET_VAL)
    att = [lax.dot_general(q_ref[bi, :, ks(hh)].astype(BF16), k_ref[bi, :, ks(hh)].astype(BF16), nt,
                           preferred_element_type=F32) * dm_ref[hh] for bi, hh in units]
    cross = [jnp.dot((q_ref[bi, :, ks(hh)] * qd_ref[hh]).astype(BF16), s_ref[bi, hh].astype(BF16),
                     preferred_element_type=F32) for bi, hh in units]
    for (bi, hh), a, x in zip(units, att, cross):
        o = jnp.dot(a.astype(BF16), v_ref[bi, :, vs(hh)], preferred_element_type=F32) + x
        gate = g_ref[bi, :, vs(hh)].astype(F32)
        o_ref[bi, :, vs(hh)] = (_rms_rows(o) * (gate * jax.nn.sigmoid(gate))).astype(BF16)
    for bi, hh in units:
        kd = (k_ref[bi, :, ks(hh)] * kd_ref[hh]).astype(BF16)
        s_ref[bi, hh] = cd_ref[hh] * s_ref[bi, hh] + lax.dot_general(kd, v_ref[bi, :, vs(hh)], tn,
                                                                     preferred_element_type=F32)


def _retention(q, k, v, gate, consts, state0, *, seqs_per_step):
    dm, qd, kd, cd = consts
    nb, t, _ = q.shape
    c = dm.shape[1]
    bb = seqs_per_step
    qk = RET_HEADS * RET_KEY
    vd = RET_HEADS * RET_VAL
    tok = lambda width: pl.BlockSpec((bb, c, width), lambda b, ci: (b, ci, 0))
    st_spec = pl.BlockSpec((bb, RET_HEADS, RET_KEY, RET_VAL), lambda b, ci: (b, 0, 0, 0))
    in_specs = [pl.BlockSpec(memory_space=pltpu.SMEM), tok(qk), tok(qk), tok(vd), tok(vd),
                _const_spec(dm.shape), _const_spec(qd.shape), _const_spec(kd.shape)]
    args = [cd, q, k, v, gate, dm, qd, kd]
    if state0 is not None:
        in_specs.append(st_spec)
        args.append(state0)
    return pl.pallas_call(
        functools.partial(_retention_kernel, has_state0=state0 is not None),
        grid=(nb // bb, t // c),
        in_specs=in_specs,
        out_specs=[tok(vd), st_spec],
        out_shape=[jax.ShapeDtypeStruct((nb, t, vd), BF16),
                   jax.ShapeDtypeStruct((nb, RET_HEADS, RET_KEY, RET_VAL), F32)],
        compiler_params=_params(2),
        name="retention_s" if state0 is not None else "retention_p",
    )(*args)


def _retention_consts(c, c_pad):
    log_g = np.log1p(-np.exp2(-5.0 - np.arange(RET_HEADS, dtype=np.float64)))
    i = np.arange(c, dtype=np.float64)
    diff = i[:, None] - i[None, :]
    dm = np.where(diff >= 0, np.exp(log_g[:, None, None] * np.maximum(diff, 0.0)), 0.0)
    qd = np.exp(log_g[:, None] * (i[None, :] + 1.0))
    kd = np.exp(log_g[:, None] * (c - 1.0 - i[None, :]))
    cd = np.exp(log_g * c)
    pad = c_pad - c
    dm = np.pad(dm, ((0, 0), (0, pad), (0, pad)))
    qd = np.broadcast_to(np.pad(qd, ((0, 0), (0, pad)))[:, :, None], (RET_HEADS, c_pad, RET_KEY))
    kd = np.broadcast_to(np.pad(kd, ((0, 0), (0, pad)))[:, :, None], (RET_HEADS, c_pad, RET_KEY))
    return tuple(jnp.asarray(np.ascontiguousarray(a), F32) for a in (dm, qd, kd, cd))


def _rotation_tables(pos):
    half = RET_KEY // 2
    theta = 1.0 / (10000.0 ** np.linspace(0.0, 1.0, half, dtype=np.float32).astype(np.float64))
    ang = np.asarray(pos, np.float64)[:, None] * theta[None, :]
    return jnp.asarray(np.cos(ang), F32), jnp.asarray(np.sin(ang), F32)


def kernel(x_prompt, x_sample, cache_k, cache_v, cache_kidx, state_ret, page_table, rel_bias, ln_mix, ln_mlp,
           att_w_in, att_q_gain, att_k_gain, att_w_out, ret_w_in, ret_w_out, mlp_w_in, mlp_w_out):
    bp, tp, _ = x_prompt.shape
    bs, ts, _ = x_sample.shape
    n_phys = cache_k.shape[1]
    assert ln_mix.shape[0] == 2 and att_w_in.shape[0] == 1 and ret_w_in.shape[0] == 1
    assert tp % RET_CHUNK == 0 and ts <= TQ and cache_k.shape[2] == PAGE
    assert (tp // PAGE) % 4 == 0 and bs % SEL_SAMPLES == 0 and bs % ATT_SAMPLES == 0 and bp % PROMPT_SEQS == 0
    kvw = ATT_KV_HEADS * HEAD_DIM
    past = page_table.shape[1] * PAGE

    w_att = jnp.pad(jnp.swapaxes(att_w_in[0], 0, 1), ((0, ATT_IN_PAD - ATT_IN), (0, 0))).astype(BF16)
    w_att_out = att_w_out[0].astype(BF16)
    w_ret = ret_w_in[0].astype(BF16)
    w_ret_out = ret_w_out[0].astype(BF16)
    w1 = mlp_w_in.astype(BF16)
    w2 = mlp_w_out.astype(BF16)
    row = lambda a: a.reshape(1, -1)

    xp = x_prompt.reshape(bp * tp, D_MODEL)
    xs = x_sample.reshape(bs * ts, D_MODEL)

    qT, qiT, wiT, vT3, kb, kib, kp, vp, kip = _attn_proj(
        xp, row(ln_mix[0]), w_att, row(att_q_gain[0]), row(att_k_gain[0]), transposed=True)
    op = _dsa_prompt(rel_bias, qT, qiT, wiT, kb, vT3, kib, batch=bp, seq=tp)
    yp = _post_mix(xp, op, w_att_out, row(ln_mlp[0]), w1[0], w2[0])

    qs, qis, zls, ks, vs, kis = _attn_proj(
        xs, row(ln_mix[0]), w_att, row(att_q_gain[0]), row(att_k_gain[0]), transposed=False)
    pad_t = lambda a: jnp.pad(a, ((0, 0), (0, TQ - ts)) + ((0, 0),) * (a.ndim - 2))
    q8 = pad_t(qs.reshape(bs, ts, ATT_HEADS, HEAD_DIM)).transpose(0, 2, 1, 3)
    q8 = q8.reshape(bs, ATT_HEADS * TQ, HEAD_DIM).astype(BF16)
    qi8 = pad_t(qis.reshape(bs, ts, IDX_HEADS, IDX_DIM)).transpose(0, 2, 1, 3)
    qi8 = qi8.reshape(bs, IDX_HEADS * TQ, IDX_DIM).astype(BF16)
    zl8 = pad_t(zls.reshape(bs, ts, LANES))
    kn8 = pad_t(ks.reshape(bs, ts, kvw))
    vn8 = pad_t(vs.reshape(bs, ts, kvw))
    mb = _dsa_select(page_table, qi8, zl8, jnp.swapaxes(cache_kidx[0], 1, 2), n_new=ts)
    os8 = _dsa_sample(page_table, rel_bias, q8, mb, kn8, vn8,
                      cache_k[0].reshape(n_phys * PAGE * ATT_KV_HEADS, HEAD_DIM),
                      cache_v[0].reshape(n_phys * PAGE * ATT_KV_HEADS, HEAD_DIM))
    os_ = os8.reshape(bs, ATT_HEADS, TQ, HEAD_DIM)[:, :, :ts].transpose(0, 2, 1, 3)
    os_ = os_.reshape(bs * ts, ATT_HEADS * HEAD_DIM)
    ys = _post_mix(xs, os_, w_att_out, row(ln_mlp[0]), w1[0], w2[0])

    cos_p, sin_p = _rotation_tables(np.arange(tp))
    q, k, v, gate = _ret_proj(yp, row(ln_mix[1]), w_ret, cos_p, sin_p)
    r3 = lambda a: a.reshape(bp, tp, a.shape[-1])
    orp, state_p = _retention(r3(q), r3(k), r3(v), r3(gate), _retention_consts(RET_CHUNK, RET_CHUNK), None,
                            seqs_per_step=1)
    yp = _post_mix(yp, orp.reshape(bp * tp, -1), w_ret_out, row(ln_mlp[1]), w1[1], w2[1])

    cos_s, sin_s = _rotation_tables(np.tile(past + np.arange(ts), bs))
    q, k, v, gate = _ret_proj(ys, row(ln_mix[1]), w_ret, cos_s, sin_s)
    r3s = lambda a: pad_t(a.reshape(bs, ts, a.shape[-1]))
    ors, state_s = _retention(r3s(q), r3s(k), r3s(v), r3s(gate), _retention_consts(ts, TQ), state_ret[0],
                            seqs_per_step=2)
    ys = _post_mix(ys, ors[:, :ts].reshape(bs * ts, -1), w_ret_out, row(ln_mlp[1]), w1[1], w2[1])

    return (yp.reshape(bp, tp, D_MODEL), ys.reshape(bs, ts, D_MODEL),
            kp.reshape(1, bp, tp, ATT_KV_HEADS, HEAD_DIM), vp.reshape(1, bp, tp, ATT_KV_HEADS, HEAD_DIM),
            kip.reshape(1, bp, tp, IDX_DIM), state_p[None],
            ks.reshape(1, bs, ts, ATT_KV_HEADS, HEAD_DIM), vs.reshape(1, bs, ts, ATT_KV_HEADS, HEAD_DIM),
            kis.reshape(1, bs, ts, IDX_DIM), state_s[None])
```
